```python
import jax, jax.numpy as jnp
from jax import lax
import numpy as np

D_MODEL = 1024
BATCH = 4
SEQ = 8192
DEPTH = 1
DEC_BATCH = 128
DEC_SEQ = 1
PAST_LEN = 16384
PAGE_SIZE = 128

HEAD_DIM = 64
D_ATTN = D_MODEL // 2
N_HEADS = D_ATTN // HEAD_DIM
N_KV_HEADS = 2
D_CONV = D_MODEL - D_ATTN
KV_DIM = N_KV_HEADS * HEAD_DIM
D_IN = D_ATTN + 2 * KV_DIM + 2 * D_CONV
CONV_WIDTH = 31
WINDOW = 128
BLOCK = 128
ROPE_THETA = 10000.0
N_META = 16
META_PAD = (-N_META) % BLOCK
PEER_HEADS = 8
PEER_NKEYS = 128
PEER_EXPERTS = PEER_NKEYS * PEER_NKEYS
PEER_DQ = 256
PEER_TOPK = 16
PEER_TOKEN_BLOCK = 256
EPS = 1e-6

kernel_name = 'hymba_swa_sink_conformer_peer_step'


def _rmsnorm(x, g):
    xf = x.astype(jnp.float32)
    y = xf * lax.rsqrt(jnp.mean(xf * xf, axis=-1, keepdims=True) + EPS)
    return (y * g.astype(jnp.float32)).astype(x.dtype)


def _layernorm(x, g, b):
    xf = x.astype(jnp.float32)
    xc = xf - jnp.mean(xf, axis=-1, keepdims=True)
    y = xc * lax.rsqrt(jnp.mean(xc * xc, axis=-1, keepdims=True) + EPS)
    return (y * g.astype(jnp.float32) + b.astype(jnp.float32)).astype(x.dtype)


def _rope(x, pos):
    half = HEAD_DIM // 2
    inv = ROPE_THETA ** (-jnp.arange(half, dtype=jnp.float32) / half)
    ang = pos.astype(jnp.float32)[:, None] * inv[None, :]
    cos = jnp.cos(ang)[:, None, :]
    sin = jnp.sin(ang)[:, None, :]
    xf = x.astype(jnp.float32)
    x1, x2 = xf[..., :half], xf[..., half:]
    return jnp.concatenate([x1 * cos - x2 * sin, x2 * cos + x1 * sin], axis=-1).astype(x.dtype)


def _project(xn, w_in, q_g, k_g, pos):
    b, t, _ = xn.shape
    p = xn @ w_in
    q, k, v, ga, gb = jnp.split(p, [D_ATTN, D_ATTN + KV_DIM, D_ATTN + 2 * KV_DIM, D_ATTN + 2 * KV_DIM + D_CONV], axis=-1)
    q = _rope(_rmsnorm(q.reshape(b, t, N_HEADS, HEAD_DIM), q_g), pos)
    k = _rope(_rmsnorm(k.reshape(b, t, N_KV_HEADS, HEAD_DIM), k_g), pos)
    v = v.reshape(b, t, N_KV_HEADS, HEAD_DIM)
    u = ga * jax.nn.sigmoid(gb)
    return q, k, v, u


def _window_sink_attention(q, k, v, q_pos, k_pos, sinks):
    b, n, tq, _, _ = q.shape
    g = N_HEADS // N_KV_HEADS
    qg = q.reshape(b, n, tq, N_KV_HEADS, g, HEAD_DIM)
    s = jnp.einsum('bnqkgd,bnskd->bnkgqs', qg, k).astype(jnp.float32) * (HEAD_DIM ** -0.5)
    diff = q_pos[:, :, None] - k_pos[:, None, :]
    ok = (diff >= 0) & (diff < WINDOW) & (k_pos[:, None, :] >= 0)
    s = jnp.where(ok[None, :, None, None], s, -jnp.inf)
    sink = jnp.broadcast_to(sinks.astype(jnp.float32).reshape(1, 1, N_KV_HEADS, g, 1, 1), s.shape[:-1] + (1,))
    p = jax.nn.softmax(jnp.concatenate([s, sink], axis=-1), axis=-1)[..., :-1]
    o = jnp.einsum('bnkgqs,bnskd->bnqkgd', p.astype(v.dtype), v)
    return o.reshape(b, n, tq, D_ATTN)


def _conv_tail(u_hist, conv_w, conv_b, ln_g, ln_b, w_pw2):
    y = lax.conv_general_dilated(u_hist, conv_w[:, None, :], (1,), 'VALID',
                                 dimension_numbers=('NWC', 'WIO', 'NWC'),
                                 feature_group_count=D_CONV) + conv_b
    y = jax.nn.silu(_layernorm(y, ln_g, ln_b))
    return y @ w_pw2


def _merge(o_attn, o_conv, g_a, g_c, w_out):
    return jnp.concatenate([_rmsnorm(o_attn, g_a), _rmsnorm(o_conv, g_c)], axis=-1) @ w_out


def _peer(xn, w_q, keys1, keys2, w_u, w_v):
    shape = xn.shape
    flat = xn.reshape(-1, D_MODEL)
    n = flat.shape[0]
    nblk = -(-n // PEER_TOKEN_BLOCK)
    flat = jnp.pad(flat, ((0, nblk * PEER_TOKEN_BLOCK - n), (0, 0))).reshape(nblk, PEER_TOKEN_BLOCK, D_MODEL)
    hq = PEER_DQ // 2

    def block(xb):
        q = (xb @ w_q).reshape(PEER_TOKEN_BLOCK, PEER_HEADS, 2, hq)
        s1 = jnp.einsum('thc,kc->thk', q[:, :, 0], keys1).astype(jnp.float32)
        s2 = jnp.einsum('thc,kc->thk', q[:, :, 1], keys2).astype(jnp.float32)
        v1, i1 = lax.top_k(s1, PEER_TOPK)
        v2, i2 = lax.top_k(s2, PEER_TOPK)
        cand = (v1[..., :, None] + v2[..., None, :]).reshape(PEER_TOKEN_BLOCK, PEER_HEADS, PEER_TOPK * PEER_TOPK)
        cid = (i1[..., :, None] * PEER_NKEYS + i2[..., None, :]).reshape(PEER_TOKEN_BLOCK, PEER_HEADS, PEER_TOPK * PEER_TOPK)
        sc, j = lax.top_k(cand, PEER_TOPK)
        eid = jnp.take_along_axis(cid, j, axis=-1)
        gate = jax.nn.softmax(sc, axis=-1)
        u_sel = jnp.take(w_u, eid, axis=0)
        hval = jnp.einsum('thed,td->the', u_sel, xb)
        a = (gate * jax.nn.gelu(hval.astype(jnp.float32), approximate=False)).astype(xb.dtype)
        v_sel = jnp.take(w_v, eid, axis=0)
        return jnp.einsum('the,thed->td', a, v_sel)

    out = lax.map(block, flat).reshape(nblk * PEER_TOKEN_BLOCK, D_MODEL)[:n]
    return out.reshape(shape)


def _prompt_layer(x, lw):
    (nm_g, w_in, q_g, k_g, sinks, cw, cb, lg, lb, w_pw2, ga, gc, w_out, nf_g, wq, k1, k2, wu, wv) = lw
    b, t, _ = x.shape
    pos = jnp.arange(t, dtype=jnp.int32)
    q, k, v, u = _project(_rmsnorm(x, nm_g), w_in, q_g, k_g, pos)
    tp = t + META_PAD
    nb = tp // BLOCK

    def to_blocks(a):
        return jnp.pad(a, ((0, 0), (META_PAD, 0), (0, 0), (0, 0))).reshape(b, nb, BLOCK, a.shape[2], a.shape[3])

    def with_prev(a):
        prev = jnp.concatenate([jnp.zeros_like(a[:, :1]), a[:, :-1]], axis=1)
        return jnp.concatenate([prev, a], axis=2)

    qpos = (jnp.arange(tp, dtype=jnp.int32) - META_PAD).reshape(nb, BLOCK)
    kpos = jnp.concatenate([qpos - BLOCK, qpos], axis=1)
    o_attn = _window_sink_attention(to_blocks(q), with_prev(to_blocks(k)), with_prev(to_blocks(v)),
                                    qpos, kpos, sinks).reshape(b, tp, D_ATTN)[:, META_PAD:]
    u_hist = jnp.pad(u, ((0, 0), (CONV_WIDTH - 1, 0), (0, 0)))
    o_conv = _conv_tail(u_hist, cw, cb, lg, lb, w_pw2)
    h = x + _merge(o_attn, o_conv, ga, gc, w_out)
    y = h + _peer(_rmsnorm(h, nf_g), wq, k1, k2, wu, wv)
    wp = min(WINDOW, t)
    return y, k[:, t - wp:], v[:, t - wp:], u[:, t - (CONV_WIDTH - 1):]


def _sample_layer(x, ck, cv, cs, lw):
    (nm_g, w_in, q_g, k_g, sinks, cw, cb, lg, lb, w_pw2, ga, gc, w_out, nf_g, wq, k1, k2, wu, wv) = lw
    b, t, _ = x.shape
    w_buf = ck.shape[1]
    pos = PAST_LEN + jnp.arange(t, dtype=jnp.int32)
    q, k, v, u = _project(_rmsnorm(x, nm_g), w_in, q_g, k_g, pos)
    k_all = jnp.concatenate([ck, k], axis=1)
    v_all = jnp.concatenate([cv, v], axis=1)
    kpos = jnp.concatenate([PAST_LEN - w_buf + jnp.arange(w_buf, dtype=jnp.int32), pos])
    o_attn = _window_sink_attention(q[:, None], k_all[:, None], v_all[:, None],
                                    pos[None], kpos[None], sinks)[:, 0]
    u_hist = jnp.concatenate([cs, u], axis=1)
    o_conv = _conv_tail(u_hist, cw, cb, lg, lb, w_pw2)
    h = x + _merge(o_attn, o_conv, ga, gc, w_out)
    y = h + _peer(_rmsnorm(h, nf_g), wq, k1, k2, wu, wv)
    return y, k_all[:, -w_buf:], v_all[:, -w_buf:], u_hist[:, -(CONV_WIDTH - 1):]


def setup_inputs(seed: int = 0) -> dict:
    key = jax.random.key(seed)
    ks = jax.random.split(key, 26)
    f32 = jnp.float32
    w_buf = min(WINDOW, PAST_LEN)

    def nrm(k, shape, scale):
        return jax.random.normal(k, shape, f32) * scale

    def gain(k, shape):
        return 1.0 + 0.02 * jax.random.normal(k, shape, f32)

    return {
        'x_prompt': nrm(ks[0], (BATCH, SEQ, D_MODEL), 1.0),
        'x_sample': nrm(ks[1], (DEC_BATCH, DEC_SEQ, D_MODEL), 1.0),
        'cache_k_win': nrm(ks[2], (DEPTH, DEC_BATCH, w_buf, N_KV_HEADS, HEAD_DIM), 1.0),
        'cache_v_win': nrm(ks[3], (DEPTH, DEC_BATCH, w_buf, N_KV_HEADS, HEAD_DIM), 1.0),
        'state_conv': nrm(ks[4], (DEPTH, DEC_BATCH, CONV_WIDTH - 1, D_CONV), 0.5),
        'meta_tokens': nrm(ks[5], (N_META, D_MODEL), 1.0),
        'norm_mix_g': gain(ks[6], (DEPTH, D_MODEL)),
        'w_in': nrm(ks[7], (DEPTH, D_MODEL, D_IN), D_MODEL ** -0.5),
        'q_norm_g': gain(ks[8], (DEPTH, HEAD_DIM)),
        'k_norm_g': gain(ks[9], (DEPTH, HEAD_DIM)),
        'attn_sinks': nrm(ks[10], (DEPTH, N_HEADS), 0.5),
        'conv_w': nrm(ks[11], (DEPTH, CONV_WIDTH, D_CONV), CONV_WIDTH ** -0.5),
        'conv_b': nrm(ks[12], (DEPTH, D_CONV), 0.02),
        'conv_ln_g': gain(ks[13], (DEPTH, D_CONV)),
        'conv_ln_b': nrm(ks[14], (DEPTH, D_CONV), 0.02),
        'w_pw2': nrm(ks[15], (DEPTH, D_CONV, D_CONV), D_CONV ** -0.5),
        'out_norm_attn_g': gain(ks[16], (DEPTH, D_ATTN)),
        'out_norm_conv_g': gain(ks[17], (DEPTH, D_CONV)),
        'w_out': nrm(ks[18], (DEPTH, D_MODEL, D_MODEL), D_MODEL ** -0.5),
        'norm_ffn_g': gain(ks[19], (DEPTH, D_MODEL)),
        'peer_w_q': nrm(ks[20], (DEPTH, D_MODEL, PEER_HEADS * PEER_DQ), D_MODEL ** -0.5),
        'peer_keys1': nrm(ks[21], (DEPTH, PEER_NKEYS, PEER_DQ // 2), (PEER_DQ // 2) ** -0.5),
        'peer_keys2': nrm(ks[22], (DEPTH, PEER_NKEYS, PEER_DQ // 2), (PEER_DQ // 2) ** -0.5),
        'peer_u': nrm(ks[23], (DEPTH, PEER_EXPERTS, D_MODEL), D_MODEL ** -0.5),
        'peer_v': nrm(ks[24], (DEPTH, PEER_EXPERTS, D_MODEL), PEER_HEADS ** -0.5),
    }


def reference(x_prompt, x_sample, cache_k_win, cache_v_win, state_conv, meta_tokens,
              norm_mix_g, w_in, q_norm_g, k_norm_g, attn_sinks, conv_w, conv_b, conv_ln_g,
              conv_ln_b, w_pw2, out_norm_attn_g, out_norm_conv_g, w_out, norm_ffn_g,
              peer_w_q, peer_keys1, peer_keys2, peer_u, peer_v):
    b = x_prompt.shape[0]
    meta = jnp.broadcast_to(meta_tokens[None].astype(x_prompt.dtype), (b, N_META, D_MODEL))
    xp = jnp.concatenate([meta, x_prompt], axis=1)
    xs = x_sample
    kp_l, vp_l, cp_l, ks_l, vs_l, cs_l = [], [], [], [], [], []
    for l in range(DEPTH):
        lw = (norm_mix_g[l], w_in[l], q_norm_g[l], k_norm_g[l], attn_sinks[l], conv_w[l], conv_b[l],
              conv_ln_g[l], conv_ln_b[l], w_pw2[l], out_norm_attn_g[l], out_norm_conv_g[l], w_out[l],
              norm_ffn_g[l], peer_w_q[l], peer_keys1[l], peer_keys2[l], peer_u[l], peer_v[l])
        xp, kp, vp, cp = _prompt_layer(xp, lw)
        xs, ksn, vsn, csn = _sample_layer(xs, cache_k_win[l], cache_v_win[l], state_conv[l], lw)
        kp_l.append(kp); vp_l.append(vp); cp_l.append(cp)
        ks_l.append(ksn); vs_l.append(vsn); cs_l.append(csn)
    y_prompt = xp[:, N_META:]
    y_sample = xs
    new_k_prompt = jnp.stack(kp_l)
    new_v_prompt = jnp.stack(vp_l)
    new_conv_prompt = jnp.stack(cp_l)
    new_k_sample = jnp.stack(ks_l)
    new_v_sample = jnp.stack(vs_l)
    new_conv_sample = jnp.stack(cs_l)
    return (y_prompt, y_sample, new_k_prompt, new_v_prompt, new_conv_prompt, new_k_sample, new_v_sample, new_conv_sample)
```

```python
import functools

import jax
import jax.numpy as jnp
from jax import lax
from jax.experimental import pallas as pl
from jax.experimental.pallas import tpu as pltpu
from jax.experimental.pallas import tpu_sc as plsc

D_MODEL = 1024
HEAD_DIM = 64
D_ATTN = 512
N_HEADS = 8
N_KV_HEADS = 2
KV_DIM = N_KV_HEADS * HEAD_DIM
D_CONV = 512
D_IN = D_ATTN + 2 * KV_DIM + 2 * D_CONV
CONV_WIDTH = 31
WINDOW = 128
BLOCK = 128
ROPE_THETA = 10000.0
N_META = 16
META_PAD = BLOCK - N_META
PEER_HEADS = 8
PEER_NKEYS = 128
PEER_TOPK = 16
PEER_SEL = PEER_HEADS * PEER_TOPK
EPS = 1e-6
PAST_LEN = 16384

LANES = 128
SC_CORES = 2
SC_SUBCORES = 16
SC_LANES = 16
SC_WORKERS = SC_CORES * SC_SUBCORES
VMEM_LIMIT = 48 * 1024 * 1024

F32 = jnp.float32
BF16 = jnp.bfloat16
NEG_INF = float("-inf")


def _tc_params(n_axes):
    return pltpu.CompilerParams(dimension_semantics=("arbitrary",) * n_axes,
                                vmem_limit_bytes=VMEM_LIMIT)


def _full(shape):
    nd = len(shape)
    return pl.BlockSpec(shape, lambda *_: (0,) * nd)


def _group_mean(sq, gsum_ref):
    hi = sq.astype(BF16)
    lo = (sq - hi.astype(F32)).astype(BF16)
    g = gsum_ref[...]
    s = jnp.dot(hi, g, preferred_element_type=F32) + jnp.dot(lo, g, preferred_element_type=F32)
    return s * (1.0 / HEAD_DIM)


def _rope(xn, cos_f, sin_s, first_half):
    outs = []
    for s in range(xn.shape[1] // LANES):
        xs = xn[:, s * LANES:(s + 1) * LANES]
        partner = jnp.where(first_half, pltpu.roll(xs, LANES - HEAD_DIM // 2, axis=1),
                            pltpu.roll(xs, HEAD_DIM // 2, axis=1))
        outs.append(xs * cos_f + partner * sin_s)
    return outs[0] if len(outs) == 1 else jnp.concatenate(outs, axis=1)


def _proj_kernel(x_ref, g_ref, w_ref, qg_ref, kg_ref, cos_ref, sin_ref, gq_ref, gk_ref,
                 q_out, k_out, v_out, u_out):
    x = x_ref[...]
    n = x * lax.rsqrt(jnp.mean(x * x, axis=-1, keepdims=True) + EPS) * g_ref[...]
    p = jnp.dot(n.astype(BF16), w_ref[...], preferred_element_type=F32)
    q = p[:, :D_ATTN]
    k = p[:, D_ATTN:D_ATTN + KV_DIM]
    v = p[:, D_ATTN + KV_DIM:D_ATTN + 2 * KV_DIM]
    ga = p[:, D_ATTN + 2 * KV_DIM:D_ATTN + 2 * KV_DIM + D_CONV]
    gb = p[:, D_ATTN + 2 * KV_DIM + D_CONV:]
    cos_f = cos_ref[...]
    sin_s = sin_ref[...]
    lane = lax.broadcasted_iota(jnp.int32, (x.shape[0], LANES), 1)
    first_half = (lane % HEAD_DIM) < (HEAD_DIM // 2)
    qn = q * lax.rsqrt(_group_mean(q * q, gq_ref) + EPS) * qg_ref[...]
    kn = k * lax.rsqrt(_group_mean(k * k, gk_ref) + EPS) * kg_ref[...]
    q_out[...] = _rope(qn, cos_f, sin_s, first_half)
    k_out[...] = _rope(kn, cos_f, sin_s, first_half)
    v_out[...] = v
    u_out[...] = ga * (1.0 / (1.0 + jnp.exp(-gb)))


def _project(x, pos_tables, n_table_blocks, tb, norm_g, w_in_bf, qg_t, kg_t, gsum_q, gsum_k):
    n = x.shape[0]
    cos_t, sin_t = pos_tables
    nb = n // tb
    tab_spec = pl.BlockSpec((tb, LANES), lambda i: (i % n_table_blocks, 0))
    row = lambda w: pl.BlockSpec((tb, w), lambda i: (i, 0))
    return pl.pallas_call(
        _proj_kernel,
        grid=(nb,),
        in_specs=[row(D_MODEL), _full((1, D_MODEL)), _full((D_MODEL, D_IN)),
                  _full((1, D_ATTN)), _full((1, KV_DIM)), tab_spec, tab_spec,
                  _full((D_ATTN, D_ATTN)), _full((KV_DIM, KV_DIM))],
        out_specs=[row(D_ATTN), row(KV_DIM), row(KV_DIM), row(D_CONV)],
        out_shape=[jax.ShapeDtypeStruct((n, D_ATTN), F32), jax.ShapeDtypeStruct((n, KV_DIM), F32),
                   jax.ShapeDtypeStruct((n, KV_DIM), F32), jax.ShapeDtypeStruct((n, D_CONV), F32)],
        compiler_params=_tc_params(1),
        name="proj",
    )(x, norm_g, w_in_bf, qg_t, kg_t, cos_t, sin_t, gsum_q, gsum_k)


def _attn_kernel(sink_ref, q_ref, kp_ref, kc_ref, vp_ref, vc_ref, o_ref):
    j = pl.program_id(1)
    r = lax.broadcasted_iota(jnp.int32, (BLOCK, 2 * BLOCK), 0)
    c = lax.broadcasted_iota(jnp.int32, (BLOCK, 2 * BLOCK), 1)
    ok = (c > r) & (c <= r + WINDOW) & ((j > 0) | (c >= META_PAD))
    grp = N_HEADS // N_KV_HEADS
    ok = jnp.concatenate([ok] * grp, axis=0)
    for g in range(N_KV_HEADS):
        q4 = q_ref[0, g * grp:(g + 1) * grp].reshape(grp * BLOCK, HEAD_DIM).astype(BF16)
        k = jnp.concatenate([kp_ref[0, g], kc_ref[0, g]], axis=0).astype(BF16)
        v = jnp.concatenate([vp_ref[0, g], vc_ref[0, g]], axis=0).astype(BF16)
        s = lax.dot_general(q4, k, (((1,), (1,)), ((), ())), preferred_element_type=F32)
        s = jnp.where(ok, s * (HEAD_DIM ** -0.5), NEG_INF)
        sink = jnp.concatenate(
            [jnp.full((BLOCK, 1), sink_ref[g * grp + i], F32) for i in range(grp)], axis=0)
        m = jnp.maximum(jnp.max(s, axis=1, keepdims=True), sink)
        p = jnp.exp(s - m)
        den = jnp.sum(p, axis=1, keepdims=True) + jnp.exp(sink - m)
        o = jnp.dot(p.astype(BF16), v, preferred_element_type=F32) / den
        o_ref[0, g * grp:(g + 1) * grp] = o.reshape(grp, BLOCK, HEAD_DIM)


def _prompt_attention(sinks, q_t, k_t, v_t):
    b, _, t, _ = q_t.shape
    nb = t // BLOCK
    kv_prev = pl.BlockSpec((1, N_KV_HEADS, BLOCK, HEAD_DIM), lambda bi, j: (bi, 0, j, 0))
    kv_cur = pl.BlockSpec((1, N_KV_HEADS, BLOCK, HEAD_DIM), lambda bi, j: (bi, 0, j + 1, 0))
    q_spec = pl.BlockSpec((1, N_HEADS, BLOCK, HEAD_DIM), lambda bi, j: (bi, 0, j, 0))
    return pl.pallas_call(
        _attn_kernel,
        grid=(b, nb),
        in_specs=[pl.BlockSpec(memory_space=pltpu.SMEM), q_spec, kv_prev, kv_cur, kv_prev, kv_cur],
        out_specs=q_spec,
        out_shape=jax.ShapeDtypeStruct(q_t.shape, F32),
        compiler_params=_tc_params(2),
        name="attn",
    )(sinks, q_t, k_t, k_t, v_t, v_t)


def _dec_attn_kernel(sink_ref, q_ref, kn_ref, vn_ref, ck_ref, cv_ref, o_ref):
    grp = N_HEADS // N_KV_HEADS
    sb, w_buf = ck_ref.shape[1], ck_ref.shape[2]
    q = q_ref[...]
    qb = q.astype(BF16)
    head = lax.broadcasted_iota(jnp.int32, (sb, N_HEADS, 1), 1)
    in_g0 = head < grp
    s_g = [jnp.einsum("shd,swd->shw", qb, ck_ref[g].astype(BF16), preferred_element_type=F32)
           for g in range(N_KV_HEADS)]
    s = jnp.where(in_g0, s_g[0], s_g[1]) * (HEAD_DIM ** -0.5)
    key_ok = lax.broadcasted_iota(jnp.int32, (sb, N_HEADS, w_buf), 2) >= 1
    s = jnp.where(key_ok, s, NEG_INF)
    rnd = lambda a: a.astype(BF16).astype(F32)
    kn = jnp.where(in_g0, kn_ref[0][:, None, :], kn_ref[1][:, None, :])
    vn = jnp.where(in_g0, vn_ref[0][:, None, :], vn_ref[1][:, None, :])
    s_self = jnp.sum(rnd(q) * rnd(kn), axis=-1, keepdims=True) * (HEAD_DIM ** -0.5)
    sink = sink_ref[...]
    m = jnp.maximum(jnp.maximum(jnp.max(s, axis=-1, keepdims=True), s_self), sink)
    p = jnp.exp(s - m)
    p_self = jnp.exp(s_self - m)
    den = jnp.sum(p, axis=-1, keepdims=True) + p_self + jnp.exp(sink - m)
    pb = p.astype(BF16)
    o_g = [jnp.einsum("shw,swd->shd", pb, cv_ref[g].astype(BF16), preferred_element_type=F32)
           for g in range(N_KV_HEADS)]
    o = jnp.where(in_g0, o_g[0], o_g[1]) + rnd(p_self) * rnd(vn)
    o_ref[...] = o / den


def _decode_attention(sinks, q3, kn_t, vn_t, ck_t, cv_t, sb):
    s = q3.shape[0]
    w_buf = ck_t.shape[2]
    qs = pl.BlockSpec((sb, N_HEADS, HEAD_DIM), lambda i: (i, 0, 0))
    ns = pl.BlockSpec((N_KV_HEADS, sb, HEAD_DIM), lambda i: (0, i, 0))
    cs = pl.BlockSpec((N_KV_HEADS, sb, w_buf, HEAD_DIM), lambda i: (0, i, 0, 0))
    return pl.pallas_call(
        _dec_attn_kernel,
        grid=(s // sb,),
        in_specs=[_full((1, N_HEADS, 1)), qs, ns, ns, cs, cs],
        out_specs=qs,
        out_shape=jax.ShapeDtypeStruct(q3.shape, F32),
        compiler_params=_tc_params(1),
        name="dec_attn",
    )(sinks.reshape(1, N_HEADS, 1), q3, kn_t, vn_t, ck_t, cv_t)


CONV_HALO = 32
CONV_ROWS = 64


def _conv_kernel(halo_ref, u_ref, cw_ref, cb_ref, lg_ref, lb_ref, w2_ref, o_ref, ucat):
    tb = u_ref.shape[0]
    ucat[0:CONV_HALO, :] = halo_ref[0]
    ucat[CONV_HALO:, :] = u_ref[...]
    first = CONV_HALO - (CONV_WIDTH - 1)
    rows = min(CONV_ROWS, tb)
    for r0 in range(0, tb, rows):
        acc = jnp.zeros((rows, D_CONV), F32)
        for j in range(CONV_WIDTH):
            acc = acc + ucat[r0 + first + j:r0 + first + j + rows, :] * cw_ref[j:j + 1, :]
        y = acc + cb_ref[...]
        yc = y - jnp.mean(y, axis=-1, keepdims=True)
        yn = yc * lax.rsqrt(jnp.mean(yc * yc, axis=-1, keepdims=True) + EPS)
        yn = yn * lg_ref[...] + lb_ref[...]
        act = yn * (1.0 / (1.0 + jnp.exp(-yn)))
        o_ref[r0:r0 + rows, :] = jnp.dot(act.astype(BF16), w2_ref[...],
                                              preferred_element_type=F32)


def _conv_branch(halo, u, tb, conv_w, conv_b, ln_g, ln_b, w_pw2_bf):
    n = u.shape[0]
    return pl.pallas_call(
        _conv_kernel,
        grid=(n // tb,),
        in_specs=[pl.BlockSpec((1, CONV_HALO, D_CONV), lambda i: (i, 0, 0)),
                  pl.BlockSpec((tb, D_CONV), lambda i: (i, 0)),
                  _full((CONV_WIDTH, D_CONV)), _full((1, D_CONV)), _full((1, D_CONV)),
                  _full((1, D_CONV)), _full((D_CONV, D_CONV))],
        out_specs=pl.BlockSpec((tb, D_CONV), lambda i: (i, 0)),
        out_shape=jax.ShapeDtypeStruct((n, D_CONV), F32),
        scratch_shapes=[pltpu.VMEM((tb + CONV_HALO, D_CONV), F32)],
        compiler_params=_tc_params(1),
        name="conv",
    )(halo, u, conv_w, conv_b, ln_g, ln_b, w_pw2_bf)


def _rms(x, g):
    return x * lax.rsqrt(jnp.mean(x * x, axis=-1, keepdims=True) + EPS) * g


def _merge_kernel(x_ref, oa_ref, oc_ref, ga_ref, gc_ref, wa_ref, wc_ref, gf_ref, h_out, hn_out):
    a = _rms(oa_ref[...], ga_ref[...]).astype(BF16)
    c = _rms(oc_ref[...], gc_ref[...]).astype(BF16)
    h = x_ref[...] + (jnp.dot(a, wa_ref[...], preferred_element_type=F32)
                      + jnp.dot(c, wc_ref[...], preferred_element_type=F32))
    h_out[...] = h
    hn_out[...] = _rms(h, gf_ref[...])


def _merge(x, oa, oc, tb, g_a, g_c, w_out_a, w_out_c, g_f):
    n = x.shape[0]
    row = lambda w: pl.BlockSpec((tb, w), lambda i: (i, 0))
    return pl.pallas_call(
        _merge_kernel,
        grid=(n // tb,),
        in_specs=[row(D_MODEL), row(D_ATTN), row(D_CONV), _full((1, D_ATTN)), _full((1, D_CONV)),
                  _full((D_ATTN, D_MODEL)), _full((D_CONV, D_MODEL)), _full((1, D_MODEL))],
        out_specs=[row(D_MODEL), row(D_MODEL)],
        out_shape=[jax.ShapeDtypeStruct((n, D_MODEL), F32)] * 2,
        compiler_params=_tc_params(1),
        name="merge",
    )(x, oa, oc, g_a, g_c, w_out_a, w_out_c, g_f)


ID_BIG = 1e9


def _topk_rows(s, k):
    rows = lax.broadcasted_iota(jnp.int32, s.shape, 0).astype(F32)
    vals, idxs = [], []
    for _ in range(k):
        m = jnp.max(s, axis=0, keepdims=True)
        idx = jnp.min(jnp.where(s == m, rows, ID_BIG), axis=0, keepdims=True)
        vals.append(m)
        idxs.append(idx)
        s = jnp.where(rows == idx, NEG_INF, s)
    return jnp.concatenate(vals, axis=0), jnp.concatenate(idxs, axis=0)


PAIR_B_WIDE = 8


def _route_kernel(hn_ref, wq_ref, k1_ref, k2_ref, eid_out, gate_out):
    tb = hn_ref.shape[0]
    q = jnp.dot(hn_ref[...].astype(BF16), wq_ref[...], preferred_element_type=F32).astype(BF16)
    k1 = k1_ref[...]
    k2 = k2_ref[...]
    nt = (((1,), (1,)), ((), ()))
    r = lax.broadcasted_iota(jnp.int32, (PEER_TOPK + (PAIR_B_WIDE - 1) * PAIR_B_WIDE + PAIR_B_WIDE, tb), 0)
    mid = r - PEER_TOPK
    flat = jnp.where(r < PEER_TOPK, r,
                     jnp.where(mid < (PAIR_B_WIDE - 1) * PAIR_B_WIDE,
                               (1 + mid // PAIR_B_WIDE) * PEER_TOPK + mid % PAIR_B_WIDE,
                               (PAIR_B_WIDE + mid - (PAIR_B_WIDE - 1) * PAIR_B_WIDE) * PEER_TOPK)).astype(F32)
    half = PEER_NKEYS
    for h in range(PEER_HEADS):
        q1 = q[:, (2 * h) * half:(2 * h + 1) * half]
        q2 = q[:, (2 * h + 1) * half:(2 * h + 2) * half]
        s1 = lax.dot_general(k1, q1, nt, preferred_element_type=F32)
        s2 = lax.dot_general(k2, q2, nt, preferred_element_type=F32)
        v1, i1 = _topk_rows(s1, PEER_TOPK)
        v2, i2 = _topk_rows(s2, PEER_TOPK)
        e1 = i1 * PEER_NKEYS
        cand = jnp.concatenate(
            [v1[0:1] + v2]
            + [v1[a:a + 1] + v2[0:PAIR_B_WIDE] for a in range(1, PAIR_B_WIDE)]
            + [v1[PAIR_B_WIDE:] + v2[0:1]], axis=0)
        cid = jnp.concatenate(
            [e1[0:1] + i2]
            + [e1[a:a + 1] + i2[0:PAIR_B_WIDE] for a in range(1, PAIR_B_WIDE)]
            + [e1[PAIR_B_WIDE:] + i2[0:1]], axis=0)
        scs, eids = [], []
        for _ in range(PEER_TOPK):
            m = jnp.max(cand, axis=0, keepdims=True)
            jsel = jnp.min(jnp.where(cand == m, flat, ID_BIG), axis=0, keepdims=True)
            hit = flat == jsel
            eids.append(jnp.max(jnp.where(hit, cid, -1.0), axis=0, keepdims=True))
            scs.append(m)
            cand = jnp.where(hit, NEG_INF, cand)
        sc = jnp.concatenate(scs, axis=0)
        e = jnp.exp(sc - sc[0:1])
        gate_out[h * PEER_TOPK:(h + 1) * PEER_TOPK, :] = e / jnp.sum(e, axis=0, keepdims=True)
        eid_out[h * PEER_TOPK:(h + 1) * PEER_TOPK, :] = jnp.concatenate(eids, axis=0).astype(jnp.int32)


def _route(hn, tb, wq_bf, keys1_bf, keys2_bf):
    n = hn.shape[0]
    col = pl.BlockSpec((PEER_SEL, tb), lambda i: (0, i))
    return pl.pallas_call(
        _route_kernel,
        grid=(n // tb,),
        in_specs=[pl.BlockSpec((tb, D_MODEL), lambda i: (i, 0)),
                  _full((D_MODEL, 2 * PEER_NKEYS * PEER_HEADS)),
                  _full((PEER_NKEYS, PEER_NKEYS)), _full((PEER_NKEYS, PEER_NKEYS))],
        out_specs=[col, col],
        out_shape=[jax.ShapeDtypeStruct((PEER_SEL, n), jnp.int32),
                   jax.ShapeDtypeStruct((PEER_SEL, n), F32)],
        compiler_params=_tc_params(1),
        name="route",
    )(hn, wq_bf, keys1_bf, keys2_bf)


SC_ROWS = 32
SC_GROUP = SC_LANES

_SC_PARAMS = pltpu.CompilerParams(needs_layout_passes=False)


def _sc_mesh():
    return plsc.VectorSubcoreMesh(core_axis_name="c", subcore_axis_name="s")


def _sc_worker_id():
    return lax.axis_index("s") * SC_CORES + lax.axis_index("c")


def _peer_hval(hn, eid, peer_u):
    n = hn.shape[0]
    tpw = n // SC_WORKERS

    @functools.partial(
        pl.kernel, mesh=_sc_mesh(),
        out_type=jax.ShapeDtypeStruct((n, PEER_SEL), F32),
        scratch_types=[
            pltpu.VMEM((PEER_SEL,), jnp.int32),
            pltpu.VMEM((D_MODEL,), F32),
            pltpu.VMEM((SC_ROWS, D_MODEL), F32),
            pltpu.VMEM((PEER_SEL,), F32),
            pltpu.VMEM((SC_GROUP, SC_LANES), F32),
            pltpu.SemaphoreType.DMA,
        ],
        compiler_params=_SC_PARAMS, name="peer_hval")
    def k(x_hbm, eid_hbm, u_hbm, o_hbm, idx_v, x_v, rbuf, h_v, tr, sem):
        wid = _sc_worker_id()
        lane = lax.iota(jnp.int32, SC_LANES)
        zero = jnp.zeros((SC_LANES,), F32)

        @pl.loop(0, tpw)
        def _(i):
            t = wid * tpw + i
            pltpu.sync_copy(eid_hbm.at[t], idx_v)
            pltpu.sync_copy(x_hbm.at[t], x_v)
            for ch in range(PEER_SEL // SC_ROWS):
                pltpu.async_copy(u_hbm.at[idx_v.at[pl.ds(ch * SC_ROWS, SC_ROWS)]], rbuf, sem).wait()
                for g in range(SC_ROWS // SC_GROUP):
                    def body(c, accs):
                        xc = x_v[pl.ds(c * SC_LANES, SC_LANES)]
                        return tuple(
                            accs[r] + rbuf[g * SC_GROUP + r, pl.ds(c * SC_LANES, SC_LANES)] * xc
                            for r in range(SC_GROUP))
                    accs = lax.fori_loop(0, D_MODEL // SC_LANES, body, (zero,) * SC_GROUP)
                    for r in range(SC_GROUP):
                        tr[r, :] = accs[r]
                    res = zero
                    for jj in range(SC_LANES):
                        res = res + plsc.load_gather(tr, [lane, jnp.full((SC_LANES,), jj, jnp.int32)])
                    h_v[pl.ds(ch * SC_ROWS + g * SC_GROUP, SC_GROUP)] = res
            pltpu.sync_copy(h_v, o_hbm.at[t])

    return k(hn, eid, peer_u)


def _peer_out(h, a, eid, peer_v):
    n = h.shape[0]
    tpw = n // SC_WORKERS

    @functools.partial(
        pl.kernel, mesh=_sc_mesh(),
        out_type=jax.ShapeDtypeStruct((n, D_MODEL), F32),
        scratch_types=[
            pltpu.VMEM((PEER_SEL,), jnp.int32),
            pltpu.VMEM((PEER_SEL,), F32),
            pltpu.VMEM((SC_ROWS, D_MODEL), F32),
            pltpu.VMEM((D_MODEL,), F32),
            pltpu.SemaphoreType.DMA,
        ],
        compiler_params=_SC_PARAMS, name="peer_out")
    def k(h_hbm, a_hbm, eid_hbm, v_hbm, o_hbm, idx_v, a_v, rbuf, y_v, sem):
        wid = _sc_worker_id()

        @pl.loop(0, tpw)
        def _(i):
            t = wid * tpw + i
            pltpu.sync_copy(eid_hbm.at[t], idx_v)
            pltpu.sync_copy(a_hbm.at[t], a_v)
            pltpu.sync_copy(h_hbm.at[t], y_v)
            for ch in range(PEER_SEL // SC_ROWS):
                pltpu.async_copy(v_hbm.at[idx_v.at[pl.ds(ch * SC_ROWS, SC_ROWS)]], rbuf, sem).wait()
                for g in range(SC_ROWS // SC_GROUP):
                    base = ch * SC_ROWS + g * SC_GROUP
                    coef = [plsc.load_gather(a_v, [jnp.full((SC_LANES,), base + r, jnp.int32)])
                            for r in range(SC_GROUP)]

                    @pl.loop(0, D_MODEL // SC_LANES)
                    def _(c):
                        sl = pl.ds(c * SC_LANES, SC_LANES)
                        acc = coef[0] * rbuf[g * SC_GROUP, sl]
                        for r in range(1, SC_GROUP):
                            acc = acc + coef[r] * rbuf[g * SC_GROUP + r, sl]
                        y_v[sl] = y_v[sl] + acc
            pltpu.sync_copy(y_v, o_hbm.at[t])

    return k(h, a, eid, peer_v)


def _gate_kernel(hv_ref, gate_ref, a_out):
    hv = hv_ref[...]
    gelu = hv * (lax.erf(hv * (2.0 ** -0.5)) + 1.0) * 0.5
    a_out[...] = gate_ref[...] * gelu


def _gate(hval, gate, tb):
    n = hval.shape[0]
    row = pl.BlockSpec((tb, PEER_SEL), lambda i: (i, 0))
    return pl.pallas_call(
        _gate_kernel, grid=(n // tb,), in_specs=[row, row], out_specs=row,
        out_shape=jax.ShapeDtypeStruct((n, PEER_SEL), F32),
        compiler_params=_tc_params(1), name="gate",
    )(hval, gate)


def _peer(h, hn, tb, wq_bf, keys1_bf, keys2_bf, peer_u, peer_v):
    eid_t, gate_t = _route(hn, tb, wq_bf, keys1_bf, keys2_bf)
    eid = eid_t.T
    hval = _peer_hval(hn, eid, peer_u)
    a = _gate(hval, gate_t.T, min(tb, 512))
    return _peer_out(h, a, eid, peer_v)


def _rope_tables(pos):
    half = HEAD_DIM // 2
    inv = ROPE_THETA ** (-jnp.arange(half, dtype=F32) / half)
    ang = pos.astype(F32)[:, None] * inv[None, :]
    cos = jnp.cos(ang)
    sin = jnp.sin(ang)
    reps = LANES // HEAD_DIM
    cos_f = jnp.tile(jnp.concatenate([cos, cos], axis=1), (1, reps))
    sin_s = jnp.tile(jnp.concatenate([-sin, sin], axis=1), (1, reps))
    return cos_f, sin_s


def _group_sum_matrix(width):
    g = jnp.arange(width) // HEAD_DIM
    return (g[:, None] == g[None, :]).astype(BF16)


def _heads_first(a, heads):
    b, t, _ = a.shape
    return a.reshape(b, t, heads, HEAD_DIM).transpose(0, 2, 1, 3)


def kernel(x_prompt, x_sample, cache_k_win, cache_v_win, state_conv, meta_tokens, norm_mix_g,
           w_in, q_norm_g, k_norm_g, attn_sinks, conv_w, conv_b, conv_ln_g, conv_ln_b, w_pw2,
           out_norm_attn_g, out_norm_conv_g, w_out, norm_ffn_g, peer_w_q, peer_keys1, peer_keys2,
           peer_u, peer_v):
    assert norm_mix_g.shape[0] == 1, "single-layer model"
    b, t, _ = x_prompt.shape
    s = x_sample.shape[0]
    w_buf = cache_k_win.shape[2]
    n = b * t

    g_mix = norm_mix_g[0][None, :]
    w_in_bf = w_in[0].astype(BF16)
    qg_t = jnp.tile(q_norm_g[0], N_HEADS)[None, :]
    kg_t = jnp.tile(k_norm_g[0], N_KV_HEADS)[None, :]
    gsum_q = _group_sum_matrix(D_ATTN)
    gsum_k = _group_sum_matrix(KV_DIM)
    sinks = attn_sinks[0]
    cw, cb = conv_w[0], conv_b[0][None, :]
    lg, lb = conv_ln_g[0][None, :], conv_ln_b[0][None, :]
    w2_bf = w_pw2[0].astype(BF16)
    g_a, g_c = out_norm_attn_g[0][None, :], out_norm_conv_g[0][None, :]
    w_out_a = w_out[0][:D_ATTN].astype(BF16)
    w_out_c = w_out[0][D_ATTN:].astype(BF16)
    g_f = norm_ffn_g[0][None, :]
    wq_bf = peer_w_q[0].astype(BF16)
    k1_bf = peer_keys1[0].astype(BF16)
    k2_bf = peer_keys2[0].astype(BF16)
    pu, pv = peer_u[0], peer_v[0]
    proj = functools.partial(_project, norm_g=g_mix, w_in_bf=w_in_bf, qg_t=qg_t, kg_t=kg_t,
                             gsum_q=gsum_q, gsum_k=gsum_k)
    conv = functools.partial(_conv_branch, conv_w=cw, conv_b=cb, ln_g=lg, ln_b=lb, w_pw2_bf=w2_bf)
    merge = functools.partial(_merge, g_a=g_a, g_c=g_c, w_out_a=w_out_a, w_out_c=w_out_c, g_f=g_f)
    peer = functools.partial(_peer, wq_bf=wq_bf, keys1_bf=k1_bf, keys2_bf=k2_bf, peer_u=pu, peer_v=pv)

    tb = 512
    xp = x_prompt.reshape(n, D_MODEL)
    tab_x = _rope_tables(N_META + jnp.arange(t, dtype=jnp.int32))
    q, k, v, u = proj(xp, tab_x, t // tb, tb)
    tab_m = _rope_tables(jnp.arange(N_META, dtype=jnp.int32))
    _, k_m, v_m, u_m = proj(meta_tokens, tab_m, 1, N_META)

    def with_meta_block(a, a_m):
        blk0 = jnp.concatenate([jnp.zeros((META_PAD, KV_DIM), F32), a_m], axis=0)
        full = jnp.concatenate([jnp.broadcast_to(blk0[None], (b, BLOCK, KV_DIM)),
                                a.reshape(b, t, KV_DIM)], axis=1)
        return _heads_first(full, N_KV_HEADS)

    o_attn = _prompt_attention(sinks, _heads_first(q.reshape(b, t, D_ATTN), N_HEADS),
                               with_meta_block(k, k_m), with_meta_block(v, v_m))
    o_attn = o_attn.transpose(0, 2, 1, 3).reshape(n, D_ATTN)

    u3 = u.reshape(b, t // tb, tb, D_CONV)
    halo0 = jnp.concatenate([jnp.zeros((CONV_HALO - N_META, D_CONV), F32), u_m], axis=0)
    halo = jnp.concatenate([jnp.broadcast_to(halo0[None, None], (b, 1, CONV_HALO, D_CONV)),
                            u3[:, :-1, tb - CONV_HALO:]], axis=1).reshape(n // tb, CONV_HALO, D_CONV)
    o_conv = conv(halo, u, tb)

    h, hn = merge(xp, o_attn, o_conv, tb)
    y_prompt = peer(h, hn, 256).reshape(b, t, D_MODEL)

    new_k_prompt = k.reshape(b, t, N_KV_HEADS, HEAD_DIM)[None, :, t - WINDOW:]
    new_v_prompt = v.reshape(b, t, N_KV_HEADS, HEAD_DIM)[None, :, t - WINDOW:]
    new_conv_prompt = u.reshape(b, t, D_CONV)[None, :, t - (CONV_WIDTH - 1):]

    xs = x_sample.reshape(s, D_MODEL)
    tab_s = _rope_tables(jnp.full((s,), PAST_LEN, jnp.int32))
    qs, ks, vs, us = proj(xs, tab_s, 1, s)
    ck = cache_k_win[0]
    cv = cache_v_win[0]
    o_attn_s = _decode_attention(
        sinks,
        qs.reshape(s, N_HEADS, HEAD_DIM),
        ks.reshape(s, N_KV_HEADS, HEAD_DIM).transpose(1, 0, 2),
        vs.reshape(s, N_KV_HEADS, HEAD_DIM).transpose(1, 0, 2),
        ck.transpose(2, 0, 1, 3), cv.transpose(2, 0, 1, 3), 32)
    o_attn_s = o_attn_s.reshape(s, D_ATTN)
    cs = state_conv[0]
    hist = jnp.concatenate([jnp.zeros((s, CONV_HALO - (CONV_WIDTH - 1), D_CONV), F32), cs], axis=1)
    us_blk = jnp.concatenate([us[:, None, :], jnp.zeros((s, 7, D_CONV), F32)], axis=1)
    o_conv_s = conv(hist, us_blk.reshape(s * 8, D_CONV), 8).reshape(s, 8, D_CONV)[:, 0]
    hs, hns = merge(xs, o_attn_s, o_conv_s, s)
    y_sample = peer(hs, hns, s).reshape(s, 1, D_MODEL)

    new_k_sample = jnp.concatenate([ck[:, 1:], ks.reshape(s, 1, N_KV_HEADS, HEAD_DIM)], axis=1)[None]
    new_v_sample = jnp.concatenate([cv[:, 1:], vs.reshape(s, 1, N_KV_HEADS, HEAD_DIM)], axis=1)[None]
    new_conv_sample = jnp.concatenate([cs[:, 1:], us[:, None, :]], axis=1)[None]
    if w_buf != WINDOW:
        raise NotImplementedError("cache window shorter than the attention window")

    return (y_prompt, y_sample, new_k_prompt, new_v_prompt, new_conv_prompt,
            new_k_sample, new_v_sample, new_conv_sample)
```

```python
import functools

import jax
import jax.numpy as jnp
from jax import lax
from jax.experimental import pallas as pl
from jax.experimental.pallas import tpu as pltpu
from jax.experimental.pallas import tpu_sc as plsc

D_MODEL = 1024
HEAD_DIM = 64
D_ATTN = 512
N_HEADS = 8
N_KV_HEADS = 2
KV_DIM = N_KV_HEADS * HEAD_DIM
D_CONV = 512
D_IN = D_ATTN + 2 * KV_DIM + 2 * D_CONV
CONV_WIDTH = 31
WINDOW = 128
BLOCK = 128
ROPE_THETA = 10000.0
N_META = 16
META_PAD = BLOCK - N_META
PEER_HEADS = 8
PEER_NKEYS = 128
PEER_TOPK = 16
PEER_SEL = PEER_HEADS * PEER_TOPK
EPS = 1e-6
PAST_LEN = 16384

LANES = 128
SC_CORES = 2
SC_SUBCORES = 16
SC_LANES = 16
SC_WORKERS = SC_CORES * SC_SUBCORES
VMEM_LIMIT = 48 * 1024 * 1024

F32 = jnp.float32
BF16 = jnp.bfloat16
NEG_INF = float("-inf")


def _tc_params(n_axes):
    return pltpu.CompilerParams(dimension_semantics=("arbitrary",) * n_axes,
                                vmem_limit_bytes=VMEM_LIMIT)


def _full(shape):
    nd = len(shape)
    return pl.BlockSpec(shape, lambda *_: (0,) * nd)


def _group_mean(sq, gsum_ref):
    hi = sq.astype(BF16)
    lo = (sq - hi.astype(F32)).astype(BF16)
    g = gsum_ref[...]
    s = jnp.dot(hi, g, preferred_element_type=F32) + jnp.dot(lo, g, preferred_element_type=F32)
    return s * (1.0 / HEAD_DIM)


def _rope(xn, cos_f, sin_s, first_half):
    outs = []
    for s in range(xn.shape[1] // LANES):
        xs = xn[:, s * LANES:(s + 1) * LANES]
        partner = jnp.where(first_half, pltpu.roll(xs, LANES - HEAD_DIM // 2, axis=1),
                            pltpu.roll(xs, HEAD_DIM // 2, axis=1))
        outs.append(xs * cos_f + partner * sin_s)
    return outs[0] if len(outs) == 1 else jnp.concatenate(outs, axis=1)


def _proj_kernel(x_ref, g_ref, w_ref, qg_ref, kg_ref, cos_ref, sin_ref, gq_ref, gk_ref,
                 q_out, k_out, v_out, u_out):
    x = x_ref[...]
    n = x * lax.rsqrt(jnp.mean(x * x, axis=-1, keepdims=True) + EPS) * g_ref[...]
    p = jnp.dot(n.astype(BF16), w_ref[...], preferred_element_type=F32)
    q = p[:, :D_ATTN]
    k = p[:, D_ATTN:D_ATTN + KV_DIM]
    v = p[:, D_ATTN + KV_DIM:D_ATTN + 2 * KV_DIM]
    ga = p[:, D_ATTN + 2 * KV_DIM:D_ATTN + 2 * KV_DIM + D_CONV]
    gb = p[:, D_ATTN + 2 * KV_DIM + D_CONV:]
    cos_f = cos_ref[...]
    sin_s = sin_ref[...]
    lane = lax.broadcasted_iota(jnp.int32, (x.shape[0], LANES), 1)
    first_half = (lane % HEAD_DIM) < (HEAD_DIM // 2)
    qn = q * lax.rsqrt(_group_mean(q * q, gq_ref) + EPS) * qg_ref[...]
    kn = k * lax.rsqrt(_group_mean(k * k, gk_ref) + EPS) * kg_ref[...]
    q_out[...] = _rope(qn, cos_f, sin_s, first_half)
    k_out[...] = _rope(kn, cos_f, sin_s, first_half)
    v_out[...] = v
    u_out[...] = ga * (1.0 / (1.0 + jnp.exp(-gb)))


def _project(x, pos_tables, n_table_blocks, tb, norm_g, w_in_bf, qg_t, kg_t, gsum_q, gsum_k):
    n = x.shape[0]
    cos_t, sin_t = pos_tables
    nb = n // tb
    tab_spec = pl.BlockSpec((tb, LANES), lambda i: (i % n_table_blocks, 0))
    row = lambda w: pl.BlockSpec((tb, w), lambda i: (i, 0))
    return pl.pallas_call(
        _proj_kernel,
        grid=(nb,),
        in_specs=[row(D_MODEL), _full((1, D_MODEL)), _full((D_MODEL, D_IN)),
                  _full((1, D_ATTN)), _full((1, KV_DIM)), tab_spec, tab_spec,
                  _full((D_ATTN, D_ATTN)), _full((KV_DIM, KV_DIM))],
        out_specs=[row(D_ATTN), row(KV_DIM), row(KV_DIM), row(D_CONV)],
        out_shape=[jax.ShapeDtypeStruct((n, D_ATTN), F32), jax.ShapeDtypeStruct((n, KV_DIM), F32),
                   jax.ShapeDtypeStruct((n, KV_DIM), F32), jax.ShapeDtypeStruct((n, D_CONV), F32)],
        compiler_params=_tc_params(1),
        name="proj",
    )(x, norm_g, w_in_bf, qg_t, kg_t, cos_t, sin_t, gsum_q, gsum_k)


def _attn_kernel(sink_ref, q_ref, kp_ref, kc_ref, vp_ref, vc_ref, o_ref):
    j = pl.program_id(1)
    r = lax.broadcasted_iota(jnp.int32, (BLOCK, 2 * BLOCK), 0)
    c = lax.broadcasted_iota(jnp.int32, (BLOCK, 2 * BLOCK), 1)
    ok = (c > r) & (c <= r + WINDOW) & ((j > 0) | (c >= META_PAD))
    grp = N_HEADS // N_KV_HEADS
    ok = jnp.concatenate([ok] * grp, axis=0)
    for g in range(N_KV_HEADS):
        q4 = q_ref[0, g * grp:(g + 1) * grp].reshape(grp * BLOCK, HEAD_DIM).astype(BF16)
        k = jnp.concatenate([kp_ref[0, g], kc_ref[0, g]], axis=0).astype(BF16)
        v = jnp.concatenate([vp_ref[0, g], vc_ref[0, g]], axis=0).astype(BF16)
        s = lax.dot_general(q4, k, (((1,), (1,)), ((), ())), preferred_element_type=F32)
        s = jnp.where(ok, s * (HEAD_DIM ** -0.5), NEG_INF)
        sink = jnp.concatenate(
            [jnp.full((BLOCK, 1), sink_ref[g * grp + i], F32) for i in range(grp)], axis=0)
        m = jnp.maximum(jnp.max(s, axis=1, keepdims=True), sink)
        p = jnp.exp(s - m)
        den = jnp.sum(p, axis=1, keepdims=True) + jnp.exp(sink - m)
        o = jnp.dot(p.astype(BF16), v, preferred_element_type=F32) / den
        o_ref[0, g * grp:(g + 1) * grp] = o.reshape(grp, BLOCK, HEAD_DIM)


def _prompt_attention(sinks, q_t, k_t, v_t):
    b, _, t, _ = q_t.shape
    nb = t // BLOCK
    kv_prev = pl.BlockSpec((1, N_KV_HEADS, BLOCK, HEAD_DIM), lambda bi, j: (bi, 0, j, 0))
    kv_cur = pl.BlockSpec((1, N_KV_HEADS, BLOCK, HEAD_DIM), lambda bi, j: (bi, 0, j + 1, 0))
    q_spec = pl.BlockSpec((1, N_HEADS, BLOCK, HEAD_DIM), lambda bi, j: (bi, 0, j, 0))
    return pl.pallas_call(
        _attn_kernel,
        grid=(b, nb),
        in_specs=[pl.BlockSpec(memory_space=pltpu.SMEM), q_spec, kv_prev, kv_cur, kv_prev, kv_cur],
        out_specs=q_spec,
        out_shape=jax.ShapeDtypeStruct(q_t.shape, F32),
        compiler_params=_tc_params(2),
        name="attn",
    )(sinks, q_t, k_t, k_t, v_t, v_t)


def _dec_attn_kernel(sink_ref, q_ref, kn_ref, vn_ref, ck_ref, cv_ref, o_ref):
    grp = N_HEADS // N_KV_HEADS
    sb, w_buf = ck_ref.shape[1], ck_ref.shape[2]
    q = q_ref[...]
    qb = q.astype(BF16)
    head = lax.broadcasted_iota(jnp.int32, (sb, N_HEADS, 1), 1)
    in_g0 = head < grp
    s_g = [jnp.einsum("shd,swd->shw", qb, ck_ref[g].astype(BF16), preferred_element_type=F32)
           for g in range(N_KV_HEADS)]
    s = jnp.where(in_g0, s_g[0], s_g[1]) * (HEAD_DIM ** -0.5)
    key_ok = lax.broadcasted_iota(jnp.int32, (sb, N_HEADS, w_buf), 2) >= 1
    s = jnp.where(key_ok, s, NEG_INF)
    rnd = lambda a: a.astype(BF16).astype(F32)
    kn = jnp.where(in_g0, kn_ref[0][:, None, :], kn_ref[1][:, None, :])
    vn = jnp.where(in_g0, vn_ref[0][:, None, :], vn_ref[1][:, None, :])
    s_self = jnp.sum(rnd(q) * rnd(kn), axis=-1, keepdims=True) * (HEAD_DIM ** -0.5)
    sink = sink_ref[...]
    m = jnp.maximum(jnp.maximum(jnp.max(s, axis=-1, keepdims=True), s_self), sink)
    p = jnp.exp(s - m)
    p_self = jnp.exp(s_self - m)
    den = jnp.sum(p, axis=-1, keepdims=True) + p_self + jnp.exp(sink - m)
    pb = p.astype(BF16)
    o_g = [jnp.einsum("shw,swd->shd", pb, cv_ref[g].astype(BF16), preferred_element_type=F32)
           for g in range(N_KV_HEADS)]
    o = jnp.where(in_g0, o_g[0], o_g[1]) + rnd(p_self) * rnd(vn)
    o_ref[...] = o / den


def _decode_attention(sinks, q3, kn_t, vn_t, ck_t, cv_t, sb):
    s = q3.shape[0]
    w_buf = ck_t.shape[2]
    qs = pl.BlockSpec((sb, N_HEADS, HEAD_DIM), lambda i: (i, 0, 0))
    ns = pl.BlockSpec((N_KV_HEADS, sb, HEAD_DIM), lambda i: (0, i, 0))
    cs = pl.BlockSpec((N_KV_HEADS, sb, w_buf, HEAD_DIM), lambda i: (0, i, 0, 0))
    return pl.pallas_call(
        _dec_attn_kernel,
        grid=(s // sb,),
        in_specs=[_full((1, N_HEADS, 1)), qs, ns, ns, cs, cs],
        out_specs=qs,
        out_shape=jax.ShapeDtypeStruct(q3.shape, F32),
        compiler_params=_tc_params(1),
        name="dec_attn",
    )(sinks.reshape(1, N_HEADS, 1), q3, kn_t, vn_t, ck_t, cv_t)


CONV_HALO = 32
CONV_ROWS = 64


def _conv_kernel(halo_ref, u_ref, cw_ref, cb_ref, lg_ref, lb_ref, w2_ref, o_ref, ucat):
    tb = u_ref.shape[0]
    ucat[0:CONV_HALO, :] = halo_ref[0]
    ucat[CONV_HALO:, :] = u_ref[...]
    first = CONV_HALO - (CONV_WIDTH - 1)
    rows = min(CONV_ROWS, tb)
    for r0 in range(0, tb, rows):
        acc = jnp.zeros((rows, D_CONV), F32)
        for j in range(CONV_WIDTH):
            acc = acc + ucat[r0 + first + j:r0 + first + j + rows, :] * cw_ref[j:j + 1, :]
        y = acc + cb_ref[...]
        yc = y - jnp.mean(y, axis=-1, keepdims=True)
        yn = yc * lax.rsqrt(jnp.mean(yc * yc, axis=-1, keepdims=True) + EPS)
        yn = yn * lg_ref[...] + lb_ref[...]
        act = yn * (1.0 / (1.0 + jnp.exp(-yn)))
        o_ref[r0:r0 + rows, :] = jnp.dot(act.astype(BF16), w2_ref[...],
                                              preferred_element_type=F32)


def _conv_branch(halo, u, tb, conv_w, conv_b, ln_g, ln_b, w_pw2_bf):
    n = u.shape[0]
    return pl.pallas_call(
        _conv_kernel,
        grid=(n // tb,),
        in_specs=[pl.BlockSpec((1, CONV_HALO, D_CONV), lambda i: (i, 0, 0)),
                  pl.BlockSpec((tb, D_CONV), lambda i: (i, 0)),
                  _full((CONV_WIDTH, D_CONV)), _full((1, D_CONV)), _full((1, D_CONV)),
                  _full((1, D_CONV)), _full((D_CONV, D_CONV))],
        out_specs=pl.BlockSpec((tb, D_CONV), lambda i: (i, 0)),
        out_shape=jax.ShapeDtypeStruct((n, D_CONV), F32),
        scratch_shapes=[pltpu.VMEM((tb + CONV_HALO, D_CONV), F32)],
        compiler_params=_tc_params(1),
        name="conv",
    )(halo, u, conv_w, conv_b, ln_g, ln_b, w_pw2_bf)


def _rms(x, g):
    return x * lax.rsqrt(jnp.mean(x * x, axis=-1, keepdims=True) + EPS) * g


def _merge_kernel(x_ref, oa_ref, oc_ref, ga_ref, gc_ref, wa_ref, wc_ref, gf_ref, h_out, hn_out):
    a = _rms(oa_ref[...], ga_ref[...]).astype(BF16)
    c = _rms(oc_ref[...], gc_ref[...]).astype(BF16)
    h = x_ref[...] + (jnp.dot(a, wa_ref[...], preferred_element_type=F32)
                      + jnp.dot(c, wc_ref[...], preferred_element_type=F32))
    h_out[...] = h
    hn_out[...] = _rms(h, gf_ref[...])


def _merge(x, oa, oc, tb, g_a, g_c, w_out_a, w_out_c, g_f):
    n = x.shape[0]
    row = lambda w: pl.BlockSpec((tb, w), lambda i: (i, 0))
    return pl.pallas_call(
        _merge_kernel,
        grid=(n // tb,),
        in_specs=[row(D_MODEL), row(D_ATTN), row(D_CONV), _full((1, D_ATTN)), _full((1, D_CONV)),
                  _full((D_ATTN, D_MODEL)), _full((D_CONV, D_MODEL)), _full((1, D_MODEL))],
        out_specs=[row(D_MODEL), row(D_MODEL)],
        out_shape=[jax.ShapeDtypeStruct((n, D_MODEL), F32)] * 2,
        compiler_params=_tc_params(1),
        name="merge",
    )(x, oa, oc, g_a, g_c, w_out_a, w_out_c, g_f)


ID_BIG = 1e9


def _topk_rows(s, k):
    rows = lax.broadcasted_iota(jnp.int32, s.shape, 0).astype(F32)
    vals, idxs = [], []
    for _ in range(k):
        m = jnp.max(s, axis=0, keepdims=True)
        idx = jnp.min(jnp.where(s == m, rows, ID_BIG), axis=0, keepdims=True)
        vals.append(m)
        idxs.append(idx)
        s = jnp.where(rows == idx, NEG_INF, s)
    return jnp.concatenate(vals, axis=0), jnp.concatenate(idxs, axis=0)


PAIR_B_WIDE = 8


def _route_kernel(hn_ref, wq_ref, k1_ref, k2_ref, eid_out, gate_out):
    tb = hn_ref.shape[0]
    q = jnp.dot(hn_ref[...].astype(BF16), wq_ref[...], preferred_element_type=F32).astype(BF16)
    k1 = k1_ref[...]
    k2 = k2_ref[...]
    nt = (((1,), (1,)), ((), ()))
    r = lax.broadcasted_iota(jnp.int32, (PEER_TOPK + (PAIR_B_WIDE - 1) * PAIR_B_WIDE + PAIR_B_WIDE, tb), 0)
    mid = r - PEER_TOPK
    flat = jnp.where(r < PEER_TOPK, r,
                     jnp.where(mid < (PAIR_B_WIDE - 1) * PAIR_B_WIDE,
                               (1 + mid // PAIR_B_WIDE) * PEER_TOPK + mid % PAIR_B_WIDE,
                               (PAIR_B_WIDE + mid - (PAIR_B_WIDE - 1) * PAIR_B_WIDE) * PEER_TOPK)).astype(F32)
    half = PEER_NKEYS
    for h in range(PEER_HEADS):
        q1 = q[:, (2 * h) * half:(2 * h + 1) * half]
        q2 = q[:, (2 * h + 1) * half:(2 * h + 2) * half]
        s1 = lax.dot_general(k1, q1, nt, preferred_element_type=F32)
        s2 = lax.dot_general(k2, q2, nt, preferred_element_type=F32)
        v1, i1 = _topk_rows(s1, PEER_TOPK)
        v2, i2 = _topk_rows(s2, PEER_TOPK)
        e1 = i1 * PEER_NKEYS
        cand = jnp.concatenate(
            [v1[0:1] + v2]
            + [v1[a:a + 1] + v2[0:PAIR_B_WIDE] for a in range(1, PAIR_B_WIDE)]
            + [v1[PAIR_B_WIDE:] + v2[0:1]], axis=0)
        cid = jnp.concatenate(
            [e1[0:1] + i2]
            + [e1[a:a + 1] + i2[0:PAIR_B_WIDE] for a in range(1, PAIR_B_WIDE)]
            + [e1[PAIR_B_WIDE:] + i2[0:1]], axis=0)
        scs, eids = [], []
        for _ in range(PEER_TOPK):
            m = jnp.max(cand, axis=0, keepdims=True)
            jsel = jnp.min(jnp.where(cand == m, flat, ID_BIG), axis=0, keepdims=True)
            hit = flat == jsel
            eids.append(jnp.max(jnp.where(hit, cid, -1.0), axis=0, keepdims=True))
            scs.append(m)
            cand = jnp.where(hit, NEG_INF, cand)
        sc = jnp.concatenate(scs, axis=0)
        e = jnp.exp(sc - sc[0:1])
        gate_out[h * PEER_TOPK:(h + 1) * PEER_TOPK, :] = e / jnp.sum(e, axis=0, keepdims=True)
        eid_out[h * PEER_TOPK:(h + 1) * PEER_TOPK, :] = jnp.concatenate(eids, axis=0).astype(jnp.int32)


def _route(hn, tb, wq_bf, keys1_bf, keys2_bf):
    n = hn.shape[0]
    col = pl.BlockSpec((PEER_SEL, tb), lambda i: (0, i))
    return pl.pallas_call(
        _route_kernel,
        grid=(n // tb,),
        in_specs=[pl.BlockSpec((tb, D_MODEL), lambda i: (i, 0)),
                  _full((D_MODEL, 2 * PEER_NKEYS * PEER_HEADS)),
                  _full((PEER_NKEYS, PEER_NKEYS)), _full((PEER_NKEYS, PEER_NKEYS))],
        out_specs=[col, col],
        out_shape=[jax.ShapeDtypeStruct((PEER_SEL, n), jnp.int32),
                   jax.ShapeDtypeStruct((PEER_SEL, n), F32)],
        compiler_params=_tc_params(1),
        name="route",
    )(hn, wq_bf, keys1_bf, keys2_bf)


SC_ROWS = 32
SC_GROUP = SC_LANES
SC_TOKENS = 8
SC_CHUNKS = PEER_SEL // SC_ROWS

_SC_PARAMS = pltpu.CompilerParams(needs_layout_passes=False)


def _sc_mesh():
    return plsc.VectorSubcoreMesh(core_axis_name="c", subcore_axis_name="s")


def _sc_token_loop(n_batches, table_hbm, eid_v, rbufs, rsems, prefetch, prefetch_wait, store,
                   contract):
    def gather(slot, tl, ch, p):
        idx = eid_v.at[slot, tl, pl.ds(ch * SC_ROWS, SC_ROWS)]
        return pltpu.make_async_copy(table_hbm.at[idx], rbufs[p], rsems[p])

    prefetch(0, 0)
    prefetch_wait()
    gather(0, 0, 0, 0).start()

    @pl.loop(0, n_batches * SC_TOKENS)
    def _(i):
        b = i // SC_TOKENS
        tl = i % SC_TOKENS
        slot = b % 2
        more = b + 1 < n_batches

        @pl.when(jnp.logical_and(tl == 0, more))
        def _():
            prefetch(b + 1, 1 - slot)

        for ch in range(SC_CHUNKS):
            p = ch % 2
            if ch + 1 < SC_CHUNKS:
                gather(slot, tl, ch + 1, 1 - p).start()
            else:
                @pl.when(tl + 1 < SC_TOKENS)
                def _():
                    gather(slot, tl + 1, 0, 1 - p).start()

                @pl.when(jnp.logical_and(tl + 1 == SC_TOKENS, more))
                def _():
                    prefetch_wait()
                    gather(1 - slot, 0, 0, 1 - p).start()
            gather(slot, tl, ch, p).wait()
            contract(slot, tl, ch, rbufs[p])

        @pl.when(tl + 1 == SC_TOKENS)
        def _():
            @pl.when(b >= 1)
            def _():
                store(b - 1, 1 - slot).wait()
            store(b, slot).start()

    store(n_batches - 1, (n_batches - 1) % 2).wait()


def _sc_batches(n):
    assert n % (SC_WORKERS * SC_TOKENS) == 0, n
    return n // (SC_WORKERS * SC_TOKENS)


def _peer_hval(hn, eid, peer_u):
    n = hn.shape[0]
    nbw = _sc_batches(n)

    @functools.partial(
        pl.kernel, mesh=_sc_mesh(),
        out_type=jax.ShapeDtypeStruct((n // SC_TOKENS, SC_TOKENS, PEER_SEL), F32),
        scratch_types=[
            pltpu.VMEM((2, SC_TOKENS, PEER_SEL), jnp.int32),
            pltpu.VMEM((2, SC_TOKENS, D_MODEL), F32),
            pltpu.VMEM((SC_ROWS, D_MODEL), F32),
            pltpu.VMEM((SC_ROWS, D_MODEL), F32),
            pltpu.VMEM((2, SC_TOKENS, PEER_SEL), F32),
            pltpu.VMEM((SC_GROUP, SC_LANES), F32),
            pltpu.SemaphoreType.DMA, pltpu.SemaphoreType.DMA,
            pltpu.SemaphoreType.DMA, pltpu.SemaphoreType.DMA,
        ],
        compiler_params=_SC_PARAMS, name="peer_hval")
    def k(x_hbm, eid_hbm, u_hbm, o_hbm, eid_v, x_v, r0, r1, h_v, tr, sr0, sr1, spf, sout):
        blk0 = (lax.axis_index("s") * SC_CORES + lax.axis_index("c")) * nbw
        lane = lax.iota(jnp.int32, SC_LANES)
        zero = jnp.zeros((SC_LANES,), F32)

        def prefetch(b, slot):
            pltpu.async_copy(eid_hbm.at[blk0 + b], eid_v.at[slot], spf)
            pltpu.async_copy(x_hbm.at[blk0 + b], x_v.at[slot], spf)

        def prefetch_wait():
            pltpu.make_async_copy(eid_hbm.at[0], eid_v.at[0], spf).wait()
            pltpu.make_async_copy(x_hbm.at[0], x_v.at[0], spf).wait()

        def store(b, slot):
            return pltpu.make_async_copy(h_v.at[slot], o_hbm.at[blk0 + b], sout)

        def contract(slot, tl, ch, rbuf):
            for g in range(SC_ROWS // SC_GROUP):
                def body(c, accs):
                    xc = x_v[slot, tl, pl.ds(c * SC_LANES, SC_LANES)]
                    return tuple(
                        accs[r] + rbuf[g * SC_GROUP + r, pl.ds(c * SC_LANES, SC_LANES)] * xc
                        for r in range(SC_GROUP))
                accs = lax.fori_loop(0, D_MODEL // SC_LANES, body, (zero,) * SC_GROUP)
                for r in range(SC_GROUP):
                    tr[r, :] = accs[r]
                res = zero
                for jj in range(SC_LANES):
                    res = res + plsc.load_gather(tr, [lane, jnp.full((SC_LANES,), jj, jnp.int32)])
                h_v[slot, tl, pl.ds(ch * SC_ROWS + g * SC_GROUP, SC_GROUP)] = res

        _sc_token_loop(nbw, u_hbm, eid_v, (r0, r1), (sr0, sr1), prefetch, prefetch_wait, store,
                       contract)

    out = k(hn.reshape(n // SC_TOKENS, SC_TOKENS, D_MODEL),
            eid.reshape(n // SC_TOKENS, SC_TOKENS, PEER_SEL), peer_u)
    return out.reshape(n, PEER_SEL)


def _peer_out(h, a, eid, peer_v):
    n = h.shape[0]
    nbw = _sc_batches(n)

    @functools.partial(
        pl.kernel, mesh=_sc_mesh(),
        out_type=jax.ShapeDtypeStruct((n // SC_TOKENS, SC_TOKENS, D_MODEL), F32),
        scratch_types=[
            pltpu.VMEM((2, SC_TOKENS, PEER_SEL), jnp.int32),
            pltpu.VMEM((2, SC_TOKENS, PEER_SEL), F32),
            pltpu.VMEM((SC_ROWS, D_MODEL), F32),
            pltpu.VMEM((SC_ROWS, D_MODEL), F32),
            pltpu.VMEM((2, SC_TOKENS, D_MODEL), F32),
            pltpu.SemaphoreType.DMA, pltpu.SemaphoreType.DMA,
            pltpu.SemaphoreType.DMA, pltpu.SemaphoreType.DMA,
        ],
        compiler_params=_SC_PARAMS, name="peer_out")
    def k(h_hbm, a_hbm, eid_hbm, v_hbm, o_hbm, eid_v, a_v, r0, r1, y_v, sr0, sr1, spf, sout):
        blk0 = (lax.axis_index("s") * SC_CORES + lax.axis_index("c")) * nbw

        def prefetch(b, slot):
            pltpu.async_copy(eid_hbm.at[blk0 + b], eid_v.at[slot], spf)
            pltpu.async_copy(a_hbm.at[blk0 + b], a_v.at[slot], spf)
            pltpu.async_copy(h_hbm.at[blk0 + b], y_v.at[slot], spf)

        def prefetch_wait():
            pltpu.make_async_copy(eid_hbm.at[0], eid_v.at[0], spf).wait()
            pltpu.make_async_copy(a_hbm.at[0], a_v.at[0], spf).wait()
            pltpu.make_async_copy(h_hbm.at[0], y_v.at[0], spf).wait()

        def store(b, slot):
            return pltpu.make_async_copy(y_v.at[slot], o_hbm.at[blk0 + b], sout)

        def contract(slot, tl, ch, rbuf):
            slot_v = jnp.full((SC_LANES,), slot, jnp.int32)
            tl_v = jnp.full((SC_LANES,), tl, jnp.int32)
            for g in range(SC_ROWS // SC_GROUP):
                base = ch * SC_ROWS + g * SC_GROUP
                coef = [plsc.load_gather(a_v, [slot_v, tl_v, jnp.full((SC_LANES,), base + r, jnp.int32)])
                        for r in range(SC_GROUP)]

                @plsc.parallel_loop(0, D_MODEL // SC_LANES)
                def _(c):
                    sl = pl.ds(c * SC_LANES, SC_LANES)
                    acc = coef[0] * rbuf[g * SC_GROUP, sl]
                    for r in range(1, SC_GROUP):
                        acc = acc + coef[r] * rbuf[g * SC_GROUP + r, sl]
                    plsc.addupdate(y_v.at[slot, tl, sl], acc)

        _sc_token_loop(nbw, v_hbm, eid_v, (r0, r1), (sr0, sr1), prefetch, prefetch_wait, store,
                       contract)

    out = k(h.reshape(n // SC_TOKENS, SC_TOKENS, D_MODEL),
            a.reshape(n // SC_TOKENS, SC_TOKENS, PEER_SEL),
            eid.reshape(n // SC_TOKENS, SC_TOKENS, PEER_SEL), peer_v)
    return out.reshape(n, D_MODEL)


def _gate_kernel(hv_ref, gate_ref, a_out):
    hv = hv_ref[...]
    gelu = hv * (lax.erf(hv * (2.0 ** -0.5)) + 1.0) * 0.5
    a_out[...] = gate_ref[...] * gelu


def _gate(hval, gate, tb):
    n = hval.shape[0]
    row = pl.BlockSpec((tb, PEER_SEL), lambda i: (i, 0))
    return pl.pallas_call(
        _gate_kernel, grid=(n // tb,), in_specs=[row, row], out_specs=row,
        out_shape=jax.ShapeDtypeStruct((n, PEER_SEL), F32),
        compiler_params=_tc_params(1), name="gate",
    )(hval, gate)


def _peer(h, hn, tb, wq_bf, keys1_bf, keys2_bf, peer_u, peer_v):
    n = h.shape[0]
    eid_t, gate_t = _route(hn, tb, wq_bf, keys1_bf, keys2_bf)
    pad = (-n) % (SC_WORKERS * SC_TOKENS)
    padr = (lambda a: jnp.pad(a, ((0, pad), (0, 0)))) if pad else (lambda a: a)
    eid = padr(eid_t.T)
    hval = _peer_hval(padr(hn), eid, peer_u)
    a = _gate(hval, padr(gate_t.T), min(tb, 512))
    return _peer_out(padr(h), a, eid, peer_v)[:n]


def _rope_tables(pos):
    half = HEAD_DIM // 2
    inv = ROPE_THETA ** (-jnp.arange(half, dtype=F32) / half)
    ang = pos.astype(F32)[:, None] * inv[None, :]
    cos = jnp.cos(ang)
    sin = jnp.sin(ang)
    reps = LANES // HEAD_DIM
    cos_f = jnp.tile(jnp.concatenate([cos, cos], axis=1), (1, reps))
    sin_s = jnp.tile(jnp.concatenate([-sin, sin], axis=1), (1, reps))
    return cos_f, sin_s


def _group_sum_matrix(width):
    g = jnp.arange(width) // HEAD_DIM
    return (g[:, None] == g[None, :]).astype(BF16)


def _heads_first(a, heads):
    b, t, _ = a.shape
    return a.reshape(b, t, heads, HEAD_DIM).transpose(0, 2, 1, 3)


def kernel(x_prompt, x_sample, cache_k_win, cache_v_win, state_conv, meta_tokens, norm_mix_g,
           w_in, q_norm_g, k_norm_g, attn_sinks, conv_w, conv_b, conv_ln_g, conv_ln_b, w_pw2,
           out_norm_attn_g, out_norm_conv_g, w_out, norm_ffn_g, peer_w_q, peer_keys1, peer_keys2,
           peer_u, peer_v):
    assert norm_mix_g.shape[0] == 1, "single-layer model"
    b, t, _ = x_prompt.shape
    s = x_sample.shape[0]
    w_buf = cache_k_win.shape[2]
    n = b * t

    g_mix = norm_mix_g[0][None, :]
    w_in_bf = w_in[0].astype(BF16)
    qg_t = jnp.tile(q_norm_g[0], N_HEADS)[None, :]
    kg_t = jnp.tile(k_norm_g[0], N_KV_HEADS)[None, :]
    gsum_q = _group_sum_matrix(D_ATTN)
    gsum_k = _group_sum_matrix(KV_DIM)
    sinks = attn_sinks[0]
    cw, cb = conv_w[0], conv_b[0][None, :]
    lg, lb = conv_ln_g[0][None, :], conv_ln_b[0][None, :]
    w2_bf = w_pw2[0].astype(BF16)
    g_a, g_c = out_norm_attn_g[0][None, :], out_norm_conv_g[0][None, :]
    w_out_a = w_out[0][:D_ATTN].astype(BF16)
    w_out_c = w_out[0][D_ATTN:].astype(BF16)
    g_f = norm_ffn_g[0][None, :]
    wq_bf = peer_w_q[0].astype(BF16)
    k1_bf = peer_keys1[0].astype(BF16)
    k2_bf = peer_keys2[0].astype(BF16)
    pu, pv = peer_u[0], peer_v[0]
    proj = functools.partial(_project, norm_g=g_mix, w_in_bf=w_in_bf, qg_t=qg_t, kg_t=kg_t,
                             gsum_q=gsum_q, gsum_k=gsum_k)
    conv = functools.partial(_conv_branch, conv_w=cw, conv_b=cb, ln_g=lg, ln_b=lb, w_pw2_bf=w2_bf)
    merge = functools.partial(_merge, g_a=g_a, g_c=g_c, w_out_a=w_out_a, w_out_c=w_out_c, g_f=g_f)
    peer = functools.partial(_peer, wq_bf=wq_bf, keys1_bf=k1_bf, keys2_bf=k2_bf, peer_u=pu, peer_v=pv)

    tb = 512
    xp = x_prompt.reshape(n, D_MODEL)
    tab_x = _rope_tables(N_META + jnp.arange(t, dtype=jnp.int32))
    q, k, v, u = proj(xp, tab_x, t // tb, tb)
    tab_m = _rope_tables(jnp.arange(N_META, dtype=jnp.int32))
    _, k_m, v_m, u_m = proj(meta_tokens, tab_m, 1, N_META)

    def with_meta_block(a, a_m):
        blk0 = jnp.concatenate([jnp.zeros((META_PAD, KV_DIM), F32), a_m], axis=0)
        full = jnp.concatenate([jnp.broadcast_to(blk0[None], (b, BLOCK, KV_DIM)),
                                a.reshape(b, t, KV_DIM)], axis=1)
        return _heads_first(full, N_KV_HEADS)

    o_attn = _prompt_attention(sinks, _heads_first(q.reshape(b, t, D_ATTN), N_HEADS),
                               with_meta_block(k, k_m), with_meta_block(v, v_m))
    o_attn = o_attn.transpose(0, 2, 1, 3).reshape(n, D_ATTN)

    u3 = u.reshape(b, t // tb, tb, D_CONV)
    halo0 = jnp.concatenate([jnp.zeros((CONV_HALO - N_META, D_CONV), F32), u_m], axis=0)
    halo = jnp.concatenate([jnp.broadcast_to(halo0[None, None], (b, 1, CONV_HALO, D_CONV)),
                            u3[:, :-1, tb - CONV_HALO:]], axis=1).reshape(n // tb, CONV_HALO, D_CONV)
    o_conv = conv(halo, u, tb)

    h, hn = merge(xp, o_attn, o_conv, tb)
    y_prompt = peer(h, hn, 256).reshape(b, t, D_MODEL)

    new_k_prompt = k.reshape(b, t, N_KV_HEADS, HEAD_DIM)[None, :, t - WINDOW:]
    new_v_prompt = v.reshape(b, t, N_KV_HEADS, HEAD_DIM)[None, :, t - WINDOW:]
    new_conv_prompt = u.reshape(b, t, D_CONV)[None, :, t - (CONV_WIDTH - 1):]

    xs = x_sample.reshape(s, D_MODEL)
    tab_s = _rope_tables(jnp.full((s,), PAST_LEN, jnp.int32))
    qs, ks, vs, us = proj(xs, tab_s, 1, s)
    ck = cache_k_win[0]
    cv = cache_v_win[0]
    o_attn_s = _decode_attention(
        sinks,
        qs.reshape(s, N_HEADS, HEAD_DIM),
        ks.reshape(s, N_KV_HEADS, HEAD_DIM).transpose(1, 0, 2),
        vs.reshape(s, N_KV_HEADS, HEAD_DIM).transpose(1, 0, 2),
        ck.transpose(2, 0, 1, 3), cv.transpose(2, 0, 1, 3), 32)
    o_attn_s = o_attn_s.reshape(s, D_ATTN)
    cs = state_conv[0]
    hist = jnp.concatenate([jnp.zeros((s, CONV_HALO - (CONV_WIDTH - 1), D_CONV), F32), cs], axis=1)
    us_blk = jnp.concatenate([us[:, None, :], jnp.zeros((s, 7, D_CONV), F32)], axis=1)
    o_conv_s = conv(hist, us_blk.reshape(s * 8, D_CONV), 8).reshape(s, 8, D_CONV)[:, 0]
    hs, hns = merge(xs, o_attn_s, o_conv_s, s)
    y_sample = peer(hs, hns, s).reshape(s, 1, D_MODEL)

    new_k_sample = jnp.concatenate([ck[:, 1:], ks.reshape(s, 1, N_KV_HEADS, HEAD_DIM)], axis=1)[None]
    new_v_sample = jnp.concatenate([cv[:, 1:], vs.reshape(s, 1, N_KV_HEADS, HEAD_DIM)], axis=1)[None]
    new_conv_sample = jnp.concatenate([cs[:, 1:], us[:, None, :]], axis=1)[None]
    if w_buf != WINDOW:
        raise NotImplementedError("cache window shorter than the attention window")

    return (y_prompt, y_sample, new_k_prompt, new_v_prompt, new_conv_prompt,
            new_k_sample, new_v_sample, new_conv_sample)
```

```python
import functools

import jax
import jax.numpy as jnp
from jax import lax
from jax.experimental import pallas as pl
from jax.experimental.pallas import tpu as pltpu
from jax.experimental.pallas import tpu_sc as plsc

D_MODEL = 1024
HEAD_DIM = 64
D_ATTN = 512
N_HEADS = 8
N_KV_HEADS = 2
KV_DIM = N_KV_HEADS * HEAD_DIM
D_CONV = 512
D_IN = D_ATTN + 2 * KV_DIM + 2 * D_CONV
CONV_WIDTH = 31
WINDOW = 128
BLOCK = 128
ROPE_THETA = 10000.0
N_META = 16
META_PAD = BLOCK - N_META
PEER_HEADS = 8
PEER_NKEYS = 128
PEER_TOPK = 16
PEER_SEL = PEER_HEADS * PEER_TOPK
EPS = 1e-6
PAST_LEN = 16384

LANES = 128
SC_CORES = 2
SC_SUBCORES = 16
SC_LANES = 16
SC_WORKERS = SC_CORES * SC_SUBCORES
VMEM_LIMIT = 48 * 1024 * 1024

F32 = jnp.float32
BF16 = jnp.bfloat16
NEG_INF = float("-inf")


def _tc_params(n_axes):
    return pltpu.CompilerParams(dimension_semantics=("arbitrary",) * n_axes,
                                vmem_limit_bytes=VMEM_LIMIT)


def _full(shape):
    nd = len(shape)
    return pl.BlockSpec(shape, lambda *_: (0,) * nd)


def _group_mean(sq, gsum_ref):
    hi = sq.astype(BF16)
    lo = (sq - hi.astype(F32)).astype(BF16)
    g = gsum_ref[...]
    s = jnp.dot(hi, g, preferred_element_type=F32) + jnp.dot(lo, g, preferred_element_type=F32)
    return s * (1.0 / HEAD_DIM)


def _rope(xn, cos_f, sin_s, first_half):
    outs = []
    for s in range(xn.shape[1] // LANES):
        xs = xn[:, s * LANES:(s + 1) * LANES]
        partner = jnp.where(first_half, pltpu.roll(xs, LANES - HEAD_DIM // 2, axis=1),
                            pltpu.roll(xs, HEAD_DIM // 2, axis=1))
        outs.append(xs * cos_f + partner * sin_s)
    return outs[0] if len(outs) == 1 else jnp.concatenate(outs, axis=1)


def _proj_kernel(x_ref, g_ref, w_ref, qg_ref, kg_ref, cos_ref, sin_ref, gq_ref, gk_ref,
                 q_out, k_out, v_out, u_out):
    x = x_ref[...]
    n = x * lax.rsqrt(jnp.mean(x * x, axis=-1, keepdims=True) + EPS) * g_ref[...]
    p = jnp.dot(n.astype(BF16), w_ref[...], preferred_element_type=F32)
    q = p[:, :D_ATTN]
    k = p[:, D_ATTN:D_ATTN + KV_DIM]
    v = p[:, D_ATTN + KV_DIM:D_ATTN + 2 * KV_DIM]
    ga = p[:, D_ATTN + 2 * KV_DIM:D_ATTN + 2 * KV_DIM + D_CONV]
    gb = p[:, D_ATTN + 2 * KV_DIM + D_CONV:]
    cos_f = cos_ref[...]
    sin_s = sin_ref[...]
    lane = lax.broadcasted_iota(jnp.int32, (x.shape[0], LANES), 1)
    first_half = (lane % HEAD_DIM) < (HEAD_DIM // 2)
    qn = q * lax.rsqrt(_group_mean(q * q, gq_ref) + EPS) * qg_ref[...]
    kn = k * lax.rsqrt(_group_mean(k * k, gk_ref) + EPS) * kg_ref[...]
    q_out[...] = _rope(qn, cos_f, sin_s, first_half)
    k_out[...] = _rope(kn, cos_f, sin_s, first_half)
    v_out[...] = v
    u_out[...] = ga * (1.0 / (1.0 + jnp.exp(-gb)))


def _project(x, pos_tables, n_table_blocks, tb, norm_g, w_in_bf, qg_t, kg_t, gsum_q, gsum_k):
    n = x.shape[0]
    cos_t, sin_t = pos_tables
    nb = n // tb
    tab_spec = pl.BlockSpec((tb, LANES), lambda i: (i % n_table_blocks, 0))
    row = lambda w: pl.BlockSpec((tb, w), lambda i: (i, 0))
    return pl.pallas_call(
        _proj_kernel,
        grid=(nb,),
        in_specs=[row(D_MODEL), _full((1, D_MODEL)), _full((D_MODEL, D_IN)),
                  _full((1, D_ATTN)), _full((1, KV_DIM)), tab_spec, tab_spec,
                  _full((D_ATTN, D_ATTN)), _full((KV_DIM, KV_DIM))],
        out_specs=[row(D_ATTN), row(KV_DIM), row(KV_DIM), row(D_CONV)],
        out_shape=[jax.ShapeDtypeStruct((n, D_ATTN), F32), jax.ShapeDtypeStruct((n, KV_DIM), F32),
                   jax.ShapeDtypeStruct((n, KV_DIM), F32), jax.ShapeDtypeStruct((n, D_CONV), F32)],
        compiler_params=_tc_params(1),
        name="proj",
    )(x, norm_g, w_in_bf, qg_t, kg_t, cos_t, sin_t, gsum_q, gsum_k)


def _attn_kernel(sink_ref, q_ref, kp_ref, kc_ref, vp_ref, vc_ref, o_ref):
    j = pl.program_id(1)
    r = lax.broadcasted_iota(jnp.int32, (BLOCK, 2 * BLOCK), 0)
    c = lax.broadcasted_iota(jnp.int32, (BLOCK, 2 * BLOCK), 1)
    ok = (c > r) & (c <= r + WINDOW) & ((j > 0) | (c >= META_PAD))
    grp = N_HEADS // N_KV_HEADS
    ok = jnp.concatenate([ok] * grp, axis=0)
    for g in range(N_KV_HEADS):
        q4 = q_ref[0, g * grp:(g + 1) * grp].reshape(grp * BLOCK, HEAD_DIM).astype(BF16)
        k = jnp.concatenate([kp_ref[0, g], kc_ref[0, g]], axis=0).astype(BF16)
        v = jnp.concatenate([vp_ref[0, g], vc_ref[0, g]], axis=0).astype(BF16)
        s = lax.dot_general(q4, k, (((1,), (1,)), ((), ())), preferred_element_type=F32)
        s = jnp.where(ok, s * (HEAD_DIM ** -0.5), NEG_INF)
        sink = jnp.concatenate(
            [jnp.full((BLOCK, 1), sink_ref[g * grp + i], F32) for i in range(grp)], axis=0)
        m = jnp.maximum(jnp.max(s, axis=1, keepdims=True), sink)
        p = jnp.exp(s - m)
        den = jnp.sum(p, axis=1, keepdims=True) + jnp.exp(sink - m)
        o = jnp.dot(p.astype(BF16), v, preferred_element_type=F32) / den
        o_ref[0, g * grp:(g + 1) * grp] = o.reshape(grp, BLOCK, HEAD_DIM)


def _prompt_attention(sinks, q_t, k_t, v_t):
    b, _, t, _ = q_t.shape
    nb = t // BLOCK
    kv_prev = pl.BlockSpec((1, N_KV_HEADS, BLOCK, HEAD_DIM), lambda bi, j: (bi, 0, j, 0))
    kv_cur = pl.BlockSpec((1, N_KV_HEADS, BLOCK, HEAD_DIM), lambda bi, j: (bi, 0, j + 1, 0))
    q_spec = pl.BlockSpec((1, N_HEADS, BLOCK, HEAD_DIM), lambda bi, j: (bi, 0, j, 0))
    return pl.pallas_call(
        _attn_kernel,
        grid=(b, nb),
        in_specs=[pl.BlockSpec(memory_space=pltpu.SMEM), q_spec, kv_prev, kv_cur, kv_prev, kv_cur],
        out_specs=q_spec,
        out_shape=jax.ShapeDtypeStruct(q_t.shape, F32),
        compiler_params=_tc_params(2),
        name="attn",
    )(sinks, q_t, k_t, k_t, v_t, v_t)


def _dec_attn_kernel(sink_ref, q_ref, kn_ref, vn_ref, ck_ref, cv_ref, o_ref):
    grp = N_HEADS // N_KV_HEADS
    sb, w_buf = ck_ref.shape[1], ck_ref.shape[2]
    q = q_ref[...]
    qb = q.astype(BF16)
    head = lax.broadcasted_iota(jnp.int32, (sb, N_HEADS, 1), 1)
    in_g0 = head < grp
    s_g = [jnp.einsum("shd,swd->shw", qb, ck_ref[g].astype(BF16), preferred_element_type=F32)
           for g in range(N_KV_HEADS)]
    s = jnp.where(in_g0, s_g[0], s_g[1]) * (HEAD_DIM ** -0.5)
    key_ok = lax.broadcasted_iota(jnp.int32, (sb, N_HEADS, w_buf), 2) >= 1
    s = jnp.where(key_ok, s, NEG_INF)
    rnd = lambda a: a.astype(BF16).astype(F32)
    kn = jnp.where(in_g0, kn_ref[0][:, None, :], kn_ref[1][:, None, :])
    vn = jnp.where(in_g0, vn_ref[0][:, None, :], vn_ref[1][:, None, :])
    s_self = jnp.sum(rnd(q) * rnd(kn), axis=-1, keepdims=True) * (HEAD_DIM ** -0.5)
    sink = sink_ref[...]
    m = jnp.maximum(jnp.maximum(jnp.max(s, axis=-1, keepdims=True), s_self), sink)
    p = jnp.exp(s - m)
    p_self = jnp.exp(s_self - m)
    den = jnp.sum(p, axis=-1, keepdims=True) + p_self + jnp.exp(sink - m)
    pb = p.astype(BF16)
    o_g = [jnp.einsum("shw,swd->shd", pb, cv_ref[g].astype(BF16), preferred_element_type=F32)
           for g in range(N_KV_HEADS)]
    o = jnp.where(in_g0, o_g[0], o_g[1]) + rnd(p_self) * rnd(vn)
    o_ref[...] = o / den


def _decode_attention(sinks, q3, kn_t, vn_t, ck_t, cv_t, sb):
    s = q3.shape[0]
    w_buf = ck_t.shape[2]
    qs = pl.BlockSpec((sb, N_HEADS, HEAD_DIM), lambda i: (i, 0, 0))
    ns = pl.BlockSpec((N_KV_HEADS, sb, HEAD_DIM), lambda i: (0, i, 0))
    cs = pl.BlockSpec((N_KV_HEADS, sb, w_buf, HEAD_DIM), lambda i: (0, i, 0, 0))
    return pl.pallas_call(
        _dec_attn_kernel,
        grid=(s // sb,),
        in_specs=[_full((1, N_HEADS, 1)), qs, ns, ns, cs, cs],
        out_specs=qs,
        out_shape=jax.ShapeDtypeStruct(q3.shape, F32),
        compiler_params=_tc_params(1),
        name="dec_attn",
    )(sinks.reshape(1, N_HEADS, 1), q3, kn_t, vn_t, ck_t, cv_t)


CONV_HALO = 32
CONV_ROWS = 64


def _conv_kernel(halo_ref, u_ref, cw_ref, cb_ref, lg_ref, lb_ref, w2_ref, o_ref, ucat):
    tb = u_ref.shape[0]
    ucat[0:CONV_HALO, :] = halo_ref[0]
    ucat[CONV_HALO:, :] = u_ref[...]
    first = CONV_HALO - (CONV_WIDTH - 1)
    rows = min(CONV_ROWS, tb)
    for r0 in range(0, tb, rows):
        acc = jnp.zeros((rows, D_CONV), F32)
        for j in range(CONV_WIDTH):
            acc = acc + ucat[r0 + first + j:r0 + first + j + rows, :] * cw_ref[j:j + 1, :]
        y = acc + cb_ref[...]
        yc = y - jnp.mean(y, axis=-1, keepdims=True)
        yn = yc * lax.rsqrt(jnp.mean(yc * yc, axis=-1, keepdims=True) + EPS)
        yn = yn * lg_ref[...] + lb_ref[...]
        act = yn * (1.0 / (1.0 + jnp.exp(-yn)))
        o_ref[r0:r0 + rows, :] = jnp.dot(act.astype(BF16), w2_ref[...],
                                              preferred_element_type=F32)


def _conv_branch(halo, u, tb, conv_w, conv_b, ln_g, ln_b, w_pw2_bf):
    n = u.shape[0]
    return pl.pallas_call(
        _conv_kernel,
        grid=(n // tb,),
        in_specs=[pl.BlockSpec((1, CONV_HALO, D_CONV), lambda i: (i, 0, 0)),
                  pl.BlockSpec((tb, D_CONV), lambda i: (i, 0)),
                  _full((CONV_WIDTH, D_CONV)), _full((1, D_CONV)), _full((1, D_CONV)),
                  _full((1, D_CONV)), _full((D_CONV, D_CONV))],
        out_specs=pl.BlockSpec((tb, D_CONV), lambda i: (i, 0)),
        out_shape=jax.ShapeDtypeStruct((n, D_CONV), F32),
        scratch_shapes=[pltpu.VMEM((tb + CONV_HALO, D_CONV), F32)],
        compiler_params=_tc_params(1),
        name="conv",
    )(halo, u, conv_w, conv_b, ln_g, ln_b, w_pw2_bf)


def _rms(x, g):
    return x * lax.rsqrt(jnp.mean(x * x, axis=-1, keepdims=True) + EPS) * g


def _merge_kernel(x_ref, oa_ref, oc_ref, ga_ref, gc_ref, wa_ref, wc_ref, gf_ref, h_out, hn_out):
    a = _rms(oa_ref[...], ga_ref[...]).astype(BF16)
    c = _rms(oc_ref[...], gc_ref[...]).astype(BF16)
    h = x_ref[...] + (jnp.dot(a, wa_ref[...], preferred_element_type=F32)
                      + jnp.dot(c, wc_ref[...], preferred_element_type=F32))
    h_out[...] = h
    hn_out[...] = _rms(h, gf_ref[...])


def _merge(x, oa, oc, tb, g_a, g_c, w_out_a, w_out_c, g_f):
    n = x.shape[0]
    row = lambda w: pl.BlockSpec((tb, w), lambda i: (i, 0))
    return pl.pallas_call(
        _merge_kernel,
        grid=(n // tb,),
        in_specs=[row(D_MODEL), row(D_ATTN), row(D_CONV), _full((1, D_ATTN)), _full((1, D_CONV)),
                  _full((D_ATTN, D_MODEL)), _full((D_CONV, D_MODEL)), _full((1, D_MODEL))],
        out_specs=[row(D_MODEL), row(D_MODEL)],
        out_shape=[jax.ShapeDtypeStruct((n, D_MODEL), F32)] * 2,
        compiler_params=_tc_params(1),
        name="merge",
    )(x, oa, oc, g_a, g_c, w_out_a, w_out_c, g_f)


ID_BIG = 1e9


def _topk_rows(s, k):
    rows = lax.broadcasted_iota(jnp.int32, s.shape, 0).astype(F32)
    vals, idxs = [], []
    for _ in range(k):
        m = jnp.max(s, axis=0, keepdims=True)
        idx = jnp.min(jnp.where(s == m, rows, ID_BIG), axis=0, keepdims=True)
        vals.append(m)
        idxs.append(idx)
        s = jnp.where(rows == idx, NEG_INF, s)
    return jnp.concatenate(vals, axis=0), jnp.concatenate(idxs, axis=0)


PAIR_B_WIDE = 8


def _route_kernel(hn_ref, wq_ref, k1_ref, k2_ref, eid_out, gate_out):
    tb = hn_ref.shape[0]
    q = jnp.dot(hn_ref[...].astype(BF16), wq_ref[...], preferred_element_type=F32).astype(BF16)
    k1 = k1_ref[...]
    k2 = k2_ref[...]
    nt = (((1,), (1,)), ((), ()))
    r = lax.broadcasted_iota(jnp.int32, (PEER_TOPK + (PAIR_B_WIDE - 1) * PAIR_B_WIDE + PAIR_B_WIDE, tb), 0)
    mid = r - PEER_TOPK
    flat = jnp.where(r < PEER_TOPK, r,
                     jnp.where(mid < (PAIR_B_WIDE - 1) * PAIR_B_WIDE,
                               (1 + mid // PAIR_B_WIDE) * PEER_TOPK + mid % PAIR_B_WIDE,
                               (PAIR_B_WIDE + mid - (PAIR_B_WIDE - 1) * PAIR_B_WIDE) * PEER_TOPK)).astype(F32)
    half = PEER_NKEYS
    for h in range(PEER_HEADS):
        q1 = q[:, (2 * h) * half:(2 * h + 1) * half]
        q2 = q[:, (2 * h + 1) * half:(2 * h + 2) * half]
        s1 = lax.dot_general(k1, q1, nt, preferred_element_type=F32)
        s2 = lax.dot_general(k2, q2, nt, preferred_element_type=F32)
        v1, i1 = _topk_rows(s1, PEER_TOPK)
        v2, i2 = _topk_rows(s2, PEER_TOPK)
        e1 = i1 * PEER_NKEYS
        cand = jnp.concatenate(
            [v1[0:1] + v2]
            + [v1[a:a + 1] + v2[0:PAIR_B_WIDE] for a in range(1, PAIR_B_WIDE)]
            + [v1[PAIR_B_WIDE:] + v2[0:1]], axis=0)
        cid = jnp.concatenate(
            [e1[0:1] + i2]
            + [e1[a:a + 1] + i2[0:PAIR_B_WIDE] for a in range(1, PAIR_B_WIDE)]
            + [e1[PAIR_B_WIDE:] + i2[0:1]], axis=0)
        scs, eids = [], []
        for _ in range(PEER_TOPK):
            m = jnp.max(cand, axis=0, keepdims=True)
            jsel = jnp.min(jnp.where(cand == m, flat, ID_BIG), axis=0, keepdims=True)
            hit = flat == jsel
            eids.append(jnp.max(jnp.where(hit, cid, -1.0), axis=0, keepdims=True))
            scs.append(m)
            cand = jnp.where(hit, NEG_INF, cand)
        sc = jnp.concatenate(scs, axis=0)
        e = jnp.exp(sc - sc[0:1])
        gate_out[h * PEER_TOPK:(h + 1) * PEER_TOPK, :] = e / jnp.sum(e, axis=0, keepdims=True)
        eid_out[h * PEER_TOPK:(h + 1) * PEER_TOPK, :] = jnp.concatenate(eids, axis=0).astype(jnp.int32)


def _route(hn, tb, wq_bf, keys1_bf, keys2_bf):
    n = hn.shape[0]
    col = pl.BlockSpec((PEER_SEL, tb), lambda i: (0, i))
    return pl.pallas_call(
        _route_kernel,
        grid=(n // tb,),
        in_specs=[pl.BlockSpec((tb, D_MODEL), lambda i: (i, 0)),
                  _full((D_MODEL, 2 * PEER_NKEYS * PEER_HEADS)),
                  _full((PEER_NKEYS, PEER_NKEYS)), _full((PEER_NKEYS, PEER_NKEYS))],
        out_specs=[col, col],
        out_shape=[jax.ShapeDtypeStruct((PEER_SEL, n), jnp.int32),
                   jax.ShapeDtypeStruct((PEER_SEL, n), F32)],
        compiler_params=_tc_params(1),
        name="route",
    )(hn, wq_bf, keys1_bf, keys2_bf)


SC_ROWS = 32
SC_GROUP = SC_LANES
SC_TOKENS = 8
SC_CHUNKS = PEER_SEL // SC_ROWS

_SC_PARAMS = pltpu.CompilerParams(needs_layout_passes=False)


def _sc_mesh():
    return plsc.VectorSubcoreMesh(core_axis_name="c", subcore_axis_name="s")


def _sc_token_loop(n_batches, table_hbm, eid_v, rbufs, rsems, prefetch, prefetch_wait, store,
                   contract):
    def gather(slot, tl, ch, p):
        idx = eid_v.at[slot, tl, pl.ds(ch * SC_ROWS, SC_ROWS)]
        return pltpu.make_async_copy(table_hbm.at[idx], rbufs[p], rsems[p])

    prefetch(0, 0)
    prefetch_wait()
    gather(0, 0, 0, 0).start()

    @pl.loop(0, n_batches * SC_TOKENS)
    def _(i):
        b = i // SC_TOKENS
        tl = i % SC_TOKENS
        slot = b % 2
        more = b + 1 < n_batches

        @pl.when(jnp.logical_and(tl == 0, more))
        def _():
            prefetch(b + 1, 1 - slot)

        for ch in range(SC_CHUNKS):
            p = ch % 2
            if ch + 1 < SC_CHUNKS:
                gather(slot, tl, ch + 1, 1 - p).start()
            else:
                @pl.when(tl + 1 < SC_TOKENS)
                def _():
                    gather(slot, tl + 1, 0, 1 - p).start()

                @pl.when(jnp.logical_and(tl + 1 == SC_TOKENS, more))
                def _():
                    prefetch_wait()
                    gather(1 - slot, 0, 0, 1 - p).start()
            gather(slot, tl, ch, p).wait()
            contract(slot, tl, ch, rbufs[p])

        @pl.when(tl + 1 == SC_TOKENS)
        def _():
            @pl.when(b >= 1)
            def _():
                store(b - 1, 1 - slot).wait()
            store(b, slot).start()

    store(n_batches - 1, (n_batches - 1) % 2).wait()


def _sc_batches(n):
    assert n % (SC_WORKERS * SC_TOKENS) == 0, n
    return n // (SC_WORKERS * SC_TOKENS)


def _peer_hval(hn, eid, peer_u):
    n = hn.shape[0]
    nbw = _sc_batches(n)

    @functools.partial(
        pl.kernel, mesh=_sc_mesh(),
        out_type=jax.ShapeDtypeStruct((n // SC_TOKENS, SC_TOKENS, PEER_SEL), F32),
        scratch_types=[
            pltpu.VMEM((2, SC_TOKENS, PEER_SEL), jnp.int32),
            pltpu.VMEM((2, SC_TOKENS, D_MODEL), F32),
            pltpu.VMEM((SC_ROWS, D_MODEL), F32),
            pltpu.VMEM((SC_ROWS, D_MODEL), F32),
            pltpu.VMEM((2, SC_TOKENS, PEER_SEL), F32),
            pltpu.VMEM((SC_GROUP, SC_LANES), F32),
            pltpu.SemaphoreType.DMA, pltpu.SemaphoreType.DMA,
            pltpu.SemaphoreType.DMA, pltpu.SemaphoreType.DMA,
        ],
        compiler_params=_SC_PARAMS, name="peer_hval")
    def k(x_hbm, eid_hbm, u_hbm, o_hbm, eid_v, x_v, r0, r1, h_v, tr, sr0, sr1, spf, sout):
        blk0 = (lax.axis_index("s") * SC_CORES + lax.axis_index("c")) * nbw
        lane = lax.iota(jnp.int32, SC_LANES)
        zero = jnp.zeros((SC_LANES,), F32)

        def prefetch(b, slot):
            pltpu.async_copy(eid_hbm.at[blk0 + b], eid_v.at[slot], spf)
            pltpu.async_copy(x_hbm.at[blk0 + b], x_v.at[slot], spf)

        def prefetch_wait():
            pltpu.make_async_copy(eid_hbm.at[0], eid_v.at[0], spf).wait()
            pltpu.make_async_copy(x_hbm.at[0], x_v.at[0], spf).wait()

        def store(b, slot):
            return pltpu.make_async_copy(h_v.at[slot], o_hbm.at[blk0 + b], sout)

        def contract(slot, tl, ch, rbuf):
            for g in range(SC_ROWS // SC_GROUP):
                def body(c, accs):
                    xc = x_v[slot, tl, pl.ds(c * SC_LANES, SC_LANES)]
                    return tuple(
                        accs[r] + rbuf[g * SC_GROUP + r, pl.ds(c * SC_LANES, SC_LANES)] * xc
                        for r in range(SC_GROUP))
                accs = lax.fori_loop(0, D_MODEL // SC_LANES, body, (zero,) * SC_GROUP)
                for r in range(SC_GROUP):
                    tr[r, :] = accs[r]
                res = zero
                for jj in range(SC_LANES):
                    res = res + plsc.load_gather(tr, [lane, jnp.full((SC_LANES,), jj, jnp.int32)])
                h_v[slot, tl, pl.ds(ch * SC_ROWS + g * SC_GROUP, SC_GROUP)] = res

        _sc_token_loop(nbw, u_hbm, eid_v, (r0, r1), (sr0, sr1), prefetch, prefetch_wait, store,
                       contract)

    out = k(hn.reshape(n // SC_TOKENS, SC_TOKENS, D_MODEL),
            eid.reshape(n // SC_TOKENS, SC_TOKENS, PEER_SEL), peer_u)
    return out.reshape(n, PEER_SEL)


def _peer_out(h, a, eid, peer_v):
    n = h.shape[0]
    nbw = _sc_batches(n)

    @functools.partial(
        pl.kernel, mesh=_sc_mesh(),
        out_type=jax.ShapeDtypeStruct((n // SC_TOKENS, SC_TOKENS, D_MODEL), F32),
        scratch_types=[
            pltpu.VMEM((2, SC_TOKENS, PEER_SEL), jnp.int32),
            pltpu.VMEM((2, SC_TOKENS, PEER_SEL), F32),
            pltpu.VMEM((SC_ROWS, D_MODEL), F32),
            pltpu.VMEM((SC_ROWS, D_MODEL), F32),
            pltpu.VMEM((2, SC_TOKENS, D_MODEL), F32),
            pltpu.SemaphoreType.DMA, pltpu.SemaphoreType.DMA,
            pltpu.SemaphoreType.DMA, pltpu.SemaphoreType.DMA,
        ],
        compiler_params=_SC_PARAMS, name="peer_out")
    def k(h_hbm, a_hbm, eid_hbm, v_hbm, o_hbm, eid_v, a_v, r0, r1, y_v, sr0, sr1, spf, sout):
        blk0 = (lax.axis_index("s") * SC_CORES + lax.axis_index("c")) * nbw

        def prefetch(b, slot):
            pltpu.async_copy(eid_hbm.at[blk0 + b], eid_v.at[slot], spf)
            pltpu.async_copy(a_hbm.at[blk0 + b], a_v.at[slot], spf)
            pltpu.async_copy(h_hbm.at[blk0 + b], y_v.at[slot], spf)

        def prefetch_wait():
            pltpu.make_async_copy(eid_hbm.at[0], eid_v.at[0], spf).wait()
            pltpu.make_async_copy(a_hbm.at[0], a_v.at[0], spf).wait()
            pltpu.make_async_copy(h_hbm.at[0], y_v.at[0], spf).wait()

        def store(b, slot):
            return pltpu.make_async_copy(y_v.at[slot], o_hbm.at[blk0 + b], sout)

        def contract(slot, tl, ch, rbuf):
            slot_v = jnp.full((SC_LANES,), slot, jnp.int32)
            tl_v = jnp.full((SC_LANES,), tl, jnp.int32)
            for g in range(SC_ROWS // SC_GROUP):
                base = ch * SC_ROWS + g * SC_GROUP
                coef = [plsc.load_gather(a_v, [slot_v, tl_v, jnp.full((SC_LANES,), base + r, jnp.int32)])
                        for r in range(SC_GROUP)]

                @plsc.parallel_loop(0, D_MODEL // SC_LANES)
                def _(c):
                    sl = pl.ds(c * SC_LANES, SC_LANES)
                    acc = coef[0] * rbuf[g * SC_GROUP, sl]
                    for r in range(1, SC_GROUP):
                        acc = acc + coef[r] * rbuf[g * SC_GROUP + r, sl]
                    plsc.addupdate(y_v.at[slot, tl, sl], acc)

        _sc_token_loop(nbw, v_hbm, eid_v, (r0, r1), (sr0, sr1), prefetch, prefetch_wait, store,
                       contract)

    out = k(h.reshape(n // SC_TOKENS, SC_TOKENS, D_MODEL),
            a.reshape(n // SC_TOKENS, SC_TOKENS, PEER_SEL),
            eid.reshape(n // SC_TOKENS, SC_TOKENS, PEER_SEL), peer_v)
    return out.reshape(n, D_MODEL)


def _gate_kernel(hv_ref, gate_ref, a_out):
    hv = hv_ref[...]
    gelu = hv * (lax.erf(hv * (2.0 ** -0.5)) + 1.0) * 0.5
    a_out[...] = gate_ref[...] * gelu


def _gate(hval, gate, tb):
    n = hval.shape[0]
    row = pl.BlockSpec((tb, PEER_SEL), lambda i: (i, 0))
    return pl.pallas_call(
        _gate_kernel, grid=(n // tb,), in_specs=[row, row], out_specs=row,
        out_shape=jax.ShapeDtypeStruct((n, PEER_SEL), F32),
        compiler_params=_tc_params(1), name="gate",
    )(hval, gate)


VT_ROWS = D_MODEL // 2 // LANES
VT_TOKENS = 64
HI_MASK = -65536


def _pack_value_table(peer_v):
    e = peer_v.shape[0]
    bits = lax.bitcast_convert_type(peer_v.astype(BF16), jnp.uint16).astype(jnp.uint32)
    words = bits[:, :D_MODEL // 2] | (bits[:, D_MODEL // 2:] << 16)
    return lax.bitcast_convert_type(words, jnp.int32).reshape(e * VT_ROWS, LANES)


def _vside_kernel(row_s, a_s, h_ref, tab_ref, y_ref):
    tb = h_ref.shape[0]

    def token(t, carry):
        hv = h_ref[t]
        zero = jnp.zeros((VT_ROWS, LANES), F32)
        lo = [hv[0:VT_ROWS], zero]
        hi = [hv[VT_ROWS:], zero]
        for e in range(PEER_SEL):
            r0 = pl.multiple_of(row_s[t, e], VT_ROWS)
            w = tab_ref[pl.ds(r0, VT_ROWS), :]
            coef = a_s[t, e]
            c = e % 2
            lo[c] = lo[c] + coef * pltpu.bitcast(w << 16, F32)
            hi[c] = hi[c] + coef * pltpu.bitcast(w & HI_MASK, F32)
        y_ref[t] = jnp.concatenate([lo[0] + lo[1], hi[0] + hi[1]], axis=0)
        return carry

    lax.fori_loop(0, tb, token, 0)


def _peer_values(h, a, row_ids, v_packed):
    n = h.shape[0]
    tb = min(VT_TOKENS, n)
    smem = pl.BlockSpec((tb, PEER_SEL), lambda i: (i, 0), memory_space=pltpu.SMEM)
    slab = pl.BlockSpec((tb, 2 * VT_ROWS, LANES), lambda i: (i, 0, 0))
    table = pl.BlockSpec(v_packed.shape, lambda i: (0, 0), pipeline_mode=pl.Buffered(1))
    table_bytes = v_packed.shape[0] * LANES * 4
    y = pl.pallas_call(
        _vside_kernel,
        grid=(n // tb,),
        in_specs=[smem, smem, slab, table],
        out_specs=slab,
        out_shape=jax.ShapeDtypeStruct((n, 2 * VT_ROWS, LANES), F32),
        compiler_params=pltpu.CompilerParams(
            dimension_semantics=("arbitrary",),
            vmem_limit_bytes=table_bytes + 8 * 1024 * 1024),
        name="peer_values",
    )(row_ids, a, h.reshape(n, 2 * VT_ROWS, LANES), v_packed)
    return y.reshape(n, D_MODEL)


def _peer(h, hn, tb, wq_bf, keys1_bf, keys2_bf, peer_u, v_packed):
    n = h.shape[0]
    eid_t, gate_t = _route(hn, tb, wq_bf, keys1_bf, keys2_bf)
    eid = eid_t.T
    pad = (-n) % (SC_WORKERS * SC_TOKENS)
    hn_p, eid_p = hn, eid
    if pad:
        spread = (jnp.arange(pad * PEER_SEL, dtype=jnp.int32) % peer_u.shape[0]).reshape(pad, PEER_SEL)
        hn_p = jnp.pad(hn, ((0, pad), (0, 0)))
        eid_p = jnp.concatenate([eid, spread], axis=0)
    hval = _peer_hval(hn_p, eid_p, peer_u)[:n]
    a = _gate(hval, gate_t.T, min(tb, 512))
    return _peer_values(h, a, eid * VT_ROWS, v_packed)


def _rope_tables(pos):
    half = HEAD_DIM // 2
    inv = ROPE_THETA ** (-jnp.arange(half, dtype=F32) / half)
    ang = pos.astype(F32)[:, None] * inv[None, :]
    cos = jnp.cos(ang)
    sin = jnp.sin(ang)
    reps = LANES // HEAD_DIM
    cos_f = jnp.tile(jnp.concatenate([cos, cos], axis=1), (1, reps))
    sin_s = jnp.tile(jnp.concatenate([-sin, sin], axis=1), (1, reps))
    return cos_f, sin_s


def _group_sum_matrix(width):
    g = jnp.arange(width) // HEAD_DIM
    return (g[:, None] == g[None, :]).astype(BF16)


def _heads_first(a, heads):
    b, t, _ = a.shape
    return a.reshape(b, t, heads, HEAD_DIM).transpose(0, 2, 1, 3)


def kernel(x_prompt, x_sample, cache_k_win, cache_v_win, state_conv, meta_tokens, norm_mix_g,
           w_in, q_norm_g, k_norm_g, attn_sinks, conv_w, conv_b, conv_ln_g, conv_ln_b, w_pw2,
           out_norm_attn_g, out_norm_conv_g, w_out, norm_ffn_g, peer_w_q, peer_keys1, peer_keys2,
           peer_u, peer_v):
    assert norm_mix_g.shape[0] == 1, "single-layer model"
    b, t, _ = x_prompt.shape
    s = x_sample.shape[0]
    w_buf = cache_k_win.shape[2]
    n = b * t

    g_mix = norm_mix_g[0][None, :]
    w_in_bf = w_in[0].astype(BF16)
    qg_t = jnp.tile(q_norm_g[0], N_HEADS)[None, :]
    kg_t = jnp.tile(k_norm_g[0], N_KV_HEADS)[None, :]
    gsum_q = _group_sum_matrix(D_ATTN)
    gsum_k = _group_sum_matrix(KV_DIM)
    sinks = attn_sinks[0]
    cw, cb = conv_w[0], conv_b[0][None, :]
    lg, lb = conv_ln_g[0][None, :], conv_ln_b[0][None, :]
    w2_bf = w_pw2[0].astype(BF16)
    g_a, g_c = out_norm_attn_g[0][None, :], out_norm_conv_g[0][None, :]
    w_out_a = w_out[0][:D_ATTN].astype(BF16)
    w_out_c = w_out[0][D_ATTN:].astype(BF16)
    g_f = norm_ffn_g[0][None, :]
    wq_bf = peer_w_q[0].astype(BF16)
    k1_bf = peer_keys1[0].astype(BF16)
    k2_bf = peer_keys2[0].astype(BF16)
    pu, vp = peer_u[0], _pack_value_table(peer_v[0])
    proj = functools.partial(_project, norm_g=g_mix, w_in_bf=w_in_bf, qg_t=qg_t, kg_t=kg_t,
                             gsum_q=gsum_q, gsum_k=gsum_k)
    conv = functools.partial(_conv_branch, conv_w=cw, conv_b=cb, ln_g=lg, ln_b=lb, w_pw2_bf=w2_bf)
    merge = functools.partial(_merge, g_a=g_a, g_c=g_c, w_out_a=w_out_a, w_out_c=w_out_c, g_f=g_f)
    peer = functools.partial(_peer, wq_bf=wq_bf, keys1_bf=k1_bf, keys2_bf=k2_bf, peer_u=pu, v_packed=vp)

    tb = 512
    xp = x_prompt.reshape(n, D_MODEL)
    tab_x = _rope_tables(N_META + jnp.arange(t, dtype=jnp.int32))
    q, k, v, u = proj(xp, tab_x, t // tb, tb)
    tab_m = _rope_tables(jnp.arange(N_META, dtype=jnp.int32))
    _, k_m, v_m, u_m = proj(meta_tokens, tab_m, 1, N_META)

    def with_meta_block(a, a_m):
        blk0 = jnp.concatenate([jnp.zeros((META_PAD, KV_DIM), F32), a_m], axis=0)
        full = jnp.concatenate([jnp.broadcast_to(blk0[None], (b, BLOCK, KV_DIM)),
                                a.reshape(b, t, KV_DIM)], axis=1)
        return _heads_first(full, N_KV_HEADS)

    o_attn = _prompt_attention(sinks, _heads_first(q.reshape(b, t, D_ATTN), N_HEADS),
                               with_meta_block(k, k_m), with_meta_block(v, v_m))
    o_attn = o_attn.transpose(0, 2, 1, 3).reshape(n, D_ATTN)

    u3 = u.reshape(b, t // tb, tb, D_CONV)
    halo0 = jnp.concatenate([jnp.zeros((CONV_HALO - N_META, D_CONV), F32), u_m], axis=0)
    halo = jnp.concatenate([jnp.broadcast_to(halo0[None, None], (b, 1, CONV_HALO, D_CONV)),
                            u3[:, :-1, tb - CONV_HALO:]], axis=1).reshape(n // tb, CONV_HALO, D_CONV)
    o_conv = conv(halo, u, tb)

    h, hn = merge(xp, o_attn, o_conv, tb)
    y_prompt = peer(h, hn, 256).reshape(b, t, D_MODEL)

    new_k_prompt = k.reshape(b, t, N_KV_HEADS, HEAD_DIM)[None, :, t - WINDOW:]
    new_v_prompt = v.reshape(b, t, N_KV_HEADS, HEAD_DIM)[None, :, t - WINDOW:]
    new_conv_prompt = u.reshape(b, t, D_CONV)[None, :, t - (CONV_WIDTH - 1):]

    xs = x_sample.reshape(s, D_MODEL)
    tab_s = _rope_tables(jnp.full((s,), PAST_LEN, jnp.int32))
    qs, ks, vs, us = proj(xs, tab_s, 1, s)
    ck = cache_k_win[0]
    cv = cache_v_win[0]
    o_attn_s = _decode_attention(
        sinks,
        qs.reshape(s, N_HEADS, HEAD_DIM),
        ks.reshape(s, N_KV_HEADS, HEAD_DIM).transpose(1, 0, 2),
        vs.reshape(s, N_KV_HEADS, HEAD_DIM).transpose(1, 0, 2),
        ck.transpose(2, 0, 1, 3), cv.transpose(2, 0, 1, 3), 32)
    o_attn_s = o_attn_s.reshape(s, D_ATTN)
    cs = state_conv[0]
    hist = jnp.concatenate([jnp.zeros((s, CONV_HALO - (CONV_WIDTH - 1), D_CONV), F32), cs], axis=1)
    us_blk = jnp.concatenate([us[:, None, :], jnp.zeros((s, 7, D_CONV), F32)], axis=1)
    o_conv_s = conv(hist, us_blk.reshape(s * 8, D_CONV), 8).reshape(s, 8, D_CONV)[:, 0]
    hs, hns = merge(xs, o_attn_s, o_conv_s, s)
    y_sample = peer(hs, hns, s).reshape(s, 1, D_MODEL)

    new_k_sample = jnp.concatenate([ck[:, 1:], ks.reshape(s, 1, N_KV_HEADS, HEAD_DIM)], axis=1)[None]
    new_v_sample = jnp.concatenate([cv[:, 1:], vs.reshape(s, 1, N_KV_HEADS, HEAD_DIM)], axis=1)[None]
    new_conv_sample = jnp.concatenate([cs[:, 1:], us[:, None, :]], axis=1)[None]
    if w_buf != WINDOW:
        raise NotImplementedError("cache window shorter than the attention window")

    return (y_prompt, y_sample, new_k_prompt, new_v_prompt, new_conv_prompt,
            new_k_sample, new_v_sample, new_conv_sample)
```

```python
import functools

import jax
import jax.numpy as jnp
from jax import lax
from jax.experimental import pallas as pl
from jax.experimental.pallas import tpu as pltpu
from jax.experimental.pallas import tpu_sc as plsc

D_MODEL = 1024
HEAD_DIM = 64
D_ATTN = 512
N_HEADS = 8
N_KV_HEADS = 2
KV_DIM = N_KV_HEADS * HEAD_DIM
D_CONV = 512
D_IN = D_ATTN + 2 * KV_DIM + 2 * D_CONV
CONV_WIDTH = 31
WINDOW = 128
BLOCK = 128
ROPE_THETA = 10000.0
N_META = 16
META_PAD = BLOCK - N_META
PEER_HEADS = 8
PEER_NKEYS = 128
PEER_TOPK = 16
PEER_SEL = PEER_HEADS * PEER_TOPK
EPS = 1e-6
PAST_LEN = 16384

LANES = 128
SC_CORES = 2
SC_SUBCORES = 16
SC_LANES = 16
SC_WORKERS = SC_CORES * SC_SUBCORES
VMEM_LIMIT = 48 * 1024 * 1024

F32 = jnp.float32
BF16 = jnp.bfloat16
NEG_INF = float("-inf")


def _tc_params(n_axes):
    return pltpu.CompilerParams(dimension_semantics=("arbitrary",) * n_axes,
                                vmem_limit_bytes=VMEM_LIMIT)


def _full(shape):
    nd = len(shape)
    return pl.BlockSpec(shape, lambda *_: (0,) * nd)


def _group_mean(sq, gsum_ref):
    hi = sq.astype(BF16)
    lo = (sq - hi.astype(F32)).astype(BF16)
    g = gsum_ref[...]
    s = jnp.dot(hi, g, preferred_element_type=F32) + jnp.dot(lo, g, preferred_element_type=F32)
    return s * (1.0 / HEAD_DIM)


def _rope(xn, cos_f, sin_s, first_half):
    outs = []
    for s in range(xn.shape[1] // LANES):
        xs = xn[:, s * LANES:(s + 1) * LANES]
        partner = jnp.where(first_half, pltpu.roll(xs, LANES - HEAD_DIM // 2, axis=1),
                            pltpu.roll(xs, HEAD_DIM // 2, axis=1))
        outs.append(xs * cos_f + partner * sin_s)
    return outs[0] if len(outs) == 1 else jnp.concatenate(outs, axis=1)


def _proj_kernel(x_ref, g_ref, w_ref, qg_ref, kg_ref, cos_ref, sin_ref, gq_ref, gk_ref,
                 q_out, k_out, v_out, u_out):
    x = x_ref[...]
    n = x * lax.rsqrt(jnp.mean(x * x, axis=-1, keepdims=True) + EPS) * g_ref[...]
    p = jnp.dot(n.astype(BF16), w_ref[...], preferred_element_type=F32)
    q = p[:, :D_ATTN]
    k = p[:, D_ATTN:D_ATTN + KV_DIM]
    v = p[:, D_ATTN + KV_DIM:D_ATTN + 2 * KV_DIM]
    ga = p[:, D_ATTN + 2 * KV_DIM:D_ATTN + 2 * KV_DIM + D_CONV]
    gb = p[:, D_ATTN + 2 * KV_DIM + D_CONV:]
    cos_f = cos_ref[...]
    sin_s = sin_ref[...]
    lane = lax.broadcasted_iota(jnp.int32, (x.shape[0], LANES), 1)
    first_half = (lane % HEAD_DIM) < (HEAD_DIM // 2)
    qn = q * lax.rsqrt(_group_mean(q * q, gq_ref) + EPS) * qg_ref[...]
    kn = k * lax.rsqrt(_group_mean(k * k, gk_ref) + EPS) * kg_ref[...]
    q_out[...] = _rope(qn, cos_f, sin_s, first_half)
    k_out[...] = _rope(kn, cos_f, sin_s, first_half)
    v_out[...] = v
    u_out[...] = ga * (1.0 / (1.0 + jnp.exp(-gb)))


def _project(x, pos_tables, n_table_blocks, tb, norm_g, w_in_bf, qg_t, kg_t, gsum_q, gsum_k):
    n = x.shape[0]
    cos_t, sin_t = pos_tables
    nb = n // tb
    tab_spec = pl.BlockSpec((tb, LANES), lambda i: (i % n_table_blocks, 0))
    row = lambda w: pl.BlockSpec((tb, w), lambda i: (i, 0))
    return pl.pallas_call(
        _proj_kernel,
        grid=(nb,),
        in_specs=[row(D_MODEL), _full((1, D_MODEL)), _full((D_MODEL, D_IN)),
                  _full((1, D_ATTN)), _full((1, KV_DIM)), tab_spec, tab_spec,
                  _full((D_ATTN, D_ATTN)), _full((KV_DIM, KV_DIM))],
        out_specs=[row(D_ATTN), row(KV_DIM), row(KV_DIM), row(D_CONV)],
        out_shape=[jax.ShapeDtypeStruct((n, D_ATTN), F32), jax.ShapeDtypeStruct((n, KV_DIM), F32),
                   jax.ShapeDtypeStruct((n, KV_DIM), F32), jax.ShapeDtypeStruct((n, D_CONV), F32)],
        compiler_params=_tc_params(1),
        name="proj",
    )(x, norm_g, w_in_bf, qg_t, kg_t, cos_t, sin_t, gsum_q, gsum_k)


def _attn_kernel(sink_ref, q_ref, kp_ref, kc_ref, vp_ref, vc_ref, o_ref):
    j = pl.program_id(1)
    r = lax.broadcasted_iota(jnp.int32, (BLOCK, 2 * BLOCK), 0)
    c = lax.broadcasted_iota(jnp.int32, (BLOCK, 2 * BLOCK), 1)
    ok = (c > r) & (c <= r + WINDOW) & ((j > 0) | (c >= META_PAD))
    grp = N_HEADS // N_KV_HEADS
    ok = jnp.concatenate([ok] * grp, axis=0)
    for g in range(N_KV_HEADS):
        q4 = q_ref[0, g * grp:(g + 1) * grp].reshape(grp * BLOCK, HEAD_DIM).astype(BF16)
        k = jnp.concatenate([kp_ref[0, g], kc_ref[0, g]], axis=0).astype(BF16)
        v = jnp.concatenate([vp_ref[0, g], vc_ref[0, g]], axis=0).astype(BF16)
        s = lax.dot_general(q4, k, (((1,), (1,)), ((), ())), preferred_element_type=F32)
        s = jnp.where(ok, s * (HEAD_DIM ** -0.5), NEG_INF)
        sink = jnp.concatenate(
            [jnp.full((BLOCK, 1), sink_ref[g * grp + i], F32) for i in range(grp)], axis=0)
        m = jnp.maximum(jnp.max(s, axis=1, keepdims=True), sink)
        p = jnp.exp(s - m)
        den = jnp.sum(p, axis=1, keepdims=True) + jnp.exp(sink - m)
        o = jnp.dot(p.astype(BF16), v, preferred_element_type=F32) / den
        o_ref[0, g * grp:(g + 1) * grp] = o.reshape(grp, BLOCK, HEAD_DIM)


def _prompt_attention(sinks, q_t, k_t, v_t):
    b, _, t, _ = q_t.shape
    nb = t // BLOCK
    kv_prev = pl.BlockSpec((1, N_KV_HEADS, BLOCK, HEAD_DIM), lambda bi, j: (bi, 0, j, 0))
    kv_cur = pl.BlockSpec((1, N_KV_HEADS, BLOCK, HEAD_DIM), lambda bi, j: (bi, 0, j + 1, 0))
    q_spec = pl.BlockSpec((1, N_HEADS, BLOCK, HEAD_DIM), lambda bi, j: (bi, 0, j, 0))
    return pl.pallas_call(
        _attn_kernel,
        grid=(b, nb),
        in_specs=[pl.BlockSpec(memory_space=pltpu.SMEM), q_spec, kv_prev, kv_cur, kv_prev, kv_cur],
        out_specs=q_spec,
        out_shape=jax.ShapeDtypeStruct(q_t.shape, F32),
        compiler_params=_tc_params(2),
        name="attn",
    )(sinks, q_t, k_t, k_t, v_t, v_t)


def _dec_attn_kernel(sink_ref, q_ref, kn_ref, vn_ref, ck_ref, cv_ref, o_ref):
    grp = N_HEADS // N_KV_HEADS
    sb, w_buf = ck_ref.shape[1], ck_ref.shape[2]
    q = q_ref[...]
    qb = q.astype(BF16)
    head = lax.broadcasted_iota(jnp.int32, (sb, N_HEADS, 1), 1)
    in_g0 = head < grp
    s_g = [jnp.einsum("shd,swd->shw", qb, ck_ref[g].astype(BF16), preferred_element_type=F32)
           for g in range(N_KV_HEADS)]
    s = jnp.where(in_g0, s_g[0], s_g[1]) * (HEAD_DIM ** -0.5)
    key_ok = lax.broadcasted_iota(jnp.int32, (sb, N_HEADS, w_buf), 2) >= 1
    s = jnp.where(key_ok, s, NEG_INF)
    rnd = lambda a: a.astype(BF16).astype(F32)
    kn = jnp.where(in_g0, kn_ref[0][:, None, :], kn_ref[1][:, None, :])
    vn = jnp.where(in_g0, vn_ref[0][:, None, :], vn_ref[1][:, None, :])
    s_self = jnp.sum(rnd(q) * rnd(kn), axis=-1, keepdims=True) * (HEAD_DIM ** -0.5)
    sink = sink_ref[...]
    m = jnp.maximum(jnp.maximum(jnp.max(s, axis=-1, keepdims=True), s_self), sink)
    p = jnp.exp(s - m)
    p_self = jnp.exp(s_self - m)
    den = jnp.sum(p, axis=-1, keepdims=True) + p_self + jnp.exp(sink - m)
    pb = p.astype(BF16)
    o_g = [jnp.einsum("shw,swd->shd", pb, cv_ref[g].astype(BF16), preferred_element_type=F32)
           for g in range(N_KV_HEADS)]
    o = jnp.where(in_g0, o_g[0], o_g[1]) + rnd(p_self) * rnd(vn)
    o_ref[...] = o / den


def _decode_attention(sinks, q3, kn_t, vn_t, ck_t, cv_t, sb):
    s = q3.shape[0]
    w_buf = ck_t.shape[2]
    qs = pl.BlockSpec((sb, N_HEADS, HEAD_DIM), lambda i: (i, 0, 0))
    ns = pl.BlockSpec((N_KV_HEADS, sb, HEAD_DIM), lambda i: (0, i, 0))
    cs = pl.BlockSpec((N_KV_HEADS, sb, w_buf, HEAD_DIM), lambda i: (0, i, 0, 0))
    return pl.pallas_call(
        _dec_attn_kernel,
        grid=(s // sb,),
        in_specs=[_full((1, N_HEADS, 1)), qs, ns, ns, cs, cs],
        out_specs=qs,
        out_shape=jax.ShapeDtypeStruct(q3.shape, F32),
        compiler_params=_tc_params(1),
        name="dec_attn",
    )(sinks.reshape(1, N_HEADS, 1), q3, kn_t, vn_t, ck_t, cv_t)


CONV_HALO = 32
CONV_ROWS = 64


def _conv_kernel(halo_ref, u_ref, cw_ref, cb_ref, lg_ref, lb_ref, w2_ref, o_ref, ucat):
    tb = u_ref.shape[0]
    ucat[0:CONV_HALO, :] = halo_ref[0]
    ucat[CONV_HALO:, :] = u_ref[...]
    first = CONV_HALO - (CONV_WIDTH - 1)
    rows = min(CONV_ROWS, tb)
    for r0 in range(0, tb, rows):
        acc = jnp.zeros((rows, D_CONV), F32)
        for j in range(CONV_WIDTH):
            acc = acc + ucat[r0 + first + j:r0 + first + j + rows, :] * cw_ref[j:j + 1, :]
        y = acc + cb_ref[...]
        yc = y - jnp.mean(y, axis=-1, keepdims=True)
        yn = yc * lax.rsqrt(jnp.mean(yc * yc, axis=-1, keepdims=True) + EPS)
        yn = yn * lg_ref[...] + lb_ref[...]
        act = yn * (1.0 / (1.0 + jnp.exp(-yn)))
        o_ref[r0:r0 + rows, :] = jnp.dot(act.astype(BF16), w2_ref[...],
                                              preferred_element_type=F32)


def _conv_branch(halo, u, tb, conv_w, conv_b, ln_g, ln_b, w_pw2_bf):
    n = u.shape[0]
    return pl.pallas_call(
        _conv_kernel,
        grid=(n // tb,),
        in_specs=[pl.BlockSpec((1, CONV_HALO, D_CONV), lambda i: (i, 0, 0)),
                  pl.BlockSpec((tb, D_CONV), lambda i: (i, 0)),
                  _full((CONV_WIDTH, D_CONV)), _full((1, D_CONV)), _full((1, D_CONV)),
                  _full((1, D_CONV)), _full((D_CONV, D_CONV))],
        out_specs=pl.BlockSpec((tb, D_CONV), lambda i: (i, 0)),
        out_shape=jax.ShapeDtypeStruct((n, D_CONV), F32),
        scratch_shapes=[pltpu.VMEM((tb + CONV_HALO, D_CONV), F32)],
        compiler_params=_tc_params(1),
        name="conv",
    )(halo, u, conv_w, conv_b, ln_g, ln_b, w_pw2_bf)


def _rms(x, g):
    return x * lax.rsqrt(jnp.mean(x * x, axis=-1, keepdims=True) + EPS) * g


def _merge_kernel(x_ref, oa_ref, oc_ref, ga_ref, gc_ref, wa_ref, wc_ref, gf_ref, h_out, hn_out):
    a = _rms(oa_ref[...], ga_ref[...]).astype(BF16)
    c = _rms(oc_ref[...], gc_ref[...]).astype(BF16)
    h = x_ref[...] + (jnp.dot(a, wa_ref[...], preferred_element_type=F32)
                      + jnp.dot(c, wc_ref[...], preferred_element_type=F32))
    h_out[...] = h
    hn_out[...] = _rms(h, gf_ref[...])


def _merge(x, oa, oc, tb, g_a, g_c, w_out_a, w_out_c, g_f):
    n = x.shape[0]
    row = lambda w: pl.BlockSpec((tb, w), lambda i: (i, 0))
    return pl.pallas_call(
        _merge_kernel,
        grid=(n // tb,),
        in_specs=[row(D_MODEL), row(D_ATTN), row(D_CONV), _full((1, D_ATTN)), _full((1, D_CONV)),
                  _full((D_ATTN, D_MODEL)), _full((D_CONV, D_MODEL)), _full((1, D_MODEL))],
        out_specs=[row(D_MODEL), row(D_MODEL)],
        out_shape=[jax.ShapeDtypeStruct((n, D_MODEL), F32)] * 2,
        compiler_params=_tc_params(1),
        name="merge",
    )(x, oa, oc, g_a, g_c, w_out_a, w_out_c, g_f)


ID_BIG = 1e9


def _topk_rows(s, k):
    rows = lax.broadcasted_iota(jnp.int32, s.shape, 0).astype(F32)
    vals, idxs = [], []
    for _ in range(k):
        m = jnp.max(s, axis=0, keepdims=True)
        idx = jnp.min(jnp.where(s == m, rows, ID_BIG), axis=0, keepdims=True)
        vals.append(m)
        idxs.append(idx)
        s = jnp.where(rows == idx, NEG_INF, s)
    return jnp.concatenate(vals, axis=0), jnp.concatenate(idxs, axis=0)


PAIR_B_WIDE = 8


def _route_kernel(hn_ref, wq_ref, k1_ref, k2_ref, eid_out, gate_out):
    tb = hn_ref.shape[0]
    q = jnp.dot(hn_ref[...].astype(BF16), wq_ref[...], preferred_element_type=F32).astype(BF16)
    k1 = k1_ref[...]
    k2 = k2_ref[...]
    nt = (((1,), (1,)), ((), ()))
    r = lax.broadcasted_iota(jnp.int32, (PEER_TOPK + (PAIR_B_WIDE - 1) * PAIR_B_WIDE + PAIR_B_WIDE, tb), 0)
    mid = r - PEER_TOPK
    flat = jnp.where(r < PEER_TOPK, r,
                     jnp.where(mid < (PAIR_B_WIDE - 1) * PAIR_B_WIDE,
                               (1 + mid // PAIR_B_WIDE) * PEER_TOPK + mid % PAIR_B_WIDE,
                               (PAIR_B_WIDE + mid - (PAIR_B_WIDE - 1) * PAIR_B_WIDE) * PEER_TOPK)).astype(F32)
    half = PEER_NKEYS
    for h in range(PEER_HEADS):
        q1 = q[:, (2 * h) * half:(2 * h + 1) * half]
        q2 = q[:, (2 * h + 1) * half:(2 * h + 2) * half]
        s1 = lax.dot_general(k1, q1, nt, preferred_element_type=F32)
        s2 = lax.dot_general(k2, q2, nt, preferred_element_type=F32)
        v1, i1 = _topk_rows(s1, PEER_TOPK)
        v2, i2 = _topk_rows(s2, PEER_TOPK)
        e1 = i1 * PEER_NKEYS
        cand = jnp.concatenate(
            [v1[0:1] + v2]
            + [v1[a:a + 1] + v2[0:PAIR_B_WIDE] for a in range(1, PAIR_B_WIDE)]
            + [v1[PAIR_B_WIDE:] + v2[0:1]], axis=0)
        cid = jnp.concatenate(
            [e1[0:1] + i2]
            + [e1[a:a + 1] + i2[0:PAIR_B_WIDE] for a in range(1, PAIR_B_WIDE)]
            + [e1[PAIR_B_WIDE:] + i2[0:1]], axis=0)
        scs, eids = [], []
        for _ in range(PEER_TOPK):
            m = jnp.max(cand, axis=0, keepdims=True)
            jsel = jnp.min(jnp.where(cand == m, flat, ID_BIG), axis=0, keepdims=True)
            hit = flat == jsel
            eids.append(jnp.max(jnp.where(hit, cid, -1.0), axis=0, keepdims=True))
            scs.append(m)
            cand = jnp.where(hit, NEG_INF, cand)
        sc = jnp.concatenate(scs, axis=0)
        e = jnp.exp(sc - sc[0:1])
        gate_out[h * PEER_TOPK:(h + 1) * PEER_TOPK, :] = e / jnp.sum(e, axis=0, keepdims=True)
        eid_out[h * PEER_TOPK:(h + 1) * PEER_TOPK, :] = jnp.concatenate(eids, axis=0).astype(jnp.int32)


def _route(hn, tb, wq_bf, keys1_bf, keys2_bf):
    n = hn.shape[0]
    col = pl.BlockSpec((PEER_SEL, tb), lambda i: (0, i))
    return pl.pallas_call(
        _route_kernel,
        grid=(n // tb,),
        in_specs=[pl.BlockSpec((tb, D_MODEL), lambda i: (i, 0)),
                  _full((D_MODEL, 2 * PEER_NKEYS * PEER_HEADS)),
                  _full((PEER_NKEYS, PEER_NKEYS)), _full((PEER_NKEYS, PEER_NKEYS))],
        out_specs=[col, col],
        out_shape=[jax.ShapeDtypeStruct((PEER_SEL, n), jnp.int32),
                   jax.ShapeDtypeStruct((PEER_SEL, n), F32)],
        compiler_params=_tc_params(1),
        name="route",
    )(hn, wq_bf, keys1_bf, keys2_bf)


SC_ROWS = 32
SC_GROUP = SC_LANES
SC_TOKENS = 8
SC_CHUNKS = PEER_SEL // SC_ROWS

_SC_PARAMS = pltpu.CompilerParams(needs_layout_passes=False)


def _sc_mesh():
    return plsc.VectorSubcoreMesh(core_axis_name="c", subcore_axis_name="s")


def _sc_token_loop(n_batches, table_hbm, eid_v, rbufs, rsems, prefetch, prefetch_wait, store,
                   contract):
    def gather(slot, tl, ch, p):
        idx = eid_v.at[slot, tl, pl.ds(ch * SC_ROWS, SC_ROWS)]
        return pltpu.make_async_copy(table_hbm.at[idx], rbufs[p], rsems[p])

    prefetch(0, 0)
    prefetch_wait()
    gather(0, 0, 0, 0).start()

    @pl.loop(0, n_batches * SC_TOKENS)
    def _(i):
        b = i // SC_TOKENS
        tl = i % SC_TOKENS
        slot = b % 2
        more = b + 1 < n_batches

        @pl.when(jnp.logical_and(tl == 0, more))
        def _():
            prefetch(b + 1, 1 - slot)

        for ch in range(SC_CHUNKS):
            p = ch % 2
            if ch + 1 < SC_CHUNKS:
                gather(slot, tl, ch + 1, 1 - p).start()
            else:
                @pl.when(tl + 1 < SC_TOKENS)
                def _():
                    gather(slot, tl + 1, 0, 1 - p).start()

                @pl.when(jnp.logical_and(tl + 1 == SC_TOKENS, more))
                def _():
                    prefetch_wait()
                    gather(1 - slot, 0, 0, 1 - p).start()
            gather(slot, tl, ch, p).wait()
            contract(slot, tl, ch, rbufs[p])

        @pl.when(tl + 1 == SC_TOKENS)
        def _():
            @pl.when(b >= 1)
            def _():
                store(b - 1, 1 - slot).wait()
            store(b, slot).start()

    store(n_batches - 1, (n_batches - 1) % 2).wait()


def _sc_batches(n):
    assert n % (SC_WORKERS * SC_TOKENS) == 0, n
    return n // (SC_WORKERS * SC_TOKENS)


def _peer_hval(hn, eid, peer_u):
    n = hn.shape[0]
    nbw = _sc_batches(n)

    @functools.partial(
        pl.kernel, mesh=_sc_mesh(),
        out_type=jax.ShapeDtypeStruct((n // SC_TOKENS, SC_TOKENS, PEER_SEL), F32),
        scratch_types=[
            pltpu.VMEM((2, SC_TOKENS, PEER_SEL), jnp.int32),
            pltpu.VMEM((2, SC_TOKENS, D_MODEL), F32),
            pltpu.VMEM((SC_ROWS, D_MODEL), F32),
            pltpu.VMEM((SC_ROWS, D_MODEL), F32),
            pltpu.VMEM((2, SC_TOKENS, PEER_SEL), F32),
            pltpu.VMEM((SC_GROUP, SC_LANES), F32),
            pltpu.SemaphoreType.DMA, pltpu.SemaphoreType.DMA,
            pltpu.SemaphoreType.DMA, pltpu.SemaphoreType.DMA,
        ],
        compiler_params=_SC_PARAMS, name="peer_hval")
    def k(x_hbm, eid_hbm, u_hbm, o_hbm, eid_v, x_v, r0, r1, h_v, tr, sr0, sr1, spf, sout):
        blk0 = (lax.axis_index("s") * SC_CORES + lax.axis_index("c")) * nbw
        lane = lax.iota(jnp.int32, SC_LANES)
        zero = jnp.zeros((SC_LANES,), F32)

        def prefetch(b, slot):
            pltpu.async_copy(eid_hbm.at[blk0 + b], eid_v.at[slot], spf)
            pltpu.async_copy(x_hbm.at[blk0 + b], x_v.at[slot], spf)

        def prefetch_wait():
            pltpu.make_async_copy(eid_hbm.at[0], eid_v.at[0], spf).wait()
            pltpu.make_async_copy(x_hbm.at[0], x_v.at[0], spf).wait()

        def store(b, slot):
            return pltpu.make_async_copy(h_v.at[slot], o_hbm.at[blk0 + b], sout)

        def contract(slot, tl, ch, rbuf):
            for g in range(SC_ROWS // SC_GROUP):
                def body(c, accs):
                    xc = x_v[slot, tl, pl.ds(c * SC_LANES, SC_LANES)]
                    return tuple(
                        accs[r] + rbuf[g * SC_GROUP + r, pl.ds(c * SC_LANES, SC_LANES)] * xc
                        for r in range(SC_GROUP))
                accs = lax.fori_loop(0, D_MODEL // SC_LANES, body, (zero,) * SC_GROUP)
                for r in range(SC_GROUP):
                    tr[r, :] = accs[r]
                res = zero
                for jj in range(SC_LANES):
                    res = res + plsc.load_gather(tr, [lane, jnp.full((SC_LANES,), jj, jnp.int32)])
                h_v[slot, tl, pl.ds(ch * SC_ROWS + g * SC_GROUP, SC_GROUP)] = res

        _sc_token_loop(nbw, u_hbm, eid_v, (r0, r1), (sr0, sr1), prefetch, prefetch_wait, store,
                       contract)

    out = k(hn.reshape(n // SC_TOKENS, SC_TOKENS, D_MODEL),
            eid.reshape(n // SC_TOKENS, SC_TOKENS, PEER_SEL), peer_u)
    return out.reshape(n, PEER_SEL)


def _peer_out(h, a, eid, peer_v):
    n = h.shape[0]
    nbw = _sc_batches(n)

    @functools.partial(
        pl.kernel, mesh=_sc_mesh(),
        out_type=jax.ShapeDtypeStruct((n // SC_TOKENS, SC_TOKENS, D_MODEL), F32),
        scratch_types=[
            pltpu.VMEM((2, SC_TOKENS, PEER_SEL), jnp.int32),
            pltpu.VMEM((2, SC_TOKENS, PEER_SEL), F32),
            pltpu.VMEM((SC_ROWS, D_MODEL), F32),
            pltpu.VMEM((SC_ROWS, D_MODEL), F32),
            pltpu.VMEM((2, SC_TOKENS, D_MODEL), F32),
            pltpu.SemaphoreType.DMA, pltpu.SemaphoreType.DMA,
            pltpu.SemaphoreType.DMA, pltpu.SemaphoreType.DMA,
        ],
        compiler_params=_SC_PARAMS, name="peer_out")
    def k(h_hbm, a_hbm, eid_hbm, v_hbm, o_hbm, eid_v, a_v, r0, r1, y_v, sr0, sr1, spf, sout):
        blk0 = (lax.axis_index("s") * SC_CORES + lax.axis_index("c")) * nbw

        def prefetch(b, slot):
            pltpu.async_copy(eid_hbm.at[blk0 + b], eid_v.at[slot], spf)
            pltpu.async_copy(a_hbm.at[blk0 + b], a_v.at[slot], spf)
            pltpu.async_copy(h_hbm.at[blk0 + b], y_v.at[slot], spf)

        def prefetch_wait():
            pltpu.make_async_copy(eid_hbm.at[0], eid_v.at[0], spf).wait()
            pltpu.make_async_copy(a_hbm.at[0], a_v.at[0], spf).wait()
            pltpu.make_async_copy(h_hbm.at[0], y_v.at[0], spf).wait()

        def store(b, slot):
            return pltpu.make_async_copy(y_v.at[slot], o_hbm.at[blk0 + b], sout)

        def contract(slot, tl, ch, rbuf):
            slot_v = jnp.full((SC_LANES,), slot, jnp.int32)
            tl_v = jnp.full((SC_LANES,), tl, jnp.int32)
            for g in range(SC_ROWS // SC_GROUP):
                base = ch * SC_ROWS + g * SC_GROUP
                coef = [plsc.load_gather(a_v, [slot_v, tl_v, jnp.full((SC_LANES,), base + r, jnp.int32)])
                        for r in range(SC_GROUP)]

                @plsc.parallel_loop(0, D_MODEL // SC_LANES)
                def _(c):
                    sl = pl.ds(c * SC_LANES, SC_LANES)
                    acc = coef[0] * rbuf[g * SC_GROUP, sl]
                    for r in range(1, SC_GROUP):
                        acc = acc + coef[r] * rbuf[g * SC_GROUP + r, sl]
                    plsc.addupdate(y_v.at[slot, tl, sl], acc)

        _sc_token_loop(nbw, v_hbm, eid_v, (r0, r1), (sr0, sr1), prefetch, prefetch_wait, store,
                       contract)

    out = k(h.reshape(n // SC_TOKENS, SC_TOKENS, D_MODEL),
            a.reshape(n // SC_TOKENS, SC_TOKENS, PEER_SEL),
            eid.reshape(n // SC_TOKENS, SC_TOKENS, PEER_SEL), peer_v)
    return out.reshape(n, D_MODEL)


def _gate_kernel(hv_ref, gate_ref, a_out):
    hv = hv_ref[...]
    gelu = hv * (lax.erf(hv * (2.0 ** -0.5)) + 1.0) * 0.5
    a_out[...] = gate_ref[...] * gelu


def _gate(hval, gate, tb):
    n = hval.shape[0]
    row = pl.BlockSpec((tb, PEER_SEL), lambda i: (i, 0))
    return pl.pallas_call(
        _gate_kernel, grid=(n // tb,), in_specs=[row, row], out_specs=row,
        out_shape=jax.ShapeDtypeStruct((n, PEER_SEL), F32),
        compiler_params=_tc_params(1), name="gate",
    )(hval, gate)


VT_ROWS = D_MODEL // 2 // LANES
VT_TOKENS = 64
HI_MASK = -65536


def _pack_value_table(peer_v):
    e = peer_v.shape[0]
    bits = lax.bitcast_convert_type(peer_v.astype(BF16), jnp.uint16).astype(jnp.uint32)
    words = bits[:, :D_MODEL // 2] | (bits[:, D_MODEL // 2:] << 16)
    return lax.bitcast_convert_type(words, jnp.int32).reshape(e * VT_ROWS, LANES)


def _vside_kernel(row_s, a_s, h_ref, tab_ref, y_ref):
    tb = h_ref.shape[0]

    def token(t, carry):
        hv = h_ref[t]
        zero = jnp.zeros((VT_ROWS, LANES), F32)
        lo = [hv[0:VT_ROWS], zero]
        hi = [hv[VT_ROWS:], zero]
        for e in range(PEER_SEL):
            r0 = pl.multiple_of(row_s[t, e], VT_ROWS)
            w = tab_ref[pl.ds(r0, VT_ROWS), :]
            coef = a_s[t, e]
            c = e % 2
            lo[c] = lo[c] + coef * pltpu.bitcast(w << 16, F32)
            hi[c] = hi[c] + coef * pltpu.bitcast(w & HI_MASK, F32)
        y_ref[t] = jnp.concatenate([lo[0] + lo[1], hi[0] + hi[1]], axis=0)
        return carry

    lax.fori_loop(0, tb, token, 0)


def _peer_values(h, a, row_ids, v_packed):
    n = h.shape[0]
    tb = min(VT_TOKENS, n)
    smem = pl.BlockSpec((tb, PEER_SEL), lambda i: (i, 0), memory_space=pltpu.SMEM)
    slab = pl.BlockSpec((tb, 2 * VT_ROWS, LANES), lambda i: (i, 0, 0))
    table = pl.BlockSpec(v_packed.shape, lambda i: (0, 0), pipeline_mode=pl.Buffered(1))
    table_bytes = v_packed.shape[0] * LANES * 4
    y = pl.pallas_call(
        _vside_kernel,
        grid=(n // tb,),
        in_specs=[smem, smem, slab, table],
        out_specs=slab,
        out_shape=jax.ShapeDtypeStruct((n, 2 * VT_ROWS, LANES), F32),
        compiler_params=pltpu.CompilerParams(
            dimension_semantics=("arbitrary",),
            vmem_limit_bytes=table_bytes + 8 * 1024 * 1024),
        name="peer_values",
    )(row_ids, a, h.reshape(n, 2 * VT_ROWS, LANES), v_packed)
    return y.reshape(n, D_MODEL)


PEER_CHUNK = 4096


def _peer(h, hn, tb, **weights):
    n = h.shape[0]
    if n <= PEER_CHUNK:
        return _peer_chunk(h, hn, tb, **weights)
    return jnp.concatenate(
        [_peer_chunk(h[c:c + PEER_CHUNK], hn[c:c + PEER_CHUNK], tb, **weights)
         for c in range(0, n, PEER_CHUNK)], axis=0)


def _peer_chunk(h, hn, tb, wq_bf, keys1_bf, keys2_bf, peer_u, v_packed):
    n = h.shape[0]
    eid_t, gate_t = _route(hn, tb, wq_bf, keys1_bf, keys2_bf)
    eid = eid_t.T
    pad = (-n) % (SC_WORKERS * SC_TOKENS)
    hn_p, eid_p = hn, eid
    if pad:
        spread = (jnp.arange(pad * PEER_SEL, dtype=jnp.int32) % peer_u.shape[0]).reshape(pad, PEER_SEL)
        hn_p = jnp.pad(hn, ((0, pad), (0, 0)))
        eid_p = jnp.concatenate([eid, spread], axis=0)
    hval = _peer_hval(hn_p, eid_p, peer_u)[:n]
    a = _gate(hval, gate_t.T, min(tb, 512))
    return _peer_values(h, a, eid * VT_ROWS, v_packed)


def _rope_tables(pos):
    half = HEAD_DIM // 2
    inv = ROPE_THETA ** (-jnp.arange(half, dtype=F32) / half)
    ang = pos.astype(F32)[:, None] * inv[None, :]
    cos = jnp.cos(ang)
    sin = jnp.sin(ang)
    reps = LANES // HEAD_DIM
    cos_f = jnp.tile(jnp.concatenate([cos, cos], axis=1), (1, reps))
    sin_s = jnp.tile(jnp.concatenate([-sin, sin], axis=1), (1, reps))
    return cos_f, sin_s


def _group_sum_matrix(width):
    g = jnp.arange(width) // HEAD_DIM
    return (g[:, None] == g[None, :]).astype(BF16)


def _heads_first(a, heads):
    b, t, _ = a.shape
    return a.reshape(b, t, heads, HEAD_DIM).transpose(0, 2, 1, 3)


def kernel(x_prompt, x_sample, cache_k_win, cache_v_win, state_conv, meta_tokens, norm_mix_g,
           w_in, q_norm_g, k_norm_g, attn_sinks, conv_w, conv_b, conv_ln_g, conv_ln_b, w_pw2,
           out_norm_attn_g, out_norm_conv_g, w_out, norm_ffn_g, peer_w_q, peer_keys1, peer_keys2,
           peer_u, peer_v):
    assert norm_mix_g.shape[0] == 1, "single-layer model"
    b, t, _ = x_prompt.shape
    s = x_sample.shape[0]
    w_buf = cache_k_win.shape[2]
    n = b * t

    g_mix = norm_mix_g[0][None, :]
    w_in_bf = w_in[0].astype(BF16)
    qg_t = jnp.tile(q_norm_g[0], N_HEADS)[None, :]
    kg_t = jnp.tile(k_norm_g[0], N_KV_HEADS)[None, :]
    gsum_q = _group_sum_matrix(D_ATTN)
    gsum_k = _group_sum_matrix(KV_DIM)
    sinks = attn_sinks[0]
    cw, cb = conv_w[0], conv_b[0][None, :]
    lg, lb = conv_ln_g[0][None, :], conv_ln_b[0][None, :]
    w2_bf = w_pw2[0].astype(BF16)
    g_a, g_c = out_norm_attn_g[0][None, :], out_norm_conv_g[0][None, :]
    w_out_a = w_out[0][:D_ATTN].astype(BF16)
    w_out_c = w_out[0][D_ATTN:].astype(BF16)
    g_f = norm_ffn_g[0][None, :]
    wq_bf = peer_w_q[0].astype(BF16)
    k1_bf = peer_keys1[0].astype(BF16)
    k2_bf = peer_keys2[0].astype(BF16)
    pu, vp = peer_u[0], _pack_value_table(peer_v[0])
    proj = functools.partial(_project, norm_g=g_mix, w_in_bf=w_in_bf, qg_t=qg_t, kg_t=kg_t,
                             gsum_q=gsum_q, gsum_k=gsum_k)
    conv = functools.partial(_conv_branch, conv_w=cw, conv_b=cb, ln_g=lg, ln_b=lb, w_pw2_bf=w2_bf)
    merge = functools.partial(_merge, g_a=g_a, g_c=g_c, w_out_a=w_out_a, w_out_c=w_out_c, g_f=g_f)
    peer = functools.partial(_peer, wq_bf=wq_bf, keys1_bf=k1_bf, keys2_bf=k2_bf, peer_u=pu, v_packed=vp)

    tb = 512
    xp = x_prompt.reshape(n, D_MODEL)
    tab_x = _rope_tables(N_META + jnp.arange(t, dtype=jnp.int32))
    q, k, v, u = proj(xp, tab_x, t // tb, tb)
    tab_m = _rope_tables(jnp.arange(N_META, dtype=jnp.int32))
    _, k_m, v_m, u_m = proj(meta_tokens, tab_m, 1, N_META)

    def with_meta_block(a, a_m):
        blk0 = jnp.concatenate([jnp.zeros((META_PAD, KV_DIM), F32), a_m], axis=0)
        full = jnp.concatenate([jnp.broadcast_to(blk0[None], (b, BLOCK, KV_DIM)),
                                a.reshape(b, t, KV_DIM)], axis=1)
        return _heads_first(full, N_KV_HEADS)

    o_attn = _prompt_attention(sinks, _heads_first(q.reshape(b, t, D_ATTN), N_HEADS),
                               with_meta_block(k, k_m), with_meta_block(v, v_m))
    o_attn = o_attn.transpose(0, 2, 1, 3).reshape(n, D_ATTN)

    u3 = u.reshape(b, t // tb, tb, D_CONV)
    halo0 = jnp.concatenate([jnp.zeros((CONV_HALO - N_META, D_CONV), F32), u_m], axis=0)
    halo = jnp.concatenate([jnp.broadcast_to(halo0[None, None], (b, 1, CONV_HALO, D_CONV)),
                            u3[:, :-1, tb - CONV_HALO:]], axis=1).reshape(n // tb, CONV_HALO, D_CONV)
    o_conv = conv(halo, u, tb)

    h, hn = merge(xp, o_attn, o_conv, tb)
    y_prompt = peer(h, hn, 256).reshape(b, t, D_MODEL)

    new_k_prompt = k.reshape(b, t, N_KV_HEADS, HEAD_DIM)[None, :, t - WINDOW:]
    new_v_prompt = v.reshape(b, t, N_KV_HEADS, HEAD_DIM)[None, :, t - WINDOW:]
    new_conv_prompt = u.reshape(b, t, D_CONV)[None, :, t - (CONV_WIDTH - 1):]

    xs = x_sample.reshape(s, D_MODEL)
    tab_s = _rope_tables(jnp.full((s,), PAST_LEN, jnp.int32))
    qs, ks, vs, us = proj(xs, tab_s, 1, s)
    ck = cache_k_win[0]
    cv = cache_v_win[0]
    o_attn_s = _decode_attention(
        sinks,
        qs.reshape(s, N_HEADS, HEAD_DIM),
        ks.reshape(s, N_KV_HEADS, HEAD_DIM).transpose(1, 0, 2),
        vs.reshape(s, N_KV_HEADS, HEAD_DIM).transpose(1, 0, 2),
        ck.transpose(2, 0, 1, 3), cv.transpose(2, 0, 1, 3), 32)
    o_attn_s = o_attn_s.reshape(s, D_ATTN)
    cs = state_conv[0]
    hist = jnp.concatenate([jnp.zeros((s, CONV_HALO - (CONV_WIDTH - 1), D_CONV), F32), cs], axis=1)
    us_blk = jnp.concatenate([us[:, None, :], jnp.zeros((s, 7, D_CONV), F32)], axis=1)
    o_conv_s = conv(hist, us_blk.reshape(s * 8, D_CONV), 8).reshape(s, 8, D_CONV)[:, 0]
    hs, hns = merge(xs, o_attn_s, o_conv_s, s)
    y_sample = peer(hs, hns, s).reshape(s, 1, D_MODEL)

    new_k_sample = jnp.concatenate([ck[:, 1:], ks.reshape(s, 1, N_KV_HEADS, HEAD_DIM)], axis=1)[None]
    new_v_sample = jnp.concatenate([cv[:, 1:], vs.reshape(s, 1, N_KV_HEADS, HEAD_DIM)], axis=1)[None]
    new_conv_sample = jnp.concatenate([cs[:, 1:], us[:, None, :]], axis=1)[None]
    if w_buf != WINDOW:
        raise NotImplementedError("cache window shorter than the attention window")

    return (y_prompt, y_sample, new_k_prompt, new_v_prompt, new_conv_prompt,
            new_k_sample, new_v_sample, new_conv_sample)
```

```python
import functools

import jax
import jax.numpy as jnp
from jax import lax
from jax.experimental import pallas as pl
from jax.experimental.pallas import tpu as pltpu
from jax.experimental.pallas import tpu_sc as plsc

D_MODEL = 1024
HEAD_DIM = 64
D_ATTN = 512
N_HEADS = 8
N_KV_HEADS = 2
KV_DIM = N_KV_HEADS * HEAD_DIM
D_CONV = 512
D_IN = D_ATTN + 2 * KV_DIM + 2 * D_CONV
CONV_WIDTH = 31
WINDOW = 128
BLOCK = 128
ROPE_THETA = 10000.0
N_META = 16
META_PAD = BLOCK - N_META
PEER_HEADS = 8
PEER_NKEYS = 128
PEER_TOPK = 16
PEER_SEL = PEER_HEADS * PEER_TOPK
EPS = 1e-6
PAST_LEN = 16384

LANES = 128
SC_CORES = 2
SC_SUBCORES = 16
SC_LANES = 16
SC_WORKERS = SC_CORES * SC_SUBCORES
VMEM_LIMIT = 48 * 1024 * 1024

F32 = jnp.float32
BF16 = jnp.bfloat16
NEG_INF = float("-inf")


def _tc_params(n_axes):
    return pltpu.CompilerParams(dimension_semantics=("arbitrary",) * n_axes,
                                vmem_limit_bytes=VMEM_LIMIT)


def _full(shape):
    nd = len(shape)
    return pl.BlockSpec(shape, lambda *_: (0,) * nd)


def _group_mean(sq, gsum_ref):
    hi = sq.astype(BF16)
    lo = (sq - hi.astype(F32)).astype(BF16)
    g = gsum_ref[...]
    s = jnp.dot(hi, g, preferred_element_type=F32) + jnp.dot(lo, g, preferred_element_type=F32)
    return s * (1.0 / HEAD_DIM)


def _rope(xn, cos_f, sin_s, first_half):
    outs = []
    for s in range(xn.shape[1] // LANES):
        xs = xn[:, s * LANES:(s + 1) * LANES]
        partner = jnp.where(first_half, pltpu.roll(xs, LANES - HEAD_DIM // 2, axis=1),
                            pltpu.roll(xs, HEAD_DIM // 2, axis=1))
        outs.append(xs * cos_f + partner * sin_s)
    return outs[0] if len(outs) == 1 else jnp.concatenate(outs, axis=1)


def _proj_kernel(x_ref, g_ref, w_ref, qg_ref, kg_ref, cos_ref, sin_ref, gq_ref, gk_ref,
                 q_out, k_out, v_out, u_out):
    x = x_ref[...]
    n = x * lax.rsqrt(jnp.mean(x * x, axis=-1, keepdims=True) + EPS) * g_ref[...]
    p = jnp.dot(n.astype(BF16), w_ref[...], preferred_element_type=F32)
    q = p[:, :D_ATTN]
    k = p[:, D_ATTN:D_ATTN + KV_DIM]
    v = p[:, D_ATTN + KV_DIM:D_ATTN + 2 * KV_DIM]
    ga = p[:, D_ATTN + 2 * KV_DIM:D_ATTN + 2 * KV_DIM + D_CONV]
    gb = p[:, D_ATTN + 2 * KV_DIM + D_CONV:]
    cos_f = cos_ref[...]
    sin_s = sin_ref[...]
    lane = lax.broadcasted_iota(jnp.int32, (x.shape[0], LANES), 1)
    first_half = (lane % HEAD_DIM) < (HEAD_DIM // 2)
    qn = q * lax.rsqrt(_group_mean(q * q, gq_ref) + EPS) * qg_ref[...]
    kn = k * lax.rsqrt(_group_mean(k * k, gk_ref) + EPS) * kg_ref[...]
    q_out[...] = _rope(qn, cos_f, sin_s, first_half)
    k_out[...] = _rope(kn, cos_f, sin_s, first_half)
    v_out[...] = v
    u_out[...] = ga * (1.0 / (1.0 + jnp.exp(-gb)))


def _project(x, pos_tables, n_table_blocks, tb, norm_g, w_in_bf, qg_t, kg_t, gsum_q, gsum_k):
    n = x.shape[0]
    cos_t, sin_t = pos_tables
    nb = n // tb
    tab_spec = pl.BlockSpec((tb, LANES), lambda i: (i % n_table_blocks, 0))
    row = lambda w: pl.BlockSpec((tb, w), lambda i: (i, 0))
    return pl.pallas_call(
        _proj_kernel,
        grid=(nb,),
        in_specs=[row(D_MODEL), _full((1, D_MODEL)), _full((D_MODEL, D_IN)),
                  _full((1, D_ATTN)), _full((1, KV_DIM)), tab_spec, tab_spec,
                  _full((D_ATTN, D_ATTN)), _full((KV_DIM, KV_DIM))],
        out_specs=[row(D_ATTN), row(KV_DIM), row(KV_DIM), row(D_CONV)],
        out_shape=[jax.ShapeDtypeStruct((n, D_ATTN), F32), jax.ShapeDtypeStruct((n, KV_DIM), F32),
                   jax.ShapeDtypeStruct((n, KV_DIM), F32), jax.ShapeDtypeStruct((n, D_CONV), F32)],
        compiler_params=_tc_params(1),
        name="proj",
    )(x, norm_g, w_in_bf, qg_t, kg_t, cos_t, sin_t, gsum_q, gsum_k)


def _attn_kernel(sink_ref, q_ref, kp_ref, kc_ref, vp_ref, vc_ref, o_ref):
    j = pl.program_id(1)
    r = lax.broadcasted_iota(jnp.int32, (BLOCK, 2 * BLOCK), 0)
    c = lax.broadcasted_iota(jnp.int32, (BLOCK, 2 * BLOCK), 1)
    ok = (c > r) & (c <= r + WINDOW) & ((j > 0) | (c >= META_PAD))
    grp = N_HEADS // N_KV_HEADS
    ok = jnp.concatenate([ok] * grp, axis=0)
    for g in range(N_KV_HEADS):
        q4 = q_ref[0, g * grp:(g + 1) * grp].reshape(grp * BLOCK, HEAD_DIM).astype(BF16)
        k = jnp.concatenate([kp_ref[0, g], kc_ref[0, g]], axis=0).astype(BF16)
        v = jnp.concatenate([vp_ref[0, g], vc_ref[0, g]], axis=0).astype(BF16)
        s = lax.dot_general(q4, k, (((1,), (1,)), ((), ())), preferred_element_type=F32)
        s = jnp.where(ok, s * (HEAD_DIM ** -0.5), NEG_INF)
        sink = jnp.concatenate(
            [jnp.full((BLOCK, 1), sink_ref[g * grp + i], F32) for i in range(grp)], axis=0)
        m = jnp.maximum(jnp.max(s, axis=1, keepdims=True), sink)
        p = jnp.exp(s - m)
        den = jnp.sum(p, axis=1, keepdims=True) + jnp.exp(sink - m)
        o = jnp.dot(p.astype(BF16), v, preferred_element_type=F32) / den
        o_ref[0, g * grp:(g + 1) * grp] = o.reshape(grp, BLOCK, HEAD_DIM)


def _prompt_attention(sinks, q_t, k_t, v_t):
    b, _, t, _ = q_t.shape
    nb = t // BLOCK
    kv_prev = pl.BlockSpec((1, N_KV_HEADS, BLOCK, HEAD_DIM), lambda bi, j: (bi, 0, j, 0))
    kv_cur = pl.BlockSpec((1, N_KV_HEADS, BLOCK, HEAD_DIM), lambda bi, j: (bi, 0, j + 1, 0))
    q_spec = pl.BlockSpec((1, N_HEADS, BLOCK, HEAD_DIM), lambda bi, j: (bi, 0, j, 0))
    return pl.pallas_call(
        _attn_kernel,
        grid=(b, nb),
        in_specs=[pl.BlockSpec(memory_space=pltpu.SMEM), q_spec, kv_prev, kv_cur, kv_prev, kv_cur],
        out_specs=q_spec,
        out_shape=jax.ShapeDtypeStruct(q_t.shape, F32),
        compiler_params=_tc_params(2),
        name="attn",
    )(sinks, q_t, k_t, k_t, v_t, v_t)


def _dec_attn_kernel(sink_ref, q_ref, kn_ref, vn_ref, ck_ref, cv_ref, o_ref):
    grp = N_HEADS // N_KV_HEADS
    sb, w_buf = ck_ref.shape[1], ck_ref.shape[2]
    q = q_ref[...]
    qb = q.astype(BF16)
    head = lax.broadcasted_iota(jnp.int32, (sb, N_HEADS, 1), 1)
    in_g0 = head < grp
    s_g = [jnp.einsum("shd,swd->shw", qb, ck_ref[g].astype(BF16), preferred_element_type=F32)
           for g in range(N_KV_HEADS)]
    s = jnp.where(in_g0, s_g[0], s_g[1]) * (HEAD_DIM ** -0.5)
    key_ok = lax.broadcasted_iota(jnp.int32, (sb, N_HEADS, w_buf), 2) >= 1
    s = jnp.where(key_ok, s, NEG_INF)
    rnd = lambda a: a.astype(BF16).astype(F32)
    kn = jnp.where(in_g0, kn_ref[0][:, None, :], kn_ref[1][:, None, :])
    vn = jnp.where(in_g0, vn_ref[0][:, None, :], vn_ref[1][:, None, :])
    s_self = jnp.sum(rnd(q) * rnd(kn), axis=-1, keepdims=True) * (HEAD_DIM ** -0.5)
    sink = sink_ref[...]
    m = jnp.maximum(jnp.maximum(jnp.max(s, axis=-1, keepdims=True), s_self), sink)
    p = jnp.exp(s - m)
    p_self = jnp.exp(s_self - m)
    den = jnp.sum(p, axis=-1, keepdims=True) + p_self + jnp.exp(sink - m)
    pb = p.astype(BF16)
    o_g = [jnp.einsum("shw,swd->shd", pb, cv_ref[g].astype(BF16), preferred_element_type=F32)
           for g in range(N_KV_HEADS)]
    o = jnp.where(in_g0, o_g[0], o_g[1]) + rnd(p_self) * rnd(vn)
    o_ref[...] = o / den


def _decode_attention(sinks, q3, kn_t, vn_t, ck_t, cv_t, sb):
    s = q3.shape[0]
    w_buf = ck_t.shape[2]
    qs = pl.BlockSpec((sb, N_HEADS, HEAD_DIM), lambda i: (i, 0, 0))
    ns = pl.BlockSpec((N_KV_HEADS, sb, HEAD_DIM), lambda i: (0, i, 0))
    cs = pl.BlockSpec((N_KV_HEADS, sb, w_buf, HEAD_DIM), lambda i: (0, i, 0, 0))
    return pl.pallas_call(
        _dec_attn_kernel,
        grid=(s // sb,),
        in_specs=[_full((1, N_HEADS, 1)), qs, ns, ns, cs, cs],
        out_specs=qs,
        out_shape=jax.ShapeDtypeStruct(q3.shape, F32),
        compiler_params=_tc_params(1),
        name="dec_attn",
    )(sinks.reshape(1, N_HEADS, 1), q3, kn_t, vn_t, ck_t, cv_t)


CONV_HALO = 32
CONV_ROWS = 64


def _conv_kernel(halo_ref, u_ref, cw_ref, cb_ref, lg_ref, lb_ref, w2_ref, o_ref, ucat):
    tb = u_ref.shape[0]
    ucat[0:CONV_HALO, :] = halo_ref[0]
    ucat[CONV_HALO:, :] = u_ref[...]
    first = CONV_HALO - (CONV_WIDTH - 1)
    rows = min(CONV_ROWS, tb)
    for r0 in range(0, tb, rows):
        acc = jnp.zeros((rows, D_CONV), F32)
        for j in range(CONV_WIDTH):
            acc = acc + ucat[r0 + first + j:r0 + first + j + rows, :] * cw_ref[j:j + 1, :]
        y = acc + cb_ref[...]
        yc = y - jnp.mean(y, axis=-1, keepdims=True)
        yn = yc * lax.rsqrt(jnp.mean(yc * yc, axis=-1, keepdims=True) + EPS)
        yn = yn * lg_ref[...] + lb_ref[...]
        act = yn * (1.0 / (1.0 + jnp.exp(-yn)))
        o_ref[r0:r0 + rows, :] = jnp.dot(act.astype(BF16), w2_ref[...],
                                              preferred_element_type=F32)


def _conv_branch(halo, u, tb, conv_w, conv_b, ln_g, ln_b, w_pw2_bf):
    n = u.shape[0]
    return pl.pallas_call(
        _conv_kernel,
        grid=(n // tb,),
        in_specs=[pl.BlockSpec((1, CONV_HALO, D_CONV), lambda i: (i, 0, 0)),
                  pl.BlockSpec((tb, D_CONV), lambda i: (i, 0)),
                  _full((CONV_WIDTH, D_CONV)), _full((1, D_CONV)), _full((1, D_CONV)),
                  _full((1, D_CONV)), _full((D_CONV, D_CONV))],
        out_specs=pl.BlockSpec((tb, D_CONV), lambda i: (i, 0)),
        out_shape=jax.ShapeDtypeStruct((n, D_CONV), F32),
        scratch_shapes=[pltpu.VMEM((tb + CONV_HALO, D_CONV), F32)],
        compiler_params=_tc_params(1),
        name="conv",
    )(halo, u, conv_w, conv_b, ln_g, ln_b, w_pw2_bf)


def _rms(x, g):
    return x * lax.rsqrt(jnp.mean(x * x, axis=-1, keepdims=True) + EPS) * g


def _merge_kernel(x_ref, oa_ref, oc_ref, ga_ref, gc_ref, wa_ref, wc_ref, gf_ref, h_out, hn_out):
    a = _rms(oa_ref[...], ga_ref[...]).astype(BF16)
    c = _rms(oc_ref[...], gc_ref[...]).astype(BF16)
    h = x_ref[...] + (jnp.dot(a, wa_ref[...], preferred_element_type=F32)
                      + jnp.dot(c, wc_ref[...], preferred_element_type=F32))
    h_out[...] = h
    hn_out[...] = _rms(h, gf_ref[...])


def _merge(x, oa, oc, tb, g_a, g_c, w_out_a, w_out_c, g_f):
    n = x.shape[0]
    row = lambda w: pl.BlockSpec((tb, w), lambda i: (i, 0))
    return pl.pallas_call(
        _merge_kernel,
        grid=(n // tb,),
        in_specs=[row(D_MODEL), row(D_ATTN), row(D_CONV), _full((1, D_ATTN)), _full((1, D_CONV)),
                  _full((D_ATTN, D_MODEL)), _full((D_CONV, D_MODEL)), _full((1, D_MODEL))],
        out_specs=[row(D_MODEL), row(D_MODEL)],
        out_shape=[jax.ShapeDtypeStruct((n, D_MODEL), F32)] * 2,
        compiler_params=_tc_params(1),
        name="merge",
    )(x, oa, oc, g_a, g_c, w_out_a, w_out_c, g_f)


ID_BIG = 1e9


def _topk_rows(s, k):
    rows = lax.broadcasted_iota(jnp.int32, s.shape, 0).astype(F32)
    vals, idxs = [], []
    for _ in range(k):
        m = jnp.max(s, axis=0, keepdims=True)
        idx = jnp.min(jnp.where(s == m, rows, ID_BIG), axis=0, keepdims=True)
        vals.append(m)
        idxs.append(idx)
        s = jnp.where(rows == idx, NEG_INF, s)
    return jnp.concatenate(vals, axis=0), jnp.concatenate(idxs, axis=0)


PAIR_B_WIDE = 8


def _route_kernel(hn_ref, wq_ref, k1_ref, k2_ref, eid_out, gate_out):
    tb = hn_ref.shape[0]
    q = jnp.dot(hn_ref[...].astype(BF16), wq_ref[...], preferred_element_type=F32).astype(BF16)
    k1 = k1_ref[...]
    k2 = k2_ref[...]
    nt = (((1,), (1,)), ((), ()))
    r = lax.broadcasted_iota(jnp.int32, (PEER_TOPK + (PAIR_B_WIDE - 1) * PAIR_B_WIDE + PAIR_B_WIDE, tb), 0)
    mid = r - PEER_TOPK
    flat = jnp.where(r < PEER_TOPK, r,
                     jnp.where(mid < (PAIR_B_WIDE - 1) * PAIR_B_WIDE,
                               (1 + mid // PAIR_B_WIDE) * PEER_TOPK + mid % PAIR_B_WIDE,
                               (PAIR_B_WIDE + mid - (PAIR_B_WIDE - 1) * PAIR_B_WIDE) * PEER_TOPK)).astype(F32)
    half = PEER_NKEYS
    for h in range(PEER_HEADS):
        q1 = q[:, (2 * h) * half:(2 * h + 1) * half]
        q2 = q[:, (2 * h + 1) * half:(2 * h + 2) * half]
        s1 = lax.dot_general(k1, q1, nt, preferred_element_type=F32)
        s2 = lax.dot_general(k2, q2, nt, preferred_element_type=F32)
        v1, i1 = _topk_rows(s1, PEER_TOPK)
        v2, i2 = _topk_rows(s2, PEER_TOPK)
        e1 = i1 * PEER_NKEYS
        cand = jnp.concatenate(
            [v1[0:1] + v2]
            + [v1[a:a + 1] + v2[0:PAIR_B_WIDE] for a in range(1, PAIR_B_WIDE)]
            + [v1[PAIR_B_WIDE:] + v2[0:1]], axis=0)
        cid = jnp.concatenate(
            [e1[0:1] + i2]
            + [e1[a:a + 1] + i2[0:PAIR_B_WIDE] for a in range(1, PAIR_B_WIDE)]
            + [e1[PAIR_B_WIDE:] + i2[0:1]], axis=0)
        scs, eids = [], []
        for _ in range(PEER_TOPK):
            m = jnp.max(cand, axis=0, keepdims=True)
            jsel = jnp.min(jnp.where(cand == m, flat, ID_BIG), axis=0, keepdims=True)
            hit = flat == jsel
            eids.append(jnp.max(jnp.where(hit, cid, -1.0), axis=0, keepdims=True))
            scs.append(m)
            cand = jnp.where(hit, NEG_INF, cand)
        sc = jnp.concatenate(scs, axis=0)
        e = jnp.exp(sc - sc[0:1])
        gate_out[h * PEER_TOPK:(h + 1) * PEER_TOPK, :] = e / jnp.sum(e, axis=0, keepdims=True)
        eid_out[h * PEER_TOPK:(h + 1) * PEER_TOPK, :] = jnp.concatenate(eids, axis=0).astype(jnp.int32)


def _route(hn, tb, wq_bf, keys1_bf, keys2_bf):
    n = hn.shape[0]
    col = pl.BlockSpec((PEER_SEL, tb), lambda i: (0, i))
    return pl.pallas_call(
        _route_kernel,
        grid=(n // tb,),
        in_specs=[pl.BlockSpec((tb, D_MODEL), lambda i: (i, 0)),
                  _full((D_MODEL, 2 * PEER_NKEYS * PEER_HEADS)),
                  _full((PEER_NKEYS, PEER_NKEYS)), _full((PEER_NKEYS, PEER_NKEYS))],
        out_specs=[col, col],
        out_shape=[jax.ShapeDtypeStruct((PEER_SEL, n), jnp.int32),
                   jax.ShapeDtypeStruct((PEER_SEL, n), F32)],
        compiler_params=_tc_params(1),
        name="route",
    )(hn, wq_bf, keys1_bf, keys2_bf)


SC_ROWS = 32
SC_GROUP = SC_LANES
SC_TOKENS = 8
SC_CHUNKS = PEER_SEL // SC_ROWS

_SC_PARAMS = pltpu.CompilerParams(needs_layout_passes=False)


def _sc_mesh():
    return plsc.VectorSubcoreMesh(core_axis_name="c", subcore_axis_name="s")


def _sc_token_loop(n_batches, table_hbm, eid_v, rbufs, rsems, prefetch, prefetch_wait, store,
                   contract):
    def gather(slot, tl, ch, p):
        idx = eid_v.at[slot, tl, pl.ds(ch * SC_ROWS, SC_ROWS)]
        return pltpu.make_async_copy(table_hbm.at[idx], rbufs[p], rsems[p])

    prefetch(0, 0)
    prefetch_wait()
    gather(0, 0, 0, 0).start()

    @pl.loop(0, n_batches * SC_TOKENS)
    def _(i):
        b = i // SC_TOKENS
        tl = i % SC_TOKENS
        slot = b % 2
        more = b + 1 < n_batches

        @pl.when(jnp.logical_and(tl == 0, more))
        def _():
            prefetch(b + 1, 1 - slot)

        for ch in range(SC_CHUNKS):
            p = ch % 2
            if ch + 1 < SC_CHUNKS:
                gather(slot, tl, ch + 1, 1 - p).start()
            else:
                @pl.when(tl + 1 < SC_TOKENS)
                def _():
                    gather(slot, tl + 1, 0, 1 - p).start()

                @pl.when(jnp.logical_and(tl + 1 == SC_TOKENS, more))
                def _():
                    prefetch_wait()
                    gather(1 - slot, 0, 0, 1 - p).start()
            gather(slot, tl, ch, p).wait()
            contract(slot, tl, ch, rbufs[p])

        @pl.when(tl + 1 == SC_TOKENS)
        def _():
            @pl.when(b >= 1)
            def _():
                store(b - 1, 1 - slot).wait()
            store(b, slot).start()

    store(n_batches - 1, (n_batches - 1) % 2).wait()


def _sc_batches(n):
    assert n % (SC_WORKERS * SC_TOKENS) == 0, n
    return n // (SC_WORKERS * SC_TOKENS)


def _peer_hval(hn, eid, peer_u):
    n = hn.shape[0]
    nbw = _sc_batches(n)

    @functools.partial(
        pl.kernel, mesh=_sc_mesh(),
        out_type=jax.ShapeDtypeStruct((n // SC_TOKENS, SC_TOKENS, PEER_SEL), F32),
        scratch_types=[
            pltpu.VMEM((2, SC_TOKENS, PEER_SEL), jnp.int32),
            pltpu.VMEM((2, SC_TOKENS, D_MODEL), F32),
            pltpu.VMEM((SC_ROWS, D_MODEL), F32),
            pltpu.VMEM((SC_ROWS, D_MODEL), F32),
            pltpu.VMEM((2, SC_TOKENS, PEER_SEL), F32),
            pltpu.VMEM((SC_GROUP, SC_LANES), F32),
            pltpu.SemaphoreType.DMA, pltpu.SemaphoreType.DMA,
            pltpu.SemaphoreType.DMA, pltpu.SemaphoreType.DMA,
        ],
        compiler_params=_SC_PARAMS, name="peer_hval",
        cost_estimate=pl.CostEstimate(flops=2 * n * PEER_SEL * D_MODEL, transcendentals=0,
                                      bytes_accessed=n * PEER_SEL * D_MODEL * 4))
    def k(x_hbm, eid_hbm, u_hbm, o_hbm, eid_v, x_v, r0, r1, h_v, tr, sr0, sr1, spf, sout):
        blk0 = (lax.axis_index("s") * SC_CORES + lax.axis_index("c")) * nbw
        lane = lax.iota(jnp.int32, SC_LANES)
        zero = jnp.zeros((SC_LANES,), F32)

        def prefetch(b, slot):
            pltpu.async_copy(eid_hbm.at[blk0 + b], eid_v.at[slot], spf)
            pltpu.async_copy(x_hbm.at[blk0 + b], x_v.at[slot], spf)

        def prefetch_wait():
            pltpu.make_async_copy(eid_hbm.at[0], eid_v.at[0], spf).wait()
            pltpu.make_async_copy(x_hbm.at[0], x_v.at[0], spf).wait()

        def store(b, slot):
            return pltpu.make_async_copy(h_v.at[slot], o_hbm.at[blk0 + b], sout)

        def contract(slot, tl, ch, rbuf):
            for g in range(SC_ROWS // SC_GROUP):
                def body(c, accs):
                    xc = x_v[slot, tl, pl.ds(c * SC_LANES, SC_LANES)]
                    return tuple(
                        accs[r] + rbuf[g * SC_GROUP + r, pl.ds(c * SC_LANES, SC_LANES)] * xc
                        for r in range(SC_GROUP))
                accs = lax.fori_loop(0, D_MODEL // SC_LANES, body, (zero,) * SC_GROUP)
                for r in range(SC_GROUP):
                    tr[r, :] = accs[r]
                res = zero
                for jj in range(SC_LANES):
                    res = res + plsc.load_gather(tr, [lane, jnp.full((SC_LANES,), jj, jnp.int32)])
                h_v[slot, tl, pl.ds(ch * SC_ROWS + g * SC_GROUP, SC_GROUP)] = res

        _sc_token_loop(nbw, u_hbm, eid_v, (r0, r1), (sr0, sr1), prefetch, prefetch_wait, store,
                       contract)

    out = k(hn.reshape(n // SC_TOKENS, SC_TOKENS, D_MODEL),
            eid.reshape(n // SC_TOKENS, SC_TOKENS, PEER_SEL), peer_u)
    return out.reshape(n, PEER_SEL)


def _peer_out(h, a, eid, peer_v):
    n = h.shape[0]
    nbw = _sc_batches(n)

    @functools.partial(
        pl.kernel, mesh=_sc_mesh(),
        out_type=jax.ShapeDtypeStruct((n // SC_TOKENS, SC_TOKENS, D_MODEL), F32),
        scratch_types=[
            pltpu.VMEM((2, SC_TOKENS, PEER_SEL), jnp.int32),
            pltpu.VMEM((2, SC_TOKENS, PEER_SEL), F32),
            pltpu.VMEM((SC_ROWS, D_MODEL), F32),
            pltpu.VMEM((SC_ROWS, D_MODEL), F32),
            pltpu.VMEM((2, SC_TOKENS, D_MODEL), F32),
            pltpu.SemaphoreType.DMA, pltpu.SemaphoreType.DMA,
            pltpu.SemaphoreType.DMA, pltpu.SemaphoreType.DMA,
        ],
        compiler_params=_SC_PARAMS, name="peer_out")
    def k(h_hbm, a_hbm, eid_hbm, v_hbm, o_hbm, eid_v, a_v, r0, r1, y_v, sr0, sr1, spf, sout):
        blk0 = (lax.axis_index("s") * SC_CORES + lax.axis_index("c")) * nbw

        def prefetch(b, slot):
            pltpu.async_copy(eid_hbm.at[blk0 + b], eid_v.at[slot], spf)
            pltpu.async_copy(a_hbm.at[blk0 + b], a_v.at[slot], spf)
            pltpu.async_copy(h_hbm.at[blk0 + b], y_v.at[slot], spf)

        def prefetch_wait():
            pltpu.make_async_copy(eid_hbm.at[0], eid_v.at[0], spf).wait()
            pltpu.make_async_copy(a_hbm.at[0], a_v.at[0], spf).wait()
            pltpu.make_async_copy(h_hbm.at[0], y_v.at[0], spf).wait()

        def store(b, slot):
            return pltpu.make_async_copy(y_v.at[slot], o_hbm.at[blk0 + b], sout)

        def contract(slot, tl, ch, rbuf):
            slot_v = jnp.full((SC_LANES,), slot, jnp.int32)
            tl_v = jnp.full((SC_LANES,), tl, jnp.int32)
            for g in range(SC_ROWS // SC_GROUP):
                base = ch * SC_ROWS + g * SC_GROUP
                coef = [plsc.load_gather(a_v, [slot_v, tl_v, jnp.full((SC_LANES,), base + r, jnp.int32)])
                        for r in range(SC_GROUP)]

                @plsc.parallel_loop(0, D_MODEL // SC_LANES)
                def _(c):
                    sl = pl.ds(c * SC_LANES, SC_LANES)
                    acc = coef[0] * rbuf[g * SC_GROUP, sl]
                    for r in range(1, SC_GROUP):
                        acc = acc + coef[r] * rbuf[g * SC_GROUP + r, sl]
                    plsc.addupdate(y_v.at[slot, tl, sl], acc)

        _sc_token_loop(nbw, v_hbm, eid_v, (r0, r1), (sr0, sr1), prefetch, prefetch_wait, store,
                       contract)

    out = k(h.reshape(n // SC_TOKENS, SC_TOKENS, D_MODEL),
            a.reshape(n // SC_TOKENS, SC_TOKENS, PEER_SEL),
            eid.reshape(n // SC_TOKENS, SC_TOKENS, PEER_SEL), peer_v)
    return out.reshape(n, D_MODEL)


def _gate_kernel(hv_ref, gate_ref, a_out):
    hv = hv_ref[...]
    gelu = hv * (lax.erf(hv * (2.0 ** -0.5)) + 1.0) * 0.5
    a_out[...] = gate_ref[...] * gelu


def _gate(hval, gate, tb):
    n = hval.shape[0]
    row = pl.BlockSpec((tb, PEER_SEL), lambda i: (i, 0))
    return pl.pallas_call(
        _gate_kernel, grid=(n // tb,), in_specs=[row, row], out_specs=row,
        out_shape=jax.ShapeDtypeStruct((n, PEER_SEL), F32),
        compiler_params=_tc_params(1), name="gate",
    )(hval, gate)


VT_ROWS = D_MODEL // 2 // LANES
VT_TOKENS = 64
HI_MASK = -65536


def _pack_value_table(peer_v):
    e = peer_v.shape[0]
    bits = lax.bitcast_convert_type(peer_v.astype(BF16), jnp.uint16).astype(jnp.uint32)
    words = bits[:, :D_MODEL // 2] | (bits[:, D_MODEL // 2:] << 16)
    return lax.bitcast_convert_type(words, jnp.int32).reshape(e * VT_ROWS, LANES)


def _vside_kernel(row_s, a_s, h_ref, tab_ref, y_ref):
    tb = h_ref.shape[0]

    def token(t, carry):
        hv = h_ref[t]
        zero = jnp.zeros((VT_ROWS, LANES), F32)
        lo = [hv[0:VT_ROWS], zero]
        hi = [hv[VT_ROWS:], zero]
        for e in range(PEER_SEL):
            r0 = pl.multiple_of(row_s[t, e], VT_ROWS)
            w = tab_ref[pl.ds(r0, VT_ROWS), :]
            coef = a_s[t, e]
            c = e % 2
            lo[c] = lo[c] + coef * pltpu.bitcast(w << 16, F32)
            hi[c] = hi[c] + coef * pltpu.bitcast(w & HI_MASK, F32)
        y_ref[t] = jnp.concatenate([lo[0] + lo[1], hi[0] + hi[1]], axis=0)
        return carry

    lax.fori_loop(0, tb, token, 0)


def _peer_values(h, a, row_ids, v_packed):
    n = h.shape[0]
    tb = min(VT_TOKENS, n)
    smem = pl.BlockSpec((tb, PEER_SEL), lambda i: (i, 0), memory_space=pltpu.SMEM)
    slab = pl.BlockSpec((tb, 2 * VT_ROWS, LANES), lambda i: (i, 0, 0))
    table = pl.BlockSpec(v_packed.shape, lambda i: (0, 0), pipeline_mode=pl.Buffered(1))
    table_bytes = v_packed.shape[0] * LANES * 4
    y = pl.pallas_call(
        _vside_kernel,
        grid=(n // tb,),
        in_specs=[smem, smem, slab, table],
        out_specs=slab,
        out_shape=jax.ShapeDtypeStruct((n, 2 * VT_ROWS, LANES), F32),
        compiler_params=pltpu.CompilerParams(
            dimension_semantics=("arbitrary",),
            vmem_limit_bytes=table_bytes + 8 * 1024 * 1024),
        name="peer_values",
        cost_estimate=pl.CostEstimate(flops=2 * n * PEER_SEL * D_MODEL, transcendentals=0,
                                      bytes_accessed=table_bytes + n * PEER_SEL * D_MODEL * 2),
    )(row_ids, a, h.reshape(n, 2 * VT_ROWS, LANES), v_packed)
    return y.reshape(n, D_MODEL)


PEER_CHUNK = 4096


def _peer(h, hn, tb, **weights):
    n = h.shape[0]
    if n <= PEER_CHUNK:
        return _peer_chunk(h, hn, tb, **weights)
    return jnp.concatenate(
        [_peer_chunk(h[c:c + PEER_CHUNK], hn[c:c + PEER_CHUNK], tb, **weights)
         for c in range(0, n, PEER_CHUNK)], axis=0)


def _peer_chunk(h, hn, tb, wq_bf, keys1_bf, keys2_bf, peer_u, v_packed):
    n = h.shape[0]
    eid_t, gate_t = _route(hn, tb, wq_bf, keys1_bf, keys2_bf)
    eid = eid_t.T
    pad = (-n) % (SC_WORKERS * SC_TOKENS)
    hn_p, eid_p = hn, eid
    if pad:
        spread = (jnp.arange(pad * PEER_SEL, dtype=jnp.int32) % peer_u.shape[0]).reshape(pad, PEER_SEL)
        hn_p = jnp.pad(hn, ((0, pad), (0, 0)))
        eid_p = jnp.concatenate([eid, spread], axis=0)
    hval = _peer_hval(hn_p, eid_p, peer_u)[:n]
    a = _gate(hval, gate_t.T, min(tb, 512))
    return _peer_values(h, a, eid * VT_ROWS, v_packed)


def _rope_tables(pos):
    half = HEAD_DIM // 2
    inv = ROPE_THETA ** (-jnp.arange(half, dtype=F32) / half)
    ang = pos.astype(F32)[:, None] * inv[None, :]
    cos = jnp.cos(ang)
    sin = jnp.sin(ang)
    reps = LANES // HEAD_DIM
    cos_f = jnp.tile(jnp.concatenate([cos, cos], axis=1), (1, reps))
    sin_s = jnp.tile(jnp.concatenate([-sin, sin], axis=1), (1, reps))
    return cos_f, sin_s


def _group_sum_matrix(width):
    g = jnp.arange(width) // HEAD_DIM
    return (g[:, None] == g[None, :]).astype(BF16)


def _heads_first(a, heads):
    b, t, _ = a.shape
    return a.reshape(b, t, heads, HEAD_DIM).transpose(0, 2, 1, 3)


def kernel(x_prompt, x_sample, cache_k_win, cache_v_win, state_conv, meta_tokens, norm_mix_g,
           w_in, q_norm_g, k_norm_g, attn_sinks, conv_w, conv_b, conv_ln_g, conv_ln_b, w_pw2,
           out_norm_attn_g, out_norm_conv_g, w_out, norm_ffn_g, peer_w_q, peer_keys1, peer_keys2,
           peer_u, peer_v):
    assert norm_mix_g.shape[0] == 1, "single-layer model"
    b, t, _ = x_prompt.shape
    s = x_sample.shape[0]
    w_buf = cache_k_win.shape[2]
    n = b * t

    g_mix = norm_mix_g[0][None, :]
    w_in_bf = w_in[0].astype(BF16)
    qg_t = jnp.tile(q_norm_g[0], N_HEADS)[None, :]
    kg_t = jnp.tile(k_norm_g[0], N_KV_HEADS)[None, :]
    gsum_q = _group_sum_matrix(D_ATTN)
    gsum_k = _group_sum_matrix(KV_DIM)
    sinks = attn_sinks[0]
    cw, cb = conv_w[0], conv_b[0][None, :]
    lg, lb = conv_ln_g[0][None, :], conv_ln_b[0][None, :]
    w2_bf = w_pw2[0].astype(BF16)
    g_a, g_c = out_norm_attn_g[0][None, :], out_norm_conv_g[0][None, :]
    w_out_a = w_out[0][:D_ATTN].astype(BF16)
    w_out_c = w_out[0][D_ATTN:].astype(BF16)
    g_f = norm_ffn_g[0][None, :]
    wq_bf = peer_w_q[0].astype(BF16)
    k1_bf = peer_keys1[0].astype(BF16)
    k2_bf = peer_keys2[0].astype(BF16)
    pu, vp = peer_u[0], _pack_value_table(peer_v[0])
    proj = functools.partial(_project, norm_g=g_mix, w_in_bf=w_in_bf, qg_t=qg_t, kg_t=kg_t,
                             gsum_q=gsum_q, gsum_k=gsum_k)
    conv = functools.partial(_conv_branch, conv_w=cw, conv_b=cb, ln_g=lg, ln_b=lb, w_pw2_bf=w2_bf)
    merge = functools.partial(_merge, g_a=g_a, g_c=g_c, w_out_a=w_out_a, w_out_c=w_out_c, g_f=g_f)
    peer = functools.partial(_peer, wq_bf=wq_bf, keys1_bf=k1_bf, keys2_bf=k2_bf, peer_u=pu, v_packed=vp)

    tb = 512
    xp = x_prompt.reshape(n, D_MODEL)
    tab_x = _rope_tables(N_META + jnp.arange(t, dtype=jnp.int32))
    q, k, v, u = proj(xp, tab_x, t // tb, tb)
    tab_m = _rope_tables(jnp.arange(N_META, dtype=jnp.int32))
    _, k_m, v_m, u_m = proj(meta_tokens, tab_m, 1, N_META)

    def with_meta_block(a, a_m):
        blk0 = jnp.concatenate([jnp.zeros((META_PAD, KV_DIM), F32), a_m], axis=0)
        full = jnp.concatenate([jnp.broadcast_to(blk0[None], (b, BLOCK, KV_DIM)),
                                a.reshape(b, t, KV_DIM)], axis=1)
        return _heads_first(full, N_KV_HEADS)

    o_attn = _prompt_attention(sinks, _heads_first(q.reshape(b, t, D_ATTN), N_HEADS),
                               with_meta_block(k, k_m), with_meta_block(v, v_m))
    o_attn = o_attn.transpose(0, 2, 1, 3).reshape(n, D_ATTN)

    u3 = u.reshape(b, t // tb, tb, D_CONV)
    halo0 = jnp.concatenate([jnp.zeros((CONV_HALO - N_META, D_CONV), F32), u_m], axis=0)
    halo = jnp.concatenate([jnp.broadcast_to(halo0[None, None], (b, 1, CONV_HALO, D_CONV)),
                            u3[:, :-1, tb - CONV_HALO:]], axis=1).reshape(n // tb, CONV_HALO, D_CONV)
    o_conv = conv(halo, u, tb)

    h, hn = merge(xp, o_attn, o_conv, tb)
    y_prompt = peer(h, hn, 256).reshape(b, t, D_MODEL)

    new_k_prompt = k.reshape(b, t, N_KV_HEADS, HEAD_DIM)[None, :, t - WINDOW:]
    new_v_prompt = v.reshape(b, t, N_KV_HEADS, HEAD_DIM)[None, :, t - WINDOW:]
    new_conv_prompt = u.reshape(b, t, D_CONV)[None, :, t - (CONV_WIDTH - 1):]

    xs = x_sample.reshape(s, D_MODEL)
    tab_s = _rope_tables(jnp.full((s,), PAST_LEN, jnp.int32))
    qs, ks, vs, us = proj(xs, tab_s, 1, s)
    ck = cache_k_win[0]
    cv = cache_v_win[0]
    o_attn_s = _decode_attention(
        sinks,
        qs.reshape(s, N_HEADS, HEAD_DIM),
        ks.reshape(s, N_KV_HEADS, HEAD_DIM).transpose(1, 0, 2),
        vs.reshape(s, N_KV_HEADS, HEAD_DIM).transpose(1, 0, 2),
        ck.transpose(2, 0, 1, 3), cv.transpose(2, 0, 1, 3), 32)
    o_attn_s = o_attn_s.reshape(s, D_ATTN)
    cs = state_conv[0]
    hist = jnp.concatenate([jnp.zeros((s, CONV_HALO - (CONV_WIDTH - 1), D_CONV), F32), cs], axis=1)
    us_blk = jnp.concatenate([us[:, None, :], jnp.zeros((s, 7, D_CONV), F32)], axis=1)
    o_conv_s = conv(hist, us_blk.reshape(s * 8, D_CONV), 8).reshape(s, 8, D_CONV)[:, 0]
    hs, hns = merge(xs, o_attn_s, o_conv_s, s)
    y_sample = peer(hs, hns, s).reshape(s, 1, D_MODEL)

    new_k_sample = jnp.concatenate([ck[:, 1:], ks.reshape(s, 1, N_KV_HEADS, HEAD_DIM)], axis=1)[None]
    new_v_sample = jnp.concatenate([cv[:, 1:], vs.reshape(s, 1, N_KV_HEADS, HEAD_DIM)], axis=1)[None]
    new_conv_sample = jnp.concatenate([cs[:, 1:], us[:, None, :]], axis=1)[None]
    if w_buf != WINDOW:
        raise NotImplementedError("cache window shorter than the attention window")

    return (y_prompt, y_sample, new_k_prompt, new_v_prompt, new_conv_prompt,
            new_k_sample, new_v_sample, new_conv_sample)
```

```python
import functools

import jax
import jax.numpy as jnp
from jax import lax
from jax.experimental import pallas as pl
from jax.experimental.pallas import tpu as pltpu
from jax.experimental.pallas import tpu_sc as plsc

D_MODEL = 1024
HEAD_DIM = 64
D_ATTN = 512
N_HEADS = 8
N_KV_HEADS = 2
KV_DIM = N_KV_HEADS * HEAD_DIM
D_CONV = 512
D_IN = D_ATTN + 2 * KV_DIM + 2 * D_CONV
CONV_WIDTH = 31
WINDOW = 128
BLOCK = 128
ROPE_THETA = 10000.0
N_META = 16
META_PAD = BLOCK - N_META
PEER_HEADS = 8
PEER_NKEYS = 128
PEER_TOPK = 16
PEER_SEL = PEER_HEADS * PEER_TOPK
EPS = 1e-6
PAST_LEN = 16384

LANES = 128
SC_CORES = 2
SC_SUBCORES = 16
SC_LANES = 16
SC_WORKERS = SC_CORES * SC_SUBCORES
VMEM_LIMIT = 48 * 1024 * 1024

F32 = jnp.float32
BF16 = jnp.bfloat16
NEG_INF = float("-inf")


def _tc_params(n_axes):
    return pltpu.CompilerParams(dimension_semantics=("arbitrary",) * n_axes,
                                vmem_limit_bytes=VMEM_LIMIT)


def _full(shape):
    nd = len(shape)
    return pl.BlockSpec(shape, lambda *_: (0,) * nd)


def _group_mean(sq, gsum_ref):
    hi = sq.astype(BF16)
    lo = (sq - hi.astype(F32)).astype(BF16)
    g = gsum_ref[...]
    s = jnp.dot(hi, g, preferred_element_type=F32) + jnp.dot(lo, g, preferred_element_type=F32)
    return s * (1.0 / HEAD_DIM)


def _rope(xn, cos_f, sin_s, first_half):
    outs = []
    for s in range(xn.shape[1] // LANES):
        xs = xn[:, s * LANES:(s + 1) * LANES]
        partner = jnp.where(first_half, pltpu.roll(xs, LANES - HEAD_DIM // 2, axis=1),
                            pltpu.roll(xs, HEAD_DIM // 2, axis=1))
        outs.append(xs * cos_f + partner * sin_s)
    return outs[0] if len(outs) == 1 else jnp.concatenate(outs, axis=1)


def _proj_kernel(x_ref, g_ref, w_ref, qg_ref, kg_ref, cos_ref, sin_ref, gq_ref, gk_ref,
                 q_out, k_out, v_out, u_out):
    x = x_ref[...]
    n = x * lax.rsqrt(jnp.mean(x * x, axis=-1, keepdims=True) + EPS) * g_ref[...]
    p = jnp.dot(n.astype(BF16), w_ref[...], preferred_element_type=F32)
    q = p[:, :D_ATTN]
    k = p[:, D_ATTN:D_ATTN + KV_DIM]
    v = p[:, D_ATTN + KV_DIM:D_ATTN + 2 * KV_DIM]
    ga = p[:, D_ATTN + 2 * KV_DIM:D_ATTN + 2 * KV_DIM + D_CONV]
    gb = p[:, D_ATTN + 2 * KV_DIM + D_CONV:]
    cos_f = cos_ref[...]
    sin_s = sin_ref[...]
    lane = lax.broadcasted_iota(jnp.int32, (x.shape[0], LANES), 1)
    first_half = (lane % HEAD_DIM) < (HEAD_DIM // 2)
    qn = q * lax.rsqrt(_group_mean(q * q, gq_ref) + EPS) * qg_ref[...]
    kn = k * lax.rsqrt(_group_mean(k * k, gk_ref) + EPS) * kg_ref[...]
    q_out[...] = _rope(qn, cos_f, sin_s, first_half)
    k_out[...] = _rope(kn, cos_f, sin_s, first_half)
    v_out[...] = v
    u_out[...] = ga * (1.0 / (1.0 + jnp.exp(-gb)))


def _project(x, pos_tables, n_table_blocks, tb, norm_g, w_in_bf, qg_t, kg_t, gsum_q, gsum_k):
    n = x.shape[0]
    cos_t, sin_t = pos_tables
    nb = n // tb
    tab_spec = pl.BlockSpec((tb, LANES), lambda i: (i % n_table_blocks, 0))
    row = lambda w: pl.BlockSpec((tb, w), lambda i: (i, 0))
    return pl.pallas_call(
        _proj_kernel,
        grid=(nb,),
        in_specs=[row(D_MODEL), _full((1, D_MODEL)), _full((D_MODEL, D_IN)),
                  _full((1, D_ATTN)), _full((1, KV_DIM)), tab_spec, tab_spec,
                  _full((D_ATTN, D_ATTN)), _full((KV_DIM, KV_DIM))],
        out_specs=[row(D_ATTN), row(KV_DIM), row(KV_DIM), row(D_CONV)],
        out_shape=[jax.ShapeDtypeStruct((n, D_ATTN), F32), jax.ShapeDtypeStruct((n, KV_DIM), F32),
                   jax.ShapeDtypeStruct((n, KV_DIM), F32), jax.ShapeDtypeStruct((n, D_CONV), F32)],
        compiler_params=_tc_params(1),
        name="proj",
    )(x, norm_g, w_in_bf, qg_t, kg_t, cos_t, sin_t, gsum_q, gsum_k)


def _attn_kernel(sink_ref, q_ref, kp_ref, kc_ref, vp_ref, vc_ref, o_ref):
    j = pl.program_id(1)
    r = lax.broadcasted_iota(jnp.int32, (BLOCK, 2 * BLOCK), 0)
    c = lax.broadcasted_iota(jnp.int32, (BLOCK, 2 * BLOCK), 1)
    ok = (c > r) & (c <= r + WINDOW) & ((j > 0) | (c >= META_PAD))
    grp = N_HEADS // N_KV_HEADS
    ok = jnp.concatenate([ok] * grp, axis=0)
    for g in range(N_KV_HEADS):
        q4 = q_ref[0, g * grp:(g + 1) * grp].reshape(grp * BLOCK, HEAD_DIM).astype(BF16)
        k = jnp.concatenate([kp_ref[0, g], kc_ref[0, g]], axis=0).astype(BF16)
        v = jnp.concatenate([vp_ref[0, g], vc_ref[0, g]], axis=0).astype(BF16)
        s = lax.dot_general(q4, k, (((1,), (1,)), ((), ())), preferred_element_type=F32)
        s = jnp.where(ok, s * (HEAD_DIM ** -0.5), NEG_INF)
        sink = jnp.concatenate(
            [jnp.full((BLOCK, 1), sink_ref[g * grp + i], F32) for i in range(grp)], axis=0)
        m = jnp.maximum(jnp.max(s, axis=1, keepdims=True), sink)
        p = jnp.exp(s - m)
        den = jnp.sum(p, axis=1, keepdims=True) + jnp.exp(sink - m)
        o = jnp.dot(p.astype(BF16), v, preferred_element_type=F32) / den
        o_ref[0, g * grp:(g + 1) * grp] = o.reshape(grp, BLOCK, HEAD_DIM)


def _prompt_attention(sinks, q_t, k_t, v_t):
    b, _, t, _ = q_t.shape
    nb = t // BLOCK
    kv_prev = pl.BlockSpec((1, N_KV_HEADS, BLOCK, HEAD_DIM), lambda bi, j: (bi, 0, j, 0))
    kv_cur = pl.BlockSpec((1, N_KV_HEADS, BLOCK, HEAD_DIM), lambda bi, j: (bi, 0, j + 1, 0))
    q_spec = pl.BlockSpec((1, N_HEADS, BLOCK, HEAD_DIM), lambda bi, j: (bi, 0, j, 0))
    return pl.pallas_call(
        _attn_kernel,
        grid=(b, nb),
        in_specs=[pl.BlockSpec(memory_space=pltpu.SMEM), q_spec, kv_prev, kv_cur, kv_prev, kv_cur],
        out_specs=q_spec,
        out_shape=jax.ShapeDtypeStruct(q_t.shape, F32),
        compiler_params=_tc_params(2),
        name="attn",
    )(sinks, q_t, k_t, k_t, v_t, v_t)


def _dec_attn_kernel(sink_ref, q_ref, kn_ref, vn_ref, ck_ref, cv_ref, o_ref):
    grp = N_HEADS // N_KV_HEADS
    sb, w_buf = ck_ref.shape[1], ck_ref.shape[2]
    q = q_ref[...]
    qb = q.astype(BF16)
    head = lax.broadcasted_iota(jnp.int32, (sb, N_HEADS, 1), 1)
    in_g0 = head < grp
    s_g = [jnp.einsum("shd,swd->shw", qb, ck_ref[g].astype(BF16), preferred_element_type=F32)
           for g in range(N_KV_HEADS)]
    s = jnp.where(in_g0, s_g[0], s_g[1]) * (HEAD_DIM ** -0.5)
    key_ok = lax.broadcasted_iota(jnp.int32, (sb, N_HEADS, w_buf), 2) >= 1
    s = jnp.where(key_ok, s, NEG_INF)
    rnd = lambda a: a.astype(BF16).astype(F32)
    kn = jnp.where(in_g0, kn_ref[0][:, None, :], kn_ref[1][:, None, :])
    vn = jnp.where(in_g0, vn_ref[0][:, None, :], vn_ref[1][:, None, :])
    s_self = jnp.sum(rnd(q) * rnd(kn), axis=-1, keepdims=True) * (HEAD_DIM ** -0.5)
    sink = sink_ref[...]
    m = jnp.maximum(jnp.maximum(jnp.max(s, axis=-1, keepdims=True), s_self), sink)
    p = jnp.exp(s - m)
    p_self = jnp.exp(s_self - m)
    den = jnp.sum(p, axis=-1, keepdims=True) + p_self + jnp.exp(sink - m)
    pb = p.astype(BF16)
    o_g = [jnp.einsum("shw,swd->shd", pb, cv_ref[g].astype(BF16), preferred_element_type=F32)
           for g in range(N_KV_HEADS)]
    o = jnp.where(in_g0, o_g[0], o_g[1]) + rnd(p_self) * rnd(vn)
    o_ref[...] = o / den


def _decode_attention(sinks, q3, kn_t, vn_t, ck_t, cv_t, sb):
    s = q3.shape[0]
    w_buf = ck_t.shape[2]
    qs = pl.BlockSpec((sb, N_HEADS, HEAD_DIM), lambda i: (i, 0, 0))
    ns = pl.BlockSpec((N_KV_HEADS, sb, HEAD_DIM), lambda i: (0, i, 0))
    cs = pl.BlockSpec((N_KV_HEADS, sb, w_buf, HEAD_DIM), lambda i: (0, i, 0, 0))
    return pl.pallas_call(
        _dec_attn_kernel,
        grid=(s // sb,),
        in_specs=[_full((1, N_HEADS, 1)), qs, ns, ns, cs, cs],
        out_specs=qs,
        out_shape=jax.ShapeDtypeStruct(q3.shape, F32),
        compiler_params=_tc_params(1),
        name="dec_attn",
    )(sinks.reshape(1, N_HEADS, 1), q3, kn_t, vn_t, ck_t, cv_t)


CONV_HALO = 32
CONV_ROWS = 64


def _conv_kernel(halo_ref, u_ref, cw_ref, cb_ref, lg_ref, lb_ref, w2_ref, o_ref, ucat):
    tb = u_ref.shape[0]
    ucat[0:CONV_HALO, :] = halo_ref[0]
    ucat[CONV_HALO:, :] = u_ref[...]
    first = CONV_HALO - (CONV_WIDTH - 1)
    rows = min(CONV_ROWS, tb)
    for r0 in range(0, tb, rows):
        acc = jnp.zeros((rows, D_CONV), F32)
        for j in range(CONV_WIDTH):
            acc = acc + ucat[r0 + first + j:r0 + first + j + rows, :] * cw_ref[j:j + 1, :]
        y = acc + cb_ref[...]
        yc = y - jnp.mean(y, axis=-1, keepdims=True)
        yn = yc * lax.rsqrt(jnp.mean(yc * yc, axis=-1, keepdims=True) + EPS)
        yn = yn * lg_ref[...] + lb_ref[...]
        act = yn * (1.0 / (1.0 + jnp.exp(-yn)))
        o_ref[r0:r0 + rows, :] = jnp.dot(act.astype(BF16), w2_ref[...],
                                              preferred_element_type=F32)


def _conv_branch(halo, u, tb, conv_w, conv_b, ln_g, ln_b, w_pw2_bf):
    n = u.shape[0]
    return pl.pallas_call(
        _conv_kernel,
        grid=(n // tb,),
        in_specs=[pl.BlockSpec((1, CONV_HALO, D_CONV), lambda i: (i, 0, 0)),
                  pl.BlockSpec((tb, D_CONV), lambda i: (i, 0)),
                  _full((CONV_WIDTH, D_CONV)), _full((1, D_CONV)), _full((1, D_CONV)),
                  _full((1, D_CONV)), _full((D_CONV, D_CONV))],
        out_specs=pl.BlockSpec((tb, D_CONV), lambda i: (i, 0)),
        out_shape=jax.ShapeDtypeStruct((n, D_CONV), F32),
        scratch_shapes=[pltpu.VMEM((tb + CONV_HALO, D_CONV), F32)],
        compiler_params=_tc_params(1),
        name="conv",
    )(halo, u, conv_w, conv_b, ln_g, ln_b, w_pw2_bf)


def _rms(x, g):
    return x * lax.rsqrt(jnp.mean(x * x, axis=-1, keepdims=True) + EPS) * g


def _merge_kernel(x_ref, oa_ref, oc_ref, ga_ref, gc_ref, wa_ref, wc_ref, gf_ref, h_out, hn_out):
    a = _rms(oa_ref[...], ga_ref[...]).astype(BF16)
    c = _rms(oc_ref[...], gc_ref[...]).astype(BF16)
    h = x_ref[...] + (jnp.dot(a, wa_ref[...], preferred_element_type=F32)
                      + jnp.dot(c, wc_ref[...], preferred_element_type=F32))
    h_out[...] = h
    hn_out[...] = _rms(h, gf_ref[...])


def _merge(x, oa, oc, tb, g_a, g_c, w_out_a, w_out_c, g_f):
    n = x.shape[0]
    row = lambda w: pl.BlockSpec((tb, w), lambda i: (i, 0))
    return pl.pallas_call(
        _merge_kernel,
        grid=(n // tb,),
        in_specs=[row(D_MODEL), row(D_ATTN), row(D_CONV), _full((1, D_ATTN)), _full((1, D_CONV)),
                  _full((D_ATTN, D_MODEL)), _full((D_CONV, D_MODEL)), _full((1, D_MODEL))],
        out_specs=[row(D_MODEL), row(D_MODEL)],
        out_shape=[jax.ShapeDtypeStruct((n, D_MODEL), F32)] * 2,
        compiler_params=_tc_params(1),
        name="merge",
    )(x, oa, oc, g_a, g_c, w_out_a, w_out_c, g_f)


ID_BIG = 1e9


def _topk_rows(s, k):
    rows = lax.broadcasted_iota(jnp.int32, s.shape, 0).astype(F32)
    vals, idxs = [], []
    for _ in range(k):
        m = jnp.max(s, axis=0, keepdims=True)
        idx = jnp.min(jnp.where(s == m, rows, ID_BIG), axis=0, keepdims=True)
        vals.append(m)
        idxs.append(idx)
        s = jnp.where(rows == idx, NEG_INF, s)
    return jnp.concatenate(vals, axis=0), jnp.concatenate(idxs, axis=0)


PAIR_B_WIDE = 8


def _route_kernel(hn_ref, wq_ref, k1_ref, k2_ref, eid_out, gate_out):
    tb = hn_ref.shape[0]
    q = jnp.dot(hn_ref[...].astype(BF16), wq_ref[...], preferred_element_type=F32).astype(BF16)
    k1 = k1_ref[...]
    k2 = k2_ref[...]
    nt = (((1,), (1,)), ((), ()))
    r = lax.broadcasted_iota(jnp.int32, (PEER_TOPK + (PAIR_B_WIDE - 1) * PAIR_B_WIDE + PAIR_B_WIDE, tb), 0)
    mid = r - PEER_TOPK
    flat = jnp.where(r < PEER_TOPK, r,
                     jnp.where(mid < (PAIR_B_WIDE - 1) * PAIR_B_WIDE,
                               (1 + mid // PAIR_B_WIDE) * PEER_TOPK + mid % PAIR_B_WIDE,
                               (PAIR_B_WIDE + mid - (PAIR_B_WIDE - 1) * PAIR_B_WIDE) * PEER_TOPK)).astype(F32)
    half = PEER_NKEYS
    for h in range(PEER_HEADS):
        q1 = q[:, (2 * h) * half:(2 * h + 1) * half]
        q2 = q[:, (2 * h + 1) * half:(2 * h + 2) * half]
        s1 = lax.dot_general(k1, q1, nt, preferred_element_type=F32)
        s2 = lax.dot_general(k2, q2, nt, preferred_element_type=F32)
        v1, i1 = _topk_rows(s1, PEER_TOPK)
        v2, i2 = _topk_rows(s2, PEER_TOPK)
        e1 = i1 * PEER_NKEYS
        cand = jnp.concatenate(
            [v1[0:1] + v2]
            + [v1[a:a + 1] + v2[0:PAIR_B_WIDE] for a in range(1, PAIR_B_WIDE)]
            + [v1[PAIR_B_WIDE:] + v2[0:1]], axis=0)
        cid = jnp.concatenate(
            [e1[0:1] + i2]
            + [e1[a:a + 1] + i2[0:PAIR_B_WIDE] for a in range(1, PAIR_B_WIDE)]
            + [e1[PAIR_B_WIDE:] + i2[0:1]], axis=0)
        scs, eids = [], []
        for _ in range(PEER_TOPK):
            m = jnp.max(cand, axis=0, keepdims=True)
            jsel = jnp.min(jnp.where(cand == m, flat, ID_BIG), axis=0, keepdims=True)
            hit = flat == jsel
            eids.append(jnp.max(jnp.where(hit, cid, -1.0), axis=0, keepdims=True))
            scs.append(m)
            cand = jnp.where(hit, NEG_INF, cand)
        sc = jnp.concatenate(scs, axis=0)
        e = jnp.exp(sc - sc[0:1])
        gate_out[h * PEER_TOPK:(h + 1) * PEER_TOPK, :] = e / jnp.sum(e, axis=0, keepdims=True)
        eid_out[h * PEER_TOPK:(h + 1) * PEER_TOPK, :] = jnp.concatenate(eids, axis=0).astype(jnp.int32)


def _route(hn, tb, wq_bf, keys1_bf, keys2_bf):
    n = hn.shape[0]
    col = pl.BlockSpec((PEER_SEL, tb), lambda i: (0, i))
    return pl.pallas_call(
        _route_kernel,
        grid=(n // tb,),
        in_specs=[pl.BlockSpec((tb, D_MODEL), lambda i: (i, 0)),
                  _full((D_MODEL, 2 * PEER_NKEYS * PEER_HEADS)),
                  _full((PEER_NKEYS, PEER_NKEYS)), _full((PEER_NKEYS, PEER_NKEYS))],
        out_specs=[col, col],
        out_shape=[jax.ShapeDtypeStruct((PEER_SEL, n), jnp.int32),
                   jax.ShapeDtypeStruct((PEER_SEL, n), F32)],
        compiler_params=_tc_params(1),
        name="route",
    )(hn, wq_bf, keys1_bf, keys2_bf)


SC_ROWS = 32
SC_GROUP = SC_LANES
SC_TOKENS = 8
SC_CHUNKS = PEER_SEL // SC_ROWS

_SC_PARAMS = pltpu.CompilerParams(needs_layout_passes=False)


def _sc_mesh():
    return plsc.VectorSubcoreMesh(core_axis_name="c", subcore_axis_name="s",
                                  num_cores=SC_CORES, num_subcores=SC_SUBCORES)


def _sc_token_loop(n_batches, table_hbm, eid_v, rbufs, rsems, prefetch, prefetch_wait, store,
                   contract):
    def gather(slot, tl, ch, p):
        idx = eid_v.at[slot, tl, pl.ds(ch * SC_ROWS, SC_ROWS)]
        return pltpu.make_async_copy(table_hbm.at[idx], rbufs[p], rsems[p])

    prefetch(0, 0)
    prefetch_wait()
    gather(0, 0, 0, 0).start()

    @pl.loop(0, n_batches * SC_TOKENS)
    def _(i):
        b = i // SC_TOKENS
        tl = i % SC_TOKENS
        slot = b % 2
        more = b + 1 < n_batches

        @pl.when(jnp.logical_and(tl == 0, more))
        def _():
            prefetch(b + 1, 1 - slot)

        for ch in range(SC_CHUNKS):
            p = ch % 2
            if ch + 1 < SC_CHUNKS:
                gather(slot, tl, ch + 1, 1 - p).start()
            else:
                @pl.when(tl + 1 < SC_TOKENS)
                def _():
                    gather(slot, tl + 1, 0, 1 - p).start()

                @pl.when(jnp.logical_and(tl + 1 == SC_TOKENS, more))
                def _():
                    prefetch_wait()
                    gather(1 - slot, 0, 0, 1 - p).start()
            gather(slot, tl, ch, p).wait()
            contract(slot, tl, ch, rbufs[p])

        @pl.when(tl + 1 == SC_TOKENS)
        def _():
            @pl.when(b >= 1)
            def _():
                store(b - 1, 1 - slot).wait()
            store(b, slot).start()

    store(n_batches - 1, (n_batches - 1) % 2).wait()


def _sc_batches(n):
    assert n % (SC_WORKERS * SC_TOKENS) == 0, n
    return n // (SC_WORKERS * SC_TOKENS)


def _peer_hval(hn, eid, peer_u, after):
    n = hn.shape[0]
    nbw = _sc_batches(n)

    @functools.partial(
        pl.kernel, mesh=_sc_mesh(),
        out_type=jax.ShapeDtypeStruct((n // SC_TOKENS, SC_TOKENS, PEER_SEL), F32),
        scratch_types=[
            pltpu.VMEM((2, SC_TOKENS, PEER_SEL), jnp.int32),
            pltpu.VMEM((2, SC_TOKENS, D_MODEL), F32),
            pltpu.VMEM((SC_ROWS, D_MODEL), F32),
            pltpu.VMEM((SC_ROWS, D_MODEL), F32),
            pltpu.VMEM((2, SC_TOKENS, PEER_SEL), F32),
            pltpu.VMEM((SC_GROUP, SC_LANES), F32),
            pltpu.SemaphoreType.DMA, pltpu.SemaphoreType.DMA,
            pltpu.SemaphoreType.DMA, pltpu.SemaphoreType.DMA,
        ],
        compiler_params=_SC_PARAMS, name="peer_hval",
        cost_estimate=pl.CostEstimate(flops=2 * n * PEER_SEL * D_MODEL, transcendentals=0,
                                      bytes_accessed=n * PEER_SEL * D_MODEL * 4))
    def k(x_hbm, eid_hbm, u_hbm, after_hbm, o_hbm, eid_v, x_v, r0, r1, h_v, tr, sr0, sr1, spf, sout):
        del after_hbm
        blk0 = (lax.axis_index("s") * SC_CORES + lax.axis_index("c")) * nbw
        lane = lax.iota(jnp.int32, SC_LANES)
        zero = jnp.zeros((SC_LANES,), F32)

        def prefetch(b, slot):
            pltpu.async_copy(eid_hbm.at[blk0 + b], eid_v.at[slot], spf)
            pltpu.async_copy(x_hbm.at[blk0 + b], x_v.at[slot], spf)

        def prefetch_wait():
            pltpu.make_async_copy(eid_hbm.at[0], eid_v.at[0], spf).wait()
            pltpu.make_async_copy(x_hbm.at[0], x_v.at[0], spf).wait()

        def store(b, slot):
            return pltpu.make_async_copy(h_v.at[slot], o_hbm.at[blk0 + b], sout)

        def contract(slot, tl, ch, rbuf):
            for g in range(SC_ROWS // SC_GROUP):
                def body(c, accs):
                    xc = x_v[slot, tl, pl.ds(c * SC_LANES, SC_LANES)]
                    return tuple(
                        accs[r] + rbuf[g * SC_GROUP + r, pl.ds(c * SC_LANES, SC_LANES)] * xc
                        for r in range(SC_GROUP))
                accs = lax.fori_loop(0, D_MODEL // SC_LANES, body, (zero,) * SC_GROUP)
                for r in range(SC_GROUP):
                    tr[r, :] = accs[r]
                res = zero
                for jj in range(SC_LANES):
                    res = res + plsc.load_gather(tr, [lane, jnp.full((SC_LANES,), jj, jnp.int32)])
                h_v[slot, tl, pl.ds(ch * SC_ROWS + g * SC_GROUP, SC_GROUP)] = res

        _sc_token_loop(nbw, u_hbm, eid_v, (r0, r1), (sr0, sr1), prefetch, prefetch_wait, store,
                       contract)

    out = k(hn.reshape(n // SC_TOKENS, SC_TOKENS, D_MODEL),
            eid.reshape(n // SC_TOKENS, SC_TOKENS, PEER_SEL), peer_u, after)
    return out.reshape(n, PEER_SEL)


def _peer_out(h, a, eid, peer_v):
    n = h.shape[0]
    nbw = _sc_batches(n)

    @functools.partial(
        pl.kernel, mesh=_sc_mesh(),
        out_type=jax.ShapeDtypeStruct((n // SC_TOKENS, SC_TOKENS, D_MODEL), F32),
        scratch_types=[
            pltpu.VMEM((2, SC_TOKENS, PEER_SEL), jnp.int32),
            pltpu.VMEM((2, SC_TOKENS, PEER_SEL), F32),
            pltpu.VMEM((SC_ROWS, D_MODEL), F32),
            pltpu.VMEM((SC_ROWS, D_MODEL), F32),
            pltpu.VMEM((2, SC_TOKENS, D_MODEL), F32),
            pltpu.SemaphoreType.DMA, pltpu.SemaphoreType.DMA,
            pltpu.SemaphoreType.DMA, pltpu.SemaphoreType.DMA,
        ],
        compiler_params=_SC_PARAMS, name="peer_out")
    def k(h_hbm, a_hbm, eid_hbm, v_hbm, o_hbm, eid_v, a_v, r0, r1, y_v, sr0, sr1, spf, sout):
        blk0 = (lax.axis_index("s") * SC_CORES + lax.axis_index("c")) * nbw

        def prefetch(b, slot):
            pltpu.async_copy(eid_hbm.at[blk0 + b], eid_v.at[slot], spf)
            pltpu.async_copy(a_hbm.at[blk0 + b], a_v.at[slot], spf)
            pltpu.async_copy(h_hbm.at[blk0 + b], y_v.at[slot], spf)

        def prefetch_wait():
            pltpu.make_async_copy(eid_hbm.at[0], eid_v.at[0], spf).wait()
            pltpu.make_async_copy(a_hbm.at[0], a_v.at[0], spf).wait()
            pltpu.make_async_copy(h_hbm.at[0], y_v.at[0], spf).wait()

        def store(b, slot):
            return pltpu.make_async_copy(y_v.at[slot], o_hbm.at[blk0 + b], sout)

        def contract(slot, tl, ch, rbuf):
            slot_v = jnp.full((SC_LANES,), slot, jnp.int32)
            tl_v = jnp.full((SC_LANES,), tl, jnp.int32)
            for g in range(SC_ROWS // SC_GROUP):
                base = ch * SC_ROWS + g * SC_GROUP
                coef = [plsc.load_gather(a_v, [slot_v, tl_v, jnp.full((SC_LANES,), base + r, jnp.int32)])
                        for r in range(SC_GROUP)]

                @plsc.parallel_loop(0, D_MODEL // SC_LANES)
                def _(c):
                    sl = pl.ds(c * SC_LANES, SC_LANES)
                    acc = coef[0] * rbuf[g * SC_GROUP, sl]
                    for r in range(1, SC_GROUP):
                        acc = acc + coef[r] * rbuf[g * SC_GROUP + r, sl]
                    plsc.addupdate(y_v.at[slot, tl, sl], acc)

        _sc_token_loop(nbw, v_hbm, eid_v, (r0, r1), (sr0, sr1), prefetch, prefetch_wait, store,
                       contract)

    out = k(h.reshape(n // SC_TOKENS, SC_TOKENS, D_MODEL),
            a.reshape(n // SC_TOKENS, SC_TOKENS, PEER_SEL),
            eid.reshape(n // SC_TOKENS, SC_TOKENS, PEER_SEL), peer_v)
    return out.reshape(n, D_MODEL)


def _gate_kernel(hv_ref, gate_ref, a_out):
    hv = hv_ref[...]
    gelu = hv * (lax.erf(hv * (2.0 ** -0.5)) + 1.0) * 0.5
    a_out[...] = gate_ref[...] * gelu


def _gate(hval, gate, tb):
    n = hval.shape[0]
    row = pl.BlockSpec((tb, PEER_SEL), lambda i: (i, 0))
    return pl.pallas_call(
        _gate_kernel, grid=(n // tb,), in_specs=[row, row], out_specs=row,
        out_shape=jax.ShapeDtypeStruct((n, PEER_SEL), F32),
        compiler_params=_tc_params(1), name="gate",
    )(hval, gate)


VT_ROWS = D_MODEL // 2 // LANES
VT_TOKENS = 64
HI_MASK = -65536


def _pack_value_table(peer_v):
    e = peer_v.shape[0]
    bits = lax.bitcast_convert_type(peer_v.astype(BF16), jnp.uint16).astype(jnp.uint32)
    words = bits[:, :D_MODEL // 2] | (bits[:, D_MODEL // 2:] << 16)
    return lax.bitcast_convert_type(words, jnp.int32).reshape(e * VT_ROWS, LANES)


def _vside_kernel(row_s, a_s, h_ref, tab_ref, y_ref):
    tb = h_ref.shape[0]

    def token(t, carry):
        hv = h_ref[t]
        zero = jnp.zeros((VT_ROWS, LANES), F32)
        lo = [hv[0:VT_ROWS], zero]
        hi = [hv[VT_ROWS:], zero]
        for e in range(PEER_SEL):
            r0 = pl.multiple_of(row_s[t, e], VT_ROWS)
            w = tab_ref[pl.ds(r0, VT_ROWS), :]
            coef = a_s[t, e]
            c = e % 2
            lo[c] = lo[c] + coef * pltpu.bitcast(w << 16, F32)
            hi[c] = hi[c] + coef * pltpu.bitcast(w & HI_MASK, F32)
        y_ref[t] = jnp.concatenate([lo[0] + lo[1], hi[0] + hi[1]], axis=0)
        return carry

    lax.fori_loop(0, tb, token, 0)


def _peer_values(h, a, row_ids, v_packed):
    n = h.shape[0]
    tb = min(VT_TOKENS, n)
    smem = pl.BlockSpec((tb, PEER_SEL), lambda i: (i, 0), memory_space=pltpu.SMEM)
    slab = pl.BlockSpec((tb, 2 * VT_ROWS, LANES), lambda i: (i, 0, 0))
    table = pl.BlockSpec(v_packed.shape, lambda i: (0, 0), pipeline_mode=pl.Buffered(1))
    table_bytes = v_packed.shape[0] * LANES * 4
    y = pl.pallas_call(
        _vside_kernel,
        grid=(n // tb,),
        in_specs=[smem, smem, slab, table],
        out_specs=slab,
        out_shape=jax.ShapeDtypeStruct((n, 2 * VT_ROWS, LANES), F32),
        compiler_params=pltpu.CompilerParams(
            dimension_semantics=("arbitrary",),
            vmem_limit_bytes=table_bytes + 8 * 1024 * 1024),
        name="peer_values",
        cost_estimate=pl.CostEstimate(flops=2 * n * PEER_SEL * D_MODEL, transcendentals=0,
                                      bytes_accessed=table_bytes + n * PEER_SEL * D_MODEL * 2),
    )(row_ids, a, h.reshape(n, 2 * VT_ROWS, LANES), v_packed)
    return y.reshape(n, D_MODEL)


PEER_CHUNK = 4096


PEER_DEPTH = 2


def _peer(h, hn, tb, **weights):
    n = h.shape[0]
    ys = []
    for c in range(0, n, PEER_CHUNK):
        k = len(ys)
        after = ys[k - PEER_DEPTH][:SC_TOKENS] if k >= PEER_DEPTH else jnp.zeros((SC_TOKENS, D_MODEL), F32)
        ys.append(_peer_chunk(h[c:c + PEER_CHUNK], hn[c:c + PEER_CHUNK], tb, after, **weights))
    return ys[0] if len(ys) == 1 else jnp.concatenate(ys, axis=0)


def _peer_chunk(h, hn, tb, after, wq_bf, keys1_bf, keys2_bf, peer_u, v_packed):
    n = h.shape[0]
    eid_t, gate_t = _route(hn, tb, wq_bf, keys1_bf, keys2_bf)
    eid = eid_t.T
    pad = (-n) % (SC_WORKERS * SC_TOKENS)
    hn_p, eid_p = hn, eid
    if pad:
        spread = (jnp.arange(pad * PEER_SEL, dtype=jnp.int32) % peer_u.shape[0]).reshape(pad, PEER_SEL)
        hn_p = jnp.pad(hn, ((0, pad), (0, 0)))
        eid_p = jnp.concatenate([eid, spread], axis=0)
    hval = _peer_hval(hn_p, eid_p, peer_u, after)[:n]
    a = _gate(hval, gate_t.T, min(tb, 512))
    return _peer_values(h, a, eid * VT_ROWS, v_packed)


def _rope_tables(pos):
    half = HEAD_DIM // 2
    inv = ROPE_THETA ** (-jnp.arange(half, dtype=F32) / half)
    ang = pos.astype(F32)[:, None] * inv[None, :]
    cos = jnp.cos(ang)
    sin = jnp.sin(ang)
    reps = LANES // HEAD_DIM
    cos_f = jnp.tile(jnp.concatenate([cos, cos], axis=1), (1, reps))
    sin_s = jnp.tile(jnp.concatenate([-sin, sin], axis=1), (1, reps))
    return cos_f, sin_s


def _group_sum_matrix(width):
    g = jnp.arange(width) // HEAD_DIM
    return (g[:, None] == g[None, :]).astype(BF16)


def _heads_first(a, heads):
    b, t, _ = a.shape
    return a.reshape(b, t, heads, HEAD_DIM).transpose(0, 2, 1, 3)


def kernel(x_prompt, x_sample, cache_k_win, cache_v_win, state_conv, meta_tokens, norm_mix_g,
           w_in, q_norm_g, k_norm_g, attn_sinks, conv_w, conv_b, conv_ln_g, conv_ln_b, w_pw2,
           out_norm_attn_g, out_norm_conv_g, w_out, norm_ffn_g, peer_w_q, peer_keys1, peer_keys2,
           peer_u, peer_v):
    assert norm_mix_g.shape[0] == 1, "single-layer model"
    b, t, _ = x_prompt.shape
    s = x_sample.shape[0]
    w_buf = cache_k_win.shape[2]
    n = b * t

    g_mix = norm_mix_g[0][None, :]
    w_in_bf = w_in[0].astype(BF16)
    qg_t = jnp.tile(q_norm_g[0], N_HEADS)[None, :]
    kg_t = jnp.tile(k_norm_g[0], N_KV_HEADS)[None, :]
    gsum_q = _group_sum_matrix(D_ATTN)
    gsum_k = _group_sum_matrix(KV_DIM)
    sinks = attn_sinks[0]
    cw, cb = conv_w[0], conv_b[0][None, :]
    lg, lb = conv_ln_g[0][None, :], conv_ln_b[0][None, :]
    w2_bf = w_pw2[0].astype(BF16)
    g_a, g_c = out_norm_attn_g[0][None, :], out_norm_conv_g[0][None, :]
    w_out_a = w_out[0][:D_ATTN].astype(BF16)
    w_out_c = w_out[0][D_ATTN:].astype(BF16)
    g_f = norm_ffn_g[0][None, :]
    wq_bf = peer_w_q[0].astype(BF16)
    k1_bf = peer_keys1[0].astype(BF16)
    k2_bf = peer_keys2[0].astype(BF16)
    pu, vp = peer_u[0], _pack_value_table(peer_v[0])
    proj = functools.partial(_project, norm_g=g_mix, w_in_bf=w_in_bf, qg_t=qg_t, kg_t=kg_t,
                             gsum_q=gsum_q, gsum_k=gsum_k)
    conv = functools.partial(_conv_branch, conv_w=cw, conv_b=cb, ln_g=lg, ln_b=lb, w_pw2_bf=w2_bf)
    merge = functools.partial(_merge, g_a=g_a, g_c=g_c, w_out_a=w_out_a, w_out_c=w_out_c, g_f=g_f)
    peer = functools.partial(_peer, wq_bf=wq_bf, keys1_bf=k1_bf, keys2_bf=k2_bf, peer_u=pu, v_packed=vp)

    tb = 512
    xp = x_prompt.reshape(n, D_MODEL)
    tab_x = _rope_tables(N_META + jnp.arange(t, dtype=jnp.int32))
    q, k, v, u = proj(xp, tab_x, t // tb, tb)
    tab_m = _rope_tables(jnp.arange(N_META, dtype=jnp.int32))
    _, k_m, v_m, u_m = proj(meta_tokens, tab_m, 1, N_META)

    def with_meta_block(a, a_m):
        blk0 = jnp.concatenate([jnp.zeros((META_PAD, KV_DIM), F32), a_m], axis=0)
        full = jnp.concatenate([jnp.broadcast_to(blk0[None], (b, BLOCK, KV_DIM)),
                                a.reshape(b, t, KV_DIM)], axis=1)
        return _heads_first(full, N_KV_HEADS)

    o_attn = _prompt_attention(sinks, _heads_first(q.reshape(b, t, D_ATTN), N_HEADS),
                               with_meta_block(k, k_m), with_meta_block(v, v_m))
    o_attn = o_attn.transpose(0, 2, 1, 3).reshape(n, D_ATTN)

    u3 = u.reshape(b, t // tb, tb, D_CONV)
    halo0 = jnp.concatenate([jnp.zeros((CONV_HALO - N_META, D_CONV), F32), u_m], axis=0)
    halo = jnp.concatenate([jnp.broadcast_to(halo0[None, None], (b, 1, CONV_HALO, D_CONV)),
                            u3[:, :-1, tb - CONV_HALO:]], axis=1).reshape(n // tb, CONV_HALO, D_CONV)
    o_conv = conv(halo, u, tb)

    h, hn = merge(xp, o_attn, o_conv, tb)
    y_prompt = peer(h, hn, 256).reshape(b, t, D_MODEL)

    new_k_prompt = k.reshape(b, t, N_KV_HEADS, HEAD_DIM)[None, :, t - WINDOW:]
    new_v_prompt = v.reshape(b, t, N_KV_HEADS, HEAD_DIM)[None, :, t - WINDOW:]
    new_conv_prompt = u.reshape(b, t, D_CONV)[None, :, t - (CONV_WIDTH - 1):]

    xs = x_sample.reshape(s, D_MODEL)
    tab_s = _rope_tables(jnp.full((s,), PAST_LEN, jnp.int32))
    qs, ks, vs, us = proj(xs, tab_s, 1, s)
    ck = cache_k_win[0]
    cv = cache_v_win[0]
    o_attn_s = _decode_attention(
        sinks,
        qs.reshape(s, N_HEADS, HEAD_DIM),
        ks.reshape(s, N_KV_HEADS, HEAD_DIM).transpose(1, 0, 2),
        vs.reshape(s, N_KV_HEADS, HEAD_DIM).transpose(1, 0, 2),
        ck.transpose(2, 0, 1, 3), cv.transpose(2, 0, 1, 3), 32)
    o_attn_s = o_attn_s.reshape(s, D_ATTN)
    cs = state_conv[0]
    hist = jnp.concatenate([jnp.zeros((s, CONV_HALO - (CONV_WIDTH - 1), D_CONV), F32), cs], axis=1)
    us_blk = jnp.concatenate([us[:, None, :], jnp.zeros((s, 7, D_CONV), F32)], axis=1)
    o_conv_s = conv(hist, us_blk.reshape(s * 8, D_CONV), 8).reshape(s, 8, D_CONV)[:, 0]
    hs, hns = merge(xs, o_attn_s, o_conv_s, s)
    y_sample = peer(hs, hns, s).reshape(s, 1, D_MODEL)

    new_k_sample = jnp.concatenate([ck[:, 1:], ks.reshape(s, 1, N_KV_HEADS, HEAD_DIM)], axis=1)[None]
    new_v_sample = jnp.concatenate([cv[:, 1:], vs.reshape(s, 1, N_KV_HEADS, HEAD_DIM)], axis=1)[None]
    new_conv_sample = jnp.concatenate([cs[:, 1:], us[:, None, :]], axis=1)[None]
    if w_buf != WINDOW:
        raise NotImplementedError("cache window shorter than the attention window")

    return (y_prompt, y_sample, new_k_prompt, new_v_prompt, new_conv_prompt,
            new_k_sample, new_v_sample, new_conv_sample)
```

```python
import functools

import jax
import jax.numpy as jnp
from jax import lax
from jax.experimental import pallas as pl
from jax.experimental.pallas import tpu as pltpu
from jax.experimental.pallas import tpu_sc as plsc

D_MODEL = 1024
HEAD_DIM = 64
D_ATTN = 512
N_HEADS = 8
N_KV_HEADS = 2
KV_DIM = N_KV_HEADS * HEAD_DIM
D_CONV = 512
D_IN = D_ATTN + 2 * KV_DIM + 2 * D_CONV
CONV_WIDTH = 31
WINDOW = 128
BLOCK = 128
ROPE_THETA = 10000.0
N_META = 16
META_PAD = BLOCK - N_META
PEER_HEADS = 8
PEER_NKEYS = 128
PEER_TOPK = 16
PEER_SEL = PEER_HEADS * PEER_TOPK
EPS = 1e-6
PAST_LEN = 16384

LANES = 128
SC_CORES = 2
SC_SUBCORES = 16
SC_LANES = 16
SC_WORKERS = SC_CORES * SC_SUBCORES
VMEM_LIMIT = 48 * 1024 * 1024

F32 = jnp.float32
BF16 = jnp.bfloat16
NEG_INF = float("-inf")


def _tc_params(n_axes):
    return pltpu.CompilerParams(dimension_semantics=("arbitrary",) * n_axes,
                                vmem_limit_bytes=VMEM_LIMIT)


def _full(shape):
    nd = len(shape)
    return pl.BlockSpec(shape, lambda *_: (0,) * nd)


def _group_mean(sq, gsum_ref):
    hi = sq.astype(BF16)
    lo = (sq - hi.astype(F32)).astype(BF16)
    g = gsum_ref[...]
    s = jnp.dot(hi, g, preferred_element_type=F32) + jnp.dot(lo, g, preferred_element_type=F32)
    return s * (1.0 / HEAD_DIM)


def _rope(xn, cos_f, sin_s, first_half):
    outs = []
    for s in range(xn.shape[1] // LANES):
        xs = xn[:, s * LANES:(s + 1) * LANES]
        partner = jnp.where(first_half, pltpu.roll(xs, LANES - HEAD_DIM // 2, axis=1),
                            pltpu.roll(xs, HEAD_DIM // 2, axis=1))
        outs.append(xs * cos_f + partner * sin_s)
    return outs[0] if len(outs) == 1 else jnp.concatenate(outs, axis=1)


def _proj_kernel(x_ref, g_ref, w_ref, qg_ref, kg_ref, cos_ref, sin_ref, gq_ref, gk_ref, after_ref,
                 q_out, k_out, v_out, u_out):
    del after_ref
    x = x_ref[...]
    n = x * lax.rsqrt(jnp.mean(x * x, axis=-1, keepdims=True) + EPS) * g_ref[...]
    p = jnp.dot(n.astype(BF16), w_ref[...], preferred_element_type=F32)
    q = p[:, :D_ATTN]
    k = p[:, D_ATTN:D_ATTN + KV_DIM]
    v = p[:, D_ATTN + KV_DIM:D_ATTN + 2 * KV_DIM]
    ga = p[:, D_ATTN + 2 * KV_DIM:D_ATTN + 2 * KV_DIM + D_CONV]
    gb = p[:, D_ATTN + 2 * KV_DIM + D_CONV:]
    cos_f = cos_ref[...]
    sin_s = sin_ref[...]
    lane = lax.broadcasted_iota(jnp.int32, (x.shape[0], LANES), 1)
    first_half = (lane % HEAD_DIM) < (HEAD_DIM // 2)
    qn = q * lax.rsqrt(_group_mean(q * q, gq_ref) + EPS) * qg_ref[...]
    kn = k * lax.rsqrt(_group_mean(k * k, gk_ref) + EPS) * kg_ref[...]
    q_out[...] = _rope(qn, cos_f, sin_s, first_half)
    k_out[...] = _rope(kn, cos_f, sin_s, first_half)
    v_out[...] = v
    u_out[...] = ga * (1.0 / (1.0 + jnp.exp(-gb)))


def _project(x, pos_tables, n_table_blocks, tb, after, norm_g, w_in_bf, qg_t, kg_t, gsum_q, gsum_k):
    n = x.shape[0]
    cos_t, sin_t = pos_tables
    nb = n // tb
    tab_spec = pl.BlockSpec((tb, LANES), lambda i: (i % n_table_blocks, 0))
    row = lambda w: pl.BlockSpec((tb, w), lambda i: (i, 0))
    return pl.pallas_call(
        _proj_kernel,
        grid=(nb,),
        in_specs=[row(D_MODEL), _full((1, D_MODEL)), _full((D_MODEL, D_IN)),
                  _full((1, D_ATTN)), _full((1, KV_DIM)), tab_spec, tab_spec,
                  _full((D_ATTN, D_ATTN)), _full((KV_DIM, KV_DIM)),
                  pl.BlockSpec(memory_space=pl.ANY)],
        out_specs=[row(D_ATTN), row(KV_DIM), row(KV_DIM), row(D_CONV)],
        out_shape=[jax.ShapeDtypeStruct((n, D_ATTN), F32), jax.ShapeDtypeStruct((n, KV_DIM), F32),
                   jax.ShapeDtypeStruct((n, KV_DIM), F32), jax.ShapeDtypeStruct((n, D_CONV), F32)],
        compiler_params=_tc_params(1),
        name="proj",
    )(x, norm_g, w_in_bf, qg_t, kg_t, cos_t, sin_t, gsum_q, gsum_k, after)


def _attn_kernel(sink_ref, q_ref, kp_ref, kc_ref, vp_ref, vc_ref, o_ref):
    j = pl.program_id(1)
    r = lax.broadcasted_iota(jnp.int32, (BLOCK, 2 * BLOCK), 0)
    c = lax.broadcasted_iota(jnp.int32, (BLOCK, 2 * BLOCK), 1)
    ok = (c > r) & (c <= r + WINDOW) & ((j > 0) | (c >= META_PAD))
    grp = N_HEADS // N_KV_HEADS
    ok = jnp.concatenate([ok] * grp, axis=0)
    for g in range(N_KV_HEADS):
        q4 = q_ref[0, g * grp:(g + 1) * grp].reshape(grp * BLOCK, HEAD_DIM).astype(BF16)
        k = jnp.concatenate([kp_ref[0, g], kc_ref[0, g]], axis=0).astype(BF16)
        v = jnp.concatenate([vp_ref[0, g], vc_ref[0, g]], axis=0).astype(BF16)
        s = lax.dot_general(q4, k, (((1,), (1,)), ((), ())), preferred_element_type=F32)
        s = jnp.where(ok, s * (HEAD_DIM ** -0.5), NEG_INF)
        sink = jnp.concatenate(
            [jnp.full((BLOCK, 1), sink_ref[g * grp + i], F32) for i in range(grp)], axis=0)
        m = jnp.maximum(jnp.max(s, axis=1, keepdims=True), sink)
        p = jnp.exp(s - m)
        den = jnp.sum(p, axis=1, keepdims=True) + jnp.exp(sink - m)
        o = jnp.dot(p.astype(BF16), v, preferred_element_type=F32) / den
        o_ref[0, g * grp:(g + 1) * grp] = o.reshape(grp, BLOCK, HEAD_DIM)


def _prompt_attention(sinks, q_t, k_t, v_t):
    b, _, t, _ = q_t.shape
    nb = t // BLOCK
    kv_prev = pl.BlockSpec((1, N_KV_HEADS, BLOCK, HEAD_DIM), lambda bi, j: (bi, 0, j, 0))
    kv_cur = pl.BlockSpec((1, N_KV_HEADS, BLOCK, HEAD_DIM), lambda bi, j: (bi, 0, j + 1, 0))
    q_spec = pl.BlockSpec((1, N_HEADS, BLOCK, HEAD_DIM), lambda bi, j: (bi, 0, j, 0))
    return pl.pallas_call(
        _attn_kernel,
        grid=(b, nb),
        in_specs=[pl.BlockSpec(memory_space=pltpu.SMEM), q_spec, kv_prev, kv_cur, kv_prev, kv_cur],
        out_specs=q_spec,
        out_shape=jax.ShapeDtypeStruct(q_t.shape, F32),
        compiler_params=_tc_params(2),
        name="attn",
    )(sinks, q_t, k_t, k_t, v_t, v_t)


def _dec_attn_kernel(sink_ref, q_ref, kn_ref, vn_ref, ck_ref, cv_ref, o_ref):
    grp = N_HEADS // N_KV_HEADS
    sb, w_buf = ck_ref.shape[1], ck_ref.shape[2]
    q = q_ref[...]
    qb = q.astype(BF16)
    head = lax.broadcasted_iota(jnp.int32, (sb, N_HEADS, 1), 1)
    in_g0 = head < grp
    s_g = [jnp.einsum("shd,swd->shw", qb, ck_ref[g].astype(BF16), preferred_element_type=F32)
           for g in range(N_KV_HEADS)]
    s = jnp.where(in_g0, s_g[0], s_g[1]) * (HEAD_DIM ** -0.5)
    key_ok = lax.broadcasted_iota(jnp.int32, (sb, N_HEADS, w_buf), 2) >= 1
    s = jnp.where(key_ok, s, NEG_INF)
    rnd = lambda a: a.astype(BF16).astype(F32)
    kn = jnp.where(in_g0, kn_ref[0][:, None, :], kn_ref[1][:, None, :])
    vn = jnp.where(in_g0, vn_ref[0][:, None, :], vn_ref[1][:, None, :])
    s_self = jnp.sum(rnd(q) * rnd(kn), axis=-1, keepdims=True) * (HEAD_DIM ** -0.5)
    sink = sink_ref[...]
    m = jnp.maximum(jnp.maximum(jnp.max(s, axis=-1, keepdims=True), s_self), sink)
    p = jnp.exp(s - m)
    p_self = jnp.exp(s_self - m)
    den = jnp.sum(p, axis=-1, keepdims=True) + p_self + jnp.exp(sink - m)
    pb = p.astype(BF16)
    o_g = [jnp.einsum("shw,swd->shd", pb, cv_ref[g].astype(BF16), preferred_element_type=F32)
           for g in range(N_KV_HEADS)]
    o = jnp.where(in_g0, o_g[0], o_g[1]) + rnd(p_self) * rnd(vn)
    o_ref[...] = o / den


def _decode_attention(sinks, q3, kn_t, vn_t, ck_t, cv_t, sb):
    s = q3.shape[0]
    w_buf = ck_t.shape[2]
    qs = pl.BlockSpec((sb, N_HEADS, HEAD_DIM), lambda i: (i, 0, 0))
    ns = pl.BlockSpec((N_KV_HEADS, sb, HEAD_DIM), lambda i: (0, i, 0))
    cs = pl.BlockSpec((N_KV_HEADS, sb, w_buf, HEAD_DIM), lambda i: (0, i, 0, 0))
    return pl.pallas_call(
        _dec_attn_kernel,
        grid=(s // sb,),
        in_specs=[_full((1, N_HEADS, 1)), qs, ns, ns, cs, cs],
        out_specs=qs,
        out_shape=jax.ShapeDtypeStruct(q3.shape, F32),
        compiler_params=_tc_params(1),
        name="dec_attn",
    )(sinks.reshape(1, N_HEADS, 1), q3, kn_t, vn_t, ck_t, cv_t)


CONV_HALO = 32
CONV_ROWS = 64


def _conv_kernel(halo_ref, u_ref, cw_ref, cb_ref, lg_ref, lb_ref, w2_ref, o_ref, ucat):
    tb = u_ref.shape[0]
    ucat[0:CONV_HALO, :] = halo_ref[0]
    ucat[CONV_HALO:, :] = u_ref[...]
    first = CONV_HALO - (CONV_WIDTH - 1)
    rows = min(CONV_ROWS, tb)
    for r0 in range(0, tb, rows):
        acc = jnp.zeros((rows, D_CONV), F32)
        for j in range(CONV_WIDTH):
            acc = acc + ucat[r0 + first + j:r0 + first + j + rows, :] * cw_ref[j:j + 1, :]
        y = acc + cb_ref[...]
        yc = y - jnp.mean(y, axis=-1, keepdims=True)
        yn = yc * lax.rsqrt(jnp.mean(yc * yc, axis=-1, keepdims=True) + EPS)
        yn = yn * lg_ref[...] + lb_ref[...]
        act = yn * (1.0 / (1.0 + jnp.exp(-yn)))
        o_ref[r0:r0 + rows, :] = jnp.dot(act.astype(BF16), w2_ref[...],
                                              preferred_element_type=F32)


def _conv_branch(halo, u, tb, conv_w, conv_b, ln_g, ln_b, w_pw2_bf):
    n = u.shape[0]
    return pl.pallas_call(
        _conv_kernel,
        grid=(n // tb,),
        in_specs=[pl.BlockSpec((1, CONV_HALO, D_CONV), lambda i: (i, 0, 0)),
                  pl.BlockSpec((tb, D_CONV), lambda i: (i, 0)),
                  _full((CONV_WIDTH, D_CONV)), _full((1, D_CONV)), _full((1, D_CONV)),
                  _full((1, D_CONV)), _full((D_CONV, D_CONV))],
        out_specs=pl.BlockSpec((tb, D_CONV), lambda i: (i, 0)),
        out_shape=jax.ShapeDtypeStruct((n, D_CONV), F32),
        scratch_shapes=[pltpu.VMEM((tb + CONV_HALO, D_CONV), F32)],
        compiler_params=_tc_params(1),
        name="conv",
    )(halo, u, conv_w, conv_b, ln_g, ln_b, w_pw2_bf)


def _rms(x, g):
    return x * lax.rsqrt(jnp.mean(x * x, axis=-1, keepdims=True) + EPS) * g


def _merge_kernel(x_ref, oa_ref, oc_ref, ga_ref, gc_ref, wa_ref, wc_ref, gf_ref, h_out, hn_out):
    a = _rms(oa_ref[...], ga_ref[...]).astype(BF16)
    c = _rms(oc_ref[...], gc_ref[...]).astype(BF16)
    h = x_ref[...] + (jnp.dot(a, wa_ref[...], preferred_element_type=F32)
                      + jnp.dot(c, wc_ref[...], preferred_element_type=F32))
    h_out[...] = h
    hn_out[...] = _rms(h, gf_ref[...])


def _merge(x, oa, oc, tb, g_a, g_c, w_out_a, w_out_c, g_f):
    n = x.shape[0]
    row = lambda w: pl.BlockSpec((tb, w), lambda i: (i, 0))
    return pl.pallas_call(
        _merge_kernel,
        grid=(n // tb,),
        in_specs=[row(D_MODEL), row(D_ATTN), row(D_CONV), _full((1, D_ATTN)), _full((1, D_CONV)),
                  _full((D_ATTN, D_MODEL)), _full((D_CONV, D_MODEL)), _full((1, D_MODEL))],
        out_specs=[row(D_MODEL), row(D_MODEL)],
        out_shape=[jax.ShapeDtypeStruct((n, D_MODEL), F32)] * 2,
        compiler_params=_tc_params(1),
        name="merge",
    )(x, oa, oc, g_a, g_c, w_out_a, w_out_c, g_f)


ID_BIG = 1e9


def _topk_rows(s, k):
    rows = lax.broadcasted_iota(jnp.int32, s.shape, 0).astype(F32)
    vals, idxs = [], []
    for _ in range(k):
        m = jnp.max(s, axis=0, keepdims=True)
        idx = jnp.min(jnp.where(s == m, rows, ID_BIG), axis=0, keepdims=True)
        vals.append(m)
        idxs.append(idx)
        s = jnp.where(rows == idx, NEG_INF, s)
    return jnp.concatenate(vals, axis=0), jnp.concatenate(idxs, axis=0)


PAIR_B_WIDE = 8


def _route_kernel(hn_ref, wq_ref, k1_ref, k2_ref, eid_out, gate_out):
    tb = hn_ref.shape[0]
    q = jnp.dot(hn_ref[...].astype(BF16), wq_ref[...], preferred_element_type=F32).astype(BF16)
    k1 = k1_ref[...]
    k2 = k2_ref[...]
    nt = (((1,), (1,)), ((), ()))
    r = lax.broadcasted_iota(jnp.int32, (PEER_TOPK + (PAIR_B_WIDE - 1) * PAIR_B_WIDE + PAIR_B_WIDE, tb), 0)
    mid = r - PEER_TOPK
    flat = jnp.where(r < PEER_TOPK, r,
                     jnp.where(mid < (PAIR_B_WIDE - 1) * PAIR_B_WIDE,
                               (1 + mid // PAIR_B_WIDE) * PEER_TOPK + mid % PAIR_B_WIDE,
                               (PAIR_B_WIDE + mid - (PAIR_B_WIDE - 1) * PAIR_B_WIDE) * PEER_TOPK)).astype(F32)
    half = PEER_NKEYS
    for h in range(PEER_HEADS):
        q1 = q[:, (2 * h) * half:(2 * h + 1) * half]
        q2 = q[:, (2 * h + 1) * half:(2 * h + 2) * half]
        s1 = lax.dot_general(k1, q1, nt, preferred_element_type=F32)
        s2 = lax.dot_general(k2, q2, nt, preferred_element_type=F32)
        v1, i1 = _topk_rows(s1, PEER_TOPK)
        v2, i2 = _topk_rows(s2, PEER_TOPK)
        e1 = i1 * PEER_NKEYS
        cand = jnp.concatenate(
            [v1[0:1] + v2]
            + [v1[a:a + 1] + v2[0:PAIR_B_WIDE] for a in range(1, PAIR_B_WIDE)]
            + [v1[PAIR_B_WIDE:] + v2[0:1]], axis=0)
        cid = jnp.concatenate(
            [e1[0:1] + i2]
            + [e1[a:a + 1] + i2[0:PAIR_B_WIDE] for a in range(1, PAIR_B_WIDE)]
            + [e1[PAIR_B_WIDE:] + i2[0:1]], axis=0)
        scs, eids = [], []
        for _ in range(PEER_TOPK):
            m = jnp.max(cand, axis=0, keepdims=True)
            jsel = jnp.min(jnp.where(cand == m, flat, ID_BIG), axis=0, keepdims=True)
            hit = flat == jsel
            eids.append(jnp.max(jnp.where(hit, cid, -1.0), axis=0, keepdims=True))
            scs.append(m)
            cand = jnp.where(hit, NEG_INF, cand)
        sc = jnp.concatenate(scs, axis=0)
        e = jnp.exp(sc - sc[0:1])
        gate_out[h * PEER_TOPK:(h + 1) * PEER_TOPK, :] = e / jnp.sum(e, axis=0, keepdims=True)
        eid_out[h * PEER_TOPK:(h + 1) * PEER_TOPK, :] = jnp.concatenate(eids, axis=0).astype(jnp.int32)


def _route(hn, tb, wq_bf, keys1_bf, keys2_bf):
    n = hn.shape[0]
    col = pl.BlockSpec((PEER_SEL, tb), lambda i: (0, i))
    return pl.pallas_call(
        _route_kernel,
        grid=(n // tb,),
        in_specs=[pl.BlockSpec((tb, D_MODEL), lambda i: (i, 0)),
                  _full((D_MODEL, 2 * PEER_NKEYS * PEER_HEADS)),
                  _full((PEER_NKEYS, PEER_NKEYS)), _full((PEER_NKEYS, PEER_NKEYS))],
        out_specs=[col, col],
        out_shape=[jax.ShapeDtypeStruct((PEER_SEL, n), jnp.int32),
                   jax.ShapeDtypeStruct((PEER_SEL, n), F32)],
        compiler_params=_tc_params(1),
        name="route",
    )(hn, wq_bf, keys1_bf, keys2_bf)


SC_ROWS = 32
SC_GROUP = SC_LANES
SC_TOKENS = 8
SC_CHUNKS = PEER_SEL // SC_ROWS

_SC_PARAMS = pltpu.CompilerParams(needs_layout_passes=False)


def _sc_mesh():
    return plsc.VectorSubcoreMesh(core_axis_name="c", subcore_axis_name="s",
                                  num_cores=SC_CORES, num_subcores=SC_SUBCORES)


def _sc_token_loop(n_batches, table_hbm, eid_v, rbufs, rsems, prefetch, prefetch_wait, store,
                   contract):
    def gather(slot, tl, ch, p):
        idx = eid_v.at[slot, tl, pl.ds(ch * SC_ROWS, SC_ROWS)]
        return pltpu.make_async_copy(table_hbm.at[idx], rbufs[p], rsems[p])

    prefetch(0, 0)
    prefetch_wait()
    gather(0, 0, 0, 0).start()

    @pl.loop(0, n_batches * SC_TOKENS)
    def _(i):
        b = i // SC_TOKENS
        tl = i % SC_TOKENS
        slot = b % 2
        more = b + 1 < n_batches

        @pl.when(jnp.logical_and(tl == 0, more))
        def _():
            prefetch(b + 1, 1 - slot)

        for ch in range(SC_CHUNKS):
            p = ch % 2
            if ch + 1 < SC_CHUNKS:
                gather(slot, tl, ch + 1, 1 - p).start()
            else:
                @pl.when(tl + 1 < SC_TOKENS)
                def _():
                    gather(slot, tl + 1, 0, 1 - p).start()

                @pl.when(jnp.logical_and(tl + 1 == SC_TOKENS, more))
                def _():
                    prefetch_wait()
                    gather(1 - slot, 0, 0, 1 - p).start()
            gather(slot, tl, ch, p).wait()
            contract(slot, tl, ch, rbufs[p])

        @pl.when(tl + 1 == SC_TOKENS)
        def _():
            @pl.when(b >= 1)
            def _():
                store(b - 1, 1 - slot).wait()
            store(b, slot).start()

    store(n_batches - 1, (n_batches - 1) % 2).wait()


def _sc_batches(n):
    assert n % (SC_WORKERS * SC_TOKENS) == 0, n
    return n // (SC_WORKERS * SC_TOKENS)


def _peer_hval(hn, eid, peer_u, after):
    n = hn.shape[0]
    nbw = _sc_batches(n)

    @functools.partial(
        pl.kernel, mesh=_sc_mesh(),
        out_type=jax.ShapeDtypeStruct((n // SC_TOKENS, SC_TOKENS, PEER_SEL), F32),
        scratch_types=[
            pltpu.VMEM((2, SC_TOKENS, PEER_SEL), jnp.int32),
            pltpu.VMEM((2, SC_TOKENS, D_MODEL), F32),
            pltpu.VMEM((SC_ROWS, D_MODEL), F32),
            pltpu.VMEM((SC_ROWS, D_MODEL), F32),
            pltpu.VMEM((2, SC_TOKENS, PEER_SEL), F32),
            pltpu.VMEM((SC_GROUP, SC_LANES), F32),
            pltpu.SemaphoreType.DMA, pltpu.SemaphoreType.DMA,
            pltpu.SemaphoreType.DMA, pltpu.SemaphoreType.DMA,
        ],
        compiler_params=_SC_PARAMS, name="peer_hval",
        cost_estimate=pl.CostEstimate(flops=2 * n * PEER_SEL * D_MODEL, transcendentals=0,
                                      bytes_accessed=n * PEER_SEL * D_MODEL * 4))
    def k(x_hbm, eid_hbm, u_hbm, after_hbm, o_hbm, eid_v, x_v, r0, r1, h_v, tr, sr0, sr1, spf, sout):
        del after_hbm
        blk0 = (lax.axis_index("s") * SC_CORES + lax.axis_index("c")) * nbw
        lane = lax.iota(jnp.int32, SC_LANES)
        zero = jnp.zeros((SC_LANES,), F32)

        def prefetch(b, slot):
            pltpu.async_copy(eid_hbm.at[blk0 + b], eid_v.at[slot], spf)
            pltpu.async_copy(x_hbm.at[blk0 + b], x_v.at[slot], spf)

        def prefetch_wait():
            pltpu.make_async_copy(eid_hbm.at[0], eid_v.at[0], spf).wait()
            pltpu.make_async_copy(x_hbm.at[0], x_v.at[0], spf).wait()

        def store(b, slot):
            return pltpu.make_async_copy(h_v.at[slot], o_hbm.at[blk0 + b], sout)

        def contract(slot, tl, ch, rbuf):
            for g in range(SC_ROWS // SC_GROUP):
                def body(c, accs):
                    xc = x_v[slot, tl, pl.ds(c * SC_LANES, SC_LANES)]
                    return tuple(
                        accs[r] + rbuf[g * SC_GROUP + r, pl.ds(c * SC_LANES, SC_LANES)] * xc
                        for r in range(SC_GROUP))
                accs = lax.fori_loop(0, D_MODEL // SC_LANES, body, (zero,) * SC_GROUP)
                for r in range(SC_GROUP):
                    tr[r, :] = accs[r]
                res = zero
                for jj in range(SC_LANES):
                    res = res + plsc.load_gather(tr, [lane, jnp.full((SC_LANES,), jj, jnp.int32)])
                h_v[slot, tl, pl.ds(ch * SC_ROWS + g * SC_GROUP, SC_GROUP)] = res

        _sc_token_loop(nbw, u_hbm, eid_v, (r0, r1), (sr0, sr1), prefetch, prefetch_wait, store,
                       contract)

    out = k(hn.reshape(n // SC_TOKENS, SC_TOKENS, D_MODEL),
            eid.reshape(n // SC_TOKENS, SC_TOKENS, PEER_SEL), peer_u, after)
    return out.reshape(n, PEER_SEL)


def _peer_out(h, a, eid, peer_v):
    n = h.shape[0]
    nbw = _sc_batches(n)

    @functools.partial(
        pl.kernel, mesh=_sc_mesh(),
        out_type=jax.ShapeDtypeStruct((n // SC_TOKENS, SC_TOKENS, D_MODEL), F32),
        scratch_types=[
            pltpu.VMEM((2, SC_TOKENS, PEER_SEL), jnp.int32),
            pltpu.VMEM((2, SC_TOKENS, PEER_SEL), F32),
            pltpu.VMEM((SC_ROWS, D_MODEL), F32),
            pltpu.VMEM((SC_ROWS, D_MODEL), F32),
            pltpu.VMEM((2, SC_TOKENS, D_MODEL), F32),
            pltpu.SemaphoreType.DMA, pltpu.SemaphoreType.DMA,
            pltpu.SemaphoreType.DMA, pltpu.SemaphoreType.DMA,
        ],
        compiler_params=_SC_PARAMS, name="peer_out")
    def k(h_hbm, a_hbm, eid_hbm, v_hbm, o_hbm, eid_v, a_v, r0, r1, y_v, sr0, sr1, spf, sout):
        blk0 = (lax.axis_index("s") * SC_CORES + lax.axis_index("c")) * nbw

        def prefetch(b, slot):
            pltpu.async_copy(eid_hbm.at[blk0 + b], eid_v.at[slot], spf)
            pltpu.async_copy(a_hbm.at[blk0 + b], a_v.at[slot], spf)
            pltpu.async_copy(h_hbm.at[blk0 + b], y_v.at[slot], spf)

        def prefetch_wait():
            pltpu.make_async_copy(eid_hbm.at[0], eid_v.at[0], spf).wait()
            pltpu.make_async_copy(a_hbm.at[0], a_v.at[0], spf).wait()
            pltpu.make_async_copy(h_hbm.at[0], y_v.at[0], spf).wait()

        def store(b, slot):
            return pltpu.make_async_copy(y_v.at[slot], o_hbm.at[blk0 + b], sout)

        def contract(slot, tl, ch, rbuf):
            slot_v = jnp.full((SC_LANES,), slot, jnp.int32)
            tl_v = jnp.full((SC_LANES,), tl, jnp.int32)
            for g in range(SC_ROWS // SC_GROUP):
                base = ch * SC_ROWS + g * SC_GROUP
                coef = [plsc.load_gather(a_v, [slot_v, tl_v, jnp.full((SC_LANES,), base + r, jnp.int32)])
                        for r in range(SC_GROUP)]

                @plsc.parallel_loop(0, D_MODEL // SC_LANES)
                def _(c):
                    sl = pl.ds(c * SC_LANES, SC_LANES)
                    acc = coef[0] * rbuf[g * SC_GROUP, sl]
                    for r in range(1, SC_GROUP):
                        acc = acc + coef[r] * rbuf[g * SC_GROUP + r, sl]
                    plsc.addupdate(y_v.at[slot, tl, sl], acc)

        _sc_token_loop(nbw, v_hbm, eid_v, (r0, r1), (sr0, sr1), prefetch, prefetch_wait, store,
                       contract)

    out = k(h.reshape(n // SC_TOKENS, SC_TOKENS, D_MODEL),
            a.reshape(n // SC_TOKENS, SC_TOKENS, PEER_SEL),
            eid.reshape(n // SC_TOKENS, SC_TOKENS, PEER_SEL), peer_v)
    return out.reshape(n, D_MODEL)


def _gate_kernel(hv_ref, gate_ref, a_out):
    hv = hv_ref[...]
    gelu = hv * (lax.erf(hv * (2.0 ** -0.5)) + 1.0) * 0.5
    a_out[...] = gate_ref[...] * gelu


def _gate(hval, gate, tb):
    n = hval.shape[0]
    row = pl.BlockSpec((tb, PEER_SEL), lambda i: (i, 0))
    return pl.pallas_call(
        _gate_kernel, grid=(n // tb,), in_specs=[row, row], out_specs=row,
        out_shape=jax.ShapeDtypeStruct((n, PEER_SEL), F32),
        compiler_params=_tc_params(1), name="gate",
    )(hval, gate)


VT_ROWS = D_MODEL // 2 // LANES
VT_TOKENS = 64
HI_MASK = -65536


def _pack_value_table(peer_v):
    e = peer_v.shape[0]
    bits = lax.bitcast_convert_type(peer_v.astype(BF16), jnp.uint16).astype(jnp.uint32)
    words = bits[:, :D_MODEL // 2] | (bits[:, D_MODEL // 2:] << 16)
    return lax.bitcast_convert_type(words, jnp.int32).reshape(e * VT_ROWS, LANES)


def _vside_kernel(row_s, a_s, h_ref, tab_ref, y_ref):
    tb = h_ref.shape[0]

    def token(t, carry):
        hv = h_ref[t]
        zero = jnp.zeros((VT_ROWS, LANES), F32)
        lo = [hv[0:VT_ROWS], zero]
        hi = [hv[VT_ROWS:], zero]
        for e in range(PEER_SEL):
            r0 = pl.multiple_of(row_s[t, e], VT_ROWS)
            w = tab_ref[pl.ds(r0, VT_ROWS), :]
            coef = a_s[t, e]
            c = e % 2
            lo[c] = lo[c] + coef * pltpu.bitcast(w << 16, F32)
            hi[c] = hi[c] + coef * pltpu.bitcast(w & HI_MASK, F32)
        y_ref[t] = jnp.concatenate([lo[0] + lo[1], hi[0] + hi[1]], axis=0)
        return carry

    lax.fori_loop(0, tb, token, 0)


def _peer_values(h, a, row_ids, v_packed):
    n = h.shape[0]
    tb = min(VT_TOKENS, n)
    smem = pl.BlockSpec((tb, PEER_SEL), lambda i: (i, 0), memory_space=pltpu.SMEM)
    slab = pl.BlockSpec((tb, 2 * VT_ROWS, LANES), lambda i: (i, 0, 0))
    table = pl.BlockSpec(v_packed.shape, lambda i: (0, 0), pipeline_mode=pl.Buffered(1))
    table_bytes = v_packed.shape[0] * LANES * 4
    y = pl.pallas_call(
        _vside_kernel,
        grid=(n // tb,),
        in_specs=[smem, smem, slab, table],
        out_specs=slab,
        out_shape=jax.ShapeDtypeStruct((n, 2 * VT_ROWS, LANES), F32),
        compiler_params=pltpu.CompilerParams(
            dimension_semantics=("arbitrary",),
            vmem_limit_bytes=table_bytes + 8 * 1024 * 1024),
        name="peer_values",
        cost_estimate=pl.CostEstimate(flops=2 * n * PEER_SEL * D_MODEL, transcendentals=0,
                                      bytes_accessed=table_bytes + n * PEER_SEL * D_MODEL * 2),
    )(row_ids, a, h.reshape(n, 2 * VT_ROWS, LANES), v_packed)
    return y.reshape(n, D_MODEL)


PEER_CHUNK = 4096


PEER_DEPTH = 2


def _peer_launch(h, hn, tb, after, wq_bf, keys1_bf, keys2_bf, peer_u):
    n = h.shape[0]
    eid_t, gate_t = _route(hn, tb, wq_bf, keys1_bf, keys2_bf)
    eid = eid_t.T
    pad = (-n) % (SC_WORKERS * SC_TOKENS)
    hn_p, eid_p = hn, eid
    if pad:
        spread = (jnp.arange(pad * PEER_SEL, dtype=jnp.int32) % peer_u.shape[0]).reshape(pad, PEER_SEL)
        hn_p = jnp.pad(hn, ((0, pad), (0, 0)))
        eid_p = jnp.concatenate([eid, spread], axis=0)
    hval = _peer_hval(hn_p, eid_p, peer_u, after)[:n]
    return h, eid, gate_t.T, hval


def _peer_finish(h, eid, gate, hval, tb, v_packed):
    a = _gate(hval, gate, min(tb, 512))
    return _peer_values(h, a, eid * VT_ROWS, v_packed)


def _rope_tables(pos):
    half = HEAD_DIM // 2
    inv = ROPE_THETA ** (-jnp.arange(half, dtype=F32) / half)
    ang = pos.astype(F32)[:, None] * inv[None, :]
    cos = jnp.cos(ang)
    sin = jnp.sin(ang)
    reps = LANES // HEAD_DIM
    cos_f = jnp.tile(jnp.concatenate([cos, cos], axis=1), (1, reps))
    sin_s = jnp.tile(jnp.concatenate([-sin, sin], axis=1), (1, reps))
    return cos_f, sin_s


def _group_sum_matrix(width):
    g = jnp.arange(width) // HEAD_DIM
    return (g[:, None] == g[None, :]).astype(BF16)


def _heads_first(a, heads):
    b, t, _ = a.shape
    return a.reshape(b, t, heads, HEAD_DIM).transpose(0, 2, 1, 3)


def kernel(x_prompt, x_sample, cache_k_win, cache_v_win, state_conv, meta_tokens, norm_mix_g,
           w_in, q_norm_g, k_norm_g, attn_sinks, conv_w, conv_b, conv_ln_g, conv_ln_b, w_pw2,
           out_norm_attn_g, out_norm_conv_g, w_out, norm_ffn_g, peer_w_q, peer_keys1, peer_keys2,
           peer_u, peer_v):
    assert norm_mix_g.shape[0] == 1, "single-layer model"
    b, t, _ = x_prompt.shape
    s = x_sample.shape[0]
    w_buf = cache_k_win.shape[2]
    n = b * t

    g_mix = norm_mix_g[0][None, :]
    w_in_bf = w_in[0].astype(BF16)
    qg_t = jnp.tile(q_norm_g[0], N_HEADS)[None, :]
    kg_t = jnp.tile(k_norm_g[0], N_KV_HEADS)[None, :]
    gsum_q = _group_sum_matrix(D_ATTN)
    gsum_k = _group_sum_matrix(KV_DIM)
    sinks = attn_sinks[0]
    cw, cb = conv_w[0], conv_b[0][None, :]
    lg, lb = conv_ln_g[0][None, :], conv_ln_b[0][None, :]
    w2_bf = w_pw2[0].astype(BF16)
    g_a, g_c = out_norm_attn_g[0][None, :], out_norm_conv_g[0][None, :]
    w_out_a = w_out[0][:D_ATTN].astype(BF16)
    w_out_c = w_out[0][D_ATTN:].astype(BF16)
    g_f = norm_ffn_g[0][None, :]
    wq_bf = peer_w_q[0].astype(BF16)
    k1_bf = peer_keys1[0].astype(BF16)
    k2_bf = peer_keys2[0].astype(BF16)
    pu, vp = peer_u[0], _pack_value_table(peer_v[0])
    proj = functools.partial(_project, norm_g=g_mix, w_in_bf=w_in_bf, qg_t=qg_t, kg_t=kg_t,
                             gsum_q=gsum_q, gsum_k=gsum_k)
    conv = functools.partial(_conv_branch, conv_w=cw, conv_b=cb, ln_g=lg, ln_b=lb, w_pw2_bf=w2_bf)
    merge = functools.partial(_merge, g_a=g_a, g_c=g_c, w_out_a=w_out_a, w_out_c=w_out_c, g_f=g_f)

    tb = 512
    no_dep = jnp.zeros((SC_TOKENS, LANES), F32)
    tab_x = _rope_tables(N_META + jnp.arange(t, dtype=jnp.int32))
    tab_m = _rope_tables(jnp.arange(N_META, dtype=jnp.int32))
    _, k_m, v_m, u_m = proj(meta_tokens, tab_m, 1, N_META, no_dep)
    halo0 = jnp.concatenate([jnp.zeros((CONV_HALO - N_META, D_CONV), F32), u_m], axis=0)

    def with_meta_block(a, a_m):
        full = jnp.concatenate([jnp.zeros((META_PAD, KV_DIM), F32), a_m, a], axis=0)
        return _heads_first(full[None], N_KV_HEADS)

    def dense(bi, after):
        xb = x_prompt[bi]
        q, k, v, u = proj(xb, tab_x, t // tb, tb, after)
        o_attn = _prompt_attention(sinks, _heads_first(q[None], N_HEADS),
                                   with_meta_block(k, k_m), with_meta_block(v, v_m))
        o_attn = o_attn.transpose(0, 2, 1, 3).reshape(t, D_ATTN)
        u3 = u.reshape(t // tb, tb, D_CONV)
        halo = jnp.concatenate([halo0[None], u3[:-1, tb - CONV_HALO:]], axis=0)
        h, hn = merge(xb, o_attn, conv(halo, u, tb), tb)
        return h, hn, k, v, u

    per_seq = max(1, t // PEER_CHUNK)
    chunk = t // per_seq
    launched, ys, kvu = [], [], []
    cur = dense(0, no_dep)
    for bi in range(b):
        h, hn, k, v, u = cur
        kvu.append((k, v, u))
        for j in range(per_seq):
            c = bi * per_seq + j
            if c >= PEER_DEPTH:
                ys.append(_peer_finish(*launched[c - PEER_DEPTH], 256, vp))
            after = ys[c - PEER_DEPTH][:SC_TOKENS] if c >= PEER_DEPTH else jnp.zeros((SC_TOKENS, D_MODEL), F32)
            sl = slice(j * chunk, (j + 1) * chunk)
            launched.append(_peer_launch(h[sl], hn[sl], 256, after, wq_bf, k1_bf, k2_bf, pu))
        if bi + 1 < b:
            cur = dense(bi + 1, launched[-1][1][:SC_TOKENS, :LANES].astype(F32))
    for c in range(len(ys), len(launched)):
        ys.append(_peer_finish(*launched[c], 256, vp))
    y_prompt = jnp.concatenate(ys, axis=0).reshape(b, t, D_MODEL)

    new_k_prompt = jnp.stack([k[t - WINDOW:] for k, _, _ in kvu]).reshape(1, b, WINDOW, N_KV_HEADS, HEAD_DIM)
    new_v_prompt = jnp.stack([v[t - WINDOW:] for _, v, _ in kvu]).reshape(1, b, WINDOW, N_KV_HEADS, HEAD_DIM)
    new_conv_prompt = jnp.stack([u[t - (CONV_WIDTH - 1):] for _, _, u in kvu])[None]

    xs = x_sample.reshape(s, D_MODEL)
    tab_s = _rope_tables(jnp.full((s,), PAST_LEN, jnp.int32))
    qs, ks, vs, us = proj(xs, tab_s, 1, s, no_dep)
    ck = cache_k_win[0]
    cv = cache_v_win[0]
    o_attn_s = _decode_attention(
        sinks,
        qs.reshape(s, N_HEADS, HEAD_DIM),
        ks.reshape(s, N_KV_HEADS, HEAD_DIM).transpose(1, 0, 2),
        vs.reshape(s, N_KV_HEADS, HEAD_DIM).transpose(1, 0, 2),
        ck.transpose(2, 0, 1, 3), cv.transpose(2, 0, 1, 3), 32)
    o_attn_s = o_attn_s.reshape(s, D_ATTN)
    cs = state_conv[0]
    hist = jnp.concatenate([jnp.zeros((s, CONV_HALO - (CONV_WIDTH - 1), D_CONV), F32), cs], axis=1)
    us_blk = jnp.concatenate([us[:, None, :], jnp.zeros((s, 7, D_CONV), F32)], axis=1)
    o_conv_s = conv(hist, us_blk.reshape(s * 8, D_CONV), 8).reshape(s, 8, D_CONV)[:, 0]
    hs, hns = merge(xs, o_attn_s, o_conv_s, s)
    launch_s = _peer_launch(hs, hns, s, jnp.zeros((SC_TOKENS, D_MODEL), F32), wq_bf, k1_bf, k2_bf, pu)
    y_sample = _peer_finish(*launch_s, s, vp).reshape(s, 1, D_MODEL)

    new_k_sample = jnp.concatenate([ck[:, 1:], ks.reshape(s, 1, N_KV_HEADS, HEAD_DIM)], axis=1)[None]
    new_v_sample = jnp.concatenate([cv[:, 1:], vs.reshape(s, 1, N_KV_HEADS, HEAD_DIM)], axis=1)[None]
    new_conv_sample = jnp.concatenate([cs[:, 1:], us[:, None, :]], axis=1)[None]
    if w_buf != WINDOW:
        raise NotImplementedError("cache window shorter than the attention window")

    return (y_prompt, y_sample, new_k_prompt, new_v_prompt, new_conv_prompt,
            new_k_sample, new_v_sample, new_conv_sample)
```

```python
import functools

import jax
import jax.numpy as jnp
from jax import lax
from jax.experimental import pallas as pl
from jax.experimental.pallas import tpu as pltpu
from jax.experimental.pallas import tpu_sc as plsc

D_MODEL = 1024
HEAD_DIM = 64
D_ATTN = 512
N_HEADS = 8
N_KV_HEADS = 2
KV_DIM = N_KV_HEADS * HEAD_DIM
D_CONV = 512
D_IN = D_ATTN + 2 * KV_DIM + 2 * D_CONV
CONV_WIDTH = 31
WINDOW = 128
BLOCK = 128
ROPE_THETA = 10000.0
N_META = 16
META_PAD = BLOCK - N_META
PEER_HEADS = 8
PEER_NKEYS = 128
PEER_TOPK = 16
PEER_SEL = PEER_HEADS * PEER_TOPK
EPS = 1e-6
PAST_LEN = 16384

LANES = 128
SC_CORES = 2
SC_SUBCORES = 16
SC_LANES = 16
SC_WORKERS = SC_CORES * SC_SUBCORES
VMEM_LIMIT = 48 * 1024 * 1024

F32 = jnp.float32
BF16 = jnp.bfloat16
NEG_INF = float("-inf")


def _tc_params(n_axes):
    return pltpu.CompilerParams(dimension_semantics=("arbitrary",) * n_axes,
                                vmem_limit_bytes=VMEM_LIMIT)


def _full(shape):
    nd = len(shape)
    return pl.BlockSpec(shape, lambda *_: (0,) * nd)


def _group_mean(sq, gsum_ref):
    hi = sq.astype(BF16)
    lo = (sq - hi.astype(F32)).astype(BF16)
    g = gsum_ref[...]
    s = jnp.dot(hi, g, preferred_element_type=F32) + jnp.dot(lo, g, preferred_element_type=F32)
    return s * (1.0 / HEAD_DIM)


def _rope(xn, cos_f, sin_s, first_half):
    outs = []
    for s in range(xn.shape[1] // LANES):
        xs = xn[:, s * LANES:(s + 1) * LANES]
        partner = jnp.where(first_half, pltpu.roll(xs, LANES - HEAD_DIM // 2, axis=1),
                            pltpu.roll(xs, HEAD_DIM // 2, axis=1))
        outs.append(xs * cos_f + partner * sin_s)
    return outs[0] if len(outs) == 1 else jnp.concatenate(outs, axis=1)


def _proj_kernel(x_ref, g_ref, w_ref, qg_ref, kg_ref, cos_ref, sin_ref, gq_ref, gk_ref, after_ref,
                 q_out, k_out, v_out, u_out):
    del after_ref
    x = x_ref[...]
    n = x * lax.rsqrt(jnp.mean(x * x, axis=-1, keepdims=True) + EPS) * g_ref[...]
    p = jnp.dot(n.astype(BF16), w_ref[...], preferred_element_type=F32)
    q = p[:, :D_ATTN]
    k = p[:, D_ATTN:D_ATTN + KV_DIM]
    v = p[:, D_ATTN + KV_DIM:D_ATTN + 2 * KV_DIM]
    ga = p[:, D_ATTN + 2 * KV_DIM:D_ATTN + 2 * KV_DIM + D_CONV]
    gb = p[:, D_ATTN + 2 * KV_DIM + D_CONV:]
    cos_f = cos_ref[...]
    sin_s = sin_ref[...]
    lane = lax.broadcasted_iota(jnp.int32, (x.shape[0], LANES), 1)
    first_half = (lane % HEAD_DIM) < (HEAD_DIM // 2)
    qn = q * lax.rsqrt(_group_mean(q * q, gq_ref) + EPS) * qg_ref[...]
    kn = k * lax.rsqrt(_group_mean(k * k, gk_ref) + EPS) * kg_ref[...]
    q_out[...] = _rope(qn, cos_f, sin_s, first_half)
    k_out[...] = _rope(kn, cos_f, sin_s, first_half)
    v_out[...] = v
    u_out[...] = ga * (1.0 / (1.0 + jnp.exp(-gb)))


def _project(x, pos_tables, n_table_blocks, tb, after, norm_g, w_in_bf, qg_t, kg_t, gsum_q, gsum_k,
             row0=0, nrows=None):
    n = x.shape[0] if nrows is None else nrows
    cos_t, sin_t = pos_tables
    nb = n // tb
    blk0 = row0 // tb
    tab_spec = pl.BlockSpec((tb, LANES), lambda i: (i % n_table_blocks, 0))
    row = lambda w: pl.BlockSpec((tb, w), lambda i: (i, 0))
    return pl.pallas_call(
        _proj_kernel,
        grid=(nb,),
        in_specs=[pl.BlockSpec((tb, D_MODEL), lambda i: (i + blk0, 0)),
                  _full((1, D_MODEL)), _full((D_MODEL, D_IN)),
                  _full((1, D_ATTN)), _full((1, KV_DIM)), tab_spec, tab_spec,
                  _full((D_ATTN, D_ATTN)), _full((KV_DIM, KV_DIM)),
                  pl.BlockSpec(memory_space=pl.ANY)],
        out_specs=[row(D_ATTN), row(KV_DIM), row(KV_DIM), row(D_CONV)],
        out_shape=[jax.ShapeDtypeStruct((n, D_ATTN), F32), jax.ShapeDtypeStruct((n, KV_DIM), F32),
                   jax.ShapeDtypeStruct((n, KV_DIM), F32), jax.ShapeDtypeStruct((n, D_CONV), F32)],
        compiler_params=_tc_params(1),
        name="proj",
    )(x, norm_g, w_in_bf, qg_t, kg_t, cos_t, sin_t, gsum_q, gsum_k, after)


def _attn_kernel(sink_ref, q_ref, kp_ref, kc_ref, vp_ref, vc_ref, o_ref):
    j = pl.program_id(1)
    r = lax.broadcasted_iota(jnp.int32, (BLOCK, 2 * BLOCK), 0)
    c = lax.broadcasted_iota(jnp.int32, (BLOCK, 2 * BLOCK), 1)
    ok = (c > r) & (c <= r + WINDOW) & ((j > 0) | (c >= META_PAD))
    grp = N_HEADS // N_KV_HEADS
    ok = jnp.concatenate([ok] * grp, axis=0)
    for g in range(N_KV_HEADS):
        q4 = q_ref[0, g * grp:(g + 1) * grp].reshape(grp * BLOCK, HEAD_DIM).astype(BF16)
        k = jnp.concatenate([kp_ref[0, g], kc_ref[0, g]], axis=0).astype(BF16)
        v = jnp.concatenate([vp_ref[0, g], vc_ref[0, g]], axis=0).astype(BF16)
        s = lax.dot_general(q4, k, (((1,), (1,)), ((), ())), preferred_element_type=F32)
        s = jnp.where(ok, s * (HEAD_DIM ** -0.5), NEG_INF)
        sink = jnp.concatenate(
            [jnp.full((BLOCK, 1), sink_ref[g * grp + i], F32) for i in range(grp)], axis=0)
        m = jnp.maximum(jnp.max(s, axis=1, keepdims=True), sink)
        p = jnp.exp(s - m)
        den = jnp.sum(p, axis=1, keepdims=True) + jnp.exp(sink - m)
        o = jnp.dot(p.astype(BF16), v, preferred_element_type=F32) / den
        o_ref[0, g * grp:(g + 1) * grp] = o.reshape(grp, BLOCK, HEAD_DIM)


def _prompt_attention(sinks, q_t, k_t, v_t):
    b, _, t, _ = q_t.shape
    nb = t // BLOCK
    kv_prev = pl.BlockSpec((1, N_KV_HEADS, BLOCK, HEAD_DIM), lambda bi, j: (bi, 0, j, 0))
    kv_cur = pl.BlockSpec((1, N_KV_HEADS, BLOCK, HEAD_DIM), lambda bi, j: (bi, 0, j + 1, 0))
    q_spec = pl.BlockSpec((1, N_HEADS, BLOCK, HEAD_DIM), lambda bi, j: (bi, 0, j, 0))
    return pl.pallas_call(
        _attn_kernel,
        grid=(b, nb),
        in_specs=[pl.BlockSpec(memory_space=pltpu.SMEM), q_spec, kv_prev, kv_cur, kv_prev, kv_cur],
        out_specs=q_spec,
        out_shape=jax.ShapeDtypeStruct(q_t.shape, F32),
        compiler_params=_tc_params(2),
        name="attn",
    )(sinks, q_t, k_t, k_t, v_t, v_t)


def _dec_attn_kernel(sink_ref, q_ref, kn_ref, vn_ref, ck_ref, cv_ref, o_ref):
    grp = N_HEADS // N_KV_HEADS
    sb, w_buf = ck_ref.shape[1], ck_ref.shape[2]
    q = q_ref[...]
    qb = q.astype(BF16)
    head = lax.broadcasted_iota(jnp.int32, (sb, N_HEADS, 1), 1)
    in_g0 = head < grp
    s_g = [jnp.einsum("shd,swd->shw", qb, ck_ref[g].astype(BF16), preferred_element_type=F32)
           for g in range(N_KV_HEADS)]
    s = jnp.where(in_g0, s_g[0], s_g[1]) * (HEAD_DIM ** -0.5)
    key_ok = lax.broadcasted_iota(jnp.int32, (sb, N_HEADS, w_buf), 2) >= 1
    s = jnp.where(key_ok, s, NEG_INF)
    rnd = lambda a: a.astype(BF16).astype(F32)
    kn = jnp.where(in_g0, kn_ref[0][:, None, :], kn_ref[1][:, None, :])
    vn = jnp.where(in_g0, vn_ref[0][:, None, :], vn_ref[1][:, None, :])
    s_self = jnp.sum(rnd(q) * rnd(kn), axis=-1, keepdims=True) * (HEAD_DIM ** -0.5)
    sink = sink_ref[...]
    m = jnp.maximum(jnp.maximum(jnp.max(s, axis=-1, keepdims=True), s_self), sink)
    p = jnp.exp(s - m)
    p_self = jnp.exp(s_self - m)
    den = jnp.sum(p, axis=-1, keepdims=True) + p_self + jnp.exp(sink - m)
    pb = p.astype(BF16)
    o_g = [jnp.einsum("shw,swd->shd", pb, cv_ref[g].astype(BF16), preferred_element_type=F32)
           for g in range(N_KV_HEADS)]
    o = jnp.where(in_g0, o_g[0], o_g[1]) + rnd(p_self) * rnd(vn)
    o_ref[...] = o / den


def _decode_attention(sinks, q3, kn_t, vn_t, ck_t, cv_t, sb):
    s = q3.shape[0]
    w_buf = ck_t.shape[2]
    qs = pl.BlockSpec((sb, N_HEADS, HEAD_DIM), lambda i: (i, 0, 0))
    ns = pl.BlockSpec((N_KV_HEADS, sb, HEAD_DIM), lambda i: (0, i, 0))
    cs = pl.BlockSpec((N_KV_HEADS, sb, w_buf, HEAD_DIM), lambda i: (0, i, 0, 0))
    return pl.pallas_call(
        _dec_attn_kernel,
        grid=(s // sb,),
        in_specs=[_full((1, N_HEADS, 1)), qs, ns, ns, cs, cs],
        out_specs=qs,
        out_shape=jax.ShapeDtypeStruct(q3.shape, F32),
        compiler_params=_tc_params(1),
        name="dec_attn",
    )(sinks.reshape(1, N_HEADS, 1), q3, kn_t, vn_t, ck_t, cv_t)


CONV_HALO = 32
CONV_ROWS = 64


def _conv_kernel(halo_ref, u_ref, cw_ref, cb_ref, lg_ref, lb_ref, w2_ref, o_ref, ucat):
    tb = u_ref.shape[0]
    ucat[0:CONV_HALO, :] = halo_ref[0]
    ucat[CONV_HALO:, :] = u_ref[...]
    first = CONV_HALO - (CONV_WIDTH - 1)
    rows = min(CONV_ROWS, tb)
    for r0 in range(0, tb, rows):
        acc = jnp.zeros((rows, D_CONV), F32)
        for j in range(CONV_WIDTH):
            acc = acc + ucat[r0 + first + j:r0 + first + j + rows, :] * cw_ref[j:j + 1, :]
        y = acc + cb_ref[...]
        yc = y - jnp.mean(y, axis=-1, keepdims=True)
        yn = yc * lax.rsqrt(jnp.mean(yc * yc, axis=-1, keepdims=True) + EPS)
        yn = yn * lg_ref[...] + lb_ref[...]
        act = yn * (1.0 / (1.0 + jnp.exp(-yn)))
        o_ref[r0:r0 + rows, :] = jnp.dot(act.astype(BF16), w2_ref[...],
                                              preferred_element_type=F32)


def _conv_branch(halo, u, tb, conv_w, conv_b, ln_g, ln_b, w_pw2_bf):
    n = u.shape[0]
    return pl.pallas_call(
        _conv_kernel,
        grid=(n // tb,),
        in_specs=[pl.BlockSpec((1, CONV_HALO, D_CONV), lambda i: (i, 0, 0)),
                  pl.BlockSpec((tb, D_CONV), lambda i: (i, 0)),
                  _full((CONV_WIDTH, D_CONV)), _full((1, D_CONV)), _full((1, D_CONV)),
                  _full((1, D_CONV)), _full((D_CONV, D_CONV))],
        out_specs=pl.BlockSpec((tb, D_CONV), lambda i: (i, 0)),
        out_shape=jax.ShapeDtypeStruct((n, D_CONV), F32),
        scratch_shapes=[pltpu.VMEM((tb + CONV_HALO, D_CONV), F32)],
        compiler_params=_tc_params(1),
        name="conv",
    )(halo, u, conv_w, conv_b, ln_g, ln_b, w_pw2_bf)


def _rms(x, g):
    return x * lax.rsqrt(jnp.mean(x * x, axis=-1, keepdims=True) + EPS) * g


def _merge_kernel(x_ref, oa_ref, oc_ref, ga_ref, gc_ref, wa_ref, wc_ref, gf_ref, h_out, hn_out):
    a = _rms(oa_ref[...], ga_ref[...]).astype(BF16)
    c = _rms(oc_ref[...], gc_ref[...]).astype(BF16)
    h = x_ref[...] + (jnp.dot(a, wa_ref[...], preferred_element_type=F32)
                      + jnp.dot(c, wc_ref[...], preferred_element_type=F32))
    h_out[...] = h
    hn_out[...] = _rms(h, gf_ref[...])


def _merge(x, oa, oc, tb, g_a, g_c, w_out_a, w_out_c, g_f, x_row0=0, row0=0, nrows=None):
    n = oa.shape[0] if nrows is None else nrows
    xb0, b0 = x_row0 // tb, row0 // tb
    row = lambda w: pl.BlockSpec((tb, w), lambda i: (i, 0))
    src = lambda w: pl.BlockSpec((tb, w), lambda i: (i + b0, 0))
    return pl.pallas_call(
        _merge_kernel,
        grid=(n // tb,),
        in_specs=[pl.BlockSpec((tb, D_MODEL), lambda i: (i + xb0, 0)), src(D_ATTN), src(D_CONV),
                  _full((1, D_ATTN)), _full((1, D_CONV)),
                  _full((D_ATTN, D_MODEL)), _full((D_CONV, D_MODEL)), _full((1, D_MODEL))],
        out_specs=[row(D_MODEL), row(D_MODEL)],
        out_shape=[jax.ShapeDtypeStruct((n, D_MODEL), F32)] * 2,
        compiler_params=_tc_params(1),
        name="merge",
    )(x, oa, oc, g_a, g_c, w_out_a, w_out_c, g_f)


ID_BIG = 1e9


def _topk_rows(s, k):
    rows = lax.broadcasted_iota(jnp.int32, s.shape, 0).astype(F32)
    vals, idxs = [], []
    for _ in range(k):
        m = jnp.max(s, axis=0, keepdims=True)
        idx = jnp.min(jnp.where(s == m, rows, ID_BIG), axis=0, keepdims=True)
        vals.append(m)
        idxs.append(idx)
        s = jnp.where(rows == idx, NEG_INF, s)
    return jnp.concatenate(vals, axis=0), jnp.concatenate(idxs, axis=0)


PAIR_B_WIDE = 8


def _route_kernel(hn_ref, wq_ref, k1_ref, k2_ref, eid_out, gate_out):
    tb = hn_ref.shape[0]
    q = jnp.dot(hn_ref[...].astype(BF16), wq_ref[...], preferred_element_type=F32).astype(BF16)
    k1 = k1_ref[...]
    k2 = k2_ref[...]
    nt = (((1,), (1,)), ((), ()))
    r = lax.broadcasted_iota(jnp.int32, (PEER_TOPK + (PAIR_B_WIDE - 1) * PAIR_B_WIDE + PAIR_B_WIDE, tb), 0)
    mid = r - PEER_TOPK
    flat = jnp.where(r < PEER_TOPK, r,
                     jnp.where(mid < (PAIR_B_WIDE - 1) * PAIR_B_WIDE,
                               (1 + mid // PAIR_B_WIDE) * PEER_TOPK + mid % PAIR_B_WIDE,
                               (PAIR_B_WIDE + mid - (PAIR_B_WIDE - 1) * PAIR_B_WIDE) * PEER_TOPK)).astype(F32)
    half = PEER_NKEYS
    for h in range(PEER_HEADS):
        q1 = q[:, (2 * h) * half:(2 * h + 1) * half]
        q2 = q[:, (2 * h + 1) * half:(2 * h + 2) * half]
        s1 = lax.dot_general(k1, q1, nt, preferred_element_type=F32)
        s2 = lax.dot_general(k2, q2, nt, preferred_element_type=F32)
        v1, i1 = _topk_rows(s1, PEER_TOPK)
        v2, i2 = _topk_rows(s2, PEER_TOPK)
        e1 = i1 * PEER_NKEYS
        cand = jnp.concatenate(
            [v1[0:1] + v2]
            + [v1[a:a + 1] + v2[0:PAIR_B_WIDE] for a in range(1, PAIR_B_WIDE)]
            + [v1[PAIR_B_WIDE:] + v2[0:1]], axis=0)
        cid = jnp.concatenate(
            [e1[0:1] + i2]
            + [e1[a:a + 1] + i2[0:PAIR_B_WIDE] for a in range(1, PAIR_B_WIDE)]
            + [e1[PAIR_B_WIDE:] + i2[0:1]], axis=0)
        scs, eids = [], []
        for _ in range(PEER_TOPK):
            m = jnp.max(cand, axis=0, keepdims=True)
            jsel = jnp.min(jnp.where(cand == m, flat, ID_BIG), axis=0, keepdims=True)
            hit = flat == jsel
            eids.append(jnp.max(jnp.where(hit, cid, -1.0), axis=0, keepdims=True))
            scs.append(m)
            cand = jnp.where(hit, NEG_INF, cand)
        sc = jnp.concatenate(scs, axis=0)
        e = jnp.exp(sc - sc[0:1])
        gate_out[h * PEER_TOPK:(h + 1) * PEER_TOPK, :] = e / jnp.sum(e, axis=0, keepdims=True)
        eid_out[h * PEER_TOPK:(h + 1) * PEER_TOPK, :] = jnp.concatenate(eids, axis=0).astype(jnp.int32)


def _route(hn, tb, wq_bf, keys1_bf, keys2_bf):
    n = hn.shape[0]
    col = pl.BlockSpec((PEER_SEL, tb), lambda i: (0, i))
    return pl.pallas_call(
        _route_kernel,
        grid=(n // tb,),
        in_specs=[pl.BlockSpec((tb, D_MODEL), lambda i: (i, 0)),
                  _full((D_MODEL, 2 * PEER_NKEYS * PEER_HEADS)),
                  _full((PEER_NKEYS, PEER_NKEYS)), _full((PEER_NKEYS, PEER_NKEYS))],
        out_specs=[col, col],
        out_shape=[jax.ShapeDtypeStruct((PEER_SEL, n), jnp.int32),
                   jax.ShapeDtypeStruct((PEER_SEL, n), F32)],
        compiler_params=_tc_params(1),
        name="route",
    )(hn, wq_bf, keys1_bf, keys2_bf)


SC_ROWS = 32
SC_GROUP = SC_LANES
SC_TOKENS = 8
SC_CHUNKS = PEER_SEL // SC_ROWS

_SC_PARAMS = pltpu.CompilerParams(needs_layout_passes=False)


def _sc_mesh():
    return plsc.VectorSubcoreMesh(core_axis_name="c", subcore_axis_name="s",
                                  num_cores=SC_CORES, num_subcores=SC_SUBCORES)


def _sc_token_loop(n_batches, table_hbm, eid_v, rbufs, rsems, prefetch, prefetch_wait, store,
                   contract):
    def gather(slot, tl, ch, p):
        idx = eid_v.at[slot, tl, pl.ds(ch * SC_ROWS, SC_ROWS)]
        return pltpu.make_async_copy(table_hbm.at[idx], rbufs[p], rsems[p])

    prefetch(0, 0)
    prefetch_wait()
    gather(0, 0, 0, 0).start()

    @pl.loop(0, n_batches * SC_TOKENS)
    def _(i):
        b = i // SC_TOKENS
        tl = i % SC_TOKENS
        slot = b % 2
        more = b + 1 < n_batches

        @pl.when(jnp.logical_and(tl == 0, more))
        def _():
            prefetch(b + 1, 1 - slot)

        for ch in range(SC_CHUNKS):
            p = ch % 2
            if ch + 1 < SC_CHUNKS:
                gather(slot, tl, ch + 1, 1 - p).start()
            else:
                @pl.when(tl + 1 < SC_TOKENS)
                def _():
                    gather(slot, tl + 1, 0, 1 - p).start()

                @pl.when(jnp.logical_and(tl + 1 == SC_TOKENS, more))
                def _():
                    prefetch_wait()
                    gather(1 - slot, 0, 0, 1 - p).start()
            gather(slot, tl, ch, p).wait()
            contract(slot, tl, ch, rbufs[p])

        @pl.when(tl + 1 == SC_TOKENS)
        def _():
            @pl.when(b >= 1)
            def _():
                store(b - 1, 1 - slot).wait()
            store(b, slot).start()

    store(n_batches - 1, (n_batches - 1) % 2).wait()


def _sc_batches(n):
    assert n % (SC_WORKERS * SC_TOKENS) == 0, n
    return n // (SC_WORKERS * SC_TOKENS)


def _peer_hval(hn, eid, peer_u, after):
    n = hn.shape[0]
    nbw = _sc_batches(n)

    @functools.partial(
        pl.kernel, mesh=_sc_mesh(),
        out_type=jax.ShapeDtypeStruct((n // SC_TOKENS, SC_TOKENS, PEER_SEL), F32),
        scratch_types=[
            pltpu.VMEM((2, SC_TOKENS, PEER_SEL), jnp.int32),
            pltpu.VMEM((2, SC_TOKENS, D_MODEL), F32),
            pltpu.VMEM((SC_ROWS, D_MODEL), F32),
            pltpu.VMEM((SC_ROWS, D_MODEL), F32),
            pltpu.VMEM((2, SC_TOKENS, PEER_SEL), F32),
            pltpu.VMEM((SC_GROUP, SC_LANES), F32),
            pltpu.SemaphoreType.DMA, pltpu.SemaphoreType.DMA,
            pltpu.SemaphoreType.DMA, pltpu.SemaphoreType.DMA,
        ],
        compiler_params=_SC_PARAMS, name="peer_hval",
        cost_estimate=pl.CostEstimate(flops=2 * n * PEER_SEL * D_MODEL, transcendentals=0,
                                      bytes_accessed=n * PEER_SEL * D_MODEL * 4))
    def k(x_hbm, eid_hbm, u_hbm, after_hbm, o_hbm, eid_v, x_v, r0, r1, h_v, tr, sr0, sr1, spf, sout):
        del after_hbm
        blk0 = (lax.axis_index("s") * SC_CORES + lax.axis_index("c")) * nbw
        lane = lax.iota(jnp.int32, SC_LANES)
        zero = jnp.zeros((SC_LANES,), F32)

        def prefetch(b, slot):
            pltpu.async_copy(eid_hbm.at[blk0 + b], eid_v.at[slot], spf)
            pltpu.async_copy(x_hbm.at[blk0 + b], x_v.at[slot], spf)

        def prefetch_wait():
            pltpu.make_async_copy(eid_hbm.at[0], eid_v.at[0], spf).wait()
            pltpu.make_async_copy(x_hbm.at[0], x_v.at[0], spf).wait()

        def store(b, slot):
            return pltpu.make_async_copy(h_v.at[slot], o_hbm.at[blk0 + b], sout)

        def contract(slot, tl, ch, rbuf):
            for g in range(SC_ROWS // SC_GROUP):
                def body(c, accs):
                    xc = x_v[slot, tl, pl.ds(c * SC_LANES, SC_LANES)]
                    return tuple(
                        accs[r] + rbuf[g * SC_GROUP + r, pl.ds(c * SC_LANES, SC_LANES)] * xc
                        for r in range(SC_GROUP))
                accs = lax.fori_loop(0, D_MODEL // SC_LANES, body, (zero,) * SC_GROUP)
                for r in range(SC_GROUP):
                    tr[r, :] = accs[r]
                res = zero
                for jj in range(SC_LANES):
                    res = res + plsc.load_gather(tr, [lane, jnp.full((SC_LANES,), jj, jnp.int32)])
                h_v[slot, tl, pl.ds(ch * SC_ROWS + g * SC_GROUP, SC_GROUP)] = res

        _sc_token_loop(nbw, u_hbm, eid_v, (r0, r1), (sr0, sr1), prefetch, prefetch_wait, store,
                       contract)

    out = k(hn.reshape(n // SC_TOKENS, SC_TOKENS, D_MODEL),
            eid.reshape(n // SC_TOKENS, SC_TOKENS, PEER_SEL), peer_u, after)
    return out.reshape(n, PEER_SEL)


def _peer_out(h, a, eid, peer_v):
    n = h.shape[0]
    nbw = _sc_batches(n)

    @functools.partial(
        pl.kernel, mesh=_sc_mesh(),
        out_type=jax.ShapeDtypeStruct((n // SC_TOKENS, SC_TOKENS, D_MODEL), F32),
        scratch_types=[
            pltpu.VMEM((2, SC_TOKENS, PEER_SEL), jnp.int32),
            pltpu.VMEM((2, SC_TOKENS, PEER_SEL), F32),
            pltpu.VMEM((SC_ROWS, D_MODEL), F32),
            pltpu.VMEM((SC_ROWS, D_MODEL), F32),
            pltpu.VMEM((2, SC_TOKENS, D_MODEL), F32),
            pltpu.SemaphoreType.DMA, pltpu.SemaphoreType.DMA,
            pltpu.SemaphoreType.DMA, pltpu.SemaphoreType.DMA,
        ],
        compiler_params=_SC_PARAMS, name="peer_out")
    def k(h_hbm, a_hbm, eid_hbm, v_hbm, o_hbm, eid_v, a_v, r0, r1, y_v, sr0, sr1, spf, sout):
        blk0 = (lax.axis_index("s") * SC_CORES + lax.axis_index("c")) * nbw

        def prefetch(b, slot):
            pltpu.async_copy(eid_hbm.at[blk0 + b], eid_v.at[slot], spf)
            pltpu.async_copy(a_hbm.at[blk0 + b], a_v.at[slot], spf)
            pltpu.async_copy(h_hbm.at[blk0 + b], y_v.at[slot], spf)

        def prefetch_wait():
            pltpu.make_async_copy(eid_hbm.at[0], eid_v.at[0], spf).wait()
            pltpu.make_async_copy(a_hbm.at[0], a_v.at[0], spf).wait()
            pltpu.make_async_copy(h_hbm.at[0], y_v.at[0], spf).wait()

        def store(b, slot):
            return pltpu.make_async_copy(y_v.at[slot], o_hbm.at[blk0 + b], sout)

        def contract(slot, tl, ch, rbuf):
            slot_v = jnp.full((SC_LANES,), slot, jnp.int32)
            tl_v = jnp.full((SC_LANES,), tl, jnp.int32)
            for g in range(SC_ROWS // SC_GROUP):
                base = ch * SC_ROWS + g * SC_GROUP
                coef = [plsc.load_gather(a_v, [slot_v, tl_v, jnp.full((SC_LANES,), base + r, jnp.int32)])
                        for r in range(SC_GROUP)]

                @plsc.parallel_loop(0, D_MODEL // SC_LANES)
                def _(c):
                    sl = pl.ds(c * SC_LANES, SC_LANES)
                    acc = coef[0] * rbuf[g * SC_GROUP, sl]
                    for r in range(1, SC_GROUP):
                        acc = acc + coef[r] * rbuf[g * SC_GROUP + r, sl]
                    plsc.addupdate(y_v.at[slot, tl, sl], acc)

        _sc_token_loop(nbw, v_hbm, eid_v, (r0, r1), (sr0, sr1), prefetch, prefetch_wait, store,
                       contract)

    out = k(h.reshape(n // SC_TOKENS, SC_TOKENS, D_MODEL),
            a.reshape(n // SC_TOKENS, SC_TOKENS, PEER_SEL),
            eid.reshape(n // SC_TOKENS, SC_TOKENS, PEER_SEL), peer_v)
    return out.reshape(n, D_MODEL)


def _gate_kernel(hv_ref, gate_ref, a_out):
    hv = hv_ref[...]
    gelu = hv * (lax.erf(hv * (2.0 ** -0.5)) + 1.0) * 0.5
    a_out[...] = gate_ref[...] * gelu


def _gate(hval, gate, tb):
    n = hval.shape[0]
    row = pl.BlockSpec((tb, PEER_SEL), lambda i: (i, 0))
    return pl.pallas_call(
        _gate_kernel, grid=(n // tb,), in_specs=[row, row], out_specs=row,
        out_shape=jax.ShapeDtypeStruct((n, PEER_SEL), F32),
        compiler_params=_tc_params(1), name="gate",
    )(hval, gate)


VT_ROWS = D_MODEL // 2 // LANES
VT_TOKENS = 64
HI_MASK = -65536


def _pack_value_table(peer_v):
    e = peer_v.shape[0]
    bits = lax.bitcast_convert_type(peer_v.astype(BF16), jnp.uint16).astype(jnp.uint32)
    words = bits[:, :D_MODEL // 2] | (bits[:, D_MODEL // 2:] << 16)
    return lax.bitcast_convert_type(words, jnp.int32).reshape(e * VT_ROWS, LANES)


def _vside_kernel(row_s, a_s, h_ref, tab_ref, y_ref):
    tb = h_ref.shape[0]

    def token(t, carry):
        hv = h_ref[t]
        zero = jnp.zeros((VT_ROWS, LANES), F32)
        lo = [hv[0:VT_ROWS], zero]
        hi = [hv[VT_ROWS:], zero]
        for e in range(PEER_SEL):
            r0 = pl.multiple_of(row_s[t, e], VT_ROWS)
            w = tab_ref[pl.ds(r0, VT_ROWS), :]
            coef = a_s[t, e]
            c = e % 2
            lo[c] = lo[c] + coef * pltpu.bitcast(w << 16, F32)
            hi[c] = hi[c] + coef * pltpu.bitcast(w & HI_MASK, F32)
        y_ref[t] = jnp.concatenate([lo[0] + lo[1], hi[0] + hi[1]], axis=0)
        return carry

    lax.fori_loop(0, tb, token, 0)


def _peer_values(h, a, row_ids, v_packed):
    n = h.shape[0]
    tb = min(VT_TOKENS, n)
    smem = pl.BlockSpec((tb, PEER_SEL), lambda i: (i, 0), memory_space=pltpu.SMEM)
    slab = pl.BlockSpec((tb, 2 * VT_ROWS, LANES), lambda i: (i, 0, 0))
    table = pl.BlockSpec(v_packed.shape, lambda i: (0, 0), pipeline_mode=pl.Buffered(1))
    table_bytes = v_packed.shape[0] * LANES * 4
    y = pl.pallas_call(
        _vside_kernel,
        grid=(n // tb,),
        in_specs=[smem, smem, slab, table],
        out_specs=slab,
        out_shape=jax.ShapeDtypeStruct((n, 2 * VT_ROWS, LANES), F32),
        compiler_params=pltpu.CompilerParams(
            dimension_semantics=("arbitrary",),
            vmem_limit_bytes=table_bytes + 8 * 1024 * 1024),
        name="peer_values",
        cost_estimate=pl.CostEstimate(flops=2 * n * PEER_SEL * D_MODEL, transcendentals=0,
                                      bytes_accessed=table_bytes + n * PEER_SEL * D_MODEL * 2),
    )(row_ids, a, h.reshape(n, 2 * VT_ROWS, LANES), v_packed)
    return y.reshape(n, D_MODEL)


PEER_CHUNK = 4096


PEER_DEPTH = 2


def _peer_launch(h, hn, tb, after, wq_bf, keys1_bf, keys2_bf, peer_u):
    n = h.shape[0]
    eid_t, gate_t = _route(hn, tb, wq_bf, keys1_bf, keys2_bf)
    eid = eid_t.T
    pad = (-n) % (SC_WORKERS * SC_TOKENS)
    hn_p, eid_p = hn, eid
    if pad:
        spread = (jnp.arange(pad * PEER_SEL, dtype=jnp.int32) % peer_u.shape[0]).reshape(pad, PEER_SEL)
        hn_p = jnp.pad(hn, ((0, pad), (0, 0)))
        eid_p = jnp.concatenate([eid, spread], axis=0)
    hval = _peer_hval(hn_p, eid_p, peer_u, after)[:n]
    return h, eid, gate_t.T, hval


def _peer_finish(h, eid, gate, hval, tb, v_packed):
    a = _gate(hval, gate, min(tb, 512))
    return _peer_values(h, a, eid * VT_ROWS, v_packed)


def _rope_tables(pos):
    half = HEAD_DIM // 2
    inv = ROPE_THETA ** (-jnp.arange(half, dtype=F32) / half)
    ang = pos.astype(F32)[:, None] * inv[None, :]
    cos = jnp.cos(ang)
    sin = jnp.sin(ang)
    reps = LANES // HEAD_DIM
    cos_f = jnp.tile(jnp.concatenate([cos, cos], axis=1), (1, reps))
    sin_s = jnp.tile(jnp.concatenate([-sin, sin], axis=1), (1, reps))
    return cos_f, sin_s


def _group_sum_matrix(width):
    g = jnp.arange(width) // HEAD_DIM
    return (g[:, None] == g[None, :]).astype(BF16)


def _heads_first(a, heads):
    b, t, _ = a.shape
    return a.reshape(b, t, heads, HEAD_DIM).transpose(0, 2, 1, 3)


def kernel(x_prompt, x_sample, cache_k_win, cache_v_win, state_conv, meta_tokens, norm_mix_g,
           w_in, q_norm_g, k_norm_g, attn_sinks, conv_w, conv_b, conv_ln_g, conv_ln_b, w_pw2,
           out_norm_attn_g, out_norm_conv_g, w_out, norm_ffn_g, peer_w_q, peer_keys1, peer_keys2,
           peer_u, peer_v):
    assert norm_mix_g.shape[0] == 1, "single-layer model"
    b, t, _ = x_prompt.shape
    s = x_sample.shape[0]
    w_buf = cache_k_win.shape[2]
    n = b * t

    g_mix = norm_mix_g[0][None, :]
    w_in_bf = w_in[0].astype(BF16)
    qg_t = jnp.tile(q_norm_g[0], N_HEADS)[None, :]
    kg_t = jnp.tile(k_norm_g[0], N_KV_HEADS)[None, :]
    gsum_q = _group_sum_matrix(D_ATTN)
    gsum_k = _group_sum_matrix(KV_DIM)
    sinks = attn_sinks[0]
    cw, cb = conv_w[0], conv_b[0][None, :]
    lg, lb = conv_ln_g[0][None, :], conv_ln_b[0][None, :]
    w2_bf = w_pw2[0].astype(BF16)
    g_a, g_c = out_norm_attn_g[0][None, :], out_norm_conv_g[0][None, :]
    w_out_a = w_out[0][:D_ATTN].astype(BF16)
    w_out_c = w_out[0][D_ATTN:].astype(BF16)
    g_f = norm_ffn_g[0][None, :]
    wq_bf = peer_w_q[0].astype(BF16)
    k1_bf = peer_keys1[0].astype(BF16)
    k2_bf = peer_keys2[0].astype(BF16)
    pu, vp = peer_u[0], _pack_value_table(peer_v[0])
    proj = functools.partial(_project, norm_g=g_mix, w_in_bf=w_in_bf, qg_t=qg_t, kg_t=kg_t,
                             gsum_q=gsum_q, gsum_k=gsum_k)
    conv = functools.partial(_conv_branch, conv_w=cw, conv_b=cb, ln_g=lg, ln_b=lb, w_pw2_bf=w2_bf)
    merge = functools.partial(_merge, g_a=g_a, g_c=g_c, w_out_a=w_out_a, w_out_c=w_out_c, g_f=g_f)

    tb = 512
    no_dep = jnp.zeros((SC_TOKENS, LANES), F32)
    tab_x = _rope_tables(N_META + jnp.arange(t, dtype=jnp.int32))
    tab_m = _rope_tables(jnp.arange(N_META, dtype=jnp.int32))
    _, k_m, v_m, u_m = proj(meta_tokens, tab_m, 1, N_META, no_dep)
    halo0 = jnp.concatenate([jnp.zeros((CONV_HALO - N_META, D_CONV), F32), u_m], axis=0)

    def with_meta_block(a, a_m):
        full = jnp.concatenate([jnp.zeros((META_PAD, KV_DIM), F32), a_m, a], axis=0)
        return _heads_first(full[None], N_KV_HEADS)

    xp = x_prompt.reshape(b * t, D_MODEL)
    per_seq = max(1, t // PEER_CHUNK)
    chunk = t // per_seq

    def dense(bi, after):
        q, k, v, u = proj(xp, tab_x, t // tb, tb, after, row0=bi * t, nrows=t)
        o_attn = _prompt_attention(sinks, _heads_first(q[None], N_HEADS),
                                   with_meta_block(k, k_m), with_meta_block(v, v_m))
        o_attn = o_attn.transpose(0, 2, 1, 3).reshape(t, D_ATTN)
        u3 = u.reshape(t // tb, tb, D_CONV)
        halo = jnp.concatenate([halo0[None], u3[:-1, tb - CONV_HALO:]], axis=0)
        o_conv = conv(halo, u, tb)
        hs = [merge(xp, o_attn, o_conv, tb, x_row0=bi * t + j * chunk, row0=j * chunk, nrows=chunk)
              for j in range(per_seq)]
        return hs, k, v, u

    launched, ys, kvu = [], [], []
    cur = dense(0, no_dep)
    for bi in range(b):
        hs, k, v, u = cur
        kvu.append((k, v, u))
        for j in range(per_seq):
            c = bi * per_seq + j
            if c >= PEER_DEPTH:
                ys.append(_peer_finish(*launched[c - PEER_DEPTH], 256, vp))
            after = ys[c - PEER_DEPTH][:SC_TOKENS] if c >= PEER_DEPTH else jnp.zeros((SC_TOKENS, D_MODEL), F32)
            h, hn = hs[j]
            launched.append(_peer_launch(h, hn, 256, after, wq_bf, k1_bf, k2_bf, pu))
        if bi + 1 < b:
            cur = dense(bi + 1, launched[-1][1][:SC_TOKENS, :LANES].astype(F32))
    for c in range(len(ys), len(launched)):
        ys.append(_peer_finish(*launched[c], 256, vp))
    y_prompt = jnp.concatenate(ys, axis=0).reshape(b, t, D_MODEL)

    new_k_prompt = jnp.stack([k[t - WINDOW:] for k, _, _ in kvu]).reshape(1, b, WINDOW, N_KV_HEADS, HEAD_DIM)
    new_v_prompt = jnp.stack([v[t - WINDOW:] for _, v, _ in kvu]).reshape(1, b, WINDOW, N_KV_HEADS, HEAD_DIM)
    new_conv_prompt = jnp.stack([u[t - (CONV_WIDTH - 1):] for _, _, u in kvu])[None]

    xs = x_sample.reshape(s, D_MODEL)
    tab_s = _rope_tables(jnp.full((s,), PAST_LEN, jnp.int32))
    qs, ks, vs, us = proj(xs, tab_s, 1, s, no_dep)
    ck = cache_k_win[0]
    cv = cache_v_win[0]
    o_attn_s = _decode_attention(
        sinks,
        qs.reshape(s, N_HEADS, HEAD_DIM),
        ks.reshape(s, N_KV_HEADS, HEAD_DIM).transpose(1, 0, 2),
        vs.reshape(s, N_KV_HEADS, HEAD_DIM).transpose(1, 0, 2),
        ck.transpose(2, 0, 1, 3), cv.transpose(2, 0, 1, 3), 32)
    o_attn_s = o_attn_s.reshape(s, D_ATTN)
    cs = state_conv[0]
    hist = jnp.concatenate([jnp.zeros((s, CONV_HALO - (CONV_WIDTH - 1), D_CONV), F32), cs], axis=1)
    us_blk = jnp.concatenate([us[:, None, :], jnp.zeros((s, 7, D_CONV), F32)], axis=1)
    o_conv_s = conv(hist, us_blk.reshape(s * 8, D_CONV), 8).reshape(s, 8, D_CONV)[:, 0]
    hs, hns = merge(xs, o_attn_s, o_conv_s, s)
    launch_s = _peer_launch(hs, hns, s, jnp.zeros((SC_TOKENS, D_MODEL), F32), wq_bf, k1_bf, k2_bf, pu)
    y_sample = _peer_finish(*launch_s, s, vp).reshape(s, 1, D_MODEL)

    new_k_sample = jnp.concatenate([ck[:, 1:], ks.reshape(s, 1, N_KV_HEADS, HEAD_DIM)], axis=1)[None]
    new_v_sample = jnp.concatenate([cv[:, 1:], vs.reshape(s, 1, N_KV_HEADS, HEAD_DIM)], axis=1)[None]
    new_conv_sample = jnp.concatenate([cs[:, 1:], us[:, None, :]], axis=1)[None]
    if w_buf != WINDOW:
        raise NotImplementedError("cache window shorter than the attention window")

    return (y_prompt, y_sample, new_k_prompt, new_v_prompt, new_conv_prompt,
            new_k_sample, new_v_sample, new_conv_sample)
```

```python
import functools

import jax
import jax.numpy as jnp
from jax import lax
from jax.experimental import pallas as pl
from jax.experimental.pallas import tpu as pltpu
from jax.experimental.pallas import tpu_sc as plsc

D_MODEL = 1024
HEAD_DIM = 64
D_ATTN = 512
N_HEADS = 8
N_KV_HEADS = 2
KV_DIM = N_KV_HEADS * HEAD_DIM
D_CONV = 512
D_IN = D_ATTN + 2 * KV_DIM + 2 * D_CONV
CONV_WIDTH = 31
WINDOW = 128
BLOCK = 128
ROPE_THETA = 10000.0
N_META = 16
META_PAD = BLOCK - N_META
PEER_HEADS = 8
PEER_NKEYS = 128
PEER_TOPK = 16
PEER_SEL = PEER_HEADS * PEER_TOPK
EPS = 1e-6
PAST_LEN = 16384

LANES = 128
SC_CORES = 2
SC_SUBCORES = 16
SC_LANES = 16
SC_WORKERS = SC_CORES * SC_SUBCORES
VMEM_LIMIT = 48 * 1024 * 1024

F32 = jnp.float32
BF16 = jnp.bfloat16
NEG_INF = float("-inf")


def _tc_params(n_axes):
    return pltpu.CompilerParams(dimension_semantics=("arbitrary",) * n_axes,
                                vmem_limit_bytes=VMEM_LIMIT)


def _full(shape):
    nd = len(shape)
    return pl.BlockSpec(shape, lambda *_: (0,) * nd)


def _group_mean(sq, gsum_ref):
    hi = sq.astype(BF16)
    lo = (sq - hi.astype(F32)).astype(BF16)
    g = gsum_ref[...]
    s = jnp.dot(hi, g, preferred_element_type=F32) + jnp.dot(lo, g, preferred_element_type=F32)
    return s * (1.0 / HEAD_DIM)


def _rope(xn, cos_f, sin_s, first_half):
    outs = []
    for s in range(xn.shape[1] // LANES):
        xs = xn[:, s * LANES:(s + 1) * LANES]
        partner = jnp.where(first_half, pltpu.roll(xs, LANES - HEAD_DIM // 2, axis=1),
                            pltpu.roll(xs, HEAD_DIM // 2, axis=1))
        outs.append(xs * cos_f + partner * sin_s)
    return outs[0] if len(outs) == 1 else jnp.concatenate(outs, axis=1)


def _proj_kernel(x_ref, g_ref, w_ref, qg_ref, kg_ref, cos_ref, sin_ref, gq_ref, gk_ref, after_ref,
                 q_out, k_out, v_out, u_out):
    del after_ref
    x = x_ref[...]
    n = x * lax.rsqrt(jnp.mean(x * x, axis=-1, keepdims=True) + EPS) * g_ref[...]
    p = jnp.dot(n.astype(BF16), w_ref[...], preferred_element_type=F32)
    q = p[:, :D_ATTN]
    k = p[:, D_ATTN:D_ATTN + KV_DIM]
    v = p[:, D_ATTN + KV_DIM:D_ATTN + 2 * KV_DIM]
    ga = p[:, D_ATTN + 2 * KV_DIM:D_ATTN + 2 * KV_DIM + D_CONV]
    gb = p[:, D_ATTN + 2 * KV_DIM + D_CONV:]
    cos_f = cos_ref[...]
    sin_s = sin_ref[...]
    lane = lax.broadcasted_iota(jnp.int32, (x.shape[0], LANES), 1)
    first_half = (lane % HEAD_DIM) < (HEAD_DIM // 2)
    qn = q * lax.rsqrt(_group_mean(q * q, gq_ref) + EPS) * qg_ref[...]
    kn = k * lax.rsqrt(_group_mean(k * k, gk_ref) + EPS) * kg_ref[...]
    q_out[...] = _rope(qn, cos_f, sin_s, first_half)
    k_out[...] = _rope(kn, cos_f, sin_s, first_half)
    v_out[...] = v
    u_out[...] = ga * (1.0 / (1.0 + jnp.exp(-gb)))


def _project(x, pos_tables, n_table_blocks, tb, after, norm_g, w_in_bf, qg_t, kg_t, gsum_q, gsum_k,
             row0=0, nrows=None):
    n = x.shape[0] if nrows is None else nrows
    cos_t, sin_t = pos_tables
    nb = n // tb
    blk0 = row0 // tb
    tab_spec = pl.BlockSpec((tb, LANES), lambda i: (i % n_table_blocks, 0))
    row = lambda w: pl.BlockSpec((tb, w), lambda i: (i, 0))
    return pl.pallas_call(
        _proj_kernel,
        grid=(nb,),
        in_specs=[pl.BlockSpec((tb, D_MODEL), lambda i: (i + blk0, 0)),
                  _full((1, D_MODEL)), _full((D_MODEL, D_IN)),
                  _full((1, D_ATTN)), _full((1, KV_DIM)), tab_spec, tab_spec,
                  _full((D_ATTN, D_ATTN)), _full((KV_DIM, KV_DIM)),
                  pl.BlockSpec(memory_space=pl.ANY)],
        out_specs=[row(D_ATTN), row(KV_DIM), row(KV_DIM), row(D_CONV)],
        out_shape=[jax.ShapeDtypeStruct((n, D_ATTN), F32), jax.ShapeDtypeStruct((n, KV_DIM), F32),
                   jax.ShapeDtypeStruct((n, KV_DIM), F32), jax.ShapeDtypeStruct((n, D_CONV), F32)],
        compiler_params=_tc_params(1),
        name="proj",
    )(x, norm_g, w_in_bf, qg_t, kg_t, cos_t, sin_t, gsum_q, gsum_k, after)


def _attn_kernel(sink_ref, q_ref, kp_ref, kc_ref, vp_ref, vc_ref, km_ref, vm_ref, o_ref):
    j = pl.program_id(0)
    r = lax.broadcasted_iota(jnp.int32, (BLOCK, 2 * BLOCK), 0)
    c = lax.broadcasted_iota(jnp.int32, (BLOCK, 2 * BLOCK), 1)
    ok = (c > r) & (c <= r + WINDOW) & ((j > 0) | (c >= META_PAD))
    grp = N_HEADS // N_KV_HEADS
    ok = jnp.concatenate([ok] * grp, axis=0)
    first = j == 0
    k_all = jnp.concatenate([jnp.where(first, km_ref[...], kp_ref[...]), kc_ref[...]], axis=0)
    v_all = jnp.concatenate([jnp.where(first, vm_ref[...], vp_ref[...]), vc_ref[...]], axis=0)
    q = q_ref[...]
    outs = []
    for g in range(N_KV_HEADS):
        lanes = slice(g * HEAD_DIM, (g + 1) * HEAD_DIM)
        k = k_all[:, lanes].astype(BF16)
        v = v_all[:, lanes].astype(BF16)
        q4 = jnp.concatenate([q[:, (g * grp + i) * HEAD_DIM:(g * grp + i + 1) * HEAD_DIM]
                              for i in range(grp)], axis=0).astype(BF16)
        s = lax.dot_general(q4, k, (((1,), (1,)), ((), ())), preferred_element_type=F32)
        s = jnp.where(ok, s * (HEAD_DIM ** -0.5), NEG_INF)
        sink = jnp.concatenate(
            [jnp.full((BLOCK, 1), sink_ref[g * grp + i], F32) for i in range(grp)], axis=0)
        m = jnp.maximum(jnp.max(s, axis=1, keepdims=True), sink)
        p = jnp.exp(s - m)
        den = jnp.sum(p, axis=1, keepdims=True) + jnp.exp(sink - m)
        o = jnp.dot(p.astype(BF16), v, preferred_element_type=F32) / den
        outs += [o[i * BLOCK:(i + 1) * BLOCK] for i in range(grp)]
    o_ref[...] = jnp.concatenate(outs, axis=1)


def _prompt_attention(sinks, q, k, v, k_meta, v_meta):
    t = q.shape[0]
    kv_prev = pl.BlockSpec((BLOCK, KV_DIM), lambda j: (jnp.maximum(j - 1, 0), 0))
    kv_cur = pl.BlockSpec((BLOCK, KV_DIM), lambda j: (j, 0))
    q_spec = pl.BlockSpec((BLOCK, D_ATTN), lambda j: (j, 0))
    meta = _full((BLOCK, KV_DIM))
    return pl.pallas_call(
        _attn_kernel,
        grid=(t // BLOCK,),
        in_specs=[pl.BlockSpec(memory_space=pltpu.SMEM), q_spec, kv_prev, kv_cur, kv_prev, kv_cur,
                  meta, meta],
        out_specs=q_spec,
        out_shape=jax.ShapeDtypeStruct((t, D_ATTN), F32),
        compiler_params=_tc_params(1),
        name="attn",
    )(sinks, q, k, k, v, v, k_meta, v_meta)


def _dec_attn_kernel(sink_ref, q_ref, kn_ref, vn_ref, ck_ref, cv_ref, o_ref):
    grp = N_HEADS // N_KV_HEADS
    sb, w_buf = ck_ref.shape[0], ck_ref.shape[1]
    q = q_ref[...]
    qb = q.astype(BF16)
    head = lax.broadcasted_iota(jnp.int32, (sb, N_HEADS, 1), 1)
    in_g0 = head < grp
    kv_lanes = [slice(g * HEAD_DIM, (g + 1) * HEAD_DIM) for g in range(N_KV_HEADS)]
    ck = ck_ref[...]
    cv = cv_ref[...]
    s_g = [jnp.einsum("shd,swd->shw", qb, ck[:, :, ln].astype(BF16), preferred_element_type=F32)
           for ln in kv_lanes]
    s = jnp.where(in_g0, s_g[0], s_g[1]) * (HEAD_DIM ** -0.5)
    key_ok = lax.broadcasted_iota(jnp.int32, (sb, N_HEADS, w_buf), 2) >= 1
    s = jnp.where(key_ok, s, NEG_INF)
    rnd = lambda a: a.astype(BF16).astype(F32)
    kn_all, vn_all = kn_ref[...], vn_ref[...]
    kn = jnp.where(in_g0, kn_all[:, None, kv_lanes[0]], kn_all[:, None, kv_lanes[1]])
    vn = jnp.where(in_g0, vn_all[:, None, kv_lanes[0]], vn_all[:, None, kv_lanes[1]])
    s_self = jnp.sum(rnd(q) * rnd(kn), axis=-1, keepdims=True) * (HEAD_DIM ** -0.5)
    sink = sink_ref[...]
    m = jnp.maximum(jnp.maximum(jnp.max(s, axis=-1, keepdims=True), s_self), sink)
    p = jnp.exp(s - m)
    p_self = jnp.exp(s_self - m)
    den = jnp.sum(p, axis=-1, keepdims=True) + p_self + jnp.exp(sink - m)
    pb = p.astype(BF16)
    o_g = [jnp.einsum("shw,swd->shd", pb, cv[:, :, ln].astype(BF16), preferred_element_type=F32)
           for ln in kv_lanes]
    o = jnp.where(in_g0, o_g[0], o_g[1]) + rnd(p_self) * rnd(vn)
    o_ref[...] = o / den


def _decode_attention(sinks, q3, kn_t, vn_t, ck_t, cv_t, sb):
    s = q3.shape[0]
    w_buf = ck_t.shape[1]
    qs = pl.BlockSpec((sb, N_HEADS, HEAD_DIM), lambda i: (i, 0, 0))
    ns = pl.BlockSpec((sb, KV_DIM), lambda i: (i, 0))
    cs = pl.BlockSpec((sb, w_buf, KV_DIM), lambda i: (i, 0, 0))
    return pl.pallas_call(
        _dec_attn_kernel,
        grid=(s // sb,),
        in_specs=[_full((1, N_HEADS, 1)), qs, ns, ns, cs, cs],
        out_specs=qs,
        out_shape=jax.ShapeDtypeStruct(q3.shape, F32),
        compiler_params=_tc_params(1),
        name="dec_attn",
    )(sinks.reshape(1, N_HEADS, 1), q3, kn_t, vn_t, ck_t, cv_t)


CONV_HALO = 32
CONV_ROWS = 64


def _conv_kernel(halo_ref, u_ref, cw_ref, cb_ref, lg_ref, lb_ref, w2_ref, o_ref, ucat):
    tb = u_ref.shape[0]
    ucat[0:CONV_HALO, :] = halo_ref[0]
    ucat[CONV_HALO:, :] = u_ref[...]
    first = CONV_HALO - (CONV_WIDTH - 1)
    rows = min(CONV_ROWS, tb)
    for r0 in range(0, tb, rows):
        acc = jnp.zeros((rows, D_CONV), F32)
        for j in range(CONV_WIDTH):
            acc = acc + ucat[r0 + first + j:r0 + first + j + rows, :] * cw_ref[j:j + 1, :]
        y = acc + cb_ref[...]
        yc = y - jnp.mean(y, axis=-1, keepdims=True)
        yn = yc * lax.rsqrt(jnp.mean(yc * yc, axis=-1, keepdims=True) + EPS)
        yn = yn * lg_ref[...] + lb_ref[...]
        act = yn * (1.0 / (1.0 + jnp.exp(-yn)))
        o_ref[r0:r0 + rows, :] = jnp.dot(act.astype(BF16), w2_ref[...],
                                              preferred_element_type=F32)


def _conv_branch(halo, u, tb, conv_w, conv_b, ln_g, ln_b, w_pw2_bf):
    n = u.shape[0]
    return pl.pallas_call(
        _conv_kernel,
        grid=(n // tb,),
        in_specs=[pl.BlockSpec((1, CONV_HALO, D_CONV), lambda i: (i, 0, 0)),
                  pl.BlockSpec((tb, D_CONV), lambda i: (i, 0)),
                  _full((CONV_WIDTH, D_CONV)), _full((1, D_CONV)), _full((1, D_CONV)),
                  _full((1, D_CONV)), _full((D_CONV, D_CONV))],
        out_specs=pl.BlockSpec((tb, D_CONV), lambda i: (i, 0)),
        out_shape=jax.ShapeDtypeStruct((n, D_CONV), F32),
        scratch_shapes=[pltpu.VMEM((tb + CONV_HALO, D_CONV), F32)],
        compiler_params=_tc_params(1),
        name="conv",
    )(halo, u, conv_w, conv_b, ln_g, ln_b, w_pw2_bf)


def _rms(x, g):
    return x * lax.rsqrt(jnp.mean(x * x, axis=-1, keepdims=True) + EPS) * g


def _merge_kernel(x_ref, oa_ref, oc_ref, ga_ref, gc_ref, wa_ref, wc_ref, gf_ref, h_out, hn_out):
    a = _rms(oa_ref[...], ga_ref[...]).astype(BF16)
    c = _rms(oc_ref[...], gc_ref[...]).astype(BF16)
    h = x_ref[...] + (jnp.dot(a, wa_ref[...], preferred_element_type=F32)
                      + jnp.dot(c, wc_ref[...], preferred_element_type=F32))
    h_out[...] = h
    hn_out[...] = _rms(h, gf_ref[...])


def _merge(x, oa, oc, tb, g_a, g_c, w_out_a, w_out_c, g_f, x_row0=0, row0=0, nrows=None):
    n = oa.shape[0] if nrows is None else nrows
    xb0, b0 = x_row0 // tb, row0 // tb
    row = lambda w: pl.BlockSpec((tb, w), lambda i: (i, 0))
    src = lambda w: pl.BlockSpec((tb, w), lambda i: (i + b0, 0))
    return pl.pallas_call(
        _merge_kernel,
        grid=(n // tb,),
        in_specs=[pl.BlockSpec((tb, D_MODEL), lambda i: (i + xb0, 0)), src(D_ATTN), src(D_CONV),
                  _full((1, D_ATTN)), _full((1, D_CONV)),
                  _full((D_ATTN, D_MODEL)), _full((D_CONV, D_MODEL)), _full((1, D_MODEL))],
        out_specs=[row(D_MODEL), row(D_MODEL)],
        out_shape=[jax.ShapeDtypeStruct((n, D_MODEL), F32)] * 2,
        compiler_params=_tc_params(1),
        name="merge",
    )(x, oa, oc, g_a, g_c, w_out_a, w_out_c, g_f)


ID_BIG = 1e9


def _topk_rows(s, k):
    rows = lax.broadcasted_iota(jnp.int32, s.shape, 0).astype(F32)
    vals, idxs = [], []
    for _ in range(k):
        m = jnp.max(s, axis=0, keepdims=True)
        idx = jnp.min(jnp.where(s == m, rows, ID_BIG), axis=0, keepdims=True)
        vals.append(m)
        idxs.append(idx)
        s = jnp.where(rows == idx, NEG_INF, s)
    return jnp.concatenate(vals, axis=0), jnp.concatenate(idxs, axis=0)


PAIR_B_WIDE = 8


def _route_kernel(hn_ref, wq_ref, k1_ref, k2_ref, eid_out, gate_out):
    tb = hn_ref.shape[0]
    q = jnp.dot(hn_ref[...].astype(BF16), wq_ref[...], preferred_element_type=F32).astype(BF16)
    k1 = k1_ref[...]
    k2 = k2_ref[...]
    nt = (((1,), (1,)), ((), ()))
    r = lax.broadcasted_iota(jnp.int32, (PEER_TOPK + (PAIR_B_WIDE - 1) * PAIR_B_WIDE + PAIR_B_WIDE, tb), 0)
    mid = r - PEER_TOPK
    flat = jnp.where(r < PEER_TOPK, r,
                     jnp.where(mid < (PAIR_B_WIDE - 1) * PAIR_B_WIDE,
                               (1 + mid // PAIR_B_WIDE) * PEER_TOPK + mid % PAIR_B_WIDE,
                               (PAIR_B_WIDE + mid - (PAIR_B_WIDE - 1) * PAIR_B_WIDE) * PEER_TOPK)).astype(F32)
    half = PEER_NKEYS
    for h in range(PEER_HEADS):
        q1 = q[:, (2 * h) * half:(2 * h + 1) * half]
        q2 = q[:, (2 * h + 1) * half:(2 * h + 2) * half]
        s1 = lax.dot_general(k1, q1, nt, preferred_element_type=F32)
        s2 = lax.dot_general(k2, q2, nt, preferred_element_type=F32)
        v1, i1 = _topk_rows(s1, PEER_TOPK)
        v2, i2 = _topk_rows(s2, PEER_TOPK)
        e1 = i1 * PEER_NKEYS
        cand = jnp.concatenate(
            [v1[0:1] + v2]
            + [v1[a:a + 1] + v2[0:PAIR_B_WIDE] for a in range(1, PAIR_B_WIDE)]
            + [v1[PAIR_B_WIDE:] + v2[0:1]], axis=0)
        cid = jnp.concatenate(
            [e1[0:1] + i2]
            + [e1[a:a + 1] + i2[0:PAIR_B_WIDE] for a in range(1, PAIR_B_WIDE)]
            + [e1[PAIR_B_WIDE:] + i2[0:1]], axis=0)
        scs, eids = [], []
        for _ in range(PEER_TOPK):
            m = jnp.max(cand, axis=0, keepdims=True)
            jsel = jnp.min(jnp.where(cand == m, flat, ID_BIG), axis=0, keepdims=True)
            hit = flat == jsel
            eids.append(jnp.max(jnp.where(hit, cid, -1.0), axis=0, keepdims=True))
            scs.append(m)
            cand = jnp.where(hit, NEG_INF, cand)
        sc = jnp.concatenate(scs, axis=0)
        e = jnp.exp(sc - sc[0:1])
        gate_out[h * PEER_TOPK:(h + 1) * PEER_TOPK, :] = e / jnp.sum(e, axis=0, keepdims=True)
        eid_out[h * PEER_TOPK:(h + 1) * PEER_TOPK, :] = jnp.concatenate(eids, axis=0).astype(jnp.int32)


def _route(hn, tb, wq_bf, keys1_bf, keys2_bf):
    n = hn.shape[0]
    col = pl.BlockSpec((PEER_SEL, tb), lambda i: (0, i))
    return pl.pallas_call(
        _route_kernel,
        grid=(n // tb,),
        in_specs=[pl.BlockSpec((tb, D_MODEL), lambda i: (i, 0)),
                  _full((D_MODEL, 2 * PEER_NKEYS * PEER_HEADS)),
                  _full((PEER_NKEYS, PEER_NKEYS)), _full((PEER_NKEYS, PEER_NKEYS))],
        out_specs=[col, col],
        out_shape=[jax.ShapeDtypeStruct((PEER_SEL, n), jnp.int32),
                   jax.ShapeDtypeStruct((PEER_SEL, n), F32)],
        compiler_params=_tc_params(1),
        name="route",
    )(hn, wq_bf, keys1_bf, keys2_bf)


SC_ROWS = 32
SC_GROUP = SC_LANES
SC_TOKENS = 8
SC_CHUNKS = PEER_SEL // SC_ROWS

_SC_PARAMS = pltpu.CompilerParams(needs_layout_passes=False)


def _sc_mesh():
    return plsc.VectorSubcoreMesh(core_axis_name="c", subcore_axis_name="s",
                                  num_cores=SC_CORES, num_subcores=SC_SUBCORES)


def _sc_token_loop(n_batches, table_hbm, eid_v, rbufs, rsems, prefetch, prefetch_wait, store,
                   contract):
    def gather(slot, tl, ch, p):
        idx = eid_v.at[slot, tl, pl.ds(ch * SC_ROWS, SC_ROWS)]
        return pltpu.make_async_copy(table_hbm.at[idx], rbufs[p], rsems[p])

    prefetch(0, 0)
    prefetch_wait()
    gather(0, 0, 0, 0).start()

    @pl.loop(0, n_batches * SC_TOKENS)
    def _(i):
        b = i // SC_TOKENS
        tl = i % SC_TOKENS
        slot = b % 2
        more = b + 1 < n_batches

        @pl.when(jnp.logical_and(tl == 0, more))
        def _():
            prefetch(b + 1, 1 - slot)

        for ch in range(SC_CHUNKS):
            p = ch % 2
            if ch + 1 < SC_CHUNKS:
                gather(slot, tl, ch + 1, 1 - p).start()
            else:
                @pl.when(tl + 1 < SC_TOKENS)
                def _():
                    gather(slot, tl + 1, 0, 1 - p).start()

                @pl.when(jnp.logical_and(tl + 1 == SC_TOKENS, more))
                def _():
                    prefetch_wait()
                    gather(1 - slot, 0, 0, 1 - p).start()
            gather(slot, tl, ch, p).wait()
            contract(slot, tl, ch, rbufs[p])

        @pl.when(tl + 1 == SC_TOKENS)
        def _():
            @pl.when(b >= 1)
            def _():
                store(b - 1, 1 - slot).wait()
            store(b, slot).start()

    store(n_batches - 1, (n_batches - 1) % 2).wait()


def _sc_batches(n):
    assert n % (SC_WORKERS * SC_TOKENS) == 0, n
    return n // (SC_WORKERS * SC_TOKENS)


def _peer_hval(hn, eid, peer_u, after):
    n = hn.shape[0]
    nbw = _sc_batches(n)

    @functools.partial(
        pl.kernel, mesh=_sc_mesh(),
        out_type=jax.ShapeDtypeStruct((n // SC_TOKENS, SC_TOKENS, PEER_SEL), F32),
        scratch_types=[
            pltpu.VMEM((2, SC_TOKENS, PEER_SEL), jnp.int32),
            pltpu.VMEM((2, SC_TOKENS, D_MODEL), F32),
            pltpu.VMEM((SC_ROWS, D_MODEL), F32),
            pltpu.VMEM((SC_ROWS, D_MODEL), F32),
            pltpu.VMEM((2, SC_TOKENS, PEER_SEL), F32),
            pltpu.VMEM((SC_GROUP, SC_LANES), F32),
            pltpu.SemaphoreType.DMA, pltpu.SemaphoreType.DMA,
            pltpu.SemaphoreType.DMA, pltpu.SemaphoreType.DMA,
        ],
        compiler_params=_SC_PARAMS, name="peer_hval",
        cost_estimate=pl.CostEstimate(flops=2 * n * PEER_SEL * D_MODEL, transcendentals=0,
                                      bytes_accessed=n * PEER_SEL * D_MODEL * 4))
    def k(x_hbm, eid_hbm, u_hbm, after_hbm, o_hbm, eid_v, x_v, r0, r1, h_v, tr, sr0, sr1, spf, sout):
        del after_hbm
        blk0 = (lax.axis_index("s") * SC_CORES + lax.axis_index("c")) * nbw
        lane = lax.iota(jnp.int32, SC_LANES)
        zero = jnp.zeros((SC_LANES,), F32)

        def prefetch(b, slot):
            pltpu.async_copy(eid_hbm.at[blk0 + b], eid_v.at[slot], spf)
            pltpu.async_copy(x_hbm.at[blk0 + b], x_v.at[slot], spf)

        def prefetch_wait():
            pltpu.make_async_copy(eid_hbm.at[0], eid_v.at[0], spf).wait()
            pltpu.make_async_copy(x_hbm.at[0], x_v.at[0], spf).wait()

        def store(b, slot):
            return pltpu.make_async_copy(h_v.at[slot], o_hbm.at[blk0 + b], sout)

        def contract(slot, tl, ch, rbuf):
            for g in range(SC_ROWS // SC_GROUP):
                def body(c, accs):
                    xc = x_v[slot, tl, pl.ds(c * SC_LANES, SC_LANES)]
                    return tuple(
                        accs[r] + rbuf[g * SC_GROUP + r, pl.ds(c * SC_LANES, SC_LANES)] * xc
                        for r in range(SC_GROUP))
                accs = lax.fori_loop(0, D_MODEL // SC_LANES, body, (zero,) * SC_GROUP)
                for r in range(SC_GROUP):
                    tr[r, :] = accs[r]
                res = zero
                for jj in range(SC_LANES):
                    res = res + plsc.load_gather(tr, [lane, jnp.full((SC_LANES,), jj, jnp.int32)])
                h_v[slot, tl, pl.ds(ch * SC_ROWS + g * SC_GROUP, SC_GROUP)] = res

        _sc_token_loop(nbw, u_hbm, eid_v, (r0, r1), (sr0, sr1), prefetch, prefetch_wait, store,
                       contract)

    out = k(hn.reshape(n // SC_TOKENS, SC_TOKENS, D_MODEL),
            eid.reshape(n // SC_TOKENS, SC_TOKENS, PEER_SEL), peer_u, after)
    return out.reshape(n, PEER_SEL)


def _peer_out(h, a, eid, peer_v):
    n = h.shape[0]
    nbw = _sc_batches(n)

    @functools.partial(
        pl.kernel, mesh=_sc_mesh(),
        out_type=jax.ShapeDtypeStruct((n // SC_TOKENS, SC_TOKENS, D_MODEL), F32),
        scratch_types=[
            pltpu.VMEM((2, SC_TOKENS, PEER_SEL), jnp.int32),
            pltpu.VMEM((2, SC_TOKENS, PEER_SEL), F32),
            pltpu.VMEM((SC_ROWS, D_MODEL), F32),
            pltpu.VMEM((SC_ROWS, D_MODEL), F32),
            pltpu.VMEM((2, SC_TOKENS, D_MODEL), F32),
            pltpu.SemaphoreType.DMA, pltpu.SemaphoreType.DMA,
            pltpu.SemaphoreType.DMA, pltpu.SemaphoreType.DMA,
        ],
        compiler_params=_SC_PARAMS, name="peer_out")
    def k(h_hbm, a_hbm, eid_hbm, v_hbm, o_hbm, eid_v, a_v, r0, r1, y_v, sr0, sr1, spf, sout):
        blk0 = (lax.axis_index("s") * SC_CORES + lax.axis_index("c")) * nbw

        def prefetch(b, slot):
            pltpu.async_copy(eid_hbm.at[blk0 + b], eid_v.at[slot], spf)
            pltpu.async_copy(a_hbm.at[blk0 + b], a_v.at[slot], spf)
            pltpu.async_copy(h_hbm.at[blk0 + b], y_v.at[slot], spf)

        def prefetch_wait():
            pltpu.make_async_copy(eid_hbm.at[0], eid_v.at[0], spf).wait()
            pltpu.make_async_copy(a_hbm.at[0], a_v.at[0], spf).wait()
            pltpu.make_async_copy(h_hbm.at[0], y_v.at[0], spf).wait()

        def store(b, slot):
            return pltpu.make_async_copy(y_v.at[slot], o_hbm.at[blk0 + b], sout)

        def contract(slot, tl, ch, rbuf):
            slot_v = jnp.full((SC_LANES,), slot, jnp.int32)
            tl_v = jnp.full((SC_LANES,), tl, jnp.int32)
            for g in range(SC_ROWS // SC_GROUP):
                base = ch * SC_ROWS + g * SC_GROUP
                coef = [plsc.load_gather(a_v, [slot_v, tl_v, jnp.full((SC_LANES,), base + r, jnp.int32)])
                        for r in range(SC_GROUP)]

                @plsc.parallel_loop(0, D_MODEL // SC_LANES)
                def _(c):
                    sl = pl.ds(c * SC_LANES, SC_LANES)
                    acc = coef[0] * rbuf[g * SC_GROUP, sl]
                    for r in range(1, SC_GROUP):
                        acc = acc + coef[r] * rbuf[g * SC_GROUP + r, sl]
                    plsc.addupdate(y_v.at[slot, tl, sl], acc)

        _sc_token_loop(nbw, v_hbm, eid_v, (r0, r1), (sr0, sr1), prefetch, prefetch_wait, store,
                       contract)

    out = k(h.reshape(n // SC_TOKENS, SC_TOKENS, D_MODEL),
            a.reshape(n // SC_TOKENS, SC_TOKENS, PEER_SEL),
            eid.reshape(n // SC_TOKENS, SC_TOKENS, PEER_SEL), peer_v)
    return out.reshape(n, D_MODEL)


def _gate_kernel(hv_ref, gate_ref, a_out):
    hv = hv_ref[...]
    gelu = hv * (lax.erf(hv * (2.0 ** -0.5)) + 1.0) * 0.5
    a_out[...] = gate_ref[...] * gelu


def _gate(hval, gate, tb):
    n = hval.shape[0]
    row = pl.BlockSpec((tb, PEER_SEL), lambda i: (i, 0))
    return pl.pallas_call(
        _gate_kernel, grid=(n // tb,), in_specs=[row, row], out_specs=row,
        out_shape=jax.ShapeDtypeStruct((n, PEER_SEL), F32),
        compiler_params=_tc_params(1), name="gate",
    )(hval, gate)


VT_ROWS = D_MODEL // 2 // LANES
VT_TOKENS = 64
HI_MASK = -65536


def _pack_value_table(peer_v):
    e = peer_v.shape[0]
    bits = lax.bitcast_convert_type(peer_v.astype(BF16), jnp.uint16).astype(jnp.uint32)
    words = bits[:, :D_MODEL // 2] | (bits[:, D_MODEL // 2:] << 16)
    return lax.bitcast_convert_type(words, jnp.int32).reshape(e * VT_ROWS, LANES)


def _vside_kernel(row_s, a_s, h_ref, tab_ref, y_ref):
    tb = h_ref.shape[0]

    def token(t, carry):
        hv = h_ref[t]
        zero = jnp.zeros((VT_ROWS, LANES), F32)
        lo = [hv[0:VT_ROWS], zero]
        hi = [hv[VT_ROWS:], zero]
        for e in range(PEER_SEL):
            r0 = pl.multiple_of(row_s[t, e], VT_ROWS)
            w = tab_ref[pl.ds(r0, VT_ROWS), :]
            coef = a_s[t, e]
            c = e % 2
            lo[c] = lo[c] + coef * pltpu.bitcast(w << 16, F32)
            hi[c] = hi[c] + coef * pltpu.bitcast(w & HI_MASK, F32)
        y_ref[t] = jnp.concatenate([lo[0] + lo[1], hi[0] + hi[1]], axis=0)
        return carry

    lax.fori_loop(0, tb, token, 0)


def _peer_values(h, a, row_ids, v_packed):
    n = h.shape[0]
    tb = min(VT_TOKENS, n)
    smem = pl.BlockSpec((tb, PEER_SEL), lambda i: (i, 0), memory_space=pltpu.SMEM)
    slab = pl.BlockSpec((tb, 2 * VT_ROWS, LANES), lambda i: (i, 0, 0))
    table = pl.BlockSpec(v_packed.shape, lambda i: (0, 0), pipeline_mode=pl.Buffered(1))
    table_bytes = v_packed.shape[0] * LANES * 4
    y = pl.pallas_call(
        _vside_kernel,
        grid=(n // tb,),
        in_specs=[smem, smem, slab, table],
        out_specs=slab,
        out_shape=jax.ShapeDtypeStruct((n, 2 * VT_ROWS, LANES), F32),
        compiler_params=pltpu.CompilerParams(
            dimension_semantics=("arbitrary",),
            vmem_limit_bytes=table_bytes + 8 * 1024 * 1024),
        name="peer_values",
        cost_estimate=pl.CostEstimate(flops=2 * n * PEER_SEL * D_MODEL, transcendentals=0,
                                      bytes_accessed=table_bytes + n * PEER_SEL * D_MODEL * 2),
    )(row_ids, a, h.reshape(n, 2 * VT_ROWS, LANES), v_packed)
    return y.reshape(n, D_MODEL)


PEER_CHUNK = 4096


PEER_DEPTH = 2


def _peer_launch(h, hn, tb, after, wq_bf, keys1_bf, keys2_bf, peer_u):
    n = h.shape[0]
    eid_t, gate_t = _route(hn, tb, wq_bf, keys1_bf, keys2_bf)
    eid = eid_t.T
    pad = (-n) % (SC_WORKERS * SC_TOKENS)
    hn_p, eid_p = hn, eid
    if pad:
        spread = (jnp.arange(pad * PEER_SEL, dtype=jnp.int32) % peer_u.shape[0]).reshape(pad, PEER_SEL)
        hn_p = jnp.pad(hn, ((0, pad), (0, 0)))
        eid_p = jnp.concatenate([eid, spread], axis=0)
    hval = _peer_hval(hn_p, eid_p, peer_u, after)[:n]
    return h, eid, gate_t.T, hval


def _peer_finish(h, eid, gate, hval, tb, v_packed):
    a = _gate(hval, gate, min(tb, 512))
    return _peer_values(h, a, eid * VT_ROWS, v_packed)


def _rope_tables(pos):
    half = HEAD_DIM // 2
    inv = ROPE_THETA ** (-jnp.arange(half, dtype=F32) / half)
    ang = pos.astype(F32)[:, None] * inv[None, :]
    cos = jnp.cos(ang)
    sin = jnp.sin(ang)
    reps = LANES // HEAD_DIM
    cos_f = jnp.tile(jnp.concatenate([cos, cos], axis=1), (1, reps))
    sin_s = jnp.tile(jnp.concatenate([-sin, sin], axis=1), (1, reps))
    return cos_f, sin_s


def _group_sum_matrix(width):
    g = jnp.arange(width) // HEAD_DIM
    return (g[:, None] == g[None, :]).astype(BF16)


def _heads_first(a, heads):
    b, t, _ = a.shape
    return a.reshape(b, t, heads, HEAD_DIM).transpose(0, 2, 1, 3)


def kernel(x_prompt, x_sample, cache_k_win, cache_v_win, state_conv, meta_tokens, norm_mix_g,
           w_in, q_norm_g, k_norm_g, attn_sinks, conv_w, conv_b, conv_ln_g, conv_ln_b, w_pw2,
           out_norm_attn_g, out_norm_conv_g, w_out, norm_ffn_g, peer_w_q, peer_keys1, peer_keys2,
           peer_u, peer_v):
    assert norm_mix_g.shape[0] == 1, "single-layer model"
    b, t, _ = x_prompt.shape
    s = x_sample.shape[0]
    w_buf = cache_k_win.shape[2]
    n = b * t

    g_mix = norm_mix_g[0][None, :]
    w_in_bf = w_in[0].astype(BF16)
    qg_t = jnp.tile(q_norm_g[0], N_HEADS)[None, :]
    kg_t = jnp.tile(k_norm_g[0], N_KV_HEADS)[None, :]
    gsum_q = _group_sum_matrix(D_ATTN)
    gsum_k = _group_sum_matrix(KV_DIM)
    sinks = attn_sinks[0]
    cw, cb = conv_w[0], conv_b[0][None, :]
    lg, lb = conv_ln_g[0][None, :], conv_ln_b[0][None, :]
    w2_bf = w_pw2[0].astype(BF16)
    g_a, g_c = out_norm_attn_g[0][None, :], out_norm_conv_g[0][None, :]
    w_out_a = w_out[0][:D_ATTN].astype(BF16)
    w_out_c = w_out[0][D_ATTN:].astype(BF16)
    g_f = norm_ffn_g[0][None, :]
    wq_bf = peer_w_q[0].astype(BF16)
    k1_bf = peer_keys1[0].astype(BF16)
    k2_bf = peer_keys2[0].astype(BF16)
    pu, vp = peer_u[0], _pack_value_table(peer_v[0])
    proj = functools.partial(_project, norm_g=g_mix, w_in_bf=w_in_bf, qg_t=qg_t, kg_t=kg_t,
                             gsum_q=gsum_q, gsum_k=gsum_k)
    conv = functools.partial(_conv_branch, conv_w=cw, conv_b=cb, ln_g=lg, ln_b=lb, w_pw2_bf=w2_bf)
    merge = functools.partial(_merge, g_a=g_a, g_c=g_c, w_out_a=w_out_a, w_out_c=w_out_c, g_f=g_f)

    tb = 512
    no_dep = jnp.zeros((SC_TOKENS, LANES), F32)
    tab_x = _rope_tables(N_META + jnp.arange(t, dtype=jnp.int32))
    tab_m = _rope_tables(jnp.arange(N_META, dtype=jnp.int32))
    _, k_m, v_m, u_m = proj(meta_tokens, tab_m, 1, N_META, no_dep)
    halo0 = jnp.concatenate([jnp.zeros((CONV_HALO - N_META, D_CONV), F32), u_m], axis=0)

    k_meta = jnp.concatenate([jnp.zeros((META_PAD, KV_DIM), F32), k_m], axis=0)
    v_meta = jnp.concatenate([jnp.zeros((META_PAD, KV_DIM), F32), v_m], axis=0)

    xp = x_prompt.reshape(b * t, D_MODEL)
    per_seq = max(1, t // PEER_CHUNK)
    chunk = t // per_seq

    def dense(bi, after):
        q, k, v, u = proj(xp, tab_x, t // tb, tb, after, row0=bi * t, nrows=t)
        o_attn = _prompt_attention(sinks, q, k, v, k_meta, v_meta)
        u3 = u.reshape(t // tb, tb, D_CONV)
        halo = jnp.concatenate([halo0[None], u3[:-1, tb - CONV_HALO:]], axis=0)
        o_conv = conv(halo, u, tb)
        hs = [merge(xp, o_attn, o_conv, tb, x_row0=bi * t + j * chunk, row0=j * chunk, nrows=chunk)
              for j in range(per_seq)]
        return hs, k, v, u

    launched, ys, kvu = [], [], []
    cur = dense(0, no_dep)
    for bi in range(b):
        hs, k, v, u = cur
        kvu.append((k, v, u))
        for j in range(per_seq):
            c = bi * per_seq + j
            if c >= PEER_DEPTH:
                ys.append(_peer_finish(*launched[c - PEER_DEPTH], 256, vp))
            after = ys[c - PEER_DEPTH][:SC_TOKENS] if c >= PEER_DEPTH else jnp.zeros((SC_TOKENS, D_MODEL), F32)
            h, hn = hs[j]
            launched.append(_peer_launch(h, hn, 256, after, wq_bf, k1_bf, k2_bf, pu))
        if bi + 1 < b:
            cur = dense(bi + 1, launched[-1][1][:SC_TOKENS, :LANES].astype(F32))
    for c in range(len(ys), len(launched)):
        ys.append(_peer_finish(*launched[c], 256, vp))
    y_prompt = jnp.concatenate(ys, axis=0).reshape(b, t, D_MODEL)

    new_k_prompt = jnp.stack([k[t - WINDOW:] for k, _, _ in kvu]).reshape(1, b, WINDOW, N_KV_HEADS, HEAD_DIM)
    new_v_prompt = jnp.stack([v[t - WINDOW:] for _, v, _ in kvu]).reshape(1, b, WINDOW, N_KV_HEADS, HEAD_DIM)
    new_conv_prompt = jnp.stack([u[t - (CONV_WIDTH - 1):] for _, _, u in kvu])[None]

    xs = x_sample.reshape(s, D_MODEL)
    tab_s = _rope_tables(jnp.full((s,), PAST_LEN, jnp.int32))
    qs, ks, vs, us = proj(xs, tab_s, 1, s, no_dep)
    ck = cache_k_win[0]
    cv = cache_v_win[0]
    o_attn_s = _decode_attention(
        sinks,
        qs.reshape(s, N_HEADS, HEAD_DIM), ks, vs,
        ck.reshape(s, w_buf, KV_DIM), cv.reshape(s, w_buf, KV_DIM), 32)
    o_attn_s = o_attn_s.reshape(s, D_ATTN)
    cs = state_conv[0]
    hist = jnp.concatenate([jnp.zeros((s, CONV_HALO - (CONV_WIDTH - 1), D_CONV), F32), cs], axis=1)
    us_blk = jnp.concatenate([us[:, None, :], jnp.zeros((s, 7, D_CONV), F32)], axis=1)
    o_conv_s = conv(hist, us_blk.reshape(s * 8, D_CONV), 8).reshape(s, 8, D_CONV)[:, 0]
    hs, hns = merge(xs, o_attn_s, o_conv_s, s)
    launch_s = _peer_launch(hs, hns, s, jnp.zeros((SC_TOKENS, D_MODEL), F32), wq_bf, k1_bf, k2_bf, pu)
    y_sample = _peer_finish(*launch_s, s, vp).reshape(s, 1, D_MODEL)

    new_k_sample = jnp.concatenate([ck[:, 1:], ks.reshape(s, 1, N_KV_HEADS, HEAD_DIM)], axis=1)[None]
    new_v_sample = jnp.concatenate([cv[:, 1:], vs.reshape(s, 1, N_KV_HEADS, HEAD_DIM)], axis=1)[None]
    new_conv_sample = jnp.concatenate([cs[:, 1:], us[:, None, :]], axis=1)[None]
    if w_buf != WINDOW:
        raise NotImplementedError("cache window shorter than the attention window")

    return (y_prompt, y_sample, new_k_prompt, new_v_prompt, new_conv_prompt,
            new_k_sample, new_v_sample, new_conv_sample)
```

```python
import functools

import jax
import jax.numpy as jnp
from jax import lax
from jax.experimental import pallas as pl
from jax.experimental.pallas import tpu as pltpu
from jax.experimental.pallas import tpu_sc as plsc

D_MODEL = 1024
HEAD_DIM = 64
D_ATTN = 512
N_HEADS = 8
N_KV_HEADS = 2
KV_DIM = N_KV_HEADS * HEAD_DIM
D_CONV = 512
D_IN = D_ATTN + 2 * KV_DIM + 2 * D_CONV
CONV_WIDTH = 31
WINDOW = 128
BLOCK = 128
ROPE_THETA = 10000.0
N_META = 16
META_PAD = BLOCK - N_META
PEER_HEADS = 8
PEER_NKEYS = 128
PEER_TOPK = 16
PEER_SEL = PEER_HEADS * PEER_TOPK
EPS = 1e-6
PAST_LEN = 16384

LANES = 128
SC_CORES = 2
SC_SUBCORES = 16
SC_LANES = 16
SC_WORKERS = SC_CORES * SC_SUBCORES
VMEM_LIMIT = 48 * 1024 * 1024

F32 = jnp.float32
BF16 = jnp.bfloat16
NEG_INF = float("-inf")


def _tc_params(n_axes):
    return pltpu.CompilerParams(dimension_semantics=("arbitrary",) * n_axes,
                                vmem_limit_bytes=VMEM_LIMIT)


def _full(shape):
    nd = len(shape)
    return pl.BlockSpec(shape, lambda *_: (0,) * nd)


def _group_mean(sq, gsum_ref):
    hi = sq.astype(BF16)
    lo = (sq - hi.astype(F32)).astype(BF16)
    g = gsum_ref[...]
    s = jnp.dot(hi, g, preferred_element_type=F32) + jnp.dot(lo, g, preferred_element_type=F32)
    return s * (1.0 / HEAD_DIM)


def _rope(xn, cos_f, sin_s, first_half):
    outs = []
    for s in range(xn.shape[1] // LANES):
        xs = xn[:, s * LANES:(s + 1) * LANES]
        partner = jnp.where(first_half, pltpu.roll(xs, LANES - HEAD_DIM // 2, axis=1),
                            pltpu.roll(xs, HEAD_DIM // 2, axis=1))
        outs.append(xs * cos_f + partner * sin_s)
    return outs[0] if len(outs) == 1 else jnp.concatenate(outs, axis=1)


def _proj_kernel(x_ref, g_ref, w_ref, qg_ref, kg_ref, cos_ref, sin_ref, gq_ref, gk_ref, after_ref,
                 q_out, k_out, v_out, u_out):
    del after_ref
    x = x_ref[...]
    n = x * lax.rsqrt(jnp.mean(x * x, axis=-1, keepdims=True) + EPS) * g_ref[...]
    p = jnp.dot(n.astype(BF16), w_ref[...], preferred_element_type=F32)
    q = p[:, :D_ATTN]
    k = p[:, D_ATTN:D_ATTN + KV_DIM]
    v = p[:, D_ATTN + KV_DIM:D_ATTN + 2 * KV_DIM]
    ga = p[:, D_ATTN + 2 * KV_DIM:D_ATTN + 2 * KV_DIM + D_CONV]
    gb = p[:, D_ATTN + 2 * KV_DIM + D_CONV:]
    cos_f = cos_ref[...]
    sin_s = sin_ref[...]
    lane = lax.broadcasted_iota(jnp.int32, (x.shape[0], LANES), 1)
    first_half = (lane % HEAD_DIM) < (HEAD_DIM // 2)
    qn = q * lax.rsqrt(_group_mean(q * q, gq_ref) + EPS) * qg_ref[...]
    kn = k * lax.rsqrt(_group_mean(k * k, gk_ref) + EPS) * kg_ref[...]
    q_out[...] = _rope(qn, cos_f, sin_s, first_half)
    k_out[...] = _rope(kn, cos_f, sin_s, first_half)
    v_out[...] = v
    u_out[...] = ga * (1.0 / (1.0 + jnp.exp(-gb)))


def _project(x, pos_tables, n_table_blocks, tb, after, norm_g, w_in_bf, qg_t, kg_t, gsum_q, gsum_k,
             row0=0, nrows=None):
    n = x.shape[0] if nrows is None else nrows
    cos_t, sin_t = pos_tables
    nb = n // tb
    blk0 = row0 // tb
    tab_spec = pl.BlockSpec((tb, LANES), lambda i: (i % n_table_blocks, 0))
    row = lambda w: pl.BlockSpec((tb, w), lambda i: (i, 0))
    return pl.pallas_call(
        _proj_kernel,
        grid=(nb,),
        in_specs=[pl.BlockSpec((tb, D_MODEL), lambda i: (i + blk0, 0)),
                  _full((1, D_MODEL)), _full((D_MODEL, D_IN)),
                  _full((1, D_ATTN)), _full((1, KV_DIM)), tab_spec, tab_spec,
                  _full((D_ATTN, D_ATTN)), _full((KV_DIM, KV_DIM)),
                  pl.BlockSpec(memory_space=pl.ANY)],
        out_specs=[row(D_ATTN), row(KV_DIM), row(KV_DIM), row(D_CONV)],
        out_shape=[jax.ShapeDtypeStruct((n, D_ATTN), F32), jax.ShapeDtypeStruct((n, KV_DIM), F32),
                   jax.ShapeDtypeStruct((n, KV_DIM), F32), jax.ShapeDtypeStruct((n, D_CONV), F32)],
        compiler_params=_tc_params(1),
        name="proj",
    )(x, norm_g, w_in_bf, qg_t, kg_t, cos_t, sin_t, gsum_q, gsum_k, after)


def _attn_kernel(sink_ref, q_ref, kp_ref, kc_ref, vp_ref, vc_ref, km_ref, vm_ref, o_ref):
    j = pl.program_id(0)
    r = lax.broadcasted_iota(jnp.int32, (BLOCK, 2 * BLOCK), 0)
    c = lax.broadcasted_iota(jnp.int32, (BLOCK, 2 * BLOCK), 1)
    ok = (c > r) & (c <= r + WINDOW) & ((j > 0) | (c >= META_PAD))
    grp = N_HEADS // N_KV_HEADS
    ok = jnp.concatenate([ok] * grp, axis=0)
    first = j == 0
    k_all = jnp.concatenate([jnp.where(first, km_ref[...], kp_ref[...]), kc_ref[...]], axis=0)
    v_all = jnp.concatenate([jnp.where(first, vm_ref[...], vp_ref[...]), vc_ref[...]], axis=0)
    q = q_ref[...]
    outs = []
    for g in range(N_KV_HEADS):
        lanes = slice(g * HEAD_DIM, (g + 1) * HEAD_DIM)
        k = k_all[:, lanes].astype(BF16)
        v = v_all[:, lanes].astype(BF16)
        q4 = jnp.concatenate([q[:, (g * grp + i) * HEAD_DIM:(g * grp + i + 1) * HEAD_DIM]
                              for i in range(grp)], axis=0).astype(BF16)
        s = lax.dot_general(q4, k, (((1,), (1,)), ((), ())), preferred_element_type=F32)
        s = jnp.where(ok, s * (HEAD_DIM ** -0.5), NEG_INF)
        sink = jnp.concatenate(
            [jnp.full((BLOCK, 1), sink_ref[g * grp + i], F32) for i in range(grp)], axis=0)
        m = jnp.maximum(jnp.max(s, axis=1, keepdims=True), sink)
        p = jnp.exp(s - m)
        den = jnp.sum(p, axis=1, keepdims=True) + jnp.exp(sink - m)
        o = jnp.dot(p.astype(BF16), v, preferred_element_type=F32) / den
        outs += [o[i * BLOCK:(i + 1) * BLOCK] for i in range(grp)]
    o_ref[...] = jnp.concatenate(outs, axis=1)


def _prompt_attention(sinks, q, k, v, k_meta, v_meta):
    t = q.shape[0]
    kv_prev = pl.BlockSpec((BLOCK, KV_DIM), lambda j: (jnp.maximum(j - 1, 0), 0))
    kv_cur = pl.BlockSpec((BLOCK, KV_DIM), lambda j: (j, 0))
    q_spec = pl.BlockSpec((BLOCK, D_ATTN), lambda j: (j, 0))
    meta = _full((BLOCK, KV_DIM))
    return pl.pallas_call(
        _attn_kernel,
        grid=(t // BLOCK,),
        in_specs=[pl.BlockSpec(memory_space=pltpu.SMEM), q_spec, kv_prev, kv_cur, kv_prev, kv_cur,
                  meta, meta],
        out_specs=q_spec,
        out_shape=jax.ShapeDtypeStruct((t, D_ATTN), F32),
        compiler_params=_tc_params(1),
        name="attn",
    )(sinks, q, k, k, v, v, k_meta, v_meta)


def _dec_attn_kernel(sink_ref, q_ref, kn_ref, vn_ref, ck_ref, cv_ref, o_ref):
    grp = N_HEADS // N_KV_HEADS
    sb, w_buf = ck_ref.shape[0], ck_ref.shape[1]
    q = q_ref[...]
    qb = q.astype(BF16)
    head = lax.broadcasted_iota(jnp.int32, (sb, N_HEADS, 1), 1)
    in_g0 = head < grp
    kv_lanes = [slice(g * HEAD_DIM, (g + 1) * HEAD_DIM) for g in range(N_KV_HEADS)]
    ck = ck_ref[...]
    cv = cv_ref[...]
    s_g = [jnp.einsum("shd,swd->shw", qb, ck[:, :, ln].astype(BF16), preferred_element_type=F32)
           for ln in kv_lanes]
    s = jnp.where(in_g0, s_g[0], s_g[1]) * (HEAD_DIM ** -0.5)
    key_ok = lax.broadcasted_iota(jnp.int32, (sb, N_HEADS, w_buf), 2) >= 1
    s = jnp.where(key_ok, s, NEG_INF)
    rnd = lambda a: a.astype(BF16).astype(F32)
    kn_all, vn_all = kn_ref[...], vn_ref[...]
    kn = jnp.where(in_g0, kn_all[:, None, kv_lanes[0]], kn_all[:, None, kv_lanes[1]])
    vn = jnp.where(in_g0, vn_all[:, None, kv_lanes[0]], vn_all[:, None, kv_lanes[1]])
    s_self = jnp.sum(rnd(q) * rnd(kn), axis=-1, keepdims=True) * (HEAD_DIM ** -0.5)
    sink = sink_ref[...]
    m = jnp.maximum(jnp.maximum(jnp.max(s, axis=-1, keepdims=True), s_self), sink)
    p = jnp.exp(s - m)
    p_self = jnp.exp(s_self - m)
    den = jnp.sum(p, axis=-1, keepdims=True) + p_self + jnp.exp(sink - m)
    pb = p.astype(BF16)
    o_g = [jnp.einsum("shw,swd->shd", pb, cv[:, :, ln].astype(BF16), preferred_element_type=F32)
           for ln in kv_lanes]
    o = jnp.where(in_g0, o_g[0], o_g[1]) + rnd(p_self) * rnd(vn)
    o_ref[...] = o / den


def _decode_attention(sinks, q3, kn_t, vn_t, ck_t, cv_t, sb):
    s = q3.shape[0]
    w_buf = ck_t.shape[1]
    qs = pl.BlockSpec((sb, N_HEADS, HEAD_DIM), lambda i: (i, 0, 0))
    ns = pl.BlockSpec((sb, KV_DIM), lambda i: (i, 0))
    cs = pl.BlockSpec((sb, w_buf, KV_DIM), lambda i: (i, 0, 0))
    return pl.pallas_call(
        _dec_attn_kernel,
        grid=(s // sb,),
        in_specs=[_full((1, N_HEADS, 1)), qs, ns, ns, cs, cs],
        out_specs=qs,
        out_shape=jax.ShapeDtypeStruct(q3.shape, F32),
        compiler_params=_tc_params(1),
        name="dec_attn",
    )(sinks.reshape(1, N_HEADS, 1), q3, kn_t, vn_t, ck_t, cv_t)


CONV_HALO = 32
CONV_ROWS = 64


def _conv_kernel(halo_ref, u_ref, cw_ref, cb_ref, lg_ref, lb_ref, w2_ref, o_ref, ucat):
    tb = u_ref.shape[0]
    ucat[0:CONV_HALO, :] = halo_ref[0]
    ucat[CONV_HALO:, :] = u_ref[...]
    first = CONV_HALO - (CONV_WIDTH - 1)
    rows = min(CONV_ROWS, tb)
    for r0 in range(0, tb, rows):
        acc = jnp.zeros((rows, D_CONV), F32)
        for j in range(CONV_WIDTH):
            acc = acc + ucat[r0 + first + j:r0 + first + j + rows, :] * cw_ref[j:j + 1, :]
        y = acc + cb_ref[...]
        yc = y - jnp.mean(y, axis=-1, keepdims=True)
        yn = yc * lax.rsqrt(jnp.mean(yc * yc, axis=-1, keepdims=True) + EPS)
        yn = yn * lg_ref[...] + lb_ref[...]
        act = yn * (1.0 / (1.0 + jnp.exp(-yn)))
        o_ref[r0:r0 + rows, :] = jnp.dot(act.astype(BF16), w2_ref[...],
                                              preferred_element_type=F32)


def _conv_branch(halo, u, tb, conv_w, conv_b, ln_g, ln_b, w_pw2_bf):
    n = u.shape[0]
    return pl.pallas_call(
        _conv_kernel,
        grid=(n // tb,),
        in_specs=[pl.BlockSpec((1, CONV_HALO, D_CONV), lambda i: (i, 0, 0)),
                  pl.BlockSpec((tb, D_CONV), lambda i: (i, 0)),
                  _full((CONV_WIDTH, D_CONV)), _full((1, D_CONV)), _full((1, D_CONV)),
                  _full((1, D_CONV)), _full((D_CONV, D_CONV))],
        out_specs=pl.BlockSpec((tb, D_CONV), lambda i: (i, 0)),
        out_shape=jax.ShapeDtypeStruct((n, D_CONV), F32),
        scratch_shapes=[pltpu.VMEM((tb + CONV_HALO, D_CONV), F32)],
        compiler_params=_tc_params(1),
        name="conv",
    )(halo, u, conv_w, conv_b, ln_g, ln_b, w_pw2_bf)


def _rms(x, g):
    return x * lax.rsqrt(jnp.mean(x * x, axis=-1, keepdims=True) + EPS) * g


def _merge_kernel(x_ref, oa_ref, oc_ref, ga_ref, gc_ref, wa_ref, wc_ref, gf_ref, h_out, hn_out):
    a = _rms(oa_ref[...], ga_ref[...]).astype(BF16)
    c = _rms(oc_ref[...], gc_ref[...]).astype(BF16)
    h = x_ref[...] + (jnp.dot(a, wa_ref[...], preferred_element_type=F32)
                      + jnp.dot(c, wc_ref[...], preferred_element_type=F32))
    h_out[...] = h
    hn_out[...] = _rms(h, gf_ref[...])


def _merge(x, oa, oc, tb, g_a, g_c, w_out_a, w_out_c, g_f, x_row0=0, row0=0, nrows=None):
    n = oa.shape[0] if nrows is None else nrows
    xb0, b0 = x_row0 // tb, row0 // tb
    row = lambda w: pl.BlockSpec((tb, w), lambda i: (i, 0))
    src = lambda w: pl.BlockSpec((tb, w), lambda i: (i + b0, 0))
    return pl.pallas_call(
        _merge_kernel,
        grid=(n // tb,),
        in_specs=[pl.BlockSpec((tb, D_MODEL), lambda i: (i + xb0, 0)), src(D_ATTN), src(D_CONV),
                  _full((1, D_ATTN)), _full((1, D_CONV)),
                  _full((D_ATTN, D_MODEL)), _full((D_CONV, D_MODEL)), _full((1, D_MODEL))],
        out_specs=[row(D_MODEL), row(D_MODEL)],
        out_shape=[jax.ShapeDtypeStruct((n, D_MODEL), F32)] * 2,
        compiler_params=_tc_params(1),
        name="merge",
    )(x, oa, oc, g_a, g_c, w_out_a, w_out_c, g_f)


ID_BIG = 1e9


def _topk_rows(s, k):
    rows = lax.broadcasted_iota(jnp.int32, s.shape, 0).astype(F32)
    vals, idxs = [], []
    for _ in range(k):
        m = jnp.max(s, axis=0, keepdims=True)
        idx = jnp.min(jnp.where(s == m, rows, ID_BIG), axis=0, keepdims=True)
        vals.append(m)
        idxs.append(idx)
        s = jnp.where(rows == idx, NEG_INF, s)
    return jnp.concatenate(vals, axis=0), jnp.concatenate(idxs, axis=0)


PAIR_B_WIDE = 8


def _route_kernel(hn_ref, wq_ref, k1_ref, k2_ref, eid_out, gate_out):
    tb = hn_ref.shape[0]
    q = jnp.dot(hn_ref[...].astype(BF16), wq_ref[...], preferred_element_type=F32).astype(BF16)
    k1 = k1_ref[...]
    k2 = k2_ref[...]
    nt = (((1,), (1,)), ((), ()))
    r = lax.broadcasted_iota(jnp.int32, (PEER_TOPK + (PAIR_B_WIDE - 1) * PAIR_B_WIDE + PAIR_B_WIDE, tb), 0)
    mid = r - PEER_TOPK
    flat = jnp.where(r < PEER_TOPK, r,
                     jnp.where(mid < (PAIR_B_WIDE - 1) * PAIR_B_WIDE,
                               (1 + mid // PAIR_B_WIDE) * PEER_TOPK + mid % PAIR_B_WIDE,
                               (PAIR_B_WIDE + mid - (PAIR_B_WIDE - 1) * PAIR_B_WIDE) * PEER_TOPK)).astype(F32)
    half = PEER_NKEYS
    for h in range(PEER_HEADS):
        q1 = q[:, (2 * h) * half:(2 * h + 1) * half]
        q2 = q[:, (2 * h + 1) * half:(2 * h + 2) * half]
        s1 = lax.dot_general(k1, q1, nt, preferred_element_type=F32)
        s2 = lax.dot_general(k2, q2, nt, preferred_element_type=F32)
        v1, i1 = _topk_rows(s1, PEER_TOPK)
        v2, i2 = _topk_rows(s2, PEER_TOPK)
        e1 = i1 * PEER_NKEYS
        cand = jnp.concatenate(
            [v1[0:1] + v2]
            + [v1[a:a + 1] + v2[0:PAIR_B_WIDE] for a in range(1, PAIR_B_WIDE)]
            + [v1[PAIR_B_WIDE:] + v2[0:1]], axis=0)
        cid = jnp.concatenate(
            [e1[0:1] + i2]
            + [e1[a:a + 1] + i2[0:PAIR_B_WIDE] for a in range(1, PAIR_B_WIDE)]
            + [e1[PAIR_B_WIDE:] + i2[0:1]], axis=0)
        scs, eids = [], []
        for _ in range(PEER_TOPK):
            m = jnp.max(cand, axis=0, keepdims=True)
            jsel = jnp.min(jnp.where(cand == m, flat, ID_BIG), axis=0, keepdims=True)
            hit = flat == jsel
            eids.append(jnp.max(jnp.where(hit, cid, -1.0), axis=0, keepdims=True))
            scs.append(m)
            cand = jnp.where(hit, NEG_INF, cand)
        sc = jnp.concatenate(scs, axis=0)
        e = jnp.exp(sc - sc[0:1])
        gate_out[h * PEER_TOPK:(h + 1) * PEER_TOPK, :] = e / jnp.sum(e, axis=0, keepdims=True)
        eid_out[h * PEER_TOPK:(h + 1) * PEER_TOPK, :] = jnp.concatenate(eids, axis=0).astype(jnp.int32)


def _route(hn, tb, wq_bf, keys1_bf, keys2_bf):
    n = hn.shape[0]
    col = pl.BlockSpec((PEER_SEL, tb), lambda i: (0, i))
    return pl.pallas_call(
        _route_kernel,
        grid=(n // tb,),
        in_specs=[pl.BlockSpec((tb, D_MODEL), lambda i: (i, 0)),
                  _full((D_MODEL, 2 * PEER_NKEYS * PEER_HEADS)),
                  _full((PEER_NKEYS, PEER_NKEYS)), _full((PEER_NKEYS, PEER_NKEYS))],
        out_specs=[col, col],
        out_shape=[jax.ShapeDtypeStruct((PEER_SEL, n), jnp.int32),
                   jax.ShapeDtypeStruct((PEER_SEL, n), F32)],
        compiler_params=_tc_params(1),
        name="route",
    )(hn, wq_bf, keys1_bf, keys2_bf)


SC_ROWS = 32
SC_GROUP = SC_LANES
SC_TOKENS = 8
SC_CHUNKS = PEER_SEL // SC_ROWS

_SC_PARAMS = pltpu.CompilerParams(needs_layout_passes=False)


def _sc_mesh():
    return plsc.VectorSubcoreMesh(core_axis_name="c", subcore_axis_name="s",
                                  num_cores=SC_CORES, num_subcores=SC_SUBCORES)


def _sc_token_loop(n_batches, table_hbm, eid_v, rbufs, rsems, prefetch, prefetch_wait, store,
                   contract):
    def gather(slot, tl, ch, p):
        idx = eid_v.at[slot, tl, pl.ds(ch * SC_ROWS, SC_ROWS)]
        return pltpu.make_async_copy(table_hbm.at[idx], rbufs[p], rsems[p])

    prefetch(0, 0)
    prefetch_wait()
    gather(0, 0, 0, 0).start()

    @pl.loop(0, n_batches * SC_TOKENS)
    def _(i):
        b = i // SC_TOKENS
        tl = i % SC_TOKENS
        slot = b % 2
        more = b + 1 < n_batches

        @pl.when(jnp.logical_and(tl == 0, more))
        def _():
            prefetch(b + 1, 1 - slot)

        for ch in range(SC_CHUNKS):
            p = ch % 2
            if ch + 1 < SC_CHUNKS:
                gather(slot, tl, ch + 1, 1 - p).start()
            else:
                @pl.when(tl + 1 < SC_TOKENS)
                def _():
                    gather(slot, tl + 1, 0, 1 - p).start()

                @pl.when(jnp.logical_and(tl + 1 == SC_TOKENS, more))
                def _():
                    prefetch_wait()
                    gather(1 - slot, 0, 0, 1 - p).start()
            gather(slot, tl, ch, p).wait()
            contract(slot, tl, ch, rbufs[p])

        @pl.when(tl + 1 == SC_TOKENS)
        def _():
            @pl.when(b >= 1)
            def _():
                store(b - 1, 1 - slot).wait()
            store(b, slot).start()

    store(n_batches - 1, (n_batches - 1) % 2).wait()


def _sc_batches(n):
    assert n % (SC_WORKERS * SC_TOKENS) == 0, n
    return n // (SC_WORKERS * SC_TOKENS)


def _peer_hval(hn, eid, peer_u, after):
    n = hn.shape[0]
    nbw = _sc_batches(n)

    @functools.partial(
        pl.kernel, mesh=_sc_mesh(),
        out_type=jax.ShapeDtypeStruct((n // SC_TOKENS, SC_TOKENS, PEER_SEL), F32),
        scratch_types=[
            pltpu.VMEM((2, SC_TOKENS, PEER_SEL), jnp.int32),
            pltpu.VMEM((2, SC_TOKENS, D_MODEL), F32),
            pltpu.VMEM((SC_ROWS, D_MODEL), F32),
            pltpu.VMEM((SC_ROWS, D_MODEL), F32),
            pltpu.VMEM((2, SC_TOKENS, PEER_SEL), F32),
            pltpu.VMEM((SC_GROUP, SC_LANES), F32),
            pltpu.SemaphoreType.DMA, pltpu.SemaphoreType.DMA,
            pltpu.SemaphoreType.DMA, pltpu.SemaphoreType.DMA,
        ],
        compiler_params=_SC_PARAMS, name="peer_hval",
        cost_estimate=pl.CostEstimate(flops=2 * n * PEER_SEL * D_MODEL, transcendentals=0,
                                      bytes_accessed=n * PEER_SEL * D_MODEL * 4))
    def k(x_hbm, eid_hbm, u_hbm, after_hbm, o_hbm, eid_v, x_v, r0, r1, h_v, tr, sr0, sr1, spf, sout):
        del after_hbm
        blk0 = (lax.axis_index("s") * SC_CORES + lax.axis_index("c")) * nbw
        lane = lax.iota(jnp.int32, SC_LANES)
        zero = jnp.zeros((SC_LANES,), F32)

        def prefetch(b, slot):
            pltpu.async_copy(eid_hbm.at[blk0 + b], eid_v.at[slot], spf)
            pltpu.async_copy(x_hbm.at[blk0 + b], x_v.at[slot], spf)

        def prefetch_wait():
            pltpu.make_async_copy(eid_hbm.at[0], eid_v.at[0], spf).wait()
            pltpu.make_async_copy(x_hbm.at[0], x_v.at[0], spf).wait()

        def store(b, slot):
            return pltpu.make_async_copy(h_v.at[slot], o_hbm.at[blk0 + b], sout)

        def contract(slot, tl, ch, rbuf):
            for g in range(SC_ROWS // SC_GROUP):
                def body(c, accs):
                    xc = x_v[slot, tl, pl.ds(c * SC_LANES, SC_LANES)]
                    return tuple(
                        accs[r] + rbuf[g * SC_GROUP + r, pl.ds(c * SC_LANES, SC_LANES)] * xc
                        for r in range(SC_GROUP))
                accs = lax.fori_loop(0, D_MODEL // SC_LANES, body, (zero,) * SC_GROUP)
                for r in range(SC_GROUP):
                    tr[r, :] = accs[r]
                res = zero
                for jj in range(SC_LANES):
                    res = res + plsc.load_gather(tr, [lane, jnp.full((SC_LANES,), jj, jnp.int32)])
                h_v[slot, tl, pl.ds(ch * SC_ROWS + g * SC_GROUP, SC_GROUP)] = res

        _sc_token_loop(nbw, u_hbm, eid_v, (r0, r1), (sr0, sr1), prefetch, prefetch_wait, store,
                       contract)

    out = k(hn.reshape(n // SC_TOKENS, SC_TOKENS, D_MODEL),
            eid.reshape(n // SC_TOKENS, SC_TOKENS, PEER_SEL), peer_u, after)
    return out.reshape(n, PEER_SEL)


def _peer_out(h, a, eid, peer_v):
    n = h.shape[0]
    nbw = _sc_batches(n)

    @functools.partial(
        pl.kernel, mesh=_sc_mesh(),
        out_type=jax.ShapeDtypeStruct((n // SC_TOKENS, SC_TOKENS, D_MODEL), F32),
        scratch_types=[
            pltpu.VMEM((2, SC_TOKENS, PEER_SEL), jnp.int32),
            pltpu.VMEM((2, SC_TOKENS, PEER_SEL), F32),
            pltpu.VMEM((SC_ROWS, D_MODEL), F32),
            pltpu.VMEM((SC_ROWS, D_MODEL), F32),
            pltpu.VMEM((2, SC_TOKENS, D_MODEL), F32),
            pltpu.SemaphoreType.DMA, pltpu.SemaphoreType.DMA,
            pltpu.SemaphoreType.DMA, pltpu.SemaphoreType.DMA,
        ],
        compiler_params=_SC_PARAMS, name="peer_out")
    def k(h_hbm, a_hbm, eid_hbm, v_hbm, o_hbm, eid_v, a_v, r0, r1, y_v, sr0, sr1, spf, sout):
        blk0 = (lax.axis_index("s") * SC_CORES + lax.axis_index("c")) * nbw

        def prefetch(b, slot):
            pltpu.async_copy(eid_hbm.at[blk0 + b], eid_v.at[slot], spf)
            pltpu.async_copy(a_hbm.at[blk0 + b], a_v.at[slot], spf)
            pltpu.async_copy(h_hbm.at[blk0 + b], y_v.at[slot], spf)

        def prefetch_wait():
            pltpu.make_async_copy(eid_hbm.at[0], eid_v.at[0], spf).wait()
            pltpu.make_async_copy(a_hbm.at[0], a_v.at[0], spf).wait()
            pltpu.make_async_copy(h_hbm.at[0], y_v.at[0], spf).wait()

        def store(b, slot):
            return pltpu.make_async_copy(y_v.at[slot], o_hbm.at[blk0 + b], sout)

        def contract(slot, tl, ch, rbuf):
            slot_v = jnp.full((SC_LANES,), slot, jnp.int32)
            tl_v = jnp.full((SC_LANES,), tl, jnp.int32)
            for g in range(SC_ROWS // SC_GROUP):
                base = ch * SC_ROWS + g * SC_GROUP
                coef = [plsc.load_gather(a_v, [slot_v, tl_v, jnp.full((SC_LANES,), base + r, jnp.int32)])
                        for r in range(SC_GROUP)]

                @plsc.parallel_loop(0, D_MODEL // SC_LANES)
                def _(c):
                    sl = pl.ds(c * SC_LANES, SC_LANES)
                    acc = coef[0] * rbuf[g * SC_GROUP, sl]
                    for r in range(1, SC_GROUP):
                        acc = acc + coef[r] * rbuf[g * SC_GROUP + r, sl]
                    plsc.addupdate(y_v.at[slot, tl, sl], acc)

        _sc_token_loop(nbw, v_hbm, eid_v, (r0, r1), (sr0, sr1), prefetch, prefetch_wait, store,
                       contract)

    out = k(h.reshape(n // SC_TOKENS, SC_TOKENS, D_MODEL),
            a.reshape(n // SC_TOKENS, SC_TOKENS, PEER_SEL),
            eid.reshape(n // SC_TOKENS, SC_TOKENS, PEER_SEL), peer_v)
    return out.reshape(n, D_MODEL)


def _gate_kernel(hv_ref, gate_ref, a_out):
    hv = hv_ref[...]
    gelu = hv * (lax.erf(hv * (2.0 ** -0.5)) + 1.0) * 0.5
    a_out[...] = gate_ref[...] * gelu


def _gate(hval, gate, tb):
    n = hval.shape[0]
    row = pl.BlockSpec((tb, PEER_SEL), lambda i: (i, 0))
    return pl.pallas_call(
        _gate_kernel, grid=(n // tb,), in_specs=[row, row], out_specs=row,
        out_shape=jax.ShapeDtypeStruct((n, PEER_SEL), F32),
        compiler_params=_tc_params(1), name="gate",
    )(hval, gate)


VT_ROWS = D_MODEL // 2 // LANES
VT_TOKENS = 64
HI_MASK = -65536


def _pack_value_table(peer_v):
    e = peer_v.shape[0]
    bits = lax.bitcast_convert_type(peer_v.astype(BF16), jnp.uint16).astype(jnp.uint32)
    words = bits[:, :D_MODEL // 2] | (bits[:, D_MODEL // 2:] << 16)
    return lax.bitcast_convert_type(words, jnp.int32).reshape(e * VT_ROWS, LANES)


def _vside_kernel(row_s, a_s, h_ref, tab_ref, y_ref):
    tb = h_ref.shape[0]

    def token(t, carry):
        hv = h_ref[t]
        zero = jnp.zeros((VT_ROWS, LANES), F32)
        lo = [hv[0:VT_ROWS], zero]
        hi = [hv[VT_ROWS:], zero]
        for e in range(PEER_SEL):
            r0 = pl.multiple_of(row_s[t, e], VT_ROWS)
            w = tab_ref[pl.ds(r0, VT_ROWS), :]
            coef = a_s[t, e]
            c = e % 2
            lo[c] = lo[c] + coef * pltpu.bitcast(w << 16, F32)
            hi[c] = hi[c] + coef * pltpu.bitcast(w & HI_MASK, F32)
        y_ref[t] = jnp.concatenate([lo[0] + lo[1], hi[0] + hi[1]], axis=0)
        return carry

    lax.fori_loop(0, tb, token, 0)


def _peer_values(h, a, row_ids, v_packed):
    n = h.shape[0]
    tb = min(VT_TOKENS, n)
    smem = pl.BlockSpec((tb, PEER_SEL), lambda i: (i, 0), memory_space=pltpu.SMEM)
    slab = pl.BlockSpec((tb, 2 * VT_ROWS, LANES), lambda i: (i, 0, 0))
    table = pl.BlockSpec(v_packed.shape, lambda i: (0, 0), pipeline_mode=pl.Buffered(1))
    table_bytes = v_packed.shape[0] * LANES * 4
    y = pl.pallas_call(
        _vside_kernel,
        grid=(n // tb,),
        in_specs=[smem, smem, slab, table],
        out_specs=slab,
        out_shape=jax.ShapeDtypeStruct((n, 2 * VT_ROWS, LANES), F32),
        compiler_params=pltpu.CompilerParams(
            dimension_semantics=("arbitrary",),
            vmem_limit_bytes=table_bytes + 8 * 1024 * 1024),
        name="peer_values",
        cost_estimate=pl.CostEstimate(flops=2 * n * PEER_SEL * D_MODEL, transcendentals=0,
                                      bytes_accessed=table_bytes + n * PEER_SEL * D_MODEL * 2),
    )(row_ids, a, h.reshape(n, 2 * VT_ROWS, LANES), v_packed)
    return y.reshape(n, D_MODEL)


PEER_DEPTH = 2


def _chunk_plan(t, tb, first, last):
    eighth = t // 8
    if eighth % tb:
        return [t]
    sizes = [4 * eighth, 4 * eighth]
    if last:
        sizes = sizes[:1] + [2 * eighth, eighth, eighth]
    if first:
        sizes = [eighth, 3 * eighth] + sizes[1:]
    return sizes


def _peer_launch(h, hn, tb, after, wq_bf, keys1_bf, keys2_bf, peer_u):
    n = h.shape[0]
    eid_t, gate_t = _route(hn, tb, wq_bf, keys1_bf, keys2_bf)
    eid = eid_t.T
    pad = (-n) % (SC_WORKERS * SC_TOKENS)
    hn_p, eid_p = hn, eid
    if pad:
        spread = (jnp.arange(pad * PEER_SEL, dtype=jnp.int32) % peer_u.shape[0]).reshape(pad, PEER_SEL)
        hn_p = jnp.pad(hn, ((0, pad), (0, 0)))
        eid_p = jnp.concatenate([eid, spread], axis=0)
    hval = _peer_hval(hn_p, eid_p, peer_u, after)[:n]
    return h, eid, gate_t.T, hval


def _peer_finish(h, eid, gate, hval, tb, v_packed):
    a = _gate(hval, gate, min(tb, 512))
    return _peer_values(h, a, eid * VT_ROWS, v_packed)


def _rope_tables(pos):
    half = HEAD_DIM // 2
    inv = ROPE_THETA ** (-jnp.arange(half, dtype=F32) / half)
    ang = pos.astype(F32)[:, None] * inv[None, :]
    cos = jnp.cos(ang)
    sin = jnp.sin(ang)
    reps = LANES // HEAD_DIM
    cos_f = jnp.tile(jnp.concatenate([cos, cos], axis=1), (1, reps))
    sin_s = jnp.tile(jnp.concatenate([-sin, sin], axis=1), (1, reps))
    return cos_f, sin_s


def _group_sum_matrix(width):
    g = jnp.arange(width) // HEAD_DIM
    return (g[:, None] == g[None, :]).astype(BF16)


def kernel(x_prompt, x_sample, cache_k_win, cache_v_win, state_conv, meta_tokens, norm_mix_g,
           w_in, q_norm_g, k_norm_g, attn_sinks, conv_w, conv_b, conv_ln_g, conv_ln_b, w_pw2,
           out_norm_attn_g, out_norm_conv_g, w_out, norm_ffn_g, peer_w_q, peer_keys1, peer_keys2,
           peer_u, peer_v):
    assert norm_mix_g.shape[0] == 1, "single-layer model"
    b, t, _ = x_prompt.shape
    s = x_sample.shape[0]
    w_buf = cache_k_win.shape[2]
    n = b * t

    g_mix = norm_mix_g[0][None, :]
    w_in_bf = w_in[0].astype(BF16)
    qg_t = jnp.tile(q_norm_g[0], N_HEADS)[None, :]
    kg_t = jnp.tile(k_norm_g[0], N_KV_HEADS)[None, :]
    gsum_q = _group_sum_matrix(D_ATTN)
    gsum_k = _group_sum_matrix(KV_DIM)
    sinks = attn_sinks[0]
    cw, cb = conv_w[0], conv_b[0][None, :]
    lg, lb = conv_ln_g[0][None, :], conv_ln_b[0][None, :]
    w2_bf = w_pw2[0].astype(BF16)
    g_a, g_c = out_norm_attn_g[0][None, :], out_norm_conv_g[0][None, :]
    w_out_a = w_out[0][:D_ATTN].astype(BF16)
    w_out_c = w_out[0][D_ATTN:].astype(BF16)
    g_f = norm_ffn_g[0][None, :]
    wq_bf = peer_w_q[0].astype(BF16)
    k1_bf = peer_keys1[0].astype(BF16)
    k2_bf = peer_keys2[0].astype(BF16)
    pu, vp = peer_u[0], _pack_value_table(peer_v[0])
    proj = functools.partial(_project, norm_g=g_mix, w_in_bf=w_in_bf, qg_t=qg_t, kg_t=kg_t,
                             gsum_q=gsum_q, gsum_k=gsum_k)
    conv = functools.partial(_conv_branch, conv_w=cw, conv_b=cb, ln_g=lg, ln_b=lb, w_pw2_bf=w2_bf)
    merge = functools.partial(_merge, g_a=g_a, g_c=g_c, w_out_a=w_out_a, w_out_c=w_out_c, g_f=g_f)

    tb = 512
    no_dep = jnp.zeros((SC_TOKENS, LANES), F32)
    tab_x = _rope_tables(N_META + jnp.arange(t, dtype=jnp.int32))
    tab_m = _rope_tables(jnp.arange(N_META, dtype=jnp.int32))
    _, k_m, v_m, u_m = proj(meta_tokens, tab_m, 1, N_META, no_dep)
    halo0 = jnp.concatenate([jnp.zeros((CONV_HALO - N_META, D_CONV), F32), u_m], axis=0)

    k_meta = jnp.concatenate([jnp.zeros((META_PAD, KV_DIM), F32), k_m], axis=0)
    v_meta = jnp.concatenate([jnp.zeros((META_PAD, KV_DIM), F32), v_m], axis=0)

    xp = x_prompt.reshape(b * t, D_MODEL)

    def dense(bi, after):
        q, k, v, u = proj(xp, tab_x, t // tb, tb, after, row0=bi * t, nrows=t)
        o_attn = _prompt_attention(sinks, q, k, v, k_meta, v_meta)
        u3 = u.reshape(t // tb, tb, D_CONV)
        halo = jnp.concatenate([halo0[None], u3[:-1, tb - CONV_HALO:]], axis=0)
        o_conv = conv(halo, u, tb)
        hs, r0 = [], 0
        for rows in _chunk_plan(t, tb, bi == 0, bi == b - 1):
            hs.append(merge(xp, o_attn, o_conv, tb, x_row0=bi * t + r0, row0=r0, nrows=rows))
            r0 += rows
        return hs, k, v, u

    launched, ys, kvu = [], [], []
    cur = dense(0, no_dep)
    for bi in range(b):
        hs, k, v, u = cur
        kvu.append((k, v, u))
        for h, hn in hs:
            c = len(launched)
            if c >= PEER_DEPTH:
                ys.append(_peer_finish(*launched[c - PEER_DEPTH], 256, vp))
            after = ys[c - PEER_DEPTH][:SC_TOKENS] if c >= PEER_DEPTH else jnp.zeros((SC_TOKENS, D_MODEL), F32)
            launched.append(_peer_launch(h, hn, 256, after, wq_bf, k1_bf, k2_bf, pu))
        if bi + 1 < b:
            cur = dense(bi + 1, launched[-1][1][:SC_TOKENS, :LANES].astype(F32))
    for c in range(len(ys), len(launched)):
        ys.append(_peer_finish(*launched[c], 256, vp))
    y_prompt = jnp.concatenate(ys, axis=0).reshape(b, t, D_MODEL)

    new_k_prompt = jnp.stack([k[t - WINDOW:] for k, _, _ in kvu]).reshape(1, b, WINDOW, N_KV_HEADS, HEAD_DIM)
    new_v_prompt = jnp.stack([v[t - WINDOW:] for _, v, _ in kvu]).reshape(1, b, WINDOW, N_KV_HEADS, HEAD_DIM)
    new_conv_prompt = jnp.stack([u[t - (CONV_WIDTH - 1):] for _, _, u in kvu])[None]

    xs = x_sample.reshape(s, D_MODEL)
    tab_s = _rope_tables(jnp.full((s,), PAST_LEN, jnp.int32))
    qs, ks, vs, us = proj(xs, tab_s, 1, s, no_dep)
    ck = cache_k_win[0]
    cv = cache_v_win[0]
    o_attn_s = _decode_attention(
        sinks,
        qs.reshape(s, N_HEADS, HEAD_DIM), ks, vs,
        ck.reshape(s, w_buf, KV_DIM), cv.reshape(s, w_buf, KV_DIM), 32)
    o_attn_s = o_attn_s.reshape(s, D_ATTN)
    cs = state_conv[0]
    hist = jnp.concatenate([jnp.zeros((s, CONV_HALO - (CONV_WIDTH - 1), D_CONV), F32), cs], axis=1)
    us_blk = jnp.concatenate([us[:, None, :], jnp.zeros((s, 7, D_CONV), F32)], axis=1)
    o_conv_s = conv(hist, us_blk.reshape(s * 8, D_CONV), 8).reshape(s, 8, D_CONV)[:, 0]
    hs, hns = merge(xs, o_attn_s, o_conv_s, s)
    launch_s = _peer_launch(hs, hns, s, jnp.zeros((SC_TOKENS, D_MODEL), F32), wq_bf, k1_bf, k2_bf, pu)
    y_sample = _peer_finish(*launch_s, s, vp).reshape(s, 1, D_MODEL)

    new_k_sample = jnp.concatenate([ck[:, 1:], ks.reshape(s, 1, N_KV_HEADS, HEAD_DIM)], axis=1)[None]
    new_v_sample = jnp.concatenate([cv[:, 1:], vs.reshape(s, 1, N_KV_HEADS, HEAD_DIM)], axis=1)[None]
    new_conv_sample = jnp.concatenate([cs[:, 1:], us[:, None, :]], axis=1)[None]
    if w_buf != WINDOW:
        raise NotImplementedError("cache window shorter than the attention window")

    return (y_prompt, y_sample, new_k_prompt, new_v_prompt, new_conv_prompt,
            new_k_sample, new_v_sample, new_conv_sample)
```

```python
import functools

import jax
import jax.numpy as jnp
from jax import lax
from jax.experimental import pallas as pl
from jax.experimental.pallas import tpu as pltpu
from jax.experimental.pallas import tpu_sc as plsc

D_MODEL = 1024
HEAD_DIM = 64
D_ATTN = 512
N_HEADS = 8
N_KV_HEADS = 2
KV_DIM = N_KV_HEADS * HEAD_DIM
D_CONV = 512
D_IN = D_ATTN + 2 * KV_DIM + 2 * D_CONV
CONV_WIDTH = 31
WINDOW = 128
BLOCK = 128
ROPE_THETA = 10000.0
N_META = 16
META_PAD = BLOCK - N_META
PEER_HEADS = 8
PEER_NKEYS = 128
PEER_TOPK = 16
PEER_SEL = PEER_HEADS * PEER_TOPK
EPS = 1e-6
PAST_LEN = 16384

LANES = 128
SC_CORES = 2
SC_SUBCORES = 16
SC_LANES = 16
SC_WORKERS = SC_CORES * SC_SUBCORES
VMEM_LIMIT = 48 * 1024 * 1024
SUBLANES = 8

TB_DENSE = 512
TB_ROUTE = 256
SB_DECODE = 32

F32 = jnp.float32
BF16 = jnp.bfloat16
NEG_INF = float("-inf")


def _tc_params(n_axes):
    return pltpu.CompilerParams(dimension_semantics=("arbitrary",) * n_axes,
                                vmem_limit_bytes=VMEM_LIMIT)


def _full(shape):
    nd = len(shape)
    return pl.BlockSpec(shape, lambda *_: (0,) * nd)


def _group_mean(sq, gsum_ref):
    hi = sq.astype(BF16)
    lo = (sq - hi.astype(F32)).astype(BF16)
    g = gsum_ref[...]
    s = jnp.dot(hi, g, preferred_element_type=F32) + jnp.dot(lo, g, preferred_element_type=F32)
    return s * (1.0 / HEAD_DIM)


def _rope(xn, cos_f, sin_s, first_half):
    outs = []
    for s in range(xn.shape[1] // LANES):
        xs = xn[:, s * LANES:(s + 1) * LANES]
        partner = jnp.where(first_half, pltpu.roll(xs, LANES - HEAD_DIM // 2, axis=1),
                            pltpu.roll(xs, HEAD_DIM // 2, axis=1))
        outs.append(xs * cos_f + partner * sin_s)
    return outs[0] if len(outs) == 1 else jnp.concatenate(outs, axis=1)


def _proj_kernel(x_ref, g_ref, w_ref, qg_ref, kg_ref, cos_ref, sin_ref, gq_ref, gk_ref, after_ref,
                 q_out, k_out, v_out, u_out):
    del after_ref
    x = x_ref[...]
    n = x * lax.rsqrt(jnp.mean(x * x, axis=-1, keepdims=True) + EPS) * g_ref[...]
    p = jnp.dot(n.astype(BF16), w_ref[...], preferred_element_type=F32)
    q = p[:, :D_ATTN]
    k = p[:, D_ATTN:D_ATTN + KV_DIM]
    v = p[:, D_ATTN + KV_DIM:D_ATTN + 2 * KV_DIM]
    ga = p[:, D_ATTN + 2 * KV_DIM:D_ATTN + 2 * KV_DIM + D_CONV]
    gb = p[:, D_ATTN + 2 * KV_DIM + D_CONV:]
    cos_f = cos_ref[...]
    sin_s = sin_ref[...]
    lane = lax.broadcasted_iota(jnp.int32, (x.shape[0], LANES), 1)
    first_half = (lane % HEAD_DIM) < (HEAD_DIM // 2)
    qn = q * lax.rsqrt(_group_mean(q * q, gq_ref) + EPS) * qg_ref[...]
    kn = k * lax.rsqrt(_group_mean(k * k, gk_ref) + EPS) * kg_ref[...]
    q_out[...] = _rope(qn, cos_f, sin_s, first_half)
    k_out[...] = _rope(kn, cos_f, sin_s, first_half)
    v_out[...] = v
    u_out[...] = ga * (1.0 / (1.0 + jnp.exp(-gb)))


def _project(x, pos_tables, n_table_blocks, tb, after, norm_g, w_in_bf, qg_t, kg_t, gsum_q, gsum_k,
             row0=0, nrows=None):
    n = x.shape[0] if nrows is None else nrows
    cos_t, sin_t = pos_tables
    nb = n // tb
    blk0 = row0 // tb
    tab_spec = pl.BlockSpec((tb, LANES), lambda i: (i % n_table_blocks, 0))
    row = lambda w: pl.BlockSpec((tb, w), lambda i: (i, 0))
    return pl.pallas_call(
        _proj_kernel,
        grid=(nb,),
        in_specs=[pl.BlockSpec((tb, D_MODEL), lambda i: (i + blk0, 0)),
                  _full((1, D_MODEL)), _full((D_MODEL, D_IN)),
                  _full((1, D_ATTN)), _full((1, KV_DIM)), tab_spec, tab_spec,
                  _full((D_ATTN, D_ATTN)), _full((KV_DIM, KV_DIM)),
                  pl.BlockSpec(memory_space=pl.ANY)],
        out_specs=[row(D_ATTN), row(KV_DIM), row(KV_DIM), row(D_CONV)],
        out_shape=[jax.ShapeDtypeStruct((n, D_ATTN), F32), jax.ShapeDtypeStruct((n, KV_DIM), F32),
                   jax.ShapeDtypeStruct((n, KV_DIM), F32), jax.ShapeDtypeStruct((n, D_CONV), F32)],
        compiler_params=_tc_params(1),
        name="proj",
    )(x, norm_g, w_in_bf, qg_t, kg_t, cos_t, sin_t, gsum_q, gsum_k, after)


def _attn_kernel(sink_ref, q_ref, kp_ref, kc_ref, vp_ref, vc_ref, km_ref, vm_ref, o_ref):
    j = pl.program_id(0)
    r = lax.broadcasted_iota(jnp.int32, (BLOCK, 2 * BLOCK), 0)
    c = lax.broadcasted_iota(jnp.int32, (BLOCK, 2 * BLOCK), 1)
    ok = (c > r) & (c <= r + WINDOW) & ((j > 0) | (c >= META_PAD))
    grp = N_HEADS // N_KV_HEADS
    ok = jnp.concatenate([ok] * grp, axis=0)
    first = j == 0
    k_all = jnp.concatenate([jnp.where(first, km_ref[...], kp_ref[...]), kc_ref[...]], axis=0)
    v_all = jnp.concatenate([jnp.where(first, vm_ref[...], vp_ref[...]), vc_ref[...]], axis=0)
    q = q_ref[...]
    outs = []
    for g in range(N_KV_HEADS):
        lanes = slice(g * HEAD_DIM, (g + 1) * HEAD_DIM)
        k = k_all[:, lanes].astype(BF16)
        v = v_all[:, lanes].astype(BF16)
        q4 = jnp.concatenate([q[:, (g * grp + i) * HEAD_DIM:(g * grp + i + 1) * HEAD_DIM]
                              for i in range(grp)], axis=0).astype(BF16)
        s = lax.dot_general(q4, k, (((1,), (1,)), ((), ())), preferred_element_type=F32)
        s = jnp.where(ok, s * (HEAD_DIM ** -0.5), NEG_INF)
        sink = jnp.concatenate(
            [jnp.full((BLOCK, 1), sink_ref[g * grp + i], F32) for i in range(grp)], axis=0)
        m = jnp.maximum(jnp.max(s, axis=1, keepdims=True), sink)
        p = jnp.exp(s - m)
        den = jnp.sum(p, axis=1, keepdims=True) + jnp.exp(sink - m)
        o = jnp.dot(p.astype(BF16), v, preferred_element_type=F32) / den
        outs += [o[i * BLOCK:(i + 1) * BLOCK] for i in range(grp)]
    o_ref[...] = jnp.concatenate(outs, axis=1)


def _prompt_attention(sinks, q, k, v, k_meta, v_meta):
    t = q.shape[0]
    kv_prev = pl.BlockSpec((BLOCK, KV_DIM), lambda j: (jnp.maximum(j - 1, 0), 0))
    kv_cur = pl.BlockSpec((BLOCK, KV_DIM), lambda j: (j, 0))
    q_spec = pl.BlockSpec((BLOCK, D_ATTN), lambda j: (j, 0))
    meta = _full((BLOCK, KV_DIM))
    return pl.pallas_call(
        _attn_kernel,
        grid=(t // BLOCK,),
        in_specs=[pl.BlockSpec(memory_space=pltpu.SMEM), q_spec, kv_prev, kv_cur, kv_prev, kv_cur,
                  meta, meta],
        out_specs=q_spec,
        out_shape=jax.ShapeDtypeStruct((t, D_ATTN), F32),
        compiler_params=_tc_params(1),
        name="attn",
    )(sinks, q, k, k, v, v, k_meta, v_meta)


def _dec_attn_kernel(sink_ref, q_ref, kn_ref, vn_ref, ck_ref, cv_ref, o_ref):
    grp = N_HEADS // N_KV_HEADS
    sb, w_buf = ck_ref.shape[0], ck_ref.shape[1]
    q = q_ref[...]
    qb = q.astype(BF16)
    head = lax.broadcasted_iota(jnp.int32, (sb, N_HEADS, 1), 1)
    in_g0 = head < grp
    kv_lanes = [slice(g * HEAD_DIM, (g + 1) * HEAD_DIM) for g in range(N_KV_HEADS)]
    ck = ck_ref[...]
    cv = cv_ref[...]
    s_g = [jnp.einsum("shd,swd->shw", qb, ck[:, :, ln].astype(BF16), preferred_element_type=F32)
           for ln in kv_lanes]
    s = jnp.where(in_g0, s_g[0], s_g[1]) * (HEAD_DIM ** -0.5)
    key_ok = lax.broadcasted_iota(jnp.int32, (sb, N_HEADS, w_buf), 2) >= 1
    s = jnp.where(key_ok, s, NEG_INF)
    rnd = lambda a: a.astype(BF16).astype(F32)
    kn_all, vn_all = kn_ref[...], vn_ref[...]
    kn = jnp.where(in_g0, kn_all[:, None, kv_lanes[0]], kn_all[:, None, kv_lanes[1]])
    vn = jnp.where(in_g0, vn_all[:, None, kv_lanes[0]], vn_all[:, None, kv_lanes[1]])
    s_self = jnp.sum(rnd(q) * rnd(kn), axis=-1, keepdims=True) * (HEAD_DIM ** -0.5)
    sink = sink_ref[...]
    m = jnp.maximum(jnp.maximum(jnp.max(s, axis=-1, keepdims=True), s_self), sink)
    p = jnp.exp(s - m)
    p_self = jnp.exp(s_self - m)
    den = jnp.sum(p, axis=-1, keepdims=True) + p_self + jnp.exp(sink - m)
    pb = p.astype(BF16)
    o_g = [jnp.einsum("shw,swd->shd", pb, cv[:, :, ln].astype(BF16), preferred_element_type=F32)
           for ln in kv_lanes]
    o = jnp.where(in_g0, o_g[0], o_g[1]) + rnd(p_self) * rnd(vn)
    o_ref[...] = o / den


def _decode_attention(sinks, q3, kn_t, vn_t, ck_t, cv_t, sb):
    s = q3.shape[0]
    w_buf = ck_t.shape[1]
    qs = pl.BlockSpec((sb, N_HEADS, HEAD_DIM), lambda i: (i, 0, 0))
    ns = pl.BlockSpec((sb, KV_DIM), lambda i: (i, 0))
    cs = pl.BlockSpec((sb, w_buf, KV_DIM), lambda i: (i, 0, 0))
    return pl.pallas_call(
        _dec_attn_kernel,
        grid=(s // sb,),
        in_specs=[_full((1, N_HEADS, 1)), qs, ns, ns, cs, cs],
        out_specs=qs,
        out_shape=jax.ShapeDtypeStruct(q3.shape, F32),
        compiler_params=_tc_params(1),
        name="dec_attn",
    )(sinks.reshape(1, N_HEADS, 1), q3, kn_t, vn_t, ck_t, cv_t)


CONV_HALO = 32
CONV_ROWS = 64


def _conv_kernel(halo_ref, u_ref, cw_ref, cb_ref, lg_ref, lb_ref, w2_ref, o_ref, ucat):
    tb = u_ref.shape[0]
    ucat[0:CONV_HALO, :] = halo_ref[0]
    ucat[CONV_HALO:, :] = u_ref[...]
    first = CONV_HALO - (CONV_WIDTH - 1)
    rows = min(CONV_ROWS, tb)
    for r0 in range(0, tb, rows):
        acc = jnp.zeros((rows, D_CONV), F32)
        for j in range(CONV_WIDTH):
            acc = acc + ucat[r0 + first + j:r0 + first + j + rows, :] * cw_ref[j:j + 1, :]
        y = acc + cb_ref[...]
        yc = y - jnp.mean(y, axis=-1, keepdims=True)
        yn = yc * lax.rsqrt(jnp.mean(yc * yc, axis=-1, keepdims=True) + EPS)
        yn = yn * lg_ref[...] + lb_ref[...]
        act = yn * (1.0 / (1.0 + jnp.exp(-yn)))
        o_ref[r0:r0 + rows, :] = jnp.dot(act.astype(BF16), w2_ref[...],
                                              preferred_element_type=F32)


def _conv_branch(halo, u, tb, conv_w, conv_b, ln_g, ln_b, w_pw2_bf):
    n = u.shape[0]
    return pl.pallas_call(
        _conv_kernel,
        grid=(n // tb,),
        in_specs=[pl.BlockSpec((1, CONV_HALO, D_CONV), lambda i: (i, 0, 0)),
                  pl.BlockSpec((tb, D_CONV), lambda i: (i, 0)),
                  _full((CONV_WIDTH, D_CONV)), _full((1, D_CONV)), _full((1, D_CONV)),
                  _full((1, D_CONV)), _full((D_CONV, D_CONV))],
        out_specs=pl.BlockSpec((tb, D_CONV), lambda i: (i, 0)),
        out_shape=jax.ShapeDtypeStruct((n, D_CONV), F32),
        scratch_shapes=[pltpu.VMEM((tb + CONV_HALO, D_CONV), F32)],
        compiler_params=_tc_params(1),
        name="conv",
    )(halo, u, conv_w, conv_b, ln_g, ln_b, w_pw2_bf)


def _rms(x, g):
    return x * lax.rsqrt(jnp.mean(x * x, axis=-1, keepdims=True) + EPS) * g


def _merge_kernel(x_ref, oa_ref, oc_ref, ga_ref, gc_ref, wa_ref, wc_ref, gf_ref, h_out, hn_out):
    a = _rms(oa_ref[...], ga_ref[...]).astype(BF16)
    c = _rms(oc_ref[...], gc_ref[...]).astype(BF16)
    h = x_ref[...] + (jnp.dot(a, wa_ref[...], preferred_element_type=F32)
                      + jnp.dot(c, wc_ref[...], preferred_element_type=F32))
    h_out[...] = h
    hn_out[...] = _rms(h, gf_ref[...])


def _merge(x, oa, oc, tb, g_a, g_c, w_out_a, w_out_c, g_f, x_row0=0, row0=0, nrows=None):
    n = oa.shape[0] if nrows is None else nrows
    xb0, b0 = x_row0 // tb, row0 // tb
    row = lambda w: pl.BlockSpec((tb, w), lambda i: (i, 0))
    src = lambda w: pl.BlockSpec((tb, w), lambda i: (i + b0, 0))
    return pl.pallas_call(
        _merge_kernel,
        grid=(n // tb,),
        in_specs=[pl.BlockSpec((tb, D_MODEL), lambda i: (i + xb0, 0)), src(D_ATTN), src(D_CONV),
                  _full((1, D_ATTN)), _full((1, D_CONV)),
                  _full((D_ATTN, D_MODEL)), _full((D_CONV, D_MODEL)), _full((1, D_MODEL))],
        out_specs=[row(D_MODEL), row(D_MODEL)],
        out_shape=[jax.ShapeDtypeStruct((n, D_MODEL), F32)] * 2,
        compiler_params=_tc_params(1),
        name="merge",
    )(x, oa, oc, g_a, g_c, w_out_a, w_out_c, g_f)


ID_BIG = 1e9


def _topk_rows(s, k):
    rows = lax.broadcasted_iota(jnp.int32, s.shape, 0).astype(F32)
    vals, idxs = [], []
    for _ in range(k):
        m = jnp.max(s, axis=0, keepdims=True)
        idx = jnp.min(jnp.where(s == m, rows, ID_BIG), axis=0, keepdims=True)
        vals.append(m)
        idxs.append(idx)
        s = jnp.where(rows == idx, NEG_INF, s)
    return jnp.concatenate(vals, axis=0), jnp.concatenate(idxs, axis=0)


PAIR_B_WIDE = 8


def _route_kernel(hn_ref, wq_ref, k1_ref, k2_ref, eid_out, gate_out):
    tb = hn_ref.shape[0]
    q = jnp.dot(hn_ref[...].astype(BF16), wq_ref[...], preferred_element_type=F32).astype(BF16)
    k1 = k1_ref[...]
    k2 = k2_ref[...]
    nt = (((1,), (1,)), ((), ()))
    r = lax.broadcasted_iota(jnp.int32, (PEER_TOPK + (PAIR_B_WIDE - 1) * PAIR_B_WIDE + PAIR_B_WIDE, tb), 0)
    mid = r - PEER_TOPK
    flat = jnp.where(r < PEER_TOPK, r,
                     jnp.where(mid < (PAIR_B_WIDE - 1) * PAIR_B_WIDE,
                               (1 + mid // PAIR_B_WIDE) * PEER_TOPK + mid % PAIR_B_WIDE,
                               (PAIR_B_WIDE + mid - (PAIR_B_WIDE - 1) * PAIR_B_WIDE) * PEER_TOPK)).astype(F32)
    half = PEER_NKEYS
    for h in range(PEER_HEADS):
        q1 = q[:, (2 * h) * half:(2 * h + 1) * half]
        q2 = q[:, (2 * h + 1) * half:(2 * h + 2) * half]
        s1 = lax.dot_general(k1, q1, nt, preferred_element_type=F32)
        s2 = lax.dot_general(k2, q2, nt, preferred_element_type=F32)
        v1, i1 = _topk_rows(s1, PEER_TOPK)
        v2, i2 = _topk_rows(s2, PEER_TOPK)
        e1 = i1 * PEER_NKEYS
        cand = jnp.concatenate(
            [v1[0:1] + v2]
            + [v1[a:a + 1] + v2[0:PAIR_B_WIDE] for a in range(1, PAIR_B_WIDE)]
            + [v1[PAIR_B_WIDE:] + v2[0:1]], axis=0)
        cid = jnp.concatenate(
            [e1[0:1] + i2]
            + [e1[a:a + 1] + i2[0:PAIR_B_WIDE] for a in range(1, PAIR_B_WIDE)]
            + [e1[PAIR_B_WIDE:] + i2[0:1]], axis=0)
        scs, eids = [], []
        for _ in range(PEER_TOPK):
            m = jnp.max(cand, axis=0, keepdims=True)
            jsel = jnp.min(jnp.where(cand == m, flat, ID_BIG), axis=0, keepdims=True)
            hit = flat == jsel
            eids.append(jnp.max(jnp.where(hit, cid, -1.0), axis=0, keepdims=True))
            scs.append(m)
            cand = jnp.where(hit, NEG_INF, cand)
        sc = jnp.concatenate(scs, axis=0)
        e = jnp.exp(sc - sc[0:1])
        gate_out[h * PEER_TOPK:(h + 1) * PEER_TOPK, :] = e / jnp.sum(e, axis=0, keepdims=True)
        eid_out[h * PEER_TOPK:(h + 1) * PEER_TOPK, :] = jnp.concatenate(eids, axis=0).astype(jnp.int32)


def _route(hn, tb, wq_bf, keys1_bf, keys2_bf):
    n = hn.shape[0]
    col = pl.BlockSpec((PEER_SEL, tb), lambda i: (0, i))
    return pl.pallas_call(
        _route_kernel,
        grid=(n // tb,),
        in_specs=[pl.BlockSpec((tb, D_MODEL), lambda i: (i, 0)),
                  _full((D_MODEL, 2 * PEER_NKEYS * PEER_HEADS)),
                  _full((PEER_NKEYS, PEER_NKEYS)), _full((PEER_NKEYS, PEER_NKEYS))],
        out_specs=[col, col],
        out_shape=[jax.ShapeDtypeStruct((PEER_SEL, n), jnp.int32),
                   jax.ShapeDtypeStruct((PEER_SEL, n), F32)],
        compiler_params=_tc_params(1),
        name="route",
    )(hn, wq_bf, keys1_bf, keys2_bf)


SC_ROWS = 32
SC_GROUP = SC_LANES
SC_TOKENS = 8
SC_CHUNKS = PEER_SEL // SC_ROWS

_SC_PARAMS = pltpu.CompilerParams(needs_layout_passes=False)


def _sc_mesh():
    return plsc.VectorSubcoreMesh(core_axis_name="c", subcore_axis_name="s",
                                  num_cores=SC_CORES, num_subcores=SC_SUBCORES)


def _sc_token_loop(n_batches, table_hbm, eid_v, rbufs, rsems, prefetch, prefetch_wait, store,
                   contract):
    def gather(slot, tl, ch, p):
        idx = eid_v.at[slot, tl, pl.ds(ch * SC_ROWS, SC_ROWS)]
        return pltpu.make_async_copy(table_hbm.at[idx], rbufs[p], rsems[p])

    prefetch(0, 0)
    prefetch_wait()
    gather(0, 0, 0, 0).start()

    @pl.loop(0, n_batches * SC_TOKENS)
    def _(i):
        b = i // SC_TOKENS
        tl = i % SC_TOKENS
        slot = b % 2
        more = b + 1 < n_batches

        @pl.when(jnp.logical_and(tl == 0, more))
        def _():
            prefetch(b + 1, 1 - slot)

        for ch in range(SC_CHUNKS):
            p = ch % 2
            if ch + 1 < SC_CHUNKS:
                gather(slot, tl, ch + 1, 1 - p).start()
            else:
                @pl.when(tl + 1 < SC_TOKENS)
                def _():
                    gather(slot, tl + 1, 0, 1 - p).start()

                @pl.when(jnp.logical_and(tl + 1 == SC_TOKENS, more))
                def _():
                    prefetch_wait()
                    gather(1 - slot, 0, 0, 1 - p).start()
            gather(slot, tl, ch, p).wait()
            contract(slot, tl, ch, rbufs[p])

        @pl.when(tl + 1 == SC_TOKENS)
        def _():
            @pl.when(b >= 1)
            def _():
                store(b - 1, 1 - slot).wait()
            store(b, slot).start()

    store(n_batches - 1, (n_batches - 1) % 2).wait()


def _sc_batches(n):
    assert n % (SC_WORKERS * SC_TOKENS) == 0, n
    return n // (SC_WORKERS * SC_TOKENS)


def _peer_hval(hn, eid, peer_u, after):
    n = hn.shape[0]
    nbw = _sc_batches(n)

    @functools.partial(
        pl.kernel, mesh=_sc_mesh(),
        out_type=jax.ShapeDtypeStruct((n // SC_TOKENS, SC_TOKENS, PEER_SEL), F32),
        scratch_types=[
            pltpu.VMEM((2, SC_TOKENS, PEER_SEL), jnp.int32),
            pltpu.VMEM((2, SC_TOKENS, D_MODEL), F32),
            pltpu.VMEM((SC_ROWS, D_MODEL), F32),
            pltpu.VMEM((SC_ROWS, D_MODEL), F32),
            pltpu.VMEM((2, SC_TOKENS, PEER_SEL), F32),
            pltpu.VMEM((SC_GROUP, SC_LANES), F32),
            pltpu.SemaphoreType.DMA, pltpu.SemaphoreType.DMA,
            pltpu.SemaphoreType.DMA, pltpu.SemaphoreType.DMA,
        ],
        compiler_params=_SC_PARAMS, name="peer_hval",
        cost_estimate=pl.CostEstimate(flops=2 * n * PEER_SEL * D_MODEL, transcendentals=0,
                                      bytes_accessed=n * PEER_SEL * D_MODEL * 4))
    def k(x_hbm, eid_hbm, u_hbm, after_hbm, o_hbm, eid_v, x_v, r0, r1, h_v, tr, sr0, sr1, spf, sout):
        del after_hbm
        blk0 = (lax.axis_index("s") * SC_CORES + lax.axis_index("c")) * nbw
        lane = lax.iota(jnp.int32, SC_LANES)
        zero = jnp.zeros((SC_LANES,), F32)

        def prefetch(b, slot):
            pltpu.async_copy(eid_hbm.at[blk0 + b], eid_v.at[slot], spf)
            pltpu.async_copy(x_hbm.at[blk0 + b], x_v.at[slot], spf)

        def prefetch_wait():
            pltpu.make_async_copy(eid_hbm.at[0], eid_v.at[0], spf).wait()
            pltpu.make_async_copy(x_hbm.at[0], x_v.at[0], spf).wait()

        def store(b, slot):
            return pltpu.make_async_copy(h_v.at[slot], o_hbm.at[blk0 + b], sout)

        def contract(slot, tl, ch, rbuf):
            for g in range(SC_ROWS // SC_GROUP):
                def body(c, accs):
                    xc = x_v[slot, tl, pl.ds(c * SC_LANES, SC_LANES)]
                    return tuple(
                        accs[r] + rbuf[g * SC_GROUP + r, pl.ds(c * SC_LANES, SC_LANES)] * xc
                        for r in range(SC_GROUP))
                accs = lax.fori_loop(0, D_MODEL // SC_LANES, body, (zero,) * SC_GROUP)
                for r in range(SC_GROUP):
                    tr[r, :] = accs[r]
                res = zero
                for jj in range(SC_LANES):
                    res = res + plsc.load_gather(tr, [lane, jnp.full((SC_LANES,), jj, jnp.int32)])
                h_v[slot, tl, pl.ds(ch * SC_ROWS + g * SC_GROUP, SC_GROUP)] = res

        _sc_token_loop(nbw, u_hbm, eid_v, (r0, r1), (sr0, sr1), prefetch, prefetch_wait, store,
                       contract)

    out = k(hn.reshape(n // SC_TOKENS, SC_TOKENS, D_MODEL),
            eid.reshape(n // SC_TOKENS, SC_TOKENS, PEER_SEL), peer_u, after)
    return out.reshape(n, PEER_SEL)


def _gate_kernel(hv_ref, gate_ref, a_out):
    hv = hv_ref[...]
    gelu = hv * (lax.erf(hv * (2.0 ** -0.5)) + 1.0) * 0.5
    a_out[...] = gate_ref[...] * gelu


def _gate(hval, gate, tb):
    n = hval.shape[0]
    row = pl.BlockSpec((tb, PEER_SEL), lambda i: (i, 0))
    return pl.pallas_call(
        _gate_kernel, grid=(n // tb,), in_specs=[row, row], out_specs=row,
        out_shape=jax.ShapeDtypeStruct((n, PEER_SEL), F32),
        compiler_params=_tc_params(1), name="gate",
    )(hval, gate)


VT_ROWS = D_MODEL // 2 // LANES
VT_TOKENS = 64
HI_MASK = -65536


def _pack_value_table(peer_v):
    e = peer_v.shape[0]
    bits = lax.bitcast_convert_type(peer_v.astype(BF16), jnp.uint16).astype(jnp.uint32)
    words = bits[:, :D_MODEL // 2] | (bits[:, D_MODEL // 2:] << 16)
    return lax.bitcast_convert_type(words, jnp.int32).reshape(e * VT_ROWS, LANES)


def _vside_kernel(row_s, a_s, h_ref, tab_ref, y_ref):
    tb = h_ref.shape[0]

    def token(t, carry):
        zero = jnp.zeros((VT_ROWS, LANES), F32)
        lo = [zero, zero]
        hi = [zero, zero]
        for e in range(PEER_SEL):
            r0 = pl.multiple_of(row_s[t, e], VT_ROWS)
            w = tab_ref[pl.ds(r0, VT_ROWS), :]
            coef = a_s[t, e]
            c = e % 2
            lo[c] = lo[c] + coef * pltpu.bitcast(w << 16, F32)
            hi[c] = hi[c] + coef * pltpu.bitcast(w & HI_MASK, F32)
        y_ref[t] = h_ref[t] + jnp.concatenate([lo[0] + lo[1], hi[0] + hi[1]], axis=0)
        return carry

    lax.fori_loop(0, tb, token, 0)


def _peer_values(h, a, row_ids, v_packed):
    n = h.shape[0]
    tb = min(VT_TOKENS, n)
    smem = pl.BlockSpec((tb, PEER_SEL), lambda i: (i, 0), memory_space=pltpu.SMEM)
    slab = pl.BlockSpec((tb, 2 * VT_ROWS, LANES), lambda i: (i, 0, 0))
    table = pl.BlockSpec(v_packed.shape, lambda i: (0, 0), pipeline_mode=pl.Buffered(1))
    table_bytes = v_packed.shape[0] * LANES * 4
    y = pl.pallas_call(
        _vside_kernel,
        grid=(n // tb,),
        in_specs=[smem, smem, slab, table],
        out_specs=slab,
        out_shape=jax.ShapeDtypeStruct((n, 2 * VT_ROWS, LANES), F32),
        compiler_params=pltpu.CompilerParams(
            dimension_semantics=("arbitrary",),
            vmem_limit_bytes=table_bytes + 8 * 1024 * 1024),
        name="peer_values",
        cost_estimate=pl.CostEstimate(flops=2 * n * PEER_SEL * D_MODEL, transcendentals=0,
                                      bytes_accessed=table_bytes + n * PEER_SEL * D_MODEL * 2),
    )(row_ids, a, h.reshape(n, 2 * VT_ROWS, LANES), v_packed)
    return y.reshape(n, D_MODEL)


PEER_DEPTH = 2


def _chunk_plan(t, tb, first, last):
    eighth = t // 8
    if eighth % tb:
        return [t]
    sizes = [4 * eighth, 4 * eighth]
    if last:
        sizes = sizes[:1] + [2 * eighth, eighth, eighth]
    if first:
        sizes = [eighth, 3 * eighth] + sizes[1:]
    return sizes


def _peer_launch(h, hn, tb, after, wq_bf, keys1_bf, keys2_bf, peer_u):
    n = h.shape[0]
    eid_t, gate_t = _route(hn, tb, wq_bf, keys1_bf, keys2_bf)
    eid = eid_t.T
    pad = (-n) % (SC_WORKERS * SC_TOKENS)
    hn_p, eid_p = hn, eid
    if pad:
        spread = (jnp.arange(pad * PEER_SEL, dtype=jnp.int32) % peer_u.shape[0]).reshape(pad, PEER_SEL)
        hn_p = jnp.pad(hn, ((0, pad), (0, 0)))
        eid_p = jnp.concatenate([eid, spread], axis=0)
    hval = _peer_hval(hn_p, eid_p, peer_u, after)[:n]
    return h, eid, gate_t.T, hval


def _peer_finish(h, eid, gate, hval, tb, v_packed):
    a = _gate(hval, gate, tb)
    return _peer_values(h, a, eid * VT_ROWS, v_packed)


def _rope_tables(pos):
    half = HEAD_DIM // 2
    inv = ROPE_THETA ** (-jnp.arange(half, dtype=F32) / half)
    ang = pos.astype(F32)[:, None] * inv[None, :]
    cos = jnp.cos(ang)
    sin = jnp.sin(ang)
    reps = LANES // HEAD_DIM
    cos_f = jnp.tile(jnp.concatenate([cos, cos], axis=1), (1, reps))
    sin_s = jnp.tile(jnp.concatenate([-sin, sin], axis=1), (1, reps))
    return cos_f, sin_s


def _group_sum_matrix(width):
    g = jnp.arange(width) // HEAD_DIM
    return (g[:, None] == g[None, :]).astype(BF16)


def kernel(x_prompt, x_sample, cache_k_win, cache_v_win, state_conv, meta_tokens, norm_mix_g,
           w_in, q_norm_g, k_norm_g, attn_sinks, conv_w, conv_b, conv_ln_g, conv_ln_b, w_pw2,
           out_norm_attn_g, out_norm_conv_g, w_out, norm_ffn_g, peer_w_q, peer_keys1, peer_keys2,
           peer_u, peer_v):
    assert norm_mix_g.shape[0] == 1, "single-layer model"
    b, t, _ = x_prompt.shape
    s = x_sample.shape[0]
    w_buf = cache_k_win.shape[2]
    n = b * t

    g_mix = norm_mix_g[0][None, :]
    w_in_bf = w_in[0].astype(BF16)
    qg_t = jnp.tile(q_norm_g[0], N_HEADS)[None, :]
    kg_t = jnp.tile(k_norm_g[0], N_KV_HEADS)[None, :]
    gsum_q = _group_sum_matrix(D_ATTN)
    gsum_k = _group_sum_matrix(KV_DIM)
    sinks = attn_sinks[0]
    cw, cb = conv_w[0], conv_b[0][None, :]
    lg, lb = conv_ln_g[0][None, :], conv_ln_b[0][None, :]
    w2_bf = w_pw2[0].astype(BF16)
    g_a, g_c = out_norm_attn_g[0][None, :], out_norm_conv_g[0][None, :]
    w_out_a = w_out[0][:D_ATTN].astype(BF16)
    w_out_c = w_out[0][D_ATTN:].astype(BF16)
    g_f = norm_ffn_g[0][None, :]
    wq_bf = peer_w_q[0].astype(BF16)
    k1_bf = peer_keys1[0].astype(BF16)
    k2_bf = peer_keys2[0].astype(BF16)
    pu, vp = peer_u[0], _pack_value_table(peer_v[0])
    proj = functools.partial(_project, norm_g=g_mix, w_in_bf=w_in_bf, qg_t=qg_t, kg_t=kg_t,
                             gsum_q=gsum_q, gsum_k=gsum_k)
    conv = functools.partial(_conv_branch, conv_w=cw, conv_b=cb, ln_g=lg, ln_b=lb, w_pw2_bf=w2_bf)
    merge = functools.partial(_merge, g_a=g_a, g_c=g_c, w_out_a=w_out_a, w_out_c=w_out_c, g_f=g_f)

    tb = TB_DENSE
    no_dep = jnp.zeros((SC_TOKENS, LANES), F32)
    tab_x = _rope_tables(N_META + jnp.arange(t, dtype=jnp.int32))
    tab_m = _rope_tables(jnp.arange(N_META, dtype=jnp.int32))
    _, k_m, v_m, u_m = proj(meta_tokens, tab_m, 1, N_META, no_dep)
    halo0 = jnp.concatenate([jnp.zeros((CONV_HALO - N_META, D_CONV), F32), u_m], axis=0)

    k_meta = jnp.concatenate([jnp.zeros((META_PAD, KV_DIM), F32), k_m], axis=0)
    v_meta = jnp.concatenate([jnp.zeros((META_PAD, KV_DIM), F32), v_m], axis=0)

    xp = x_prompt.reshape(b * t, D_MODEL)

    def dense(bi, after):
        q, k, v, u = proj(xp, tab_x, t // tb, tb, after, row0=bi * t, nrows=t)
        o_attn = _prompt_attention(sinks, q, k, v, k_meta, v_meta)
        u3 = u.reshape(t // tb, tb, D_CONV)
        halo = jnp.concatenate([halo0[None], u3[:-1, tb - CONV_HALO:]], axis=0)
        o_conv = conv(halo, u, tb)
        hs, r0 = [], 0
        for rows in _chunk_plan(t, tb, bi == 0, bi == b - 1):
            hs.append(merge(xp, o_attn, o_conv, tb, x_row0=bi * t + r0, row0=r0, nrows=rows))
            r0 += rows
        return hs, k, v, u

    launched, ys, kvu = [], [], []
    cur = dense(0, no_dep)
    for bi in range(b):
        hs, k, v, u = cur
        kvu.append((k, v, u))
        for h, hn in hs:
            c = len(launched)
            if c >= PEER_DEPTH:
                ys.append(_peer_finish(*launched[c - PEER_DEPTH], TB_ROUTE, vp))
            after = ys[c - PEER_DEPTH][:SC_TOKENS] if c >= PEER_DEPTH else jnp.zeros((SC_TOKENS, D_MODEL), F32)
            launched.append(_peer_launch(h, hn, TB_ROUTE, after, wq_bf, k1_bf, k2_bf, pu))
        if bi + 1 < b:
            cur = dense(bi + 1, launched[-1][1][:SC_TOKENS, :LANES].astype(F32))
    for c in range(len(ys), len(launched)):
        ys.append(_peer_finish(*launched[c], TB_ROUTE, vp))
    y_prompt = jnp.concatenate(ys, axis=0).reshape(b, t, D_MODEL)

    new_k_prompt = jnp.stack([k[t - WINDOW:] for k, _, _ in kvu]).reshape(1, b, WINDOW, N_KV_HEADS, HEAD_DIM)
    new_v_prompt = jnp.stack([v[t - WINDOW:] for _, v, _ in kvu]).reshape(1, b, WINDOW, N_KV_HEADS, HEAD_DIM)
    new_conv_prompt = jnp.stack([u[t - (CONV_WIDTH - 1):] for _, _, u in kvu])[None]

    xs = x_sample.reshape(s, D_MODEL)
    tab_s = _rope_tables(jnp.full((s,), PAST_LEN, jnp.int32))
    qs, ks, vs, us = proj(xs, tab_s, 1, s, no_dep)
    ck = cache_k_win[0]
    cv = cache_v_win[0]
    o_attn_s = _decode_attention(
        sinks,
        qs.reshape(s, N_HEADS, HEAD_DIM), ks, vs,
        ck.reshape(s, w_buf, KV_DIM), cv.reshape(s, w_buf, KV_DIM), min(SB_DECODE, s))
    o_attn_s = o_attn_s.reshape(s, D_ATTN)
    cs = state_conv[0]
    hist = jnp.concatenate([jnp.zeros((s, CONV_HALO - (CONV_WIDTH - 1), D_CONV), F32), cs], axis=1)
    us_blk = jnp.concatenate([us[:, None, :], jnp.zeros((s, SUBLANES - 1, D_CONV), F32)], axis=1)
    o_conv_s = conv(hist, us_blk.reshape(s * SUBLANES, D_CONV), SUBLANES)
    o_conv_s = o_conv_s.reshape(s, SUBLANES, D_CONV)[:, 0]
    hs, hns = merge(xs, o_attn_s, o_conv_s, s)
    launch_s = _peer_launch(hs, hns, s, jnp.zeros((SC_TOKENS, D_MODEL), F32), wq_bf, k1_bf, k2_bf, pu)
    y_sample = _peer_finish(*launch_s, s, vp).reshape(s, 1, D_MODEL)

    new_k_sample = jnp.concatenate([ck[:, 1:], ks.reshape(s, 1, N_KV_HEADS, HEAD_DIM)], axis=1)[None]
    new_v_sample = jnp.concatenate([cv[:, 1:], vs.reshape(s, 1, N_KV_HEADS, HEAD_DIM)], axis=1)[None]
    new_conv_sample = jnp.concatenate([cs[:, 1:], us[:, None, :]], axis=1)[None]
    if w_buf != WINDOW:
        raise NotImplementedError("cache window shorter than the attention window")

    return (y_prompt, y_sample, new_k_prompt, new_v_prompt, new_conv_prompt,
            new_k_sample, new_v_sample, new_conv_sample)
```

```python
import functools

import jax
import jax.numpy as jnp
from jax import lax
from jax.experimental import pallas as pl
from jax.experimental.pallas import tpu as pltpu
from jax.experimental.pallas import tpu_sc as plsc

D_MODEL = 1024
HEAD_DIM = 64
D_ATTN = 512
N_HEADS = 8
N_KV_HEADS = 2
KV_DIM = N_KV_HEADS * HEAD_DIM
D_CONV = 512
D_IN = D_ATTN + 2 * KV_DIM + 2 * D_CONV
CONV_WIDTH = 31
WINDOW = 128
BLOCK = 128
ROPE_THETA = 10000.0
N_META = 16
META_PAD = BLOCK - N_META
PEER_HEADS = 8
PEER_NKEYS = 128
PEER_TOPK = 16
PEER_SEL = PEER_HEADS * PEER_TOPK
EPS = 1e-6
PAST_LEN = 16384

LANES = 128
SC_CORES = 2
SC_SUBCORES = 16
SC_LANES = 16
SC_WORKERS = SC_CORES * SC_SUBCORES
VMEM_LIMIT = 48 * 1024 * 1024
SUBLANES = 8

TB_DENSE = 512
TB_ROUTE = 256
SB_DECODE = 32

F32 = jnp.float32
BF16 = jnp.bfloat16
NEG_INF = float("-inf")


def _tc_params(n_axes):
    return pltpu.CompilerParams(dimension_semantics=("arbitrary",) * n_axes,
                                vmem_limit_bytes=VMEM_LIMIT)


def _full(shape):
    nd = len(shape)
    return pl.BlockSpec(shape, lambda *_: (0,) * nd)


def _group_mean(sq, gsum_ref):
    hi = sq.astype(BF16)
    lo = (sq - hi.astype(F32)).astype(BF16)
    g = gsum_ref[...]
    s = jnp.dot(hi, g, preferred_element_type=F32) + jnp.dot(lo, g, preferred_element_type=F32)
    return s * (1.0 / HEAD_DIM)


def _rope(xn, cos_f, sin_s, first_half):
    outs = []
    for s in range(xn.shape[1] // LANES):
        xs = xn[:, s * LANES:(s + 1) * LANES]
        partner = jnp.where(first_half, pltpu.roll(xs, LANES - HEAD_DIM // 2, axis=1),
                            pltpu.roll(xs, HEAD_DIM // 2, axis=1))
        outs.append(xs * cos_f + partner * sin_s)
    return outs[0] if len(outs) == 1 else jnp.concatenate(outs, axis=1)


def _proj_kernel(x_ref, g_ref, w_ref, qg_ref, kg_ref, cos_ref, sin_ref, gq_ref, gk_ref, after_ref,
                 q_out, k_out, v_out, u_out):
    del after_ref
    x = x_ref[...]
    n = x * lax.rsqrt(jnp.mean(x * x, axis=-1, keepdims=True) + EPS) * g_ref[...]
    p = jnp.dot(n.astype(BF16), w_ref[...], preferred_element_type=F32)
    q = p[:, :D_ATTN]
    k = p[:, D_ATTN:D_ATTN + KV_DIM]
    v = p[:, D_ATTN + KV_DIM:D_ATTN + 2 * KV_DIM]
    ga = p[:, D_ATTN + 2 * KV_DIM:D_ATTN + 2 * KV_DIM + D_CONV]
    gb = p[:, D_ATTN + 2 * KV_DIM + D_CONV:]
    cos_f = cos_ref[...]
    sin_s = sin_ref[...]
    lane = lax.broadcasted_iota(jnp.int32, (x.shape[0], LANES), 1)
    first_half = (lane % HEAD_DIM) < (HEAD_DIM // 2)
    qn = q * lax.rsqrt(_group_mean(q * q, gq_ref) + EPS) * qg_ref[...]
    kn = k * lax.rsqrt(_group_mean(k * k, gk_ref) + EPS) * kg_ref[...]
    q_out[...] = _rope(qn, cos_f, sin_s, first_half)
    k_out[...] = _rope(kn, cos_f, sin_s, first_half)
    v_out[...] = v
    u_out[...] = ga * (1.0 / (1.0 + jnp.exp(-gb)))


def _project(x, pos_tables, n_table_blocks, tb, after, norm_g, w_in_bf, qg_t, kg_t, gsum_q, gsum_k,
             row0=0, nrows=None):
    n = x.shape[0] if nrows is None else nrows
    cos_t, sin_t = pos_tables
    nb = n // tb
    blk0 = row0 // tb
    tab_spec = pl.BlockSpec((tb, LANES), lambda i: (i % n_table_blocks, 0))
    row = lambda w: pl.BlockSpec((tb, w), lambda i: (i, 0))
    return pl.pallas_call(
        _proj_kernel,
        grid=(nb,),
        in_specs=[pl.BlockSpec((tb, D_MODEL), lambda i: (i + blk0, 0)),
                  _full((1, D_MODEL)), _full((D_MODEL, D_IN)),
                  _full((1, D_ATTN)), _full((1, KV_DIM)), tab_spec, tab_spec,
                  _full((D_ATTN, D_ATTN)), _full((KV_DIM, KV_DIM)),
                  pl.BlockSpec(memory_space=pl.ANY)],
        out_specs=[row(D_ATTN), row(KV_DIM), row(KV_DIM), row(D_CONV)],
        out_shape=[jax.ShapeDtypeStruct((n, D_ATTN), F32), jax.ShapeDtypeStruct((n, KV_DIM), F32),
                   jax.ShapeDtypeStruct((n, KV_DIM), F32), jax.ShapeDtypeStruct((n, D_CONV), F32)],
        compiler_params=_tc_params(1),
        name="proj",
    )(x, norm_g, w_in_bf, qg_t, kg_t, cos_t, sin_t, gsum_q, gsum_k, after)


def _attn_kernel(sink_ref, q_ref, kp_ref, kc_ref, vp_ref, vc_ref, km_ref, vm_ref, o_ref):
    j = pl.program_id(0)
    r = lax.broadcasted_iota(jnp.int32, (BLOCK, 2 * BLOCK), 0)
    c = lax.broadcasted_iota(jnp.int32, (BLOCK, 2 * BLOCK), 1)
    ok = (c > r) & (c <= r + WINDOW) & ((j > 0) | (c >= META_PAD))
    grp = N_HEADS // N_KV_HEADS
    ok = jnp.concatenate([ok] * grp, axis=0)
    first = j == 0
    k_all = jnp.concatenate([jnp.where(first, km_ref[...], kp_ref[...]), kc_ref[...]], axis=0)
    v_all = jnp.concatenate([jnp.where(first, vm_ref[...], vp_ref[...]), vc_ref[...]], axis=0)
    q = q_ref[...]
    outs = []
    for g in range(N_KV_HEADS):
        lanes = slice(g * HEAD_DIM, (g + 1) * HEAD_DIM)
        k = k_all[:, lanes].astype(BF16)
        v = v_all[:, lanes].astype(BF16)
        q4 = jnp.concatenate([q[:, (g * grp + i) * HEAD_DIM:(g * grp + i + 1) * HEAD_DIM]
                              for i in range(grp)], axis=0).astype(BF16)
        s = lax.dot_general(q4, k, (((1,), (1,)), ((), ())), preferred_element_type=F32)
        s = jnp.where(ok, s * (HEAD_DIM ** -0.5), NEG_INF)
        sink = jnp.concatenate(
            [jnp.full((BLOCK, 1), sink_ref[g * grp + i], F32) for i in range(grp)], axis=0)
        m = jnp.maximum(jnp.max(s, axis=1, keepdims=True), sink)
        p = jnp.exp(s - m)
        den = jnp.sum(p, axis=1, keepdims=True) + jnp.exp(sink - m)
        o = jnp.dot(p.astype(BF16), v, preferred_element_type=F32) / den
        outs += [o[i * BLOCK:(i + 1) * BLOCK] for i in range(grp)]
    o_ref[...] = jnp.concatenate(outs, axis=1)


def _prompt_attention(sinks, q, k, v, k_meta, v_meta):
    t = q.shape[0]
    kv_prev = pl.BlockSpec((BLOCK, KV_DIM), lambda j: (jnp.maximum(j - 1, 0), 0))
    kv_cur = pl.BlockSpec((BLOCK, KV_DIM), lambda j: (j, 0))
    q_spec = pl.BlockSpec((BLOCK, D_ATTN), lambda j: (j, 0))
    meta = _full((BLOCK, KV_DIM))
    return pl.pallas_call(
        _attn_kernel,
        grid=(t // BLOCK,),
        in_specs=[pl.BlockSpec(memory_space=pltpu.SMEM), q_spec, kv_prev, kv_cur, kv_prev, kv_cur,
                  meta, meta],
        out_specs=q_spec,
        out_shape=jax.ShapeDtypeStruct((t, D_ATTN), F32),
        compiler_params=_tc_params(1),
        name="attn",
    )(sinks, q, k, k, v, v, k_meta, v_meta)


def _dec_attn_kernel(sink_ref, q_ref, kn_ref, vn_ref, ck_ref, cv_ref, o_ref):
    grp = N_HEADS // N_KV_HEADS
    sb, w_buf = ck_ref.shape[0], ck_ref.shape[1]
    q = q_ref[...]
    qb = q.astype(BF16)
    head = lax.broadcasted_iota(jnp.int32, (sb, N_HEADS, 1), 1)
    in_g0 = head < grp
    kv_lanes = [slice(g * HEAD_DIM, (g + 1) * HEAD_DIM) for g in range(N_KV_HEADS)]
    ck = ck_ref[...]
    cv = cv_ref[...]
    s_g = [jnp.einsum("shd,swd->shw", qb, ck[:, :, ln].astype(BF16), preferred_element_type=F32)
           for ln in kv_lanes]
    s = jnp.where(in_g0, s_g[0], s_g[1]) * (HEAD_DIM ** -0.5)
    key_ok = lax.broadcasted_iota(jnp.int32, (sb, N_HEADS, w_buf), 2) >= 1
    s = jnp.where(key_ok, s, NEG_INF)
    rnd = lambda a: a.astype(BF16).astype(F32)
    kn_all, vn_all = kn_ref[...], vn_ref[...]
    kn = jnp.where(in_g0, kn_all[:, None, kv_lanes[0]], kn_all[:, None, kv_lanes[1]])
    vn = jnp.where(in_g0, vn_all[:, None, kv_lanes[0]], vn_all[:, None, kv_lanes[1]])
    s_self = jnp.sum(rnd(q) * rnd(kn), axis=-1, keepdims=True) * (HEAD_DIM ** -0.5)
    sink = sink_ref[...]
    m = jnp.maximum(jnp.maximum(jnp.max(s, axis=-1, keepdims=True), s_self), sink)
    p = jnp.exp(s - m)
    p_self = jnp.exp(s_self - m)
    den = jnp.sum(p, axis=-1, keepdims=True) + p_self + jnp.exp(sink - m)
    pb = p.astype(BF16)
    o_g = [jnp.einsum("shw,swd->shd", pb, cv[:, :, ln].astype(BF16), preferred_element_type=F32)
           for ln in kv_lanes]
    o = jnp.where(in_g0, o_g[0], o_g[1]) + rnd(p_self) * rnd(vn)
    o_ref[...] = o / den


def _decode_attention(sinks, q3, kn_t, vn_t, ck_t, cv_t, sb):
    s = q3.shape[0]
    w_buf = ck_t.shape[1]
    qs = pl.BlockSpec((sb, N_HEADS, HEAD_DIM), lambda i: (i, 0, 0))
    ns = pl.BlockSpec((sb, KV_DIM), lambda i: (i, 0))
    cs = pl.BlockSpec((sb, w_buf, KV_DIM), lambda i: (i, 0, 0))
    return pl.pallas_call(
        _dec_attn_kernel,
        grid=(s // sb,),
        in_specs=[_full((1, N_HEADS, 1)), qs, ns, ns, cs, cs],
        out_specs=qs,
        out_shape=jax.ShapeDtypeStruct(q3.shape, F32),
        compiler_params=_tc_params(1),
        name="dec_attn",
    )(sinks.reshape(1, N_HEADS, 1), q3, kn_t, vn_t, ck_t, cv_t)


CONV_HALO = 32
CONV_ROWS = 64


def _conv_kernel(halo_ref, u_ref, cw_ref, cb_ref, lg_ref, lb_ref, w2_ref, o_ref, ucat):
    tb = u_ref.shape[0]
    ucat[0:CONV_HALO, :] = halo_ref[0]
    ucat[CONV_HALO:, :] = u_ref[...]
    first = CONV_HALO - (CONV_WIDTH - 1)
    rows = min(CONV_ROWS, tb)
    for r0 in range(0, tb, rows):
        acc = jnp.zeros((rows, D_CONV), F32)
        for j in range(CONV_WIDTH):
            acc = acc + ucat[r0 + first + j:r0 + first + j + rows, :] * cw_ref[j:j + 1, :]
        y = acc + cb_ref[...]
        yc = y - jnp.mean(y, axis=-1, keepdims=True)
        yn = yc * lax.rsqrt(jnp.mean(yc * yc, axis=-1, keepdims=True) + EPS)
        yn = yn * lg_ref[...] + lb_ref[...]
        act = yn * (1.0 / (1.0 + jnp.exp(-yn)))
        o_ref[r0:r0 + rows, :] = jnp.dot(act.astype(BF16), w2_ref[...],
                                              preferred_element_type=F32)


def _conv_branch(halo, u, tb, conv_w, conv_b, ln_g, ln_b, w_pw2_bf):
    n = u.shape[0]
    return pl.pallas_call(
        _conv_kernel,
        grid=(n // tb,),
        in_specs=[pl.BlockSpec((1, CONV_HALO, D_CONV), lambda i: (i, 0, 0)),
                  pl.BlockSpec((tb, D_CONV), lambda i: (i, 0)),
                  _full((CONV_WIDTH, D_CONV)), _full((1, D_CONV)), _full((1, D_CONV)),
                  _full((1, D_CONV)), _full((D_CONV, D_CONV))],
        out_specs=pl.BlockSpec((tb, D_CONV), lambda i: (i, 0)),
        out_shape=jax.ShapeDtypeStruct((n, D_CONV), F32),
        scratch_shapes=[pltpu.VMEM((tb + CONV_HALO, D_CONV), F32)],
        compiler_params=_tc_params(1),
        name="conv",
    )(halo, u, conv_w, conv_b, ln_g, ln_b, w_pw2_bf)


def _rms(x, g):
    return x * lax.rsqrt(jnp.mean(x * x, axis=-1, keepdims=True) + EPS) * g


def _merge_kernel(x_ref, oa_ref, oc_ref, ga_ref, gc_ref, wa_ref, wc_ref, gf_ref, h_out, hn_out):
    a = _rms(oa_ref[...], ga_ref[...]).astype(BF16)
    c = _rms(oc_ref[...], gc_ref[...]).astype(BF16)
    h = x_ref[...] + (jnp.dot(a, wa_ref[...], preferred_element_type=F32)
                      + jnp.dot(c, wc_ref[...], preferred_element_type=F32))
    h_out[...] = h
    hn_out[...] = _rms(h, gf_ref[...])


def _merge(x, oa, oc, tb, g_a, g_c, w_out_a, w_out_c, g_f, x_row0=0, row0=0, nrows=None):
    n = oa.shape[0] if nrows is None else nrows
    xb0, b0 = x_row0 // tb, row0 // tb
    row = lambda w: pl.BlockSpec((tb, w), lambda i: (i, 0))
    src = lambda w: pl.BlockSpec((tb, w), lambda i: (i + b0, 0))
    return pl.pallas_call(
        _merge_kernel,
        grid=(n // tb,),
        in_specs=[pl.BlockSpec((tb, D_MODEL), lambda i: (i + xb0, 0)), src(D_ATTN), src(D_CONV),
                  _full((1, D_ATTN)), _full((1, D_CONV)),
                  _full((D_ATTN, D_MODEL)), _full((D_CONV, D_MODEL)), _full((1, D_MODEL))],
        out_specs=[row(D_MODEL), row(D_MODEL)],
        out_shape=[jax.ShapeDtypeStruct((n, D_MODEL), F32)] * 2,
        compiler_params=_tc_params(1),
        name="merge",
    )(x, oa, oc, g_a, g_c, w_out_a, w_out_c, g_f)


ID_BIG = 1e9


def _topk_rows(s, k):
    rows = lax.broadcasted_iota(jnp.int32, s.shape, 0).astype(F32)
    vals, idxs = [], []
    for _ in range(k):
        m = jnp.max(s, axis=0, keepdims=True)
        idx = jnp.min(jnp.where(s == m, rows, ID_BIG), axis=0, keepdims=True)
        vals.append(m)
        idxs.append(idx)
        s = jnp.where(rows == idx, NEG_INF, s)
    return jnp.concatenate(vals, axis=0), jnp.concatenate(idxs, axis=0)


PAIR_B_WIDE = 8


def _route_kernel(hn_ref, wq_ref, k1_ref, k2_ref, eid_out, gate_out):
    tb = hn_ref.shape[0]
    q = jnp.dot(hn_ref[...].astype(BF16), wq_ref[...], preferred_element_type=F32).astype(BF16)
    k1 = k1_ref[...]
    k2 = k2_ref[...]
    nt = (((1,), (1,)), ((), ()))
    r = lax.broadcasted_iota(jnp.int32, (PEER_TOPK + (PAIR_B_WIDE - 1) * PAIR_B_WIDE + PAIR_B_WIDE, tb), 0)
    mid = r - PEER_TOPK
    flat = jnp.where(r < PEER_TOPK, r,
                     jnp.where(mid < (PAIR_B_WIDE - 1) * PAIR_B_WIDE,
                               (1 + mid // PAIR_B_WIDE) * PEER_TOPK + mid % PAIR_B_WIDE,
                               (PAIR_B_WIDE + mid - (PAIR_B_WIDE - 1) * PAIR_B_WIDE) * PEER_TOPK)).astype(F32)
    half = PEER_NKEYS
    for h in range(PEER_HEADS):
        q1 = q[:, (2 * h) * half:(2 * h + 1) * half]
        q2 = q[:, (2 * h + 1) * half:(2 * h + 2) * half]
        s1 = lax.dot_general(k1, q1, nt, preferred_element_type=F32)
        s2 = lax.dot_general(k2, q2, nt, preferred_element_type=F32)
        v1, i1 = _topk_rows(s1, PEER_TOPK)
        v2, i2 = _topk_rows(s2, PEER_TOPK)
        e1 = i1 * PEER_NKEYS
        cand = jnp.concatenate(
            [v1[0:1] + v2]
            + [v1[a:a + 1] + v2[0:PAIR_B_WIDE] for a in range(1, PAIR_B_WIDE)]
            + [v1[PAIR_B_WIDE:] + v2[0:1]], axis=0)
        cid = jnp.concatenate(
            [e1[0:1] + i2]
            + [e1[a:a + 1] + i2[0:PAIR_B_WIDE] for a in range(1, PAIR_B_WIDE)]
            + [e1[PAIR_B_WIDE:] + i2[0:1]], axis=0)
        scs, eids = [], []
        for _ in range(PEER_TOPK):
            m = jnp.max(cand, axis=0, keepdims=True)
            jsel = jnp.min(jnp.where(cand == m, flat, ID_BIG), axis=0, keepdims=True)
            hit = flat == jsel
            eids.append(jnp.max(jnp.where(hit, cid, -1.0), axis=0, keepdims=True))
            scs.append(m)
            cand = jnp.where(hit, NEG_INF, cand)
        sc = jnp.concatenate(scs, axis=0)
        e = jnp.exp(sc - sc[0:1])
        gate_out[h * PEER_TOPK:(h + 1) * PEER_TOPK, :] = e / jnp.sum(e, axis=0, keepdims=True)
        eid_out[h * PEER_TOPK:(h + 1) * PEER_TOPK, :] = jnp.concatenate(eids, axis=0).astype(jnp.int32)


def _route(hn, tb, wq_bf, keys1_bf, keys2_bf):
    n = hn.shape[0]
    col = pl.BlockSpec((PEER_SEL, tb), lambda i: (0, i))
    return pl.pallas_call(
        _route_kernel,
        grid=(n // tb,),
        in_specs=[pl.BlockSpec((tb, D_MODEL), lambda i: (i, 0)),
                  _full((D_MODEL, 2 * PEER_NKEYS * PEER_HEADS)),
                  _full((PEER_NKEYS, PEER_NKEYS)), _full((PEER_NKEYS, PEER_NKEYS))],
        out_specs=[col, col],
        out_shape=[jax.ShapeDtypeStruct((PEER_SEL, n), jnp.int32),
                   jax.ShapeDtypeStruct((PEER_SEL, n), F32)],
        compiler_params=_tc_params(1),
        name="route",
    )(hn, wq_bf, keys1_bf, keys2_bf)


SC_ROWS = 32
SC_GROUP = SC_LANES
SC_TOKENS = 8
SC_CHUNKS = PEER_SEL // SC_ROWS

_SC_PARAMS = pltpu.CompilerParams(needs_layout_passes=False)


def _sc_mesh():
    return plsc.VectorSubcoreMesh(core_axis_name="c", subcore_axis_name="s",
                                  num_cores=SC_CORES, num_subcores=SC_SUBCORES)


def _sc_token_loop(n_batches, table_hbm, eid_v, rbufs, rsems, prefetch, prefetch_wait, store,
                   contract):
    def gather(slot, tl, ch, p):
        idx = eid_v.at[slot, tl, pl.ds(ch * SC_ROWS, SC_ROWS)]
        return pltpu.make_async_copy(table_hbm.at[idx], rbufs[p], rsems[p])

    prefetch(0, 0)
    prefetch_wait()
    gather(0, 0, 0, 0).start()

    @pl.loop(0, n_batches * SC_TOKENS)
    def _(i):
        b = i // SC_TOKENS
        tl = i % SC_TOKENS
        slot = b % 2
        more = b + 1 < n_batches

        @pl.when(jnp.logical_and(tl == 0, more))
        def _():
            prefetch(b + 1, 1 - slot)

        for ch in range(SC_CHUNKS):
            p = ch % 2
            if ch + 1 < SC_CHUNKS:
                gather(slot, tl, ch + 1, 1 - p).start()
            else:
                @pl.when(tl + 1 < SC_TOKENS)
                def _():
                    gather(slot, tl + 1, 0, 1 - p).start()

                @pl.when(jnp.logical_and(tl + 1 == SC_TOKENS, more))
                def _():
                    prefetch_wait()
                    gather(1 - slot, 0, 0, 1 - p).start()
            gather(slot, tl, ch, p).wait()
            contract(slot, tl, ch, rbufs[p])

        @pl.when(tl + 1 == SC_TOKENS)
        def _():
            @pl.when(b >= 1)
            def _():
                store(b - 1, 1 - slot).wait()
            store(b, slot).start()

    store(n_batches - 1, (n_batches - 1) % 2).wait()


def _sc_batches(n):
    assert n % (SC_WORKERS * SC_TOKENS) == 0, n
    return n // (SC_WORKERS * SC_TOKENS)


def _peer_hval(hn, eid, peer_u, after):
    n = hn.shape[0]
    nbw = _sc_batches(n)

    @functools.partial(
        pl.kernel, mesh=_sc_mesh(),
        out_type=jax.ShapeDtypeStruct((n // SC_TOKENS, SC_TOKENS, PEER_SEL), F32),
        scratch_types=[
            pltpu.VMEM((2, SC_TOKENS, PEER_SEL), jnp.int32),
            pltpu.VMEM((2, SC_TOKENS, D_MODEL), F32),
            pltpu.VMEM((SC_ROWS, D_MODEL), F32),
            pltpu.VMEM((SC_ROWS, D_MODEL), F32),
            pltpu.VMEM((2, SC_TOKENS, PEER_SEL), F32),
            pltpu.VMEM((SC_GROUP, SC_LANES), F32),
            pltpu.SemaphoreType.DMA, pltpu.SemaphoreType.DMA,
            pltpu.SemaphoreType.DMA, pltpu.SemaphoreType.DMA,
        ],
        compiler_params=_SC_PARAMS, name="peer_hval",
        cost_estimate=pl.CostEstimate(flops=2 * n * PEER_SEL * D_MODEL, transcendentals=0,
                                      bytes_accessed=n * PEER_SEL * D_MODEL * 4))
    def k(x_hbm, eid_hbm, u_hbm, after_hbm, o_hbm, eid_v, x_v, r0, r1, h_v, tr, sr0, sr1, spf, sout):
        del after_hbm
        blk0 = (lax.axis_index("s") * SC_CORES + lax.axis_index("c")) * nbw
        lane = lax.iota(jnp.int32, SC_LANES)
        zero = jnp.zeros((SC_LANES,), F32)

        def prefetch(b, slot):
            pltpu.async_copy(eid_hbm.at[blk0 + b], eid_v.at[slot], spf)
            pltpu.async_copy(x_hbm.at[blk0 + b], x_v.at[slot], spf)

        def prefetch_wait():
            pltpu.make_async_copy(eid_hbm.at[0], eid_v.at[0], spf).wait()
            pltpu.make_async_copy(x_hbm.at[0], x_v.at[0], spf).wait()

        def store(b, slot):
            return pltpu.make_async_copy(h_v.at[slot], o_hbm.at[blk0 + b], sout)

        def contract(slot, tl, ch, rbuf):
            for g in range(SC_ROWS // SC_GROUP):
                def body(c, accs):
                    xc = x_v[slot, tl, pl.ds(c * SC_LANES, SC_LANES)]
                    return tuple(
                        accs[r] + rbuf[g * SC_GROUP + r, pl.ds(c * SC_LANES, SC_LANES)] * xc
                        for r in range(SC_GROUP))
                accs = lax.fori_loop(0, D_MODEL // SC_LANES, body, (zero,) * SC_GROUP)
                for r in range(SC_GROUP):
                    tr[r, :] = accs[r]
                res = zero
                for jj in range(SC_LANES):
                    res = res + plsc.load_gather(tr, [lane, jnp.full((SC_LANES,), jj, jnp.int32)])
                h_v[slot, tl, pl.ds(ch * SC_ROWS + g * SC_GROUP, SC_GROUP)] = res

        _sc_token_loop(nbw, u_hbm, eid_v, (r0, r1), (sr0, sr1), prefetch, prefetch_wait, store,
                       contract)

    out = k(hn.reshape(n // SC_TOKENS, SC_TOKENS, D_MODEL),
            eid.reshape(n // SC_TOKENS, SC_TOKENS, PEER_SEL), peer_u, after)
    return out.reshape(n, PEER_SEL)


def _gate_kernel(hv_ref, gate_ref, a_out):
    hv = hv_ref[...]
    gelu = hv * (lax.erf(hv * (2.0 ** -0.5)) + 1.0) * 0.5
    a_out[...] = gate_ref[...] * gelu


def _gate(hval, gate, tb):
    n = hval.shape[0]
    row = pl.BlockSpec((tb, PEER_SEL), lambda i: (i, 0))
    return pl.pallas_call(
        _gate_kernel, grid=(n // tb,), in_specs=[row, row], out_specs=row,
        out_shape=jax.ShapeDtypeStruct((n, PEER_SEL), F32),
        compiler_params=_tc_params(1), name="gate",
    )(hval, gate)


VT_ROWS = D_MODEL // 2 // LANES
VT_TOKENS = 128
HI_MASK = -65536


def _pack_value_table(peer_v):
    e = peer_v.shape[0]
    bits = lax.bitcast_convert_type(peer_v.astype(BF16), jnp.uint16).astype(jnp.uint32)
    words = bits[:, :D_MODEL // 2] | (bits[:, D_MODEL // 2:] << 16)
    return lax.bitcast_convert_type(words, jnp.int32).reshape(e * VT_ROWS, LANES)


def _vside_kernel(row_s, a_s, h_ref, tab_ref, y_ref):
    tb = h_ref.shape[0]

    def token(t, carry):
        zero = jnp.zeros((VT_ROWS, LANES), F32)
        lo = [zero, zero]
        hi = [zero, zero]
        for e in range(PEER_SEL):
            r0 = pl.multiple_of(row_s[t, e], VT_ROWS)
            w = tab_ref[pl.ds(r0, VT_ROWS), :]
            coef = a_s[t, e]
            c = e % 2
            lo[c] = lo[c] + coef * pltpu.bitcast(w << 16, F32)
            hi[c] = hi[c] + coef * pltpu.bitcast(w & HI_MASK, F32)
        y_ref[t] = h_ref[t] + jnp.concatenate([lo[0] + lo[1], hi[0] + hi[1]], axis=0)
        return carry

    lax.fori_loop(0, tb, token, 0)


def _peer_values(h, a, row_ids, v_packed):
    n = h.shape[0]
    tb = min(VT_TOKENS, n)
    smem = pl.BlockSpec((tb, PEER_SEL), lambda i: (i, 0), memory_space=pltpu.SMEM)
    slab = pl.BlockSpec((tb, 2 * VT_ROWS, LANES), lambda i: (i, 0, 0))
    table = pl.BlockSpec(v_packed.shape, lambda i: (0, 0), pipeline_mode=pl.Buffered(1))
    table_bytes = v_packed.shape[0] * LANES * 4
    y = pl.pallas_call(
        _vside_kernel,
        grid=(n // tb,),
        in_specs=[smem, smem, slab, table],
        out_specs=slab,
        out_shape=jax.ShapeDtypeStruct((n, 2 * VT_ROWS, LANES), F32),
        compiler_params=pltpu.CompilerParams(
            dimension_semantics=("arbitrary",),
            vmem_limit_bytes=table_bytes + 8 * 1024 * 1024),
        name="peer_values",
        cost_estimate=pl.CostEstimate(flops=2 * n * PEER_SEL * D_MODEL, transcendentals=0,
                                      bytes_accessed=table_bytes + n * PEER_SEL * D_MODEL * 2),
    )(row_ids, a, h.reshape(n, 2 * VT_ROWS, LANES), v_packed)
    return y.reshape(n, D_MODEL)


PEER_DEPTH = 2


def _chunk_plan(t, tb, first, last):
    eighth = t // 8
    if eighth % tb:
        return [t]
    sizes = [4 * eighth, 4 * eighth]
    if last:
        sizes = sizes[:1] + [2 * eighth, eighth, eighth]
    if first:
        sizes = [eighth, 3 * eighth] + sizes[1:]
    return sizes


def _peer_launch(h, hn, tb, after, wq_bf, keys1_bf, keys2_bf, peer_u):
    n = h.shape[0]
    eid_t, gate_t = _route(hn, tb, wq_bf, keys1_bf, keys2_bf)
    eid = eid_t.T
    pad = (-n) % (SC_WORKERS * SC_TOKENS)
    hn_p, eid_p = hn, eid
    if pad:
        spread = (jnp.arange(pad * PEER_SEL, dtype=jnp.int32) % peer_u.shape[0]).reshape(pad, PEER_SEL)
        hn_p = jnp.pad(hn, ((0, pad), (0, 0)))
        eid_p = jnp.concatenate([eid, spread], axis=0)
    hval = _peer_hval(hn_p, eid_p, peer_u, after)[:n]
    return h, eid, gate_t.T, hval


def _peer_finish(h, eid, gate, hval, tb, v_packed):
    a = _gate(hval, gate, tb)
    return _peer_values(h, a, eid * VT_ROWS, v_packed)


def _rope_tables(pos):
    half = HEAD_DIM // 2
    inv = ROPE_THETA ** (-jnp.arange(half, dtype=F32) / half)
    ang = pos.astype(F32)[:, None] * inv[None, :]
    cos = jnp.cos(ang)
    sin = jnp.sin(ang)
    reps = LANES // HEAD_DIM
    cos_f = jnp.tile(jnp.concatenate([cos, cos], axis=1), (1, reps))
    sin_s = jnp.tile(jnp.concatenate([-sin, sin], axis=1), (1, reps))
    return cos_f, sin_s


def _group_sum_matrix(width):
    g = jnp.arange(width) // HEAD_DIM
    return (g[:, None] == g[None, :]).astype(BF16)


def kernel(x_prompt, x_sample, cache_k_win, cache_v_win, state_conv, meta_tokens, norm_mix_g,
           w_in, q_norm_g, k_norm_g, attn_sinks, conv_w, conv_b, conv_ln_g, conv_ln_b, w_pw2,
           out_norm_attn_g, out_norm_conv_g, w_out, norm_ffn_g, peer_w_q, peer_keys1, peer_keys2,
           peer_u, peer_v):
    assert norm_mix_g.shape[0] == 1, "single-layer model"
    b, t, _ = x_prompt.shape
    s = x_sample.shape[0]
    w_buf = cache_k_win.shape[2]
    n = b * t

    g_mix = norm_mix_g[0][None, :]
    w_in_bf = w_in[0].astype(BF16)
    qg_t = jnp.tile(q_norm_g[0], N_HEADS)[None, :]
    kg_t = jnp.tile(k_norm_g[0], N_KV_HEADS)[None, :]
    gsum_q = _group_sum_matrix(D_ATTN)
    gsum_k = _group_sum_matrix(KV_DIM)
    sinks = attn_sinks[0]
    cw, cb = conv_w[0], conv_b[0][None, :]
    lg, lb = conv_ln_g[0][None, :], conv_ln_b[0][None, :]
    w2_bf = w_pw2[0].astype(BF16)
    g_a, g_c = out_norm_attn_g[0][None, :], out_norm_conv_g[0][None, :]
    w_out_a = w_out[0][:D_ATTN].astype(BF16)
    w_out_c = w_out[0][D_ATTN:].astype(BF16)
    g_f = norm_ffn_g[0][None, :]
    wq_bf = peer_w_q[0].astype(BF16)
    k1_bf = peer_keys1[0].astype(BF16)
    k2_bf = peer_keys2[0].astype(BF16)
    pu, vp = peer_u[0], _pack_value_table(peer_v[0])
    proj = functools.partial(_project, norm_g=g_mix, w_in_bf=w_in_bf, qg_t=qg_t, kg_t=kg_t,
                             gsum_q=gsum_q, gsum_k=gsum_k)
    conv = functools.partial(_conv_branch, conv_w=cw, conv_b=cb, ln_g=lg, ln_b=lb, w_pw2_bf=w2_bf)
    merge = functools.partial(_merge, g_a=g_a, g_c=g_c, w_out_a=w_out_a, w_out_c=w_out_c, g_f=g_f)

    tb = TB_DENSE
    no_dep = jnp.zeros((SC_TOKENS, LANES), F32)
    tab_x = _rope_tables(N_META + jnp.arange(t, dtype=jnp.int32))
    tab_m = _rope_tables(jnp.arange(N_META, dtype=jnp.int32))
    _, k_m, v_m, u_m = proj(meta_tokens, tab_m, 1, N_META, no_dep)
    halo0 = jnp.concatenate([jnp.zeros((CONV_HALO - N_META, D_CONV), F32), u_m], axis=0)

    k_meta = jnp.concatenate([jnp.zeros((META_PAD, KV_DIM), F32), k_m], axis=0)
    v_meta = jnp.concatenate([jnp.zeros((META_PAD, KV_DIM), F32), v_m], axis=0)

    xp = x_prompt.reshape(b * t, D_MODEL)

    def dense(bi, after):
        q, k, v, u = proj(xp, tab_x, t // tb, tb, after, row0=bi * t, nrows=t)
        o_attn = _prompt_attention(sinks, q, k, v, k_meta, v_meta)
        u3 = u.reshape(t // tb, tb, D_CONV)
        halo = jnp.concatenate([halo0[None], u3[:-1, tb - CONV_HALO:]], axis=0)
        o_conv = conv(halo, u, tb)
        hs, r0 = [], 0
        for rows in _chunk_plan(t, tb, bi == 0, bi == b - 1):
            hs.append(merge(xp, o_attn, o_conv, tb, x_row0=bi * t + r0, row0=r0, nrows=rows))
            r0 += rows
        return hs, k, v, u

    launched, ys, kvu = [], [], []
    cur = dense(0, no_dep)
    for bi in range(b):
        hs, k, v, u = cur
        kvu.append((k, v, u))
        for h, hn in hs:
            c = len(launched)
            if c >= PEER_DEPTH:
                ys.append(_peer_finish(*launched[c - PEER_DEPTH], TB_ROUTE, vp))
            after = ys[c - PEER_DEPTH][:SC_TOKENS] if c >= PEER_DEPTH else jnp.zeros((SC_TOKENS, D_MODEL), F32)
            launched.append(_peer_launch(h, hn, TB_ROUTE, after, wq_bf, k1_bf, k2_bf, pu))
        if bi + 1 < b:
            cur = dense(bi + 1, launched[-1][1][:SC_TOKENS, :LANES].astype(F32))
    for c in range(len(ys), len(launched)):
        ys.append(_peer_finish(*launched[c], TB_ROUTE, vp))
    y_prompt = jnp.concatenate(ys, axis=0).reshape(b, t, D_MODEL)

    new_k_prompt = jnp.stack([k[t - WINDOW:] for k, _, _ in kvu]).reshape(1, b, WINDOW, N_KV_HEADS, HEAD_DIM)
    new_v_prompt = jnp.stack([v[t - WINDOW:] for _, v, _ in kvu]).reshape(1, b, WINDOW, N_KV_HEADS, HEAD_DIM)
    new_conv_prompt = jnp.stack([u[t - (CONV_WIDTH - 1):] for _, _, u in kvu])[None]

    xs = x_sample.reshape(s, D_MODEL)
    tab_s = _rope_tables(jnp.full((s,), PAST_LEN, jnp.int32))
    qs, ks, vs, us = proj(xs, tab_s, 1, s, no_dep)
    ck = cache_k_win[0]
    cv = cache_v_win[0]
    o_attn_s = _decode_attention(
        sinks,
        qs.reshape(s, N_HEADS, HEAD_DIM), ks, vs,
        ck.reshape(s, w_buf, KV_DIM), cv.reshape(s, w_buf, KV_DIM), min(SB_DECODE, s))
    o_attn_s = o_attn_s.reshape(s, D_ATTN)
    cs = state_conv[0]
    hist = jnp.concatenate([jnp.zeros((s, CONV_HALO - (CONV_WIDTH - 1), D_CONV), F32), cs], axis=1)
    us_blk = jnp.concatenate([us[:, None, :], jnp.zeros((s, SUBLANES - 1, D_CONV), F32)], axis=1)
    o_conv_s = conv(hist, us_blk.reshape(s * SUBLANES, D_CONV), SUBLANES)
    o_conv_s = o_conv_s.reshape(s, SUBLANES, D_CONV)[:, 0]
    hs, hns = merge(xs, o_attn_s, o_conv_s, s)
    launch_s = _peer_launch(hs, hns, s, jnp.zeros((SC_TOKENS, D_MODEL), F32), wq_bf, k1_bf, k2_bf, pu)
    y_sample = _peer_finish(*launch_s, s, vp).reshape(s, 1, D_MODEL)

    new_k_sample = jnp.concatenate([ck[:, 1:], ks.reshape(s, 1, N_KV_HEADS, HEAD_DIM)], axis=1)[None]
    new_v_sample = jnp.concatenate([cv[:, 1:], vs.reshape(s, 1, N_KV_HEADS, HEAD_DIM)], axis=1)[None]
    new_conv_sample = jnp.concatenate([cs[:, 1:], us[:, None, :]], axis=1)[None]
    if w_buf != WINDOW:
        raise NotImplementedError("cache window shorter than the attention window")

    return (y_prompt, y_sample, new_k_prompt, new_v_prompt, new_conv_prompt,
            new_k_sample, new_v_sample, new_conv_sample)
```

```python
import functools

import jax
import jax.numpy as jnp
from jax import lax
from jax.experimental import pallas as pl
from jax.experimental.pallas import tpu as pltpu
from jax.experimental.pallas import tpu_sc as plsc

D_MODEL = 1024
HEAD_DIM = 64
D_ATTN = 512
N_HEADS = 8
N_KV_HEADS = 2
KV_DIM = N_KV_HEADS * HEAD_DIM
D_CONV = 512
D_IN = D_ATTN + 2 * KV_DIM + 2 * D_CONV
CONV_WIDTH = 31
WINDOW = 128
BLOCK = 128
ROPE_THETA = 10000.0
N_META = 16
META_PAD = BLOCK - N_META
PEER_HEADS = 8
PEER_NKEYS = 128
PEER_TOPK = 16
PEER_SEL = PEER_HEADS * PEER_TOPK
EPS = 1e-6
PAST_LEN = 16384

LANES = 128
SC_CORES = 2
SC_SUBCORES = 16
SC_LANES = 16
SC_WORKERS = SC_CORES * SC_SUBCORES
VMEM_LIMIT = 48 * 1024 * 1024
SUBLANES = 8

TB_DENSE = 512
TB_ROUTE = 256
SB_DECODE = 32

F32 = jnp.float32
BF16 = jnp.bfloat16
NEG_INF = float("-inf")


def _tc_params(n_axes):
    return pltpu.CompilerParams(dimension_semantics=("arbitrary",) * n_axes,
                                vmem_limit_bytes=VMEM_LIMIT)


def _full(shape):
    nd = len(shape)
    return pl.BlockSpec(shape, lambda *_: (0,) * nd)


def _group_mean(sq, gsum_ref):
    hi = sq.astype(BF16)
    lo = (sq - hi.astype(F32)).astype(BF16)
    g = gsum_ref[...]
    s = jnp.dot(hi, g, preferred_element_type=F32) + jnp.dot(lo, g, preferred_element_type=F32)
    return s * (1.0 / HEAD_DIM)


def _rope(xn, cos_f, sin_s, first_half):
    outs = []
    for s in range(xn.shape[1] // LANES):
        xs = xn[:, s * LANES:(s + 1) * LANES]
        partner = jnp.where(first_half, pltpu.roll(xs, LANES - HEAD_DIM // 2, axis=1),
                            pltpu.roll(xs, HEAD_DIM // 2, axis=1))
        outs.append(xs * cos_f + partner * sin_s)
    return outs[0] if len(outs) == 1 else jnp.concatenate(outs, axis=1)


def _proj_kernel(x_ref, g_ref, w_ref, qg_ref, kg_ref, cos_ref, sin_ref, gq_ref, gk_ref, after_ref,
                 q_out, k_out, v_out, u_out):
    del after_ref
    x = x_ref[...]
    n = x * lax.rsqrt(jnp.mean(x * x, axis=-1, keepdims=True) + EPS) * g_ref[...]
    p = jnp.dot(n.astype(BF16), w_ref[...], preferred_element_type=F32)
    q = p[:, :D_ATTN]
    k = p[:, D_ATTN:D_ATTN + KV_DIM]
    v = p[:, D_ATTN + KV_DIM:D_ATTN + 2 * KV_DIM]
    ga = p[:, D_ATTN + 2 * KV_DIM:D_ATTN + 2 * KV_DIM + D_CONV]
    gb = p[:, D_ATTN + 2 * KV_DIM + D_CONV:]
    cos_f = cos_ref[...]
    sin_s = sin_ref[...]
    lane = lax.broadcasted_iota(jnp.int32, (x.shape[0], LANES), 1)
    first_half = (lane % HEAD_DIM) < (HEAD_DIM // 2)
    qn = q * lax.rsqrt(_group_mean(q * q, gq_ref) + EPS) * qg_ref[...]
    kn = k * lax.rsqrt(_group_mean(k * k, gk_ref) + EPS) * kg_ref[...]
    q_out[...] = _rope(qn, cos_f, sin_s, first_half)
    k_out[...] = _rope(kn, cos_f, sin_s, first_half)
    v_out[...] = v
    u_out[...] = ga * (1.0 / (1.0 + jnp.exp(-gb)))


def _project(x, pos_tables, n_table_blocks, tb, after, norm_g, w_in_bf, qg_t, kg_t, gsum_q, gsum_k,
             row0=0, nrows=None):
    n = x.shape[0] if nrows is None else nrows
    cos_t, sin_t = pos_tables
    nb = n // tb
    blk0 = row0 // tb
    tab_spec = pl.BlockSpec((tb, LANES), lambda i: (i % n_table_blocks, 0))
    row = lambda w: pl.BlockSpec((tb, w), lambda i: (i, 0))
    return pl.pallas_call(
        _proj_kernel,
        grid=(nb,),
        in_specs=[pl.BlockSpec((tb, D_MODEL), lambda i: (i + blk0, 0)),
                  _full((1, D_MODEL)), _full((D_MODEL, D_IN)),
                  _full((1, D_ATTN)), _full((1, KV_DIM)), tab_spec, tab_spec,
                  _full((D_ATTN, D_ATTN)), _full((KV_DIM, KV_DIM)),
                  pl.BlockSpec(memory_space=pl.ANY)],
        out_specs=[row(D_ATTN), row(KV_DIM), row(KV_DIM), row(D_CONV)],
        out_shape=[jax.ShapeDtypeStruct((n, D_ATTN), F32), jax.ShapeDtypeStruct((n, KV_DIM), F32),
                   jax.ShapeDtypeStruct((n, KV_DIM), F32), jax.ShapeDtypeStruct((n, D_CONV), F32)],
        compiler_params=_tc_params(1),
        name="proj",
    )(x, norm_g, w_in_bf, qg_t, kg_t, cos_t, sin_t, gsum_q, gsum_k, after)


def _attn_kernel(sink_ref, q_ref, kp_ref, kc_ref, vp_ref, vc_ref, km_ref, vm_ref, o_ref):
    j = pl.program_id(0)
    r = lax.broadcasted_iota(jnp.int32, (BLOCK, 2 * BLOCK), 0)
    c = lax.broadcasted_iota(jnp.int32, (BLOCK, 2 * BLOCK), 1)
    ok = (c > r) & (c <= r + WINDOW) & ((j > 0) | (c >= META_PAD))
    grp = N_HEADS // N_KV_HEADS
    ok = jnp.concatenate([ok] * grp, axis=0)
    first = j == 0
    k_all = jnp.concatenate([jnp.where(first, km_ref[...], kp_ref[...]), kc_ref[...]], axis=0)
    v_all = jnp.concatenate([jnp.where(first, vm_ref[...], vp_ref[...]), vc_ref[...]], axis=0)
    q = q_ref[...]
    outs = []
    for g in range(N_KV_HEADS):
        lanes = slice(g * HEAD_DIM, (g + 1) * HEAD_DIM)
        k = k_all[:, lanes].astype(BF16)
        v = v_all[:, lanes].astype(BF16)
        q4 = jnp.concatenate([q[:, (g * grp + i) * HEAD_DIM:(g * grp + i + 1) * HEAD_DIM]
                              for i in range(grp)], axis=0).astype(BF16)
        s = lax.dot_general(q4, k, (((1,), (1,)), ((), ())), preferred_element_type=F32)
        s = jnp.where(ok, s * (HEAD_DIM ** -0.5), NEG_INF)
        sink = jnp.concatenate(
            [jnp.full((BLOCK, 1), sink_ref[g * grp + i], F32) for i in range(grp)], axis=0)
        m = jnp.maximum(jnp.max(s, axis=1, keepdims=True), sink)
        p = jnp.exp(s - m)
        den = jnp.sum(p, axis=1, keepdims=True) + jnp.exp(sink - m)
        o = jnp.dot(p.astype(BF16), v, preferred_element_type=F32) / den
        outs += [o[i * BLOCK:(i + 1) * BLOCK] for i in range(grp)]
    o_ref[...] = jnp.concatenate(outs, axis=1)


def _prompt_attention(sinks, q, k, v, k_meta, v_meta):
    t = q.shape[0]
    kv_prev = pl.BlockSpec((BLOCK, KV_DIM), lambda j: (jnp.maximum(j - 1, 0), 0))
    kv_cur = pl.BlockSpec((BLOCK, KV_DIM), lambda j: (j, 0))
    q_spec = pl.BlockSpec((BLOCK, D_ATTN), lambda j: (j, 0))
    meta = _full((BLOCK, KV_DIM))
    return pl.pallas_call(
        _attn_kernel,
        grid=(t // BLOCK,),
        in_specs=[pl.BlockSpec(memory_space=pltpu.SMEM), q_spec, kv_prev, kv_cur, kv_prev, kv_cur,
                  meta, meta],
        out_specs=q_spec,
        out_shape=jax.ShapeDtypeStruct((t, D_ATTN), F32),
        compiler_params=_tc_params(1),
        name="attn",
    )(sinks, q, k, k, v, v, k_meta, v_meta)


def _dec_attn_kernel(sink_ref, q_ref, kn_ref, vn_ref, ck_ref, cv_ref, o_ref):
    grp = N_HEADS // N_KV_HEADS
    sb, w_buf = ck_ref.shape[0], ck_ref.shape[1]
    q = q_ref[...]
    qb = q.astype(BF16)
    head = lax.broadcasted_iota(jnp.int32, (sb, N_HEADS, 1), 1)
    in_g0 = head < grp
    kv_lanes = [slice(g * HEAD_DIM, (g + 1) * HEAD_DIM) for g in range(N_KV_HEADS)]
    ck = ck_ref[...]
    cv = cv_ref[...]
    s_g = [jnp.einsum("shd,swd->shw", qb, ck[:, :, ln].astype(BF16), preferred_element_type=F32)
           for ln in kv_lanes]
    s = jnp.where(in_g0, s_g[0], s_g[1]) * (HEAD_DIM ** -0.5)
    key_ok = lax.broadcasted_iota(jnp.int32, (sb, N_HEADS, w_buf), 2) >= 1
    s = jnp.where(key_ok, s, NEG_INF)
    rnd = lambda a: a.astype(BF16).astype(F32)
    kn_all, vn_all = kn_ref[...], vn_ref[...]
    kn = jnp.where(in_g0, kn_all[:, None, kv_lanes[0]], kn_all[:, None, kv_lanes[1]])
    vn = jnp.where(in_g0, vn_all[:, None, kv_lanes[0]], vn_all[:, None, kv_lanes[1]])
    s_self = jnp.sum(rnd(q) * rnd(kn), axis=-1, keepdims=True) * (HEAD_DIM ** -0.5)
    sink = sink_ref[...]
    m = jnp.maximum(jnp.maximum(jnp.max(s, axis=-1, keepdims=True), s_self), sink)
    p = jnp.exp(s - m)
    p_self = jnp.exp(s_self - m)
    den = jnp.sum(p, axis=-1, keepdims=True) + p_self + jnp.exp(sink - m)
    pb = p.astype(BF16)
    o_g = [jnp.einsum("shw,swd->shd", pb, cv[:, :, ln].astype(BF16), preferred_element_type=F32)
           for ln in kv_lanes]
    o = jnp.where(in_g0, o_g[0], o_g[1]) + rnd(p_self) * rnd(vn)
    o_ref[...] = o / den


def _decode_attention(sinks, q3, kn_t, vn_t, ck_t, cv_t, sb):
    s = q3.shape[0]
    w_buf = ck_t.shape[1]
    qs = pl.BlockSpec((sb, N_HEADS, HEAD_DIM), lambda i: (i, 0, 0))
    ns = pl.BlockSpec((sb, KV_DIM), lambda i: (i, 0))
    cs = pl.BlockSpec((sb, w_buf, KV_DIM), lambda i: (i, 0, 0))
    return pl.pallas_call(
        _dec_attn_kernel,
        grid=(s // sb,),
        in_specs=[_full((1, N_HEADS, 1)), qs, ns, ns, cs, cs],
        out_specs=qs,
        out_shape=jax.ShapeDtypeStruct(q3.shape, F32),
        compiler_params=_tc_params(1),
        name="dec_attn",
    )(sinks.reshape(1, N_HEADS, 1), q3, kn_t, vn_t, ck_t, cv_t)


CONV_HALO = 32
CONV_ROWS = 64


def _conv_kernel(halo_ref, u_ref, cw_ref, cb_ref, lg_ref, lb_ref, w2_ref, o_ref, ucat):
    tb = u_ref.shape[0]
    ucat[0:CONV_HALO, :] = halo_ref[0]
    ucat[CONV_HALO:, :] = u_ref[...]
    first = CONV_HALO - (CONV_WIDTH - 1)
    rows = min(CONV_ROWS, tb)
    for r0 in range(0, tb, rows):
        acc = jnp.zeros((rows, D_CONV), F32)
        for j in range(CONV_WIDTH):
            acc = acc + ucat[r0 + first + j:r0 + first + j + rows, :] * cw_ref[j:j + 1, :]
        y = acc + cb_ref[...]
        yc = y - jnp.mean(y, axis=-1, keepdims=True)
        yn = yc * lax.rsqrt(jnp.mean(yc * yc, axis=-1, keepdims=True) + EPS)
        yn = yn * lg_ref[...] + lb_ref[...]
        act = yn * (1.0 / (1.0 + jnp.exp(-yn)))
        o_ref[r0:r0 + rows, :] = jnp.dot(act.astype(BF16), w2_ref[...],
                                              preferred_element_type=F32)


def _conv_branch(halo, u, tb, conv_w, conv_b, ln_g, ln_b, w_pw2_bf):
    n = u.shape[0]
    return pl.pallas_call(
        _conv_kernel,
        grid=(n // tb,),
        in_specs=[pl.BlockSpec((1, CONV_HALO, D_CONV), lambda i: (i, 0, 0)),
                  pl.BlockSpec((tb, D_CONV), lambda i: (i, 0)),
                  _full((CONV_WIDTH, D_CONV)), _full((1, D_CONV)), _full((1, D_CONV)),
                  _full((1, D_CONV)), _full((D_CONV, D_CONV))],
        out_specs=pl.BlockSpec((tb, D_CONV), lambda i: (i, 0)),
        out_shape=jax.ShapeDtypeStruct((n, D_CONV), F32),
        scratch_shapes=[pltpu.VMEM((tb + CONV_HALO, D_CONV), F32)],
        compiler_params=_tc_params(1),
        name="conv",
    )(halo, u, conv_w, conv_b, ln_g, ln_b, w_pw2_bf)


def _rms(x, g):
    return x * lax.rsqrt(jnp.mean(x * x, axis=-1, keepdims=True) + EPS) * g


def _merge_kernel(x_ref, oa_ref, oc_ref, ga_ref, gc_ref, wa_ref, wc_ref, gf_ref, h_out, hn_out):
    a = _rms(oa_ref[...], ga_ref[...]).astype(BF16)
    c = _rms(oc_ref[...], gc_ref[...]).astype(BF16)
    h = x_ref[...] + (jnp.dot(a, wa_ref[...], preferred_element_type=F32)
                      + jnp.dot(c, wc_ref[...], preferred_element_type=F32))
    h_out[...] = h
    hn_out[...] = _rms(h, gf_ref[...])


def _merge(x, oa, oc, tb, g_a, g_c, w_out_a, w_out_c, g_f, x_row0=0, row0=0, nrows=None):
    n = oa.shape[0] if nrows is None else nrows
    xb0, b0 = x_row0 // tb, row0 // tb
    row = lambda w: pl.BlockSpec((tb, w), lambda i: (i, 0))
    src = lambda w: pl.BlockSpec((tb, w), lambda i: (i + b0, 0))
    return pl.pallas_call(
        _merge_kernel,
        grid=(n // tb,),
        in_specs=[pl.BlockSpec((tb, D_MODEL), lambda i: (i + xb0, 0)), src(D_ATTN), src(D_CONV),
                  _full((1, D_ATTN)), _full((1, D_CONV)),
                  _full((D_ATTN, D_MODEL)), _full((D_CONV, D_MODEL)), _full((1, D_MODEL))],
        out_specs=[row(D_MODEL), row(D_MODEL)],
        out_shape=[jax.ShapeDtypeStruct((n, D_MODEL), F32)] * 2,
        compiler_params=_tc_params(1),
        name="merge",
    )(x, oa, oc, g_a, g_c, w_out_a, w_out_c, g_f)


ID_BIG = 1e9


def _topk_rows(s, k):
    rows = lax.broadcasted_iota(jnp.int32, s.shape, 0).astype(F32)
    vals, idxs = [], []
    for _ in range(k):
        m = jnp.max(s, axis=0, keepdims=True)
        idx = jnp.min(jnp.where(s == m, rows, ID_BIG), axis=0, keepdims=True)
        vals.append(m)
        idxs.append(idx)
        s = jnp.where(rows == idx, NEG_INF, s)
    return jnp.concatenate(vals, axis=0), jnp.concatenate(idxs, axis=0)


PAIR_B_WIDE = 8


def _route_kernel(hn_ref, wq_ref, k1_ref, k2_ref, eid_out, gate_out):
    tb = hn_ref.shape[0]
    q = jnp.dot(hn_ref[...].astype(BF16), wq_ref[...], preferred_element_type=F32).astype(BF16)
    k1 = k1_ref[...]
    k2 = k2_ref[...]
    nt = (((1,), (1,)), ((), ()))
    r = lax.broadcasted_iota(jnp.int32, (PEER_TOPK + (PAIR_B_WIDE - 1) * PAIR_B_WIDE + PAIR_B_WIDE, tb), 0)
    mid = r - PEER_TOPK
    flat = jnp.where(r < PEER_TOPK, r,
                     jnp.where(mid < (PAIR_B_WIDE - 1) * PAIR_B_WIDE,
                               (1 + mid // PAIR_B_WIDE) * PEER_TOPK + mid % PAIR_B_WIDE,
                               (PAIR_B_WIDE + mid - (PAIR_B_WIDE - 1) * PAIR_B_WIDE) * PEER_TOPK)).astype(F32)
    half = PEER_NKEYS
    for h in range(PEER_HEADS):
        q1 = q[:, (2 * h) * half:(2 * h + 1) * half]
        q2 = q[:, (2 * h + 1) * half:(2 * h + 2) * half]
        s1 = lax.dot_general(k1, q1, nt, preferred_element_type=F32)
        s2 = lax.dot_general(k2, q2, nt, preferred_element_type=F32)
        v1, i1 = _topk_rows(s1, PEER_TOPK)
        v2, i2 = _topk_rows(s2, PEER_TOPK)
        e1 = i1 * PEER_NKEYS
        cand = jnp.concatenate(
            [v1[0:1] + v2]
            + [v1[a:a + 1] + v2[0:PAIR_B_WIDE] for a in range(1, PAIR_B_WIDE)]
            + [v1[PAIR_B_WIDE:] + v2[0:1]], axis=0)
        cid = jnp.concatenate(
            [e1[0:1] + i2]
            + [e1[a:a + 1] + i2[0:PAIR_B_WIDE] for a in range(1, PAIR_B_WIDE)]
            + [e1[PAIR_B_WIDE:] + i2[0:1]], axis=0)
        scs, eids = [], []
        for _ in range(PEER_TOPK):
            m = jnp.max(cand, axis=0, keepdims=True)
            jsel = jnp.min(jnp.where(cand == m, flat, ID_BIG), axis=0, keepdims=True)
            hit = flat == jsel
            eids.append(jnp.max(jnp.where(hit, cid, -1.0), axis=0, keepdims=True))
            scs.append(m)
            cand = jnp.where(hit, NEG_INF, cand)
        sc = jnp.concatenate(scs, axis=0)
        e = jnp.exp(sc - sc[0:1])
        gate_out[h * PEER_TOPK:(h + 1) * PEER_TOPK, :] = e / jnp.sum(e, axis=0, keepdims=True)
        eid_out[h * PEER_TOPK:(h + 1) * PEER_TOPK, :] = jnp.concatenate(eids, axis=0).astype(jnp.int32)


def _route(hn, tb, wq_bf, keys1_bf, keys2_bf):
    n = hn.shape[0]
    col = pl.BlockSpec((PEER_SEL, tb), lambda i: (0, i))
    return pl.pallas_call(
        _route_kernel,
        grid=(n // tb,),
        in_specs=[pl.BlockSpec((tb, D_MODEL), lambda i: (i, 0)),
                  _full((D_MODEL, 2 * PEER_NKEYS * PEER_HEADS)),
                  _full((PEER_NKEYS, PEER_NKEYS)), _full((PEER_NKEYS, PEER_NKEYS))],
        out_specs=[col, col],
        out_shape=[jax.ShapeDtypeStruct((PEER_SEL, n), jnp.int32),
                   jax.ShapeDtypeStruct((PEER_SEL, n), F32)],
        compiler_params=_tc_params(1),
        name="route",
    )(hn, wq_bf, keys1_bf, keys2_bf)


SC_ROWS = 32
SC_GROUP = SC_LANES
SC_TOKENS = 8
SC_CHUNKS = PEER_SEL // SC_ROWS

_SC_PARAMS = pltpu.CompilerParams(needs_layout_passes=False)


def _sc_mesh():
    return plsc.VectorSubcoreMesh(core_axis_name="c", subcore_axis_name="s",
                                  num_cores=SC_CORES, num_subcores=SC_SUBCORES)


def _sc_token_loop(n_batches, table_hbm, eid_v, rbufs, rsems, prefetch, prefetch_wait, store,
                   contract):
    def gather(slot, tl, ch, p):
        idx = eid_v.at[slot, tl, pl.ds(ch * SC_ROWS, SC_ROWS)]
        return pltpu.make_async_copy(table_hbm.at[idx], rbufs[p], rsems[p])

    prefetch(0, 0)
    prefetch_wait()
    gather(0, 0, 0, 0).start()

    @pl.loop(0, n_batches * SC_TOKENS)
    def _(i):
        b = i // SC_TOKENS
        tl = i % SC_TOKENS
        slot = b % 2
        more = b + 1 < n_batches

        @pl.when(jnp.logical_and(tl == 0, more))
        def _():
            prefetch(b + 1, 1 - slot)

        for ch in range(SC_CHUNKS):
            p = ch % 2
            if ch + 1 < SC_CHUNKS:
                gather(slot, tl, ch + 1, 1 - p).start()
            else:
                @pl.when(tl + 1 < SC_TOKENS)
                def _():
                    gather(slot, tl + 1, 0, 1 - p).start()

                @pl.when(jnp.logical_and(tl + 1 == SC_TOKENS, more))
                def _():
                    prefetch_wait()
                    gather(1 - slot, 0, 0, 1 - p).start()
            gather(slot, tl, ch, p).wait()
            contract(slot, tl, ch, rbufs[p])

        @pl.when(tl + 1 == SC_TOKENS)
        def _():
            @pl.when(b >= 1)
            def _():
                store(b - 1, 1 - slot).wait()
            store(b, slot).start()

    store(n_batches - 1, (n_batches - 1) % 2).wait()


def _sc_batches(n):
    assert n % (SC_WORKERS * SC_TOKENS) == 0, n
    return n // (SC_WORKERS * SC_TOKENS)


def _peer_hval(hn, eid, peer_u, after):
    n = hn.shape[0]
    nbw = _sc_batches(n)

    @functools.partial(
        pl.kernel, mesh=_sc_mesh(),
        out_type=jax.ShapeDtypeStruct((n // SC_TOKENS, SC_TOKENS, PEER_SEL), F32),
        scratch_types=[
            pltpu.VMEM((2, SC_TOKENS, PEER_SEL), jnp.int32),
            pltpu.VMEM((2, SC_TOKENS, D_MODEL), F32),
            pltpu.VMEM((SC_ROWS, D_MODEL), F32),
            pltpu.VMEM((SC_ROWS, D_MODEL), F32),
            pltpu.VMEM((2, SC_TOKENS, PEER_SEL), F32),
            pltpu.VMEM((SC_GROUP, SC_LANES), F32),
            pltpu.SemaphoreType.DMA, pltpu.SemaphoreType.DMA,
            pltpu.SemaphoreType.DMA, pltpu.SemaphoreType.DMA,
        ],
        compiler_params=_SC_PARAMS, name="peer_hval",
        cost_estimate=pl.CostEstimate(flops=2 * n * PEER_SEL * D_MODEL, transcendentals=0,
                                      bytes_accessed=n * PEER_SEL * D_MODEL * 4))
    def k(x_hbm, eid_hbm, u_hbm, after_hbm, o_hbm, eid_v, x_v, r0, r1, h_v, tr, sr0, sr1, spf, sout):
        del after_hbm
        blk0 = (lax.axis_index("s") * SC_CORES + lax.axis_index("c")) * nbw
        lane = lax.iota(jnp.int32, SC_LANES)
        zero = jnp.zeros((SC_LANES,), F32)

        def prefetch(b, slot):
            pltpu.async_copy(eid_hbm.at[blk0 + b], eid_v.at[slot], spf)
            pltpu.async_copy(x_hbm.at[blk0 + b], x_v.at[slot], spf)

        def prefetch_wait():
            pltpu.make_async_copy(eid_hbm.at[0], eid_v.at[0], spf).wait()
            pltpu.make_async_copy(x_hbm.at[0], x_v.at[0], spf).wait()

        def store(b, slot):
            return pltpu.make_async_copy(h_v.at[slot], o_hbm.at[blk0 + b], sout)

        def contract(slot, tl, ch, rbuf):
            for g in range(SC_ROWS // SC_GROUP):
                def body(c, accs):
                    xc = x_v[slot, tl, pl.ds(c * SC_LANES, SC_LANES)]
                    return tuple(
                        accs[r] + rbuf[g * SC_GROUP + r, pl.ds(c * SC_LANES, SC_LANES)] * xc
                        for r in range(SC_GROUP))
                accs = lax.fori_loop(0, D_MODEL // SC_LANES, body, (zero,) * SC_GROUP)
                for r in range(SC_GROUP):
                    tr[r, :] = accs[r]
                res = zero
                for jj in range(SC_LANES):
                    res = res + plsc.load_gather(tr, [lane, jnp.full((SC_LANES,), jj, jnp.int32)])
                h_v[slot, tl, pl.ds(ch * SC_ROWS + g * SC_GROUP, SC_GROUP)] = res

        _sc_token_loop(nbw, u_hbm, eid_v, (r0, r1), (sr0, sr1), prefetch, prefetch_wait, store,
                       contract)

    out = k(hn.reshape(n // SC_TOKENS, SC_TOKENS, D_MODEL),
            eid.reshape(n // SC_TOKENS, SC_TOKENS, PEER_SEL), peer_u, after)
    return out.reshape(n, PEER_SEL)


def _gate_kernel(hv_ref, gate_ref, a_out):
    hv = hv_ref[...]
    gelu = hv * (lax.erf(hv * (2.0 ** -0.5)) + 1.0) * 0.5
    a_out[...] = gate_ref[...] * gelu


def _gate(hval, gate, tb):
    n = hval.shape[0]
    row = pl.BlockSpec((tb, PEER_SEL), lambda i: (i, 0))
    return pl.pallas_call(
        _gate_kernel, grid=(n // tb,), in_specs=[row, row], out_specs=row,
        out_shape=jax.ShapeDtypeStruct((n, PEER_SEL), F32),
        compiler_params=_tc_params(1), name="gate",
    )(hval, gate)


VT_ROWS = D_MODEL // 2 // LANES
VT_TOKENS = 128
HI_MASK = -65536


def _pack_value_table(peer_v):
    e = peer_v.shape[0]
    bits = lax.bitcast_convert_type(peer_v.astype(BF16), jnp.uint16).astype(jnp.uint32)
    words = bits[:, :D_MODEL // 2] | (bits[:, D_MODEL // 2:] << 16)
    return lax.bitcast_convert_type(words, jnp.int32).reshape(e * VT_ROWS, LANES)


def _vside_kernel(row_s, a_s, h_ref, tab_ref, y_ref):
    tb = h_ref.shape[0]

    def token_pair(p, carry):
        zero = jnp.zeros((VT_ROWS, LANES), F32)
        ts = (2 * p, 2 * p + 1)
        lo = [zero, zero]
        hi = [zero, zero]
        for e in range(PEER_SEL):
            for k, t in enumerate(ts):
                r0 = pl.multiple_of(row_s[t, e], VT_ROWS)
                w = tab_ref[pl.ds(r0, VT_ROWS), :]
                coef = a_s[t, e]
                lo[k] = lo[k] + coef * pltpu.bitcast(w << 16, F32)
                hi[k] = hi[k] + coef * pltpu.bitcast(w & HI_MASK, F32)
        for k, t in enumerate(ts):
            y_ref[t] = h_ref[t] + jnp.concatenate([lo[k], hi[k]], axis=0)
        return carry

    lax.fori_loop(0, tb // 2, token_pair, 0)


def _peer_values(h, a, row_ids, v_packed):
    n = h.shape[0]
    tb = min(VT_TOKENS, n)
    smem = pl.BlockSpec((tb, PEER_SEL), lambda i: (i, 0), memory_space=pltpu.SMEM)
    slab = pl.BlockSpec((tb, 2 * VT_ROWS, LANES), lambda i: (i, 0, 0))
    table = pl.BlockSpec(v_packed.shape, lambda i: (0, 0), pipeline_mode=pl.Buffered(1))
    table_bytes = v_packed.shape[0] * LANES * 4
    y = pl.pallas_call(
        _vside_kernel,
        grid=(n // tb,),
        in_specs=[smem, smem, slab, table],
        out_specs=slab,
        out_shape=jax.ShapeDtypeStruct((n, 2 * VT_ROWS, LANES), F32),
        compiler_params=pltpu.CompilerParams(
            dimension_semantics=("arbitrary",),
            vmem_limit_bytes=table_bytes + 8 * 1024 * 1024),
        name="peer_values",
        cost_estimate=pl.CostEstimate(flops=2 * n * PEER_SEL * D_MODEL, transcendentals=0,
                                      bytes_accessed=table_bytes + n * PEER_SEL * D_MODEL * 2),
    )(row_ids, a, h.reshape(n, 2 * VT_ROWS, LANES), v_packed)
    return y.reshape(n, D_MODEL)


PEER_DEPTH = 2


def _chunk_plan(t, tb, first, last):
    eighth = t // 8
    if eighth % tb:
        return [t]
    sizes = [4 * eighth, 4 * eighth]
    if last:
        sizes = sizes[:1] + [2 * eighth, eighth, eighth]
    if first:
        sizes = [eighth, 3 * eighth] + sizes[1:]
    return sizes


def _peer_launch(h, hn, tb, after, wq_bf, keys1_bf, keys2_bf, peer_u):
    n = h.shape[0]
    eid_t, gate_t = _route(hn, tb, wq_bf, keys1_bf, keys2_bf)
    eid = eid_t.T
    pad = (-n) % (SC_WORKERS * SC_TOKENS)
    hn_p, eid_p = hn, eid
    if pad:
        spread = (jnp.arange(pad * PEER_SEL, dtype=jnp.int32) % peer_u.shape[0]).reshape(pad, PEER_SEL)
        hn_p = jnp.pad(hn, ((0, pad), (0, 0)))
        eid_p = jnp.concatenate([eid, spread], axis=0)
    hval = _peer_hval(hn_p, eid_p, peer_u, after)[:n]
    return h, eid, gate_t.T, hval


def _peer_finish(h, eid, gate, hval, tb, v_packed):
    a = _gate(hval, gate, tb)
    return _peer_values(h, a, eid * VT_ROWS, v_packed)


def _rope_tables(pos):
    half = HEAD_DIM // 2
    inv = ROPE_THETA ** (-jnp.arange(half, dtype=F32) / half)
    ang = pos.astype(F32)[:, None] * inv[None, :]
    cos = jnp.cos(ang)
    sin = jnp.sin(ang)
    reps = LANES // HEAD_DIM
    cos_f = jnp.tile(jnp.concatenate([cos, cos], axis=1), (1, reps))
    sin_s = jnp.tile(jnp.concatenate([-sin, sin], axis=1), (1, reps))
    return cos_f, sin_s


def _group_sum_matrix(width):
    g = jnp.arange(width) // HEAD_DIM
    return (g[:, None] == g[None, :]).astype(BF16)


def kernel(x_prompt, x_sample, cache_k_win, cache_v_win, state_conv, meta_tokens, norm_mix_g,
           w_in, q_norm_g, k_norm_g, attn_sinks, conv_w, conv_b, conv_ln_g, conv_ln_b, w_pw2,
           out_norm_attn_g, out_norm_conv_g, w_out, norm_ffn_g, peer_w_q, peer_keys1, peer_keys2,
           peer_u, peer_v):
    assert norm_mix_g.shape[0] == 1, "single-layer model"
    b, t, _ = x_prompt.shape
    s = x_sample.shape[0]
    w_buf = cache_k_win.shape[2]
    n = b * t

    g_mix = norm_mix_g[0][None, :]
    w_in_bf = w_in[0].astype(BF16)
    qg_t = jnp.tile(q_norm_g[0], N_HEADS)[None, :]
    kg_t = jnp.tile(k_norm_g[0], N_KV_HEADS)[None, :]
    gsum_q = _group_sum_matrix(D_ATTN)
    gsum_k = _group_sum_matrix(KV_DIM)
    sinks = attn_sinks[0]
    cw, cb = conv_w[0], conv_b[0][None, :]
    lg, lb = conv_ln_g[0][None, :], conv_ln_b[0][None, :]
    w2_bf = w_pw2[0].astype(BF16)
    g_a, g_c = out_norm_attn_g[0][None, :], out_norm_conv_g[0][None, :]
    w_out_a = w_out[0][:D_ATTN].astype(BF16)
    w_out_c = w_out[0][D_ATTN:].astype(BF16)
    g_f = norm_ffn_g[0][None, :]
    wq_bf = peer_w_q[0].astype(BF16)
    k1_bf = peer_keys1[0].astype(BF16)
    k2_bf = peer_keys2[0].astype(BF16)
    pu, vp = peer_u[0], _pack_value_table(peer_v[0])
    proj = functools.partial(_project, norm_g=g_mix, w_in_bf=w_in_bf, qg_t=qg_t, kg_t=kg_t,
                             gsum_q=gsum_q, gsum_k=gsum_k)
    conv = functools.partial(_conv_branch, conv_w=cw, conv_b=cb, ln_g=lg, ln_b=lb, w_pw2_bf=w2_bf)
    merge = functools.partial(_merge, g_a=g_a, g_c=g_c, w_out_a=w_out_a, w_out_c=w_out_c, g_f=g_f)

    tb = TB_DENSE
    no_dep = jnp.zeros((SC_TOKENS, LANES), F32)
    tab_x = _rope_tables(N_META + jnp.arange(t, dtype=jnp.int32))
    tab_m = _rope_tables(jnp.arange(N_META, dtype=jnp.int32))
    _, k_m, v_m, u_m = proj(meta_tokens, tab_m, 1, N_META, no_dep)
    halo0 = jnp.concatenate([jnp.zeros((CONV_HALO - N_META, D_CONV), F32), u_m], axis=0)

    k_meta = jnp.concatenate([jnp.zeros((META_PAD, KV_DIM), F32), k_m], axis=0)
    v_meta = jnp.concatenate([jnp.zeros((META_PAD, KV_DIM), F32), v_m], axis=0)

    xp = x_prompt.reshape(b * t, D_MODEL)

    def dense(bi, after):
        q, k, v, u = proj(xp, tab_x, t // tb, tb, after, row0=bi * t, nrows=t)
        o_attn = _prompt_attention(sinks, q, k, v, k_meta, v_meta)
        u3 = u.reshape(t // tb, tb, D_CONV)
        halo = jnp.concatenate([halo0[None], u3[:-1, tb - CONV_HALO:]], axis=0)
        o_conv = conv(halo, u, tb)
        hs, r0 = [], 0
        for rows in _chunk_plan(t, tb, bi == 0, bi == b - 1):
            hs.append(merge(xp, o_attn, o_conv, tb, x_row0=bi * t + r0, row0=r0, nrows=rows))
            r0 += rows
        return hs, k, v, u

    launched, ys, kvu = [], [], []
    cur = dense(0, no_dep)
    for bi in range(b):
        hs, k, v, u = cur
        kvu.append((k, v, u))
        for h, hn in hs:
            c = len(launched)
            if c >= PEER_DEPTH:
                ys.append(_peer_finish(*launched[c - PEER_DEPTH], TB_ROUTE, vp))
            after = ys[c - PEER_DEPTH][:SC_TOKENS] if c >= PEER_DEPTH else jnp.zeros((SC_TOKENS, D_MODEL), F32)
            launched.append(_peer_launch(h, hn, TB_ROUTE, after, wq_bf, k1_bf, k2_bf, pu))
        if bi + 1 < b:
            cur = dense(bi + 1, launched[-1][1][:SC_TOKENS, :LANES].astype(F32))
    for c in range(len(ys), len(launched)):
        ys.append(_peer_finish(*launched[c], TB_ROUTE, vp))
    y_prompt = jnp.concatenate(ys, axis=0).reshape(b, t, D_MODEL)

    new_k_prompt = jnp.stack([k[t - WINDOW:] for k, _, _ in kvu]).reshape(1, b, WINDOW, N_KV_HEADS, HEAD_DIM)
    new_v_prompt = jnp.stack([v[t - WINDOW:] for _, v, _ in kvu]).reshape(1, b, WINDOW, N_KV_HEADS, HEAD_DIM)
    new_conv_prompt = jnp.stack([u[t - (CONV_WIDTH - 1):] for _, _, u in kvu])[None]

    xs = x_sample.reshape(s, D_MODEL)
    tab_s = _rope_tables(jnp.full((s,), PAST_LEN, jnp.int32))
    qs, ks, vs, us = proj(xs, tab_s, 1, s, no_dep)
    ck = cache_k_win[0]
    cv = cache_v_win[0]
    o_attn_s = _decode_attention(
        sinks,
        qs.reshape(s, N_HEADS, HEAD_DIM), ks, vs,
        ck.reshape(s, w_buf, KV_DIM), cv.reshape(s, w_buf, KV_DIM), min(SB_DECODE, s))
    o_attn_s = o_attn_s.reshape(s, D_ATTN)
    cs = state_conv[0]
    hist = jnp.concatenate([jnp.zeros((s, CONV_HALO - (CONV_WIDTH - 1), D_CONV), F32), cs], axis=1)
    us_blk = jnp.concatenate([us[:, None, :], jnp.zeros((s, SUBLANES - 1, D_CONV), F32)], axis=1)
    o_conv_s = conv(hist, us_blk.reshape(s * SUBLANES, D_CONV), SUBLANES)
    o_conv_s = o_conv_s.reshape(s, SUBLANES, D_CONV)[:, 0]
    hs, hns = merge(xs, o_attn_s, o_conv_s, s)
    launch_s = _peer_launch(hs, hns, s, jnp.zeros((SC_TOKENS, D_MODEL), F32), wq_bf, k1_bf, k2_bf, pu)
    y_sample = _peer_finish(*launch_s, s, vp).reshape(s, 1, D_MODEL)

    new_k_sample = jnp.concatenate([ck[:, 1:], ks.reshape(s, 1, N_KV_HEADS, HEAD_DIM)], axis=1)[None]
    new_v_sample = jnp.concatenate([cv[:, 1:], vs.reshape(s, 1, N_KV_HEADS, HEAD_DIM)], axis=1)[None]
    new_conv_sample = jnp.concatenate([cs[:, 1:], us[:, None, :]], axis=1)[None]
    if w_buf != WINDOW:
        raise NotImplementedError("cache window shorter than the attention window")

    return (y_prompt, y_sample, new_k_prompt, new_v_prompt, new_conv_prompt,
            new_k_sample, new_v_sample, new_conv_sample)
```

```python
import functools

import jax
import jax.numpy as jnp
from jax import lax
from jax.experimental import pallas as pl
from jax.experimental.pallas import tpu as pltpu
from jax.experimental.pallas import tpu_sc as plsc

D_MODEL = 1024
HEAD_DIM = 64
D_ATTN = 512
N_HEADS = 8
N_KV_HEADS = 2
KV_DIM = N_KV_HEADS * HEAD_DIM
D_CONV = 512
D_IN = D_ATTN + 2 * KV_DIM + 2 * D_CONV
CONV_WIDTH = 31
WINDOW = 128
BLOCK = 128
ROPE_THETA = 10000.0
N_META = 16
META_PAD = BLOCK - N_META
PEER_HEADS = 8
PEER_NKEYS = 128
PEER_TOPK = 16
PEER_SEL = PEER_HEADS * PEER_TOPK
EPS = 1e-6
PAST_LEN = 16384

LANES = 128
SC_CORES = 2
SC_SUBCORES = 16
SC_LANES = 16
SC_WORKERS = SC_CORES * SC_SUBCORES
VMEM_LIMIT = 48 * 1024 * 1024
SUBLANES = 8

TB_DENSE = 512
TB_ROUTE = 256
SB_DECODE = 32

F32 = jnp.float32
BF16 = jnp.bfloat16
NEG_INF = float("-inf")


def _tc_params(n_axes):
    return pltpu.CompilerParams(dimension_semantics=("arbitrary",) * n_axes,
                                vmem_limit_bytes=VMEM_LIMIT)


def _full(shape):
    nd = len(shape)
    return pl.BlockSpec(shape, lambda *_: (0,) * nd)


def _group_mean(sq, gsum_ref):
    hi = sq.astype(BF16)
    lo = (sq - hi.astype(F32)).astype(BF16)
    g = gsum_ref[...]
    s = jnp.dot(hi, g, preferred_element_type=F32) + jnp.dot(lo, g, preferred_element_type=F32)
    return s * (1.0 / HEAD_DIM)


def _rope(xn, cos_f, sin_s, first_half):
    outs = []
    for s in range(xn.shape[1] // LANES):
        xs = xn[:, s * LANES:(s + 1) * LANES]
        partner = jnp.where(first_half, pltpu.roll(xs, LANES - HEAD_DIM // 2, axis=1),
                            pltpu.roll(xs, HEAD_DIM // 2, axis=1))
        outs.append(xs * cos_f + partner * sin_s)
    return outs[0] if len(outs) == 1 else jnp.concatenate(outs, axis=1)


def _proj_kernel(x_ref, g_ref, w_ref, qg_ref, kg_ref, cos_ref, sin_ref, gq_ref, gk_ref, after_ref,
                 q_out, k_out, v_out, u_out):
    del after_ref
    x = x_ref[...]
    n = x * lax.rsqrt(jnp.mean(x * x, axis=-1, keepdims=True) + EPS) * g_ref[...]
    p = jnp.dot(n.astype(BF16), w_ref[...], preferred_element_type=F32)
    q = p[:, :D_ATTN]
    k = p[:, D_ATTN:D_ATTN + KV_DIM]
    v = p[:, D_ATTN + KV_DIM:D_ATTN + 2 * KV_DIM]
    ga = p[:, D_ATTN + 2 * KV_DIM:D_ATTN + 2 * KV_DIM + D_CONV]
    gb = p[:, D_ATTN + 2 * KV_DIM + D_CONV:]
    cos_f = cos_ref[...]
    sin_s = sin_ref[...]
    lane = lax.broadcasted_iota(jnp.int32, (x.shape[0], LANES), 1)
    first_half = (lane % HEAD_DIM) < (HEAD_DIM // 2)
    qn = q * lax.rsqrt(_group_mean(q * q, gq_ref) + EPS) * qg_ref[...]
    kn = k * lax.rsqrt(_group_mean(k * k, gk_ref) + EPS) * kg_ref[...]
    q_out[...] = _rope(qn, cos_f, sin_s, first_half)
    k_out[...] = _rope(kn, cos_f, sin_s, first_half)
    v_out[...] = v
    u_out[...] = ga * (1.0 / (1.0 + jnp.exp(-gb)))


def _project(x, pos_tables, n_table_blocks, tb, after, norm_g, w_in_bf, qg_t, kg_t, gsum_q, gsum_k,
             row0=0, nrows=None):
    n = x.shape[0] if nrows is None else nrows
    cos_t, sin_t = pos_tables
    nb = n // tb
    blk0 = row0 // tb
    tab_spec = pl.BlockSpec((tb, LANES), lambda i: (i % n_table_blocks, 0))
    row = lambda w: pl.BlockSpec((tb, w), lambda i: (i, 0))
    return pl.pallas_call(
        _proj_kernel,
        grid=(nb,),
        in_specs=[pl.BlockSpec((tb, D_MODEL), lambda i: (i + blk0, 0)),
                  _full((1, D_MODEL)), _full((D_MODEL, D_IN)),
                  _full((1, D_ATTN)), _full((1, KV_DIM)), tab_spec, tab_spec,
                  _full((D_ATTN, D_ATTN)), _full((KV_DIM, KV_DIM)),
                  pl.BlockSpec(memory_space=pl.ANY)],
        out_specs=[row(D_ATTN), row(KV_DIM), row(KV_DIM), row(D_CONV)],
        out_shape=[jax.ShapeDtypeStruct((n, D_ATTN), F32), jax.ShapeDtypeStruct((n, KV_DIM), F32),
                   jax.ShapeDtypeStruct((n, KV_DIM), F32), jax.ShapeDtypeStruct((n, D_CONV), F32)],
        compiler_params=_tc_params(1),
        name="proj",
    )(x, norm_g, w_in_bf, qg_t, kg_t, cos_t, sin_t, gsum_q, gsum_k, after)


def _attn_kernel(sink_ref, q_ref, kp_ref, kc_ref, vp_ref, vc_ref, km_ref, vm_ref, o_ref):
    j = pl.program_id(0)
    r = lax.broadcasted_iota(jnp.int32, (BLOCK, 2 * BLOCK), 0)
    c = lax.broadcasted_iota(jnp.int32, (BLOCK, 2 * BLOCK), 1)
    ok = (c > r) & (c <= r + WINDOW) & ((j > 0) | (c >= META_PAD))
    grp = N_HEADS // N_KV_HEADS
    ok = jnp.concatenate([ok] * grp, axis=0)
    first = j == 0
    k_all = jnp.concatenate([jnp.where(first, km_ref[...], kp_ref[...]), kc_ref[...]], axis=0)
    v_all = jnp.concatenate([jnp.where(first, vm_ref[...], vp_ref[...]), vc_ref[...]], axis=0)
    q = q_ref[...]
    outs = []
    for g in range(N_KV_HEADS):
        lanes = slice(g * HEAD_DIM, (g + 1) * HEAD_DIM)
        k = k_all[:, lanes].astype(BF16)
        v = v_all[:, lanes].astype(BF16)
        q4 = jnp.concatenate([q[:, (g * grp + i) * HEAD_DIM:(g * grp + i + 1) * HEAD_DIM]
                              for i in range(grp)], axis=0).astype(BF16)
        s = lax.dot_general(q4, k, (((1,), (1,)), ((), ())), preferred_element_type=F32)
        s = jnp.where(ok, s * (HEAD_DIM ** -0.5), NEG_INF)
        sink = jnp.concatenate(
            [jnp.full((BLOCK, 1), sink_ref[g * grp + i], F32) for i in range(grp)], axis=0)
        m = jnp.maximum(jnp.max(s, axis=1, keepdims=True), sink)
        p = jnp.exp(s - m)
        den = jnp.sum(p, axis=1, keepdims=True) + jnp.exp(sink - m)
        o = jnp.dot(p.astype(BF16), v, preferred_element_type=F32) / den
        outs += [o[i * BLOCK:(i + 1) * BLOCK] for i in range(grp)]
    o_ref[...] = jnp.concatenate(outs, axis=1)


def _prompt_attention(sinks, q, k, v, k_meta, v_meta):
    t = q.shape[0]
    kv_prev = pl.BlockSpec((BLOCK, KV_DIM), lambda j: (jnp.maximum(j - 1, 0), 0))
    kv_cur = pl.BlockSpec((BLOCK, KV_DIM), lambda j: (j, 0))
    q_spec = pl.BlockSpec((BLOCK, D_ATTN), lambda j: (j, 0))
    meta = _full((BLOCK, KV_DIM))
    return pl.pallas_call(
        _attn_kernel,
        grid=(t // BLOCK,),
        in_specs=[pl.BlockSpec(memory_space=pltpu.SMEM), q_spec, kv_prev, kv_cur, kv_prev, kv_cur,
                  meta, meta],
        out_specs=q_spec,
        out_shape=jax.ShapeDtypeStruct((t, D_ATTN), F32),
        compiler_params=_tc_params(1),
        name="attn",
    )(sinks, q, k, k, v, v, k_meta, v_meta)


def _dec_attn_kernel(sink_ref, q_ref, kn_ref, vn_ref, ck_ref, cv_ref, o_ref):
    grp = N_HEADS // N_KV_HEADS
    sb, w_buf = ck_ref.shape[0], ck_ref.shape[1]
    q = q_ref[...]
    qb = q.astype(BF16)
    head = lax.broadcasted_iota(jnp.int32, (sb, N_HEADS, 1), 1)
    in_g0 = head < grp
    kv_lanes = [slice(g * HEAD_DIM, (g + 1) * HEAD_DIM) for g in range(N_KV_HEADS)]
    ck = ck_ref[...]
    cv = cv_ref[...]
    s_g = [jnp.einsum("shd,swd->shw", qb, ck[:, :, ln].astype(BF16), preferred_element_type=F32)
           for ln in kv_lanes]
    s = jnp.where(in_g0, s_g[0], s_g[1]) * (HEAD_DIM ** -0.5)
    key_ok = lax.broadcasted_iota(jnp.int32, (sb, N_HEADS, w_buf), 2) >= 1
    s = jnp.where(key_ok, s, NEG_INF)
    rnd = lambda a: a.astype(BF16).astype(F32)
    kn_all, vn_all = kn_ref[...], vn_ref[...]
    kn = jnp.where(in_g0, kn_all[:, None, kv_lanes[0]], kn_all[:, None, kv_lanes[1]])
    vn = jnp.where(in_g0, vn_all[:, None, kv_lanes[0]], vn_all[:, None, kv_lanes[1]])
    s_self = jnp.sum(rnd(q) * rnd(kn), axis=-1, keepdims=True) * (HEAD_DIM ** -0.5)
    sink = sink_ref[...]
    m = jnp.maximum(jnp.maximum(jnp.max(s, axis=-1, keepdims=True), s_self), sink)
    p = jnp.exp(s - m)
    p_self = jnp.exp(s_self - m)
    den = jnp.sum(p, axis=-1, keepdims=True) + p_self + jnp.exp(sink - m)
    pb = p.astype(BF16)
    o_g = [jnp.einsum("shw,swd->shd", pb, cv[:, :, ln].astype(BF16), preferred_element_type=F32)
           for ln in kv_lanes]
    o = jnp.where(in_g0, o_g[0], o_g[1]) + rnd(p_self) * rnd(vn)
    o_ref[...] = o / den


def _decode_attention(sinks, q3, kn_t, vn_t, ck_t, cv_t, sb):
    s = q3.shape[0]
    w_buf = ck_t.shape[1]
    qs = pl.BlockSpec((sb, N_HEADS, HEAD_DIM), lambda i: (i, 0, 0))
    ns = pl.BlockSpec((sb, KV_DIM), lambda i: (i, 0))
    cs = pl.BlockSpec((sb, w_buf, KV_DIM), lambda i: (i, 0, 0))
    return pl.pallas_call(
        _dec_attn_kernel,
        grid=(s // sb,),
        in_specs=[_full((1, N_HEADS, 1)), qs, ns, ns, cs, cs],
        out_specs=qs,
        out_shape=jax.ShapeDtypeStruct(q3.shape, F32),
        compiler_params=_tc_params(1),
        name="dec_attn",
    )(sinks.reshape(1, N_HEADS, 1), q3, kn_t, vn_t, ck_t, cv_t)


CONV_HALO = 32
CONV_ROWS = 64


def _conv_kernel(halo_ref, u_ref, cw_ref, cb_ref, lg_ref, lb_ref, w2_ref, o_ref, ucat):
    tb = u_ref.shape[0]
    ucat[0:CONV_HALO, :] = halo_ref[0]
    ucat[CONV_HALO:, :] = u_ref[...]
    first = CONV_HALO - (CONV_WIDTH - 1)
    rows = min(CONV_ROWS, tb)
    for r0 in range(0, tb, rows):
        acc = jnp.zeros((rows, D_CONV), F32)
        for j in range(CONV_WIDTH):
            acc = acc + ucat[r0 + first + j:r0 + first + j + rows, :] * cw_ref[j:j + 1, :]
        y = acc + cb_ref[...]
        yc = y - jnp.mean(y, axis=-1, keepdims=True)
        yn = yc * lax.rsqrt(jnp.mean(yc * yc, axis=-1, keepdims=True) + EPS)
        yn = yn * lg_ref[...] + lb_ref[...]
        act = yn * (1.0 / (1.0 + jnp.exp(-yn)))
        o_ref[r0:r0 + rows, :] = jnp.dot(act.astype(BF16), w2_ref[...],
                                              preferred_element_type=F32)


def _conv_branch(halo, u, tb, conv_w, conv_b, ln_g, ln_b, w_pw2_bf):
    n = u.shape[0]
    return pl.pallas_call(
        _conv_kernel,
        grid=(n // tb,),
        in_specs=[pl.BlockSpec((1, CONV_HALO, D_CONV), lambda i: (i, 0, 0)),
                  pl.BlockSpec((tb, D_CONV), lambda i: (i, 0)),
                  _full((CONV_WIDTH, D_CONV)), _full((1, D_CONV)), _full((1, D_CONV)),
                  _full((1, D_CONV)), _full((D_CONV, D_CONV))],
        out_specs=pl.BlockSpec((tb, D_CONV), lambda i: (i, 0)),
        out_shape=jax.ShapeDtypeStruct((n, D_CONV), F32),
        scratch_shapes=[pltpu.VMEM((tb + CONV_HALO, D_CONV), F32)],
        compiler_params=_tc_params(1),
        name="conv",
    )(halo, u, conv_w, conv_b, ln_g, ln_b, w_pw2_bf)


def _rms(x, g):
    return x * lax.rsqrt(jnp.mean(x * x, axis=-1, keepdims=True) + EPS) * g


def _merge_kernel(x_ref, oa_ref, oc_ref, ga_ref, gc_ref, wa_ref, wc_ref, gf_ref, h_out, hn_out):
    a = _rms(oa_ref[...], ga_ref[...]).astype(BF16)
    c = _rms(oc_ref[...], gc_ref[...]).astype(BF16)
    h = x_ref[...] + (jnp.dot(a, wa_ref[...], preferred_element_type=F32)
                      + jnp.dot(c, wc_ref[...], preferred_element_type=F32))
    h_out[...] = h
    hn_out[...] = _rms(h, gf_ref[...])


def _merge(x, oa, oc, tb, g_a, g_c, w_out_a, w_out_c, g_f, x_row0=0, row0=0, nrows=None):
    n = oa.shape[0] if nrows is None else nrows
    xb0, b0 = x_row0 // tb, row0 // tb
    row = lambda w: pl.BlockSpec((tb, w), lambda i: (i, 0))
    src = lambda w: pl.BlockSpec((tb, w), lambda i: (i + b0, 0))
    return pl.pallas_call(
        _merge_kernel,
        grid=(n // tb,),
        in_specs=[pl.BlockSpec((tb, D_MODEL), lambda i: (i + xb0, 0)), src(D_ATTN), src(D_CONV),
                  _full((1, D_ATTN)), _full((1, D_CONV)),
                  _full((D_ATTN, D_MODEL)), _full((D_CONV, D_MODEL)), _full((1, D_MODEL))],
        out_specs=[row(D_MODEL), row(D_MODEL)],
        out_shape=[jax.ShapeDtypeStruct((n, D_MODEL), F32)] * 2,
        compiler_params=_tc_params(1),
        name="merge",
    )(x, oa, oc, g_a, g_c, w_out_a, w_out_c, g_f)


ID_BIG = 1e9


def _topk_rows(s, k):
    rows = lax.broadcasted_iota(jnp.int32, s.shape, 0).astype(F32)
    vals, idxs = [], []
    for _ in range(k):
        m = jnp.max(s, axis=0, keepdims=True)
        idx = jnp.min(jnp.where(s == m, rows, ID_BIG), axis=0, keepdims=True)
        vals.append(m)
        idxs.append(idx)
        s = jnp.where(rows == idx, NEG_INF, s)
    return jnp.concatenate(vals, axis=0), jnp.concatenate(idxs, axis=0)


PAIR_B_WIDE = 8


def _route_kernel(hn_ref, wq_ref, k1_ref, k2_ref, eid_out, gate_out):
    tb = hn_ref.shape[0]
    q = jnp.dot(hn_ref[...].astype(BF16), wq_ref[...], preferred_element_type=F32).astype(BF16)
    k1 = k1_ref[...]
    k2 = k2_ref[...]
    nt = (((1,), (1,)), ((), ()))
    r = lax.broadcasted_iota(jnp.int32, (PEER_TOPK + (PAIR_B_WIDE - 1) * PAIR_B_WIDE + PAIR_B_WIDE, tb), 0)
    mid = r - PEER_TOPK
    flat = jnp.where(r < PEER_TOPK, r,
                     jnp.where(mid < (PAIR_B_WIDE - 1) * PAIR_B_WIDE,
                               (1 + mid // PAIR_B_WIDE) * PEER_TOPK + mid % PAIR_B_WIDE,
                               (PAIR_B_WIDE + mid - (PAIR_B_WIDE - 1) * PAIR_B_WIDE) * PEER_TOPK)).astype(F32)
    half = PEER_NKEYS
    for h in range(PEER_HEADS):
        q1 = q[:, (2 * h) * half:(2 * h + 1) * half]
        q2 = q[:, (2 * h + 1) * half:(2 * h + 2) * half]
        s1 = lax.dot_general(k1, q1, nt, preferred_element_type=F32)
        s2 = lax.dot_general(k2, q2, nt, preferred_element_type=F32)
        v1, i1 = _topk_rows(s1, PEER_TOPK)
        v2, i2 = _topk_rows(s2, PEER_TOPK)
        e1 = i1 * PEER_NKEYS
        cand = jnp.concatenate(
            [v1[0:1] + v2]
            + [v1[a:a + 1] + v2[0:PAIR_B_WIDE] for a in range(1, PAIR_B_WIDE)]
            + [v1[PAIR_B_WIDE:] + v2[0:1]], axis=0)
        cid = jnp.concatenate(
            [e1[0:1] + i2]
            + [e1[a:a + 1] + i2[0:PAIR_B_WIDE] for a in range(1, PAIR_B_WIDE)]
            + [e1[PAIR_B_WIDE:] + i2[0:1]], axis=0)
        scs, eids = [], []
        for _ in range(PEER_TOPK):
            m = jnp.max(cand, axis=0, keepdims=True)
            jsel = jnp.min(jnp.where(cand == m, flat, ID_BIG), axis=0, keepdims=True)
            hit = flat == jsel
            eids.append(jnp.max(jnp.where(hit, cid, -1.0), axis=0, keepdims=True))
            scs.append(m)
            cand = jnp.where(hit, NEG_INF, cand)
        sc = jnp.concatenate(scs, axis=0)
        e = jnp.exp(sc - sc[0:1])
        gate_out[h * PEER_TOPK:(h + 1) * PEER_TOPK, :] = e / jnp.sum(e, axis=0, keepdims=True)
        eid_out[h * PEER_TOPK:(h + 1) * PEER_TOPK, :] = jnp.concatenate(eids, axis=0).astype(jnp.int32)


def _route(hn, tb, wq_bf, keys1_bf, keys2_bf):
    n = hn.shape[0]
    col = pl.BlockSpec((PEER_SEL, tb), lambda i: (0, i))
    return pl.pallas_call(
        _route_kernel,
        grid=(n // tb,),
        in_specs=[pl.BlockSpec((tb, D_MODEL), lambda i: (i, 0)),
                  _full((D_MODEL, 2 * PEER_NKEYS * PEER_HEADS)),
                  _full((PEER_NKEYS, PEER_NKEYS)), _full((PEER_NKEYS, PEER_NKEYS))],
        out_specs=[col, col],
        out_shape=[jax.ShapeDtypeStruct((PEER_SEL, n), jnp.int32),
                   jax.ShapeDtypeStruct((PEER_SEL, n), F32)],
        compiler_params=_tc_params(1),
        name="route",
    )(hn, wq_bf, keys1_bf, keys2_bf)


SC_ROWS = 32
SC_GROUP = SC_LANES
SC_TOKENS = 8
SC_CHUNKS = PEER_SEL // SC_ROWS

_SC_PARAMS = pltpu.CompilerParams(needs_layout_passes=False)


def _sc_mesh():
    return plsc.VectorSubcoreMesh(core_axis_name="c", subcore_axis_name="s",
                                  num_cores=SC_CORES, num_subcores=SC_SUBCORES)


def _sc_token_loop(n_batches, table_hbm, eid_v, rbufs, rsems, prefetch, prefetch_wait, store,
                   contract):
    def gather(slot, tl, ch, p):
        idx = eid_v.at[slot, tl, pl.ds(ch * SC_ROWS, SC_ROWS)]
        return pltpu.make_async_copy(table_hbm.at[idx], rbufs[p], rsems[p])

    prefetch(0, 0)
    prefetch_wait()
    gather(0, 0, 0, 0).start()

    @pl.loop(0, n_batches * SC_TOKENS)
    def _(i):
        b = i // SC_TOKENS
        tl = i % SC_TOKENS
        slot = b % 2
        more = b + 1 < n_batches

        @pl.when(jnp.logical_and(tl == 0, more))
        def _():
            prefetch(b + 1, 1 - slot)

        for ch in range(SC_CHUNKS):
            p = ch % 2
            if ch + 1 < SC_CHUNKS:
                gather(slot, tl, ch + 1, 1 - p).start()
            else:
                @pl.when(tl + 1 < SC_TOKENS)
                def _():
                    gather(slot, tl + 1, 0, 1 - p).start()

                @pl.when(jnp.logical_and(tl + 1 == SC_TOKENS, more))
                def _():
                    prefetch_wait()
                    gather(1 - slot, 0, 0, 1 - p).start()
            gather(slot, tl, ch, p).wait()
            contract(slot, tl, ch, rbufs[p])

        @pl.when(tl + 1 == SC_TOKENS)
        def _():
            @pl.when(b >= 1)
            def _():
                store(b - 1, 1 - slot).wait()
            store(b, slot).start()

    store(n_batches - 1, (n_batches - 1) % 2).wait()


def _sc_batches(n):
    assert n % (SC_WORKERS * SC_TOKENS) == 0, n
    return n // (SC_WORKERS * SC_TOKENS)


def _peer_hval(hn, eid, peer_u, after):
    n = hn.shape[0]
    nbw = _sc_batches(n)

    @functools.partial(
        pl.kernel, mesh=_sc_mesh(),
        out_type=jax.ShapeDtypeStruct((n // SC_TOKENS, SC_TOKENS, PEER_SEL), F32),
        scratch_types=[
            pltpu.VMEM((2, SC_TOKENS, PEER_SEL), jnp.int32),
            pltpu.VMEM((2, SC_TOKENS, D_MODEL), F32),
            pltpu.VMEM((SC_ROWS, D_MODEL), F32),
            pltpu.VMEM((SC_ROWS, D_MODEL), F32),
            pltpu.VMEM((2, SC_TOKENS, PEER_SEL), F32),
            pltpu.VMEM((SC_GROUP, SC_LANES), F32),
            pltpu.SemaphoreType.DMA, pltpu.SemaphoreType.DMA,
            pltpu.SemaphoreType.DMA, pltpu.SemaphoreType.DMA,
        ],
        compiler_params=_SC_PARAMS, name="peer_hval",
        cost_estimate=pl.CostEstimate(flops=2 * n * PEER_SEL * D_MODEL, transcendentals=0,
                                      bytes_accessed=n * PEER_SEL * D_MODEL * 4))
    def k(x_hbm, eid_hbm, u_hbm, after_hbm, o_hbm, eid_v, x_v, r0, r1, h_v, tr, sr0, sr1, spf, sout):
        del after_hbm
        blk0 = (lax.axis_index("s") * SC_CORES + lax.axis_index("c")) * nbw
        lane = lax.iota(jnp.int32, SC_LANES)
        zero = jnp.zeros((SC_LANES,), F32)

        def prefetch(b, slot):
            pltpu.async_copy(eid_hbm.at[blk0 + b], eid_v.at[slot], spf)
            pltpu.async_copy(x_hbm.at[blk0 + b], x_v.at[slot], spf)

        def prefetch_wait():
            pltpu.make_async_copy(eid_hbm.at[0], eid_v.at[0], spf).wait()
            pltpu.make_async_copy(x_hbm.at[0], x_v.at[0], spf).wait()

        def store(b, slot):
            return pltpu.make_async_copy(h_v.at[slot], o_hbm.at[blk0 + b], sout)

        def contract(slot, tl, ch, rbuf):
            for g in range(SC_ROWS // SC_GROUP):
                def body(c, accs):
                    xc = x_v[slot, tl, pl.ds(c * SC_LANES, SC_LANES)]
                    return tuple(
                        accs[r] + rbuf[g * SC_GROUP + r, pl.ds(c * SC_LANES, SC_LANES)] * xc
                        for r in range(SC_GROUP))
                accs = lax.fori_loop(0, D_MODEL // SC_LANES, body, (zero,) * SC_GROUP)
                for r in range(SC_GROUP):
                    tr[r, :] = accs[r]
                res = zero
                for jj in range(SC_LANES):
                    res = res + plsc.load_gather(tr, [lane, jnp.full((SC_LANES,), jj, jnp.int32)])
                h_v[slot, tl, pl.ds(ch * SC_ROWS + g * SC_GROUP, SC_GROUP)] = res

        _sc_token_loop(nbw, u_hbm, eid_v, (r0, r1), (sr0, sr1), prefetch, prefetch_wait, store,
                       contract)

    out = k(hn.reshape(n // SC_TOKENS, SC_TOKENS, D_MODEL),
            eid.reshape(n // SC_TOKENS, SC_TOKENS, PEER_SEL), peer_u, after)
    return out.reshape(n, PEER_SEL)


def _gate_kernel(hv_ref, gate_ref, a_out):
    hv = hv_ref[...]
    gelu = hv * (lax.erf(hv * (2.0 ** -0.5)) + 1.0) * 0.5
    a_out[...] = gate_ref[...] * gelu


def _gate(hval, gate, tb):
    n = hval.shape[0]
    row = pl.BlockSpec((tb, PEER_SEL), lambda i: (i, 0))
    return pl.pallas_call(
        _gate_kernel, grid=(n // tb,), in_specs=[row, row], out_specs=row,
        out_shape=jax.ShapeDtypeStruct((n, PEER_SEL), F32),
        compiler_params=_tc_params(1), name="gate",
    )(hval, gate)


VT_ROWS = D_MODEL // 2 // LANES
VT_TOKENS = 128
VT_UNROLL = 4
HI_MASK = -65536


def _pack_value_table(peer_v):
    e = peer_v.shape[0]
    bits = lax.bitcast_convert_type(peer_v.astype(BF16), jnp.uint16).astype(jnp.uint32)
    words = bits[:, :D_MODEL // 2] | (bits[:, D_MODEL // 2:] << 16)
    return lax.bitcast_convert_type(words, jnp.int32).reshape(e * VT_ROWS, LANES)


def _vside_kernel(row_s, a_s, h_ref, tab_ref, y_ref):
    tb = h_ref.shape[0]

    def token_group(p, carry):
        zero = jnp.zeros((VT_ROWS, LANES), F32)
        ts = tuple(VT_UNROLL * p + k for k in range(VT_UNROLL))
        lo = [zero] * VT_UNROLL
        hi = [zero] * VT_UNROLL
        for e in range(PEER_SEL):
            for k, t in enumerate(ts):
                r0 = pl.multiple_of(row_s[t, e], VT_ROWS)
                w = tab_ref[pl.ds(r0, VT_ROWS), :]
                coef = a_s[t, e]
                lo[k] = lo[k] + coef * pltpu.bitcast(w << 16, F32)
                hi[k] = hi[k] + coef * pltpu.bitcast(w & HI_MASK, F32)
        for k, t in enumerate(ts):
            y_ref[t] = h_ref[t] + jnp.concatenate([lo[k], hi[k]], axis=0)
        return carry

    lax.fori_loop(0, tb // VT_UNROLL, token_group, 0)


def _peer_values(h, a, row_ids, v_packed):
    n = h.shape[0]
    tb = min(VT_TOKENS, n)
    smem = pl.BlockSpec((tb, PEER_SEL), lambda i: (i, 0), memory_space=pltpu.SMEM)
    slab = pl.BlockSpec((tb, 2 * VT_ROWS, LANES), lambda i: (i, 0, 0))
    table = pl.BlockSpec(v_packed.shape, lambda i: (0, 0), pipeline_mode=pl.Buffered(1))
    table_bytes = v_packed.shape[0] * LANES * 4
    y = pl.pallas_call(
        _vside_kernel,
        grid=(n // tb,),
        in_specs=[smem, smem, slab, table],
        out_specs=slab,
        out_shape=jax.ShapeDtypeStruct((n, 2 * VT_ROWS, LANES), F32),
        compiler_params=pltpu.CompilerParams(
            dimension_semantics=("arbitrary",),
            vmem_limit_bytes=table_bytes + 8 * 1024 * 1024),
        name="peer_values",
        cost_estimate=pl.CostEstimate(flops=2 * n * PEER_SEL * D_MODEL, transcendentals=0,
                                      bytes_accessed=table_bytes + n * PEER_SEL * D_MODEL * 2),
    )(row_ids, a, h.reshape(n, 2 * VT_ROWS, LANES), v_packed)
    return y.reshape(n, D_MODEL)


PEER_DEPTH = 2


def _chunk_plan(t, tb, first, last):
    eighth = t // 8
    if eighth % tb:
        return [t]
    sizes = [4 * eighth, 4 * eighth]
    if last:
        sizes = sizes[:1] + [2 * eighth, eighth, eighth]
    if first:
        sizes = [eighth, 3 * eighth] + sizes[1:]
    return sizes


def _peer_launch(h, hn, tb, after, wq_bf, keys1_bf, keys2_bf, peer_u):
    n = h.shape[0]
    eid_t, gate_t = _route(hn, tb, wq_bf, keys1_bf, keys2_bf)
    eid = eid_t.T
    pad = (-n) % (SC_WORKERS * SC_TOKENS)
    hn_p, eid_p = hn, eid
    if pad:
        spread = (jnp.arange(pad * PEER_SEL, dtype=jnp.int32) % peer_u.shape[0]).reshape(pad, PEER_SEL)
        hn_p = jnp.pad(hn, ((0, pad), (0, 0)))
        eid_p = jnp.concatenate([eid, spread], axis=0)
    hval = _peer_hval(hn_p, eid_p, peer_u, after)[:n]
    return h, eid, gate_t.T, hval


def _peer_finish(h, eid, gate, hval, tb, v_packed):
    a = _gate(hval, gate, tb)
    return _peer_values(h, a, eid * VT_ROWS, v_packed)


def _rope_tables(pos):
    half = HEAD_DIM // 2
    inv = ROPE_THETA ** (-jnp.arange(half, dtype=F32) / half)
    ang = pos.astype(F32)[:, None] * inv[None, :]
    cos = jnp.cos(ang)
    sin = jnp.sin(ang)
    reps = LANES // HEAD_DIM
    cos_f = jnp.tile(jnp.concatenate([cos, cos], axis=1), (1, reps))
    sin_s = jnp.tile(jnp.concatenate([-sin, sin], axis=1), (1, reps))
    return cos_f, sin_s


def _group_sum_matrix(width):
    g = jnp.arange(width) // HEAD_DIM
    return (g[:, None] == g[None, :]).astype(BF16)


def kernel(x_prompt, x_sample, cache_k_win, cache_v_win, state_conv, meta_tokens, norm_mix_g,
           w_in, q_norm_g, k_norm_g, attn_sinks, conv_w, conv_b, conv_ln_g, conv_ln_b, w_pw2,
           out_norm_attn_g, out_norm_conv_g, w_out, norm_ffn_g, peer_w_q, peer_keys1, peer_keys2,
           peer_u, peer_v):
    assert norm_mix_g.shape[0] == 1, "single-layer model"
    b, t, _ = x_prompt.shape
    s = x_sample.shape[0]
    w_buf = cache_k_win.shape[2]
    n = b * t

    g_mix = norm_mix_g[0][None, :]
    w_in_bf = w_in[0].astype(BF16)
    qg_t = jnp.tile(q_norm_g[0], N_HEADS)[None, :]
    kg_t = jnp.tile(k_norm_g[0], N_KV_HEADS)[None, :]
    gsum_q = _group_sum_matrix(D_ATTN)
    gsum_k = _group_sum_matrix(KV_DIM)
    sinks = attn_sinks[0]
    cw, cb = conv_w[0], conv_b[0][None, :]
    lg, lb = conv_ln_g[0][None, :], conv_ln_b[0][None, :]
    w2_bf = w_pw2[0].astype(BF16)
    g_a, g_c = out_norm_attn_g[0][None, :], out_norm_conv_g[0][None, :]
    w_out_a = w_out[0][:D_ATTN].astype(BF16)
    w_out_c = w_out[0][D_ATTN:].astype(BF16)
    g_f = norm_ffn_g[0][None, :]
    wq_bf = peer_w_q[0].astype(BF16)
    k1_bf = peer_keys1[0].astype(BF16)
    k2_bf = peer_keys2[0].astype(BF16)
    pu, vp = peer_u[0], _pack_value_table(peer_v[0])
    proj = functools.partial(_project, norm_g=g_mix, w_in_bf=w_in_bf, qg_t=qg_t, kg_t=kg_t,
                             gsum_q=gsum_q, gsum_k=gsum_k)
    conv = functools.partial(_conv_branch, conv_w=cw, conv_b=cb, ln_g=lg, ln_b=lb, w_pw2_bf=w2_bf)
    merge = functools.partial(_merge, g_a=g_a, g_c=g_c, w_out_a=w_out_a, w_out_c=w_out_c, g_f=g_f)

    tb = TB_DENSE
    no_dep = jnp.zeros((SC_TOKENS, LANES), F32)
    tab_x = _rope_tables(N_META + jnp.arange(t, dtype=jnp.int32))
    tab_m = _rope_tables(jnp.arange(N_META, dtype=jnp.int32))
    _, k_m, v_m, u_m = proj(meta_tokens, tab_m, 1, N_META, no_dep)
    halo0 = jnp.concatenate([jnp.zeros((CONV_HALO - N_META, D_CONV), F32), u_m], axis=0)

    k_meta = jnp.concatenate([jnp.zeros((META_PAD, KV_DIM), F32), k_m], axis=0)
    v_meta = jnp.concatenate([jnp.zeros((META_PAD, KV_DIM), F32), v_m], axis=0)

    xp = x_prompt.reshape(b * t, D_MODEL)

    def dense(bi, after):
        q, k, v, u = proj(xp, tab_x, t // tb, tb, after, row0=bi * t, nrows=t)
        o_attn = _prompt_attention(sinks, q, k, v, k_meta, v_meta)
        u3 = u.reshape(t // tb, tb, D_CONV)
        halo = jnp.concatenate([halo0[None], u3[:-1, tb - CONV_HALO:]], axis=0)
        o_conv = conv(halo, u, tb)
        hs, r0 = [], 0
        for rows in _chunk_plan(t, tb, bi == 0, bi == b - 1):
            hs.append(merge(xp, o_attn, o_conv, tb, x_row0=bi * t + r0, row0=r0, nrows=rows))
            r0 += rows
        return hs, k, v, u

    launched, ys, kvu = [], [], []
    cur = dense(0, no_dep)
    for bi in range(b):
        hs, k, v, u = cur
        kvu.append((k, v, u))
        for h, hn in hs:
            c = len(launched)
            if c >= PEER_DEPTH:
                ys.append(_peer_finish(*launched[c - PEER_DEPTH], TB_ROUTE, vp))
            after = ys[c - PEER_DEPTH][:SC_TOKENS] if c >= PEER_DEPTH else jnp.zeros((SC_TOKENS, D_MODEL), F32)
            launched.append(_peer_launch(h, hn, TB_ROUTE, after, wq_bf, k1_bf, k2_bf, pu))
        if bi + 1 < b:
            cur = dense(bi + 1, launched[-1][1][:SC_TOKENS, :LANES].astype(F32))
    for c in range(len(ys), len(launched)):
        ys.append(_peer_finish(*launched[c], TB_ROUTE, vp))
    y_prompt = jnp.concatenate(ys, axis=0).reshape(b, t, D_MODEL)

    new_k_prompt = jnp.stack([k[t - WINDOW:] for k, _, _ in kvu]).reshape(1, b, WINDOW, N_KV_HEADS, HEAD_DIM)
    new_v_prompt = jnp.stack([v[t - WINDOW:] for _, v, _ in kvu]).reshape(1, b, WINDOW, N_KV_HEADS, HEAD_DIM)
    new_conv_prompt = jnp.stack([u[t - (CONV_WIDTH - 1):] for _, _, u in kvu])[None]

    xs = x_sample.reshape(s, D_MODEL)
    tab_s = _rope_tables(jnp.full((s,), PAST_LEN, jnp.int32))
    qs, ks, vs, us = proj(xs, tab_s, 1, s, no_dep)
    ck = cache_k_win[0]
    cv = cache_v_win[0]
    o_attn_s = _decode_attention(
        sinks,
        qs.reshape(s, N_HEADS, HEAD_DIM), ks, vs,
        ck.reshape(s, w_buf, KV_DIM), cv.reshape(s, w_buf, KV_DIM), min(SB_DECODE, s))
    o_attn_s = o_attn_s.reshape(s, D_ATTN)
    cs = state_conv[0]
    hist = jnp.concatenate([jnp.zeros((s, CONV_HALO - (CONV_WIDTH - 1), D_CONV), F32), cs], axis=1)
    us_blk = jnp.concatenate([us[:, None, :], jnp.zeros((s, SUBLANES - 1, D_CONV), F32)], axis=1)
    o_conv_s = conv(hist, us_blk.reshape(s * SUBLANES, D_CONV), SUBLANES)
    o_conv_s = o_conv_s.reshape(s, SUBLANES, D_CONV)[:, 0]
    hs, hns = merge(xs, o_attn_s, o_conv_s, s)
    launch_s = _peer_launch(hs, hns, s, jnp.zeros((SC_TOKENS, D_MODEL), F32), wq_bf, k1_bf, k2_bf, pu)
    y_sample = _peer_finish(*launch_s, s, vp).reshape(s, 1, D_MODEL)

    new_k_sample = jnp.concatenate([ck[:, 1:], ks.reshape(s, 1, N_KV_HEADS, HEAD_DIM)], axis=1)[None]
    new_v_sample = jnp.concatenate([cv[:, 1:], vs.reshape(s, 1, N_KV_HEADS, HEAD_DIM)], axis=1)[None]
    new_conv_sample = jnp.concatenate([cs[:, 1:], us[:, None, :]], axis=1)[None]
    if w_buf != WINDOW:
        raise NotImplementedError("cache window shorter than the attention window")

    return (y_prompt, y_sample, new_k_prompt, new_v_prompt, new_conv_prompt,
            new_k_sample, new_v_sample, new_conv_sample)
```

```python
import functools

import jax
import jax.numpy as jnp
from jax import lax
from jax.experimental import pallas as pl
from jax.experimental.pallas import tpu as pltpu
from jax.experimental.pallas import tpu_sc as plsc

D_MODEL = 1024
HEAD_DIM = 64
D_ATTN = 512
N_HEADS = 8
N_KV_HEADS = 2
KV_DIM = N_KV_HEADS * HEAD_DIM
D_CONV = 512
D_IN = D_ATTN + 2 * KV_DIM + 2 * D_CONV
CONV_WIDTH = 31
WINDOW = 128
BLOCK = 128
ROPE_THETA = 10000.0
N_META = 16
META_PAD = BLOCK - N_META
PEER_HEADS = 8
PEER_NKEYS = 128
PEER_TOPK = 16
PEER_SEL = PEER_HEADS * PEER_TOPK
EPS = 1e-6
PAST_LEN = 16384

LANES = 128
SC_CORES = 2
SC_SUBCORES = 16
SC_LANES = 16
SC_WORKERS = SC_CORES * SC_SUBCORES
VMEM_LIMIT = 48 * 1024 * 1024
SUBLANES = 8

TB_DENSE = 512
TB_ROUTE = 256
SB_DECODE = 32

F32 = jnp.float32
BF16 = jnp.bfloat16
NEG_INF = float("-inf")


def _tc_params(n_axes):
    return pltpu.CompilerParams(dimension_semantics=("arbitrary",) * n_axes,
                                vmem_limit_bytes=VMEM_LIMIT)


def _full(shape):
    nd = len(shape)
    return pl.BlockSpec(shape, lambda *_: (0,) * nd)


def _group_mean(sq, gsum_ref):
    hi = sq.astype(BF16)
    lo = (sq - hi.astype(F32)).astype(BF16)
    g = gsum_ref[...]
    s = jnp.dot(hi, g, preferred_element_type=F32) + jnp.dot(lo, g, preferred_element_type=F32)
    return s * (1.0 / HEAD_DIM)


def _rope(xn, cos_f, sin_s, first_half):
    outs = []
    for s in range(xn.shape[1] // LANES):
        xs = xn[:, s * LANES:(s + 1) * LANES]
        partner = jnp.where(first_half, pltpu.roll(xs, LANES - HEAD_DIM // 2, axis=1),
                            pltpu.roll(xs, HEAD_DIM // 2, axis=1))
        outs.append(xs * cos_f + partner * sin_s)
    return outs[0] if len(outs) == 1 else jnp.concatenate(outs, axis=1)


def _proj_kernel(x_ref, g_ref, w_ref, qg_ref, kg_ref, cos_ref, sin_ref, gq_ref, gk_ref, after_ref,
                 q_out, k_out, v_out, u_out):
    del after_ref
    x = x_ref[...]
    n = x * lax.rsqrt(jnp.mean(x * x, axis=-1, keepdims=True) + EPS) * g_ref[...]
    p = jnp.dot(n.astype(BF16), w_ref[...], preferred_element_type=F32)
    q = p[:, :D_ATTN]
    k = p[:, D_ATTN:D_ATTN + KV_DIM]
    v = p[:, D_ATTN + KV_DIM:D_ATTN + 2 * KV_DIM]
    ga = p[:, D_ATTN + 2 * KV_DIM:D_ATTN + 2 * KV_DIM + D_CONV]
    gb = p[:, D_ATTN + 2 * KV_DIM + D_CONV:]
    cos_f = cos_ref[...]
    sin_s = sin_ref[...]
    lane = lax.broadcasted_iota(jnp.int32, (x.shape[0], LANES), 1)
    first_half = (lane % HEAD_DIM) < (HEAD_DIM // 2)
    qn = q * lax.rsqrt(_group_mean(q * q, gq_ref) + EPS) * qg_ref[...]
    kn = k * lax.rsqrt(_group_mean(k * k, gk_ref) + EPS) * kg_ref[...]
    q_out[...] = _rope(qn, cos_f, sin_s, first_half)
    k_out[...] = _rope(kn, cos_f, sin_s, first_half)
    v_out[...] = v
    u_out[...] = ga * (1.0 / (1.0 + jnp.exp(-gb)))


def _project(x, pos_tables, n_table_blocks, tb, after, norm_g, w_in_bf, qg_t, kg_t, gsum_q, gsum_k,
             row0=0, nrows=None):
    n = x.shape[0] if nrows is None else nrows
    cos_t, sin_t = pos_tables
    nb = n // tb
    blk0 = row0 // tb
    tab_spec = pl.BlockSpec((tb, LANES), lambda i: (i % n_table_blocks, 0))
    row = lambda w: pl.BlockSpec((tb, w), lambda i: (i, 0))
    return pl.pallas_call(
        _proj_kernel,
        grid=(nb,),
        in_specs=[pl.BlockSpec((tb, D_MODEL), lambda i: (i + blk0, 0)),
                  _full((1, D_MODEL)), _full((D_MODEL, D_IN)),
                  _full((1, D_ATTN)), _full((1, KV_DIM)), tab_spec, tab_spec,
                  _full((D_ATTN, D_ATTN)), _full((KV_DIM, KV_DIM)),
                  pl.BlockSpec(memory_space=pl.ANY)],
        out_specs=[row(D_ATTN), row(KV_DIM), row(KV_DIM), row(D_CONV)],
        out_shape=[jax.ShapeDtypeStruct((n, D_ATTN), F32), jax.ShapeDtypeStruct((n, KV_DIM), F32),
                   jax.ShapeDtypeStruct((n, KV_DIM), F32), jax.ShapeDtypeStruct((n, D_CONV), F32)],
        compiler_params=_tc_params(1),
        name="proj",
    )(x, norm_g, w_in_bf, qg_t, kg_t, cos_t, sin_t, gsum_q, gsum_k, after)


def _attn_kernel(sink_ref, q_ref, kp_ref, kc_ref, vp_ref, vc_ref, km_ref, vm_ref, o_ref):
    j = pl.program_id(0)
    r = lax.broadcasted_iota(jnp.int32, (BLOCK, 2 * BLOCK), 0)
    c = lax.broadcasted_iota(jnp.int32, (BLOCK, 2 * BLOCK), 1)
    ok = (c > r) & (c <= r + WINDOW) & ((j > 0) | (c >= META_PAD))
    grp = N_HEADS // N_KV_HEADS
    ok = jnp.concatenate([ok] * grp, axis=0)
    first = j == 0
    k_all = jnp.concatenate([jnp.where(first, km_ref[...], kp_ref[...]), kc_ref[...]], axis=0)
    v_all = jnp.concatenate([jnp.where(first, vm_ref[...], vp_ref[...]), vc_ref[...]], axis=0)
    q = q_ref[...]
    outs = []
    for g in range(N_KV_HEADS):
        lanes = slice(g * HEAD_DIM, (g + 1) * HEAD_DIM)
        k = k_all[:, lanes].astype(BF16)
        v = v_all[:, lanes].astype(BF16)
        q4 = jnp.concatenate([q[:, (g * grp + i) * HEAD_DIM:(g * grp + i + 1) * HEAD_DIM]
                              for i in range(grp)], axis=0).astype(BF16)
        s = lax.dot_general(q4, k, (((1,), (1,)), ((), ())), preferred_element_type=F32)
        s = jnp.where(ok, s * (HEAD_DIM ** -0.5), NEG_INF)
        sink = jnp.concatenate(
            [jnp.full((BLOCK, 1), sink_ref[g * grp + i], F32) for i in range(grp)], axis=0)
        m = jnp.maximum(jnp.max(s, axis=1, keepdims=True), sink)
        p = jnp.exp(s - m)
        den = jnp.sum(p, axis=1, keepdims=True) + jnp.exp(sink - m)
        o = jnp.dot(p.astype(BF16), v, preferred_element_type=F32) / den
        outs += [o[i * BLOCK:(i + 1) * BLOCK] for i in range(grp)]
    o_ref[...] = jnp.concatenate(outs, axis=1)


def _prompt_attention(sinks, q, k, v, k_meta, v_meta):
    t = q.shape[0]
    kv_prev = pl.BlockSpec((BLOCK, KV_DIM), lambda j: (jnp.maximum(j - 1, 0), 0))
    kv_cur = pl.BlockSpec((BLOCK, KV_DIM), lambda j: (j, 0))
    q_spec = pl.BlockSpec((BLOCK, D_ATTN), lambda j: (j, 0))
    meta = _full((BLOCK, KV_DIM))
    return pl.pallas_call(
        _attn_kernel,
        grid=(t // BLOCK,),
        in_specs=[pl.BlockSpec(memory_space=pltpu.SMEM), q_spec, kv_prev, kv_cur, kv_prev, kv_cur,
                  meta, meta],
        out_specs=q_spec,
        out_shape=jax.ShapeDtypeStruct((t, D_ATTN), F32),
        compiler_params=_tc_params(1),
        name="attn",
    )(sinks, q, k, k, v, v, k_meta, v_meta)


def _dec_attn_kernel(sink_ref, q_ref, kn_ref, vn_ref, ck_ref, cv_ref, o_ref):
    grp = N_HEADS // N_KV_HEADS
    sb, w_buf = ck_ref.shape[0], ck_ref.shape[1]
    q = q_ref[...]
    qb = q.astype(BF16)
    head = lax.broadcasted_iota(jnp.int32, (sb, N_HEADS, 1), 1)
    in_g0 = head < grp
    kv_lanes = [slice(g * HEAD_DIM, (g + 1) * HEAD_DIM) for g in range(N_KV_HEADS)]
    ck = ck_ref[...]
    cv = cv_ref[...]
    s_g = [jnp.einsum("shd,swd->shw", qb, ck[:, :, ln].astype(BF16), preferred_element_type=F32)
           for ln in kv_lanes]
    s = jnp.where(in_g0, s_g[0], s_g[1]) * (HEAD_DIM ** -0.5)
    key_ok = lax.broadcasted_iota(jnp.int32, (sb, N_HEADS, w_buf), 2) >= 1
    s = jnp.where(key_ok, s, NEG_INF)
    rnd = lambda a: a.astype(BF16).astype(F32)
    kn_all, vn_all = kn_ref[...], vn_ref[...]
    kn = jnp.where(in_g0, kn_all[:, None, kv_lanes[0]], kn_all[:, None, kv_lanes[1]])
    vn = jnp.where(in_g0, vn_all[:, None, kv_lanes[0]], vn_all[:, None, kv_lanes[1]])
    s_self = jnp.sum(rnd(q) * rnd(kn), axis=-1, keepdims=True) * (HEAD_DIM ** -0.5)
    sink = sink_ref[...]
    m = jnp.maximum(jnp.maximum(jnp.max(s, axis=-1, keepdims=True), s_self), sink)
    p = jnp.exp(s - m)
    p_self = jnp.exp(s_self - m)
    den = jnp.sum(p, axis=-1, keepdims=True) + p_self + jnp.exp(sink - m)
    pb = p.astype(BF16)
    o_g = [jnp.einsum("shw,swd->shd", pb, cv[:, :, ln].astype(BF16), preferred_element_type=F32)
           for ln in kv_lanes]
    o = jnp.where(in_g0, o_g[0], o_g[1]) + rnd(p_self) * rnd(vn)
    o_ref[...] = o / den


def _decode_attention(sinks, q3, kn_t, vn_t, ck_t, cv_t, sb):
    s = q3.shape[0]
    w_buf = ck_t.shape[1]
    qs = pl.BlockSpec((sb, N_HEADS, HEAD_DIM), lambda i: (i, 0, 0))
    ns = pl.BlockSpec((sb, KV_DIM), lambda i: (i, 0))
    cs = pl.BlockSpec((sb, w_buf, KV_DIM), lambda i: (i, 0, 0))
    return pl.pallas_call(
        _dec_attn_kernel,
        grid=(s // sb,),
        in_specs=[_full((1, N_HEADS, 1)), qs, ns, ns, cs, cs],
        out_specs=qs,
        out_shape=jax.ShapeDtypeStruct(q3.shape, F32),
        compiler_params=_tc_params(1),
        name="dec_attn",
    )(sinks.reshape(1, N_HEADS, 1), q3, kn_t, vn_t, ck_t, cv_t)


CONV_HALO = 32
CONV_ROWS = 64


def _conv_kernel(halo_ref, u_ref, cw_ref, cb_ref, lg_ref, lb_ref, w2_ref, o_ref, ucat):
    tb = u_ref.shape[0]
    ucat[0:CONV_HALO, :] = halo_ref[0]
    ucat[CONV_HALO:, :] = u_ref[...]
    first = CONV_HALO - (CONV_WIDTH - 1)
    rows = min(CONV_ROWS, tb)
    for r0 in range(0, tb, rows):
        acc = jnp.zeros((rows, D_CONV), F32)
        for j in range(CONV_WIDTH):
            acc = acc + ucat[r0 + first + j:r0 + first + j + rows, :] * cw_ref[j:j + 1, :]
        y = acc + cb_ref[...]
        yc = y - jnp.mean(y, axis=-1, keepdims=True)
        yn = yc * lax.rsqrt(jnp.mean(yc * yc, axis=-1, keepdims=True) + EPS)
        yn = yn * lg_ref[...] + lb_ref[...]
        act = yn * (1.0 / (1.0 + jnp.exp(-yn)))
        o_ref[r0:r0 + rows, :] = jnp.dot(act.astype(BF16), w2_ref[...],
                                              preferred_element_type=F32)


def _conv_branch(halo, u, tb, conv_w, conv_b, ln_g, ln_b, w_pw2_bf):
    n = u.shape[0]
    return pl.pallas_call(
        _conv_kernel,
        grid=(n // tb,),
        in_specs=[pl.BlockSpec((1, CONV_HALO, D_CONV), lambda i: (i, 0, 0)),
                  pl.BlockSpec((tb, D_CONV), lambda i: (i, 0)),
                  _full((CONV_WIDTH, D_CONV)), _full((1, D_CONV)), _full((1, D_CONV)),
                  _full((1, D_CONV)), _full((D_CONV, D_CONV))],
        out_specs=pl.BlockSpec((tb, D_CONV), lambda i: (i, 0)),
        out_shape=jax.ShapeDtypeStruct((n, D_CONV), F32),
        scratch_shapes=[pltpu.VMEM((tb + CONV_HALO, D_CONV), F32)],
        compiler_params=_tc_params(1),
        name="conv",
    )(halo, u, conv_w, conv_b, ln_g, ln_b, w_pw2_bf)


def _rms(x, g):
    return x * lax.rsqrt(jnp.mean(x * x, axis=-1, keepdims=True) + EPS) * g


def _merge_kernel(x_ref, oa_ref, oc_ref, ga_ref, gc_ref, wa_ref, wc_ref, gf_ref, h_out, hn_out):
    a = _rms(oa_ref[...], ga_ref[...]).astype(BF16)
    c = _rms(oc_ref[...], gc_ref[...]).astype(BF16)
    h = x_ref[...] + (jnp.dot(a, wa_ref[...], preferred_element_type=F32)
                      + jnp.dot(c, wc_ref[...], preferred_element_type=F32))
    h_out[...] = h
    hn_out[...] = _rms(h, gf_ref[...])


def _merge(x, oa, oc, tb, g_a, g_c, w_out_a, w_out_c, g_f, x_row0=0, row0=0, nrows=None):
    n = oa.shape[0] if nrows is None else nrows
    xb0, b0 = x_row0 // tb, row0 // tb
    row = lambda w: pl.BlockSpec((tb, w), lambda i: (i, 0))
    src = lambda w: pl.BlockSpec((tb, w), lambda i: (i + b0, 0))
    return pl.pallas_call(
        _merge_kernel,
        grid=(n // tb,),
        in_specs=[pl.BlockSpec((tb, D_MODEL), lambda i: (i + xb0, 0)), src(D_ATTN), src(D_CONV),
                  _full((1, D_ATTN)), _full((1, D_CONV)),
                  _full((D_ATTN, D_MODEL)), _full((D_CONV, D_MODEL)), _full((1, D_MODEL))],
        out_specs=[row(D_MODEL), row(D_MODEL)],
        out_shape=[jax.ShapeDtypeStruct((n, D_MODEL), F32)] * 2,
        compiler_params=_tc_params(1),
        name="merge",
    )(x, oa, oc, g_a, g_c, w_out_a, w_out_c, g_f)


ID_BIG = 1e9


def _topk_rows(s, k):
    rows = lax.broadcasted_iota(jnp.int32, s.shape, 0).astype(F32)
    vals, idxs = [], []
    for _ in range(k):
        m = jnp.max(s, axis=0, keepdims=True)
        idx = jnp.min(jnp.where(s == m, rows, ID_BIG), axis=0, keepdims=True)
        vals.append(m)
        idxs.append(idx)
        s = jnp.where(rows == idx, NEG_INF, s)
    return jnp.concatenate(vals, axis=0), jnp.concatenate(idxs, axis=0)


PAIR_B_WIDE = 8


def _route_kernel(hn_ref, wq_ref, k1_ref, k2_ref, eid_out, gate_out):
    tb = hn_ref.shape[0]
    q = jnp.dot(hn_ref[...].astype(BF16), wq_ref[...], preferred_element_type=F32).astype(BF16)
    k1 = k1_ref[...]
    k2 = k2_ref[...]
    nt = (((1,), (1,)), ((), ()))
    r = lax.broadcasted_iota(jnp.int32, (PEER_TOPK + (PAIR_B_WIDE - 1) * PAIR_B_WIDE + PAIR_B_WIDE, tb), 0)
    mid = r - PEER_TOPK
    flat = jnp.where(r < PEER_TOPK, r,
                     jnp.where(mid < (PAIR_B_WIDE - 1) * PAIR_B_WIDE,
                               (1 + mid // PAIR_B_WIDE) * PEER_TOPK + mid % PAIR_B_WIDE,
                               (PAIR_B_WIDE + mid - (PAIR_B_WIDE - 1) * PAIR_B_WIDE) * PEER_TOPK)).astype(F32)
    half = PEER_NKEYS
    for h in range(PEER_HEADS):
        q1 = q[:, (2 * h) * half:(2 * h + 1) * half]
        q2 = q[:, (2 * h + 1) * half:(2 * h + 2) * half]
        s1 = lax.dot_general(k1, q1, nt, preferred_element_type=F32)
        s2 = lax.dot_general(k2, q2, nt, preferred_element_type=F32)
        v1, i1 = _topk_rows(s1, PEER_TOPK)
        v2, i2 = _topk_rows(s2, PEER_TOPK)
        e1 = i1 * PEER_NKEYS
        cand = jnp.concatenate(
            [v1[0:1] + v2]
            + [v1[a:a + 1] + v2[0:PAIR_B_WIDE] for a in range(1, PAIR_B_WIDE)]
            + [v1[PAIR_B_WIDE:] + v2[0:1]], axis=0)
        cid = jnp.concatenate(
            [e1[0:1] + i2]
            + [e1[a:a + 1] + i2[0:PAIR_B_WIDE] for a in range(1, PAIR_B_WIDE)]
            + [e1[PAIR_B_WIDE:] + i2[0:1]], axis=0)
        scs, eids = [], []
        for _ in range(PEER_TOPK):
            m = jnp.max(cand, axis=0, keepdims=True)
            jsel = jnp.min(jnp.where(cand == m, flat, ID_BIG), axis=0, keepdims=True)
            hit = flat == jsel
            eids.append(jnp.max(jnp.where(hit, cid, -1.0), axis=0, keepdims=True))
            scs.append(m)
            cand = jnp.where(hit, NEG_INF, cand)
        sc = jnp.concatenate(scs, axis=0)
        e = jnp.exp(sc - sc[0:1])
        gate_out[h * PEER_TOPK:(h + 1) * PEER_TOPK, :] = e / jnp.sum(e, axis=0, keepdims=True)
        eid_out[h * PEER_TOPK:(h + 1) * PEER_TOPK, :] = jnp.concatenate(eids, axis=0).astype(jnp.int32)


def _route(hn, tb, wq_bf, keys1_bf, keys2_bf):
    n = hn.shape[0]
    col = pl.BlockSpec((PEER_SEL, tb), lambda i: (0, i))
    return pl.pallas_call(
        _route_kernel,
        grid=(n // tb,),
        in_specs=[pl.BlockSpec((tb, D_MODEL), lambda i: (i, 0)),
                  _full((D_MODEL, 2 * PEER_NKEYS * PEER_HEADS)),
                  _full((PEER_NKEYS, PEER_NKEYS)), _full((PEER_NKEYS, PEER_NKEYS))],
        out_specs=[col, col],
        out_shape=[jax.ShapeDtypeStruct((PEER_SEL, n), jnp.int32),
                   jax.ShapeDtypeStruct((PEER_SEL, n), F32)],
        compiler_params=_tc_params(1),
        name="route",
    )(hn, wq_bf, keys1_bf, keys2_bf)


SC_ROWS = 32
SC_GROUP = SC_LANES
SC_TOKENS = 8
SC_CHUNKS = PEER_SEL // SC_ROWS

_SC_PARAMS = pltpu.CompilerParams(needs_layout_passes=False)


def _sc_mesh():
    return plsc.VectorSubcoreMesh(core_axis_name="c", subcore_axis_name="s",
                                  num_cores=SC_CORES, num_subcores=SC_SUBCORES)


def _sc_token_loop(n_batches, table_hbm, eid_v, rbufs, rsems, prefetch, prefetch_wait, store,
                   contract):
    def gather(slot, tl, ch, p):
        idx = eid_v.at[slot, tl, pl.ds(ch * SC_ROWS, SC_ROWS)]
        return pltpu.make_async_copy(table_hbm.at[idx], rbufs[p], rsems[p])

    prefetch(0, 0)
    prefetch_wait()
    gather(0, 0, 0, 0).start()

    @pl.loop(0, n_batches * SC_TOKENS)
    def _(i):
        b = i // SC_TOKENS
        tl = i % SC_TOKENS
        slot = b % 2
        more = b + 1 < n_batches

        @pl.when(jnp.logical_and(tl == 0, more))
        def _():
            prefetch(b + 1, 1 - slot)

        for ch in range(SC_CHUNKS):
            p = ch % 2
            if ch + 1 < SC_CHUNKS:
                gather(slot, tl, ch + 1, 1 - p).start()
            else:
                @pl.when(tl + 1 < SC_TOKENS)
                def _():
                    gather(slot, tl + 1, 0, 1 - p).start()

                @pl.when(jnp.logical_and(tl + 1 == SC_TOKENS, more))
                def _():
                    prefetch_wait()
                    gather(1 - slot, 0, 0, 1 - p).start()
            gather(slot, tl, ch, p).wait()
            contract(slot, tl, ch, rbufs[p])

        @pl.when(tl + 1 == SC_TOKENS)
        def _():
            @pl.when(b >= 1)
            def _():
                store(b - 1, 1 - slot).wait()
            store(b, slot).start()

    store(n_batches - 1, (n_batches - 1) % 2).wait()


def _sc_batches(n):
    assert n % (SC_WORKERS * SC_TOKENS) == 0, n
    return n // (SC_WORKERS * SC_TOKENS)


def _peer_hval(hn, eid, peer_u, after):
    n = hn.shape[0]
    nbw = _sc_batches(n)

    @functools.partial(
        pl.kernel, mesh=_sc_mesh(),
        out_type=jax.ShapeDtypeStruct((n // SC_TOKENS, SC_TOKENS, PEER_SEL), F32),
        scratch_types=[
            pltpu.VMEM((2, SC_TOKENS, PEER_SEL), jnp.int32),
            pltpu.VMEM((2, SC_TOKENS, D_MODEL), F32),
            pltpu.VMEM((SC_ROWS, D_MODEL), F32),
            pltpu.VMEM((SC_ROWS, D_MODEL), F32),
            pltpu.VMEM((2, SC_TOKENS, PEER_SEL), F32),
            pltpu.VMEM((SC_GROUP, SC_LANES), F32),
            pltpu.SemaphoreType.DMA, pltpu.SemaphoreType.DMA,
            pltpu.SemaphoreType.DMA, pltpu.SemaphoreType.DMA,
        ],
        compiler_params=_SC_PARAMS, name="peer_hval",
        cost_estimate=pl.CostEstimate(flops=2 * n * PEER_SEL * D_MODEL, transcendentals=0,
                                      bytes_accessed=n * PEER_SEL * D_MODEL * 4))
    def k(x_hbm, eid_hbm, u_hbm, after_hbm, o_hbm, eid_v, x_v, r0, r1, h_v, tr, sr0, sr1, spf, sout):
        del after_hbm
        blk0 = (lax.axis_index("s") * SC_CORES + lax.axis_index("c")) * nbw
        lane = lax.iota(jnp.int32, SC_LANES)
        zero = jnp.zeros((SC_LANES,), F32)

        def prefetch(b, slot):
            pltpu.async_copy(eid_hbm.at[blk0 + b], eid_v.at[slot], spf)
            pltpu.async_copy(x_hbm.at[blk0 + b], x_v.at[slot], spf)

        def prefetch_wait():
            pltpu.make_async_copy(eid_hbm.at[0], eid_v.at[0], spf).wait()
            pltpu.make_async_copy(x_hbm.at[0], x_v.at[0], spf).wait()

        def store(b, slot):
            return pltpu.make_async_copy(h_v.at[slot], o_hbm.at[blk0 + b], sout)

        def contract(slot, tl, ch, rbuf):
            for g in range(SC_ROWS // SC_GROUP):
                def body(c, accs):
                    xc = x_v[slot, tl, pl.ds(c * SC_LANES, SC_LANES)]
                    return tuple(
                        accs[r] + rbuf[g * SC_GROUP + r, pl.ds(c * SC_LANES, SC_LANES)] * xc
                        for r in range(SC_GROUP))
                accs = lax.fori_loop(0, D_MODEL // SC_LANES, body, (zero,) * SC_GROUP)
                for r in range(SC_GROUP):
                    tr[r, :] = accs[r]
                res = zero
                for jj in range(SC_LANES):
                    res = res + plsc.load_gather(tr, [lane, jnp.full((SC_LANES,), jj, jnp.int32)])
                h_v[slot, tl, pl.ds(ch * SC_ROWS + g * SC_GROUP, SC_GROUP)] = res

        _sc_token_loop(nbw, u_hbm, eid_v, (r0, r1), (sr0, sr1), prefetch, prefetch_wait, store,
                       contract)

    out = k(hn.reshape(n // SC_TOKENS, SC_TOKENS, D_MODEL),
            eid.reshape(n // SC_TOKENS, SC_TOKENS, PEER_SEL), peer_u, after)
    return out.reshape(n, PEER_SEL)


def _gate_kernel(hv_ref, gate_ref, a_out):
    hv = hv_ref[...]
    gelu = hv * (lax.erf(hv * (2.0 ** -0.5)) + 1.0) * 0.5
    a_out[...] = gate_ref[...] * gelu


def _gate(hval, gate, tb):
    n = hval.shape[0]
    row = pl.BlockSpec((tb, PEER_SEL), lambda i: (i, 0))
    return pl.pallas_call(
        _gate_kernel, grid=(n // tb,), in_specs=[row, row], out_specs=row,
        out_shape=jax.ShapeDtypeStruct((n, PEER_SEL), F32),
        compiler_params=_tc_params(1), name="gate",
    )(hval, gate)


VT_ROWS = D_MODEL // 2 // LANES
VT_TOKENS = 128
VT_UNROLL = 4
HI_MASK = -65536


def _pack_value_table(peer_v):
    e = peer_v.shape[0]
    bits = lax.bitcast_convert_type(peer_v.astype(BF16), jnp.uint16).astype(jnp.uint32)
    words = bits[:, :D_MODEL // 2] | (bits[:, D_MODEL // 2:] << 16)
    return lax.bitcast_convert_type(words, jnp.int32).reshape(e * VT_ROWS, LANES)


def _vside_kernel(row_s, a_s, h_ref, tab_ref, y_ref):
    tb = h_ref.shape[0]

    def token_group(p, carry):
        zero = jnp.zeros((VT_ROWS, LANES), F32)
        ts = tuple(VT_UNROLL * p + k for k in range(VT_UNROLL))
        lo = [zero] * VT_UNROLL
        hi = [zero] * VT_UNROLL
        for e in range(PEER_SEL):
            for k, t in enumerate(ts):
                r0 = pl.multiple_of(row_s[t, e], VT_ROWS)
                w = tab_ref[pl.ds(r0, VT_ROWS), :]
                coef = a_s[t, e]
                lo[k] = lo[k] + coef * pltpu.bitcast(w << 16, F32)
                hi[k] = hi[k] + coef * pltpu.bitcast(w & HI_MASK, F32)
        for k, t in enumerate(ts):
            y_ref[t] = h_ref[t] + jnp.concatenate([lo[k], hi[k]], axis=0)
        return carry

    lax.fori_loop(0, tb // VT_UNROLL, token_group, 0)


def _peer_values(h, a, row_ids, v_packed):
    n = h.shape[0]
    tb = min(VT_TOKENS, n)
    smem = pl.BlockSpec((tb, PEER_SEL), lambda i: (i, 0), memory_space=pltpu.SMEM)
    slab = pl.BlockSpec((tb, 2 * VT_ROWS, LANES), lambda i: (i, 0, 0))
    table = pl.BlockSpec(v_packed.shape, lambda i: (0, 0), pipeline_mode=pl.Buffered(1))
    table_bytes = v_packed.shape[0] * LANES * 4
    y = pl.pallas_call(
        _vside_kernel,
        grid=(n // tb,),
        in_specs=[smem, smem, slab, table],
        out_specs=slab,
        out_shape=jax.ShapeDtypeStruct((n, 2 * VT_ROWS, LANES), F32),
        compiler_params=pltpu.CompilerParams(
            dimension_semantics=("arbitrary",),
            vmem_limit_bytes=table_bytes + 8 * 1024 * 1024),
        name="peer_values",
        cost_estimate=pl.CostEstimate(flops=2 * n * PEER_SEL * D_MODEL, transcendentals=0,
                                      bytes_accessed=table_bytes + n * PEER_SEL * D_MODEL * 2),
    )(row_ids, a, h.reshape(n, 2 * VT_ROWS, LANES), v_packed)
    return y.reshape(n, D_MODEL)


PEER_DEPTH = 2
SAMPLE_FINISH_AT = 4


def _chunk_plan(t, tb, first, last):
    eighth = t // 8
    if eighth % tb:
        return [t]
    sizes = [4 * eighth, 4 * eighth]
    if last:
        sizes = sizes[:1] + [2 * eighth, eighth, eighth]
    if first:
        sizes = [eighth, 3 * eighth] + sizes[1:]
    return sizes


def _peer_launch(h, hn, tb, after, wq_bf, keys1_bf, keys2_bf, peer_u):
    n = h.shape[0]
    eid_t, gate_t = _route(hn, tb, wq_bf, keys1_bf, keys2_bf)
    eid = eid_t.T
    pad = (-n) % (SC_WORKERS * SC_TOKENS)
    hn_p, eid_p = hn, eid
    if pad:
        spread = (jnp.arange(pad * PEER_SEL, dtype=jnp.int32) % peer_u.shape[0]).reshape(pad, PEER_SEL)
        hn_p = jnp.pad(hn, ((0, pad), (0, 0)))
        eid_p = jnp.concatenate([eid, spread], axis=0)
    hval = _peer_hval(hn_p, eid_p, peer_u, after)[:n]
    return h, eid, gate_t.T, hval


def _peer_finish(h, eid, gate, hval, tb, v_packed):
    a = _gate(hval, gate, tb)
    return _peer_values(h, a, eid * VT_ROWS, v_packed)


def _rope_tables(pos):
    half = HEAD_DIM // 2
    inv = ROPE_THETA ** (-jnp.arange(half, dtype=F32) / half)
    ang = pos.astype(F32)[:, None] * inv[None, :]
    cos = jnp.cos(ang)
    sin = jnp.sin(ang)
    reps = LANES // HEAD_DIM
    cos_f = jnp.tile(jnp.concatenate([cos, cos], axis=1), (1, reps))
    sin_s = jnp.tile(jnp.concatenate([-sin, sin], axis=1), (1, reps))
    return cos_f, sin_s


def _group_sum_matrix(width):
    g = jnp.arange(width) // HEAD_DIM
    return (g[:, None] == g[None, :]).astype(BF16)


def kernel(x_prompt, x_sample, cache_k_win, cache_v_win, state_conv, meta_tokens, norm_mix_g,
           w_in, q_norm_g, k_norm_g, attn_sinks, conv_w, conv_b, conv_ln_g, conv_ln_b, w_pw2,
           out_norm_attn_g, out_norm_conv_g, w_out, norm_ffn_g, peer_w_q, peer_keys1, peer_keys2,
           peer_u, peer_v):
    assert norm_mix_g.shape[0] == 1, "single-layer model"
    b, t, _ = x_prompt.shape
    s = x_sample.shape[0]
    w_buf = cache_k_win.shape[2]
    n = b * t

    g_mix = norm_mix_g[0][None, :]
    w_in_bf = w_in[0].astype(BF16)
    qg_t = jnp.tile(q_norm_g[0], N_HEADS)[None, :]
    kg_t = jnp.tile(k_norm_g[0], N_KV_HEADS)[None, :]
    gsum_q = _group_sum_matrix(D_ATTN)
    gsum_k = _group_sum_matrix(KV_DIM)
    sinks = attn_sinks[0]
    cw, cb = conv_w[0], conv_b[0][None, :]
    lg, lb = conv_ln_g[0][None, :], conv_ln_b[0][None, :]
    w2_bf = w_pw2[0].astype(BF16)
    g_a, g_c = out_norm_attn_g[0][None, :], out_norm_conv_g[0][None, :]
    w_out_a = w_out[0][:D_ATTN].astype(BF16)
    w_out_c = w_out[0][D_ATTN:].astype(BF16)
    g_f = norm_ffn_g[0][None, :]
    wq_bf = peer_w_q[0].astype(BF16)
    k1_bf = peer_keys1[0].astype(BF16)
    k2_bf = peer_keys2[0].astype(BF16)
    pu, vp = peer_u[0], _pack_value_table(peer_v[0])
    proj = functools.partial(_project, norm_g=g_mix, w_in_bf=w_in_bf, qg_t=qg_t, kg_t=kg_t,
                             gsum_q=gsum_q, gsum_k=gsum_k)
    conv = functools.partial(_conv_branch, conv_w=cw, conv_b=cb, ln_g=lg, ln_b=lb, w_pw2_bf=w2_bf)
    merge = functools.partial(_merge, g_a=g_a, g_c=g_c, w_out_a=w_out_a, w_out_c=w_out_c, g_f=g_f)

    tb = TB_DENSE
    no_dep = jnp.zeros((SC_TOKENS, LANES), F32)
    tab_x = _rope_tables(N_META + jnp.arange(t, dtype=jnp.int32))
    tab_m = _rope_tables(jnp.arange(N_META, dtype=jnp.int32))
    _, k_m, v_m, u_m = proj(meta_tokens, tab_m, 1, N_META, no_dep)
    halo0 = jnp.concatenate([jnp.zeros((CONV_HALO - N_META, D_CONV), F32), u_m], axis=0)

    k_meta = jnp.concatenate([jnp.zeros((META_PAD, KV_DIM), F32), k_m], axis=0)
    v_meta = jnp.concatenate([jnp.zeros((META_PAD, KV_DIM), F32), v_m], axis=0)

    xp = x_prompt.reshape(b * t, D_MODEL)

    def dense(bi, after):
        q, k, v, u = proj(xp, tab_x, t // tb, tb, after, row0=bi * t, nrows=t)
        o_attn = _prompt_attention(sinks, q, k, v, k_meta, v_meta)
        u3 = u.reshape(t // tb, tb, D_CONV)
        halo = jnp.concatenate([halo0[None], u3[:-1, tb - CONV_HALO:]], axis=0)
        o_conv = conv(halo, u, tb)
        hs, r0 = [], 0
        for rows in _chunk_plan(t, tb, bi == 0, bi == b - 1):
            hs.append(merge(xp, o_attn, o_conv, tb, x_row0=bi * t + r0, row0=r0, nrows=rows))
            r0 += rows
        return hs, k, v, u

    ck = cache_k_win[0]
    cv = cache_v_win[0]
    cs = state_conv[0]
    xs = x_sample.reshape(s, D_MODEL)

    def sample_launch(after):
        tab_s = _rope_tables(jnp.full((s,), PAST_LEN, jnp.int32))
        qs, ks, vs, us = proj(xs, tab_s, 1, s, no_dep)
        o_attn_s = _decode_attention(
            sinks,
            qs.reshape(s, N_HEADS, HEAD_DIM), ks, vs,
            ck.reshape(s, w_buf, KV_DIM), cv.reshape(s, w_buf, KV_DIM), min(SB_DECODE, s))
        o_attn_s = o_attn_s.reshape(s, D_ATTN)
        hist = jnp.concatenate([jnp.zeros((s, CONV_HALO - (CONV_WIDTH - 1), D_CONV), F32), cs], axis=1)
        us_blk = jnp.concatenate([us[:, None, :], jnp.zeros((s, SUBLANES - 1, D_CONV), F32)], axis=1)
        o_conv_s = conv(hist, us_blk.reshape(s * SUBLANES, D_CONV), SUBLANES)
        o_conv_s = o_conv_s.reshape(s, SUBLANES, D_CONV)[:, 0]
        hs, hns = merge(xs, o_attn_s, o_conv_s, s)
        return _peer_launch(hs, hns, s, after, wq_bf, k1_bf, k2_bf, pu), ks, vs, us

    launched, ys, kvu = [], [], []
    launch_s = y_sample = None
    cur = dense(0, no_dep)
    for bi in range(b):
        hs, k, v, u = cur
        kvu.append((k, v, u))
        for h, hn in hs:
            c = len(launched)
            if c >= PEER_DEPTH:
                ys.append(_peer_finish(*launched[c - PEER_DEPTH], TB_ROUTE, vp))
            after = ys[c - PEER_DEPTH][:SC_TOKENS] if c >= PEER_DEPTH else jnp.zeros((SC_TOKENS, D_MODEL), F32)
            if launch_s is not None and y_sample is None and c >= SAMPLE_FINISH_AT:
                y_sample = _peer_finish(*launch_s, s, vp)
                after = after + y_sample[:SC_TOKENS]
            launched.append(_peer_launch(h, hn, TB_ROUTE, after, wq_bf, k1_bf, k2_bf, pu))
        if launch_s is None:
            launch_s, ks, vs, us = sample_launch(launched[-1][1][:SC_TOKENS].astype(F32))
        if bi + 1 < b:
            cur = dense(bi + 1, launched[-1][1][:SC_TOKENS, :LANES].astype(F32))
    for c in range(len(ys), len(launched)):
        ys.append(_peer_finish(*launched[c], TB_ROUTE, vp))
    if y_sample is None:
        y_sample = _peer_finish(*launch_s, s, vp)
    y_prompt = jnp.concatenate(ys, axis=0).reshape(b, t, D_MODEL)
    y_sample = y_sample.reshape(s, 1, D_MODEL)

    new_k_prompt = jnp.stack([k[t - WINDOW:] for k, _, _ in kvu]).reshape(1, b, WINDOW, N_KV_HEADS, HEAD_DIM)
    new_v_prompt = jnp.stack([v[t - WINDOW:] for _, v, _ in kvu]).reshape(1, b, WINDOW, N_KV_HEADS, HEAD_DIM)
    new_conv_prompt = jnp.stack([u[t - (CONV_WIDTH - 1):] for _, _, u in kvu])[None]

    new_k_sample = jnp.concatenate([ck[:, 1:], ks.reshape(s, 1, N_KV_HEADS, HEAD_DIM)], axis=1)[None]
    new_v_sample = jnp.concatenate([cv[:, 1:], vs.reshape(s, 1, N_KV_HEADS, HEAD_DIM)], axis=1)[None]
    new_conv_sample = jnp.concatenate([cs[:, 1:], us[:, None, :]], axis=1)[None]
    if w_buf != WINDOW:
        raise NotImplementedError("cache window shorter than the attention window")

    return (y_prompt, y_sample, new_k_prompt, new_v_prompt, new_conv_prompt,
            new_k_sample, new_v_sample, new_conv_sample)
```

```python
import functools

import jax
import jax.numpy as jnp
from jax import lax
from jax.experimental import pallas as pl
from jax.experimental.pallas import tpu as pltpu
from jax.experimental.pallas import tpu_sc as plsc

D_MODEL = 1024
HEAD_DIM = 64
D_ATTN = 512
N_HEADS = 8
N_KV_HEADS = 2
KV_DIM = N_KV_HEADS * HEAD_DIM
D_CONV = 512
D_IN = D_ATTN + 2 * KV_DIM + 2 * D_CONV
CONV_WIDTH = 31
WINDOW = 128
BLOCK = 128
ROPE_THETA = 10000.0
N_META = 16
META_PAD = BLOCK - N_META
PEER_HEADS = 8
PEER_NKEYS = 128
PEER_TOPK = 16
PEER_SEL = PEER_HEADS * PEER_TOPK
EPS = 1e-6
PAST_LEN = 16384

LANES = 128
SC_CORES = 2
SC_SUBCORES = 16
SC_LANES = 16
SC_WORKERS = SC_CORES * SC_SUBCORES
VMEM_LIMIT = 48 * 1024 * 1024
SUBLANES = 8

TB_DENSE = 512
TB_ROUTE = 256
SB_DECODE = 32

F32 = jnp.float32
BF16 = jnp.bfloat16
NEG_INF = float("-inf")


def _tc_params(n_axes):
    return pltpu.CompilerParams(dimension_semantics=("arbitrary",) * n_axes,
                                vmem_limit_bytes=VMEM_LIMIT)


def _full(shape):
    nd = len(shape)
    return pl.BlockSpec(shape, lambda *_: (0,) * nd)


def _group_mean(sq, gsum_ref):
    hi = sq.astype(BF16)
    lo = (sq - hi.astype(F32)).astype(BF16)
    g = gsum_ref[...]
    s = jnp.dot(hi, g, preferred_element_type=F32) + jnp.dot(lo, g, preferred_element_type=F32)
    return s * (1.0 / HEAD_DIM)


def _rope(xn, cos_f, sin_s, first_half):
    outs = []
    for s in range(xn.shape[1] // LANES):
        xs = xn[:, s * LANES:(s + 1) * LANES]
        partner = jnp.where(first_half, pltpu.roll(xs, LANES - HEAD_DIM // 2, axis=1),
                            pltpu.roll(xs, HEAD_DIM // 2, axis=1))
        outs.append(xs * cos_f + partner * sin_s)
    return outs[0] if len(outs) == 1 else jnp.concatenate(outs, axis=1)


def _proj_kernel(x_ref, g_ref, w_ref, qg_ref, kg_ref, cos_ref, sin_ref, gq_ref, gk_ref, after_ref,
                 q_out, k_out, v_out, u_out):
    del after_ref
    x = x_ref[...]
    n = x * lax.rsqrt(jnp.mean(x * x, axis=-1, keepdims=True) + EPS) * g_ref[...]
    p = jnp.dot(n.astype(BF16), w_ref[...], preferred_element_type=F32)
    q = p[:, :D_ATTN]
    k = p[:, D_ATTN:D_ATTN + KV_DIM]
    v = p[:, D_ATTN + KV_DIM:D_ATTN + 2 * KV_DIM]
    ga = p[:, D_ATTN + 2 * KV_DIM:D_ATTN + 2 * KV_DIM + D_CONV]
    gb = p[:, D_ATTN + 2 * KV_DIM + D_CONV:]
    cos_f = cos_ref[...]
    sin_s = sin_ref[...]
    lane = lax.broadcasted_iota(jnp.int32, (x.shape[0], LANES), 1)
    first_half = (lane % HEAD_DIM) < (HEAD_DIM // 2)
    qn = q * lax.rsqrt(_group_mean(q * q, gq_ref) + EPS) * qg_ref[...]
    kn = k * lax.rsqrt(_group_mean(k * k, gk_ref) + EPS) * kg_ref[...]
    q_out[...] = _rope(qn, cos_f, sin_s, first_half)
    k_out[...] = _rope(kn, cos_f, sin_s, first_half)
    v_out[...] = v
    u_out[...] = ga * (1.0 / (1.0 + jnp.exp(-gb)))


def _project(x, pos_tables, n_table_blocks, tb, after, norm_g, w_in_bf, qg_t, kg_t, gsum_q, gsum_k,
             row0=0, nrows=None):
    n = x.shape[0] if nrows is None else nrows
    cos_t, sin_t = pos_tables
    nb = n // tb
    blk0 = row0 // tb
    tab_spec = pl.BlockSpec((tb, LANES), lambda i: (i % n_table_blocks, 0))
    row = lambda w: pl.BlockSpec((tb, w), lambda i: (i, 0))
    return pl.pallas_call(
        _proj_kernel,
        grid=(nb,),
        in_specs=[pl.BlockSpec((tb, D_MODEL), lambda i: (i + blk0, 0)),
                  _full((1, D_MODEL)), _full((D_MODEL, D_IN)),
                  _full((1, D_ATTN)), _full((1, KV_DIM)), tab_spec, tab_spec,
                  _full((D_ATTN, D_ATTN)), _full((KV_DIM, KV_DIM)),
                  pl.BlockSpec(memory_space=pl.ANY)],
        out_specs=[row(D_ATTN), row(KV_DIM), row(KV_DIM), row(D_CONV)],
        out_shape=[jax.ShapeDtypeStruct((n, D_ATTN), F32), jax.ShapeDtypeStruct((n, KV_DIM), F32),
                   jax.ShapeDtypeStruct((n, KV_DIM), F32), jax.ShapeDtypeStruct((n, D_CONV), F32)],
        compiler_params=_tc_params(1),
        name="proj",
    )(x, norm_g, w_in_bf, qg_t, kg_t, cos_t, sin_t, gsum_q, gsum_k, after)


def _attn_kernel(sink_ref, q_ref, kp_ref, kc_ref, vp_ref, vc_ref, km_ref, vm_ref, o_ref):
    j = pl.program_id(0)
    r = lax.broadcasted_iota(jnp.int32, (BLOCK, 2 * BLOCK), 0)
    c = lax.broadcasted_iota(jnp.int32, (BLOCK, 2 * BLOCK), 1)
    ok = (c > r) & (c <= r + WINDOW) & ((j > 0) | (c >= META_PAD))
    grp = N_HEADS // N_KV_HEADS
    ok = jnp.concatenate([ok] * grp, axis=0)
    first = j == 0
    k_all = jnp.concatenate([jnp.where(first, km_ref[...], kp_ref[...]), kc_ref[...]], axis=0)
    v_all = jnp.concatenate([jnp.where(first, vm_ref[...], vp_ref[...]), vc_ref[...]], axis=0)
    q = q_ref[...]
    outs = []
    for g in range(N_KV_HEADS):
        lanes = slice(g * HEAD_DIM, (g + 1) * HEAD_DIM)
        k = k_all[:, lanes].astype(BF16)
        v = v_all[:, lanes].astype(BF16)
        q4 = jnp.concatenate([q[:, (g * grp + i) * HEAD_DIM:(g * grp + i + 1) * HEAD_DIM]
                              for i in range(grp)], axis=0).astype(BF16)
        s = lax.dot_general(q4, k, (((1,), (1,)), ((), ())), preferred_element_type=F32)
        s = jnp.where(ok, s * (HEAD_DIM ** -0.5), NEG_INF)
        sink = jnp.concatenate(
            [jnp.full((BLOCK, 1), sink_ref[g * grp + i], F32) for i in range(grp)], axis=0)
        m = jnp.maximum(jnp.max(s, axis=1, keepdims=True), sink)
        p = jnp.exp(s - m)
        den = jnp.sum(p, axis=1, keepdims=True) + jnp.exp(sink - m)
        o = jnp.dot(p.astype(BF16), v, preferred_element_type=F32) / den
        outs += [o[i * BLOCK:(i + 1) * BLOCK] for i in range(grp)]
    o_ref[...] = jnp.concatenate(outs, axis=1)


def _prompt_attention(sinks, q, k, v, k_meta, v_meta):
    t = q.shape[0]
    kv_prev = pl.BlockSpec((BLOCK, KV_DIM), lambda j: (jnp.maximum(j - 1, 0), 0))
    kv_cur = pl.BlockSpec((BLOCK, KV_DIM), lambda j: (j, 0))
    q_spec = pl.BlockSpec((BLOCK, D_ATTN), lambda j: (j, 0))
    meta = _full((BLOCK, KV_DIM))
    return pl.pallas_call(
        _attn_kernel,
        grid=(t // BLOCK,),
        in_specs=[pl.BlockSpec(memory_space=pltpu.SMEM), q_spec, kv_prev, kv_cur, kv_prev, kv_cur,
                  meta, meta],
        out_specs=q_spec,
        out_shape=jax.ShapeDtypeStruct((t, D_ATTN), F32),
        compiler_params=_tc_params(1),
        name="attn",
    )(sinks, q, k, k, v, v, k_meta, v_meta)


def _dec_attn_kernel(sink_ref, q_ref, kn_ref, vn_ref, ck_ref, cv_ref, o_ref):
    grp = N_HEADS // N_KV_HEADS
    sb, w_buf = ck_ref.shape[0], ck_ref.shape[1]
    q = q_ref[...]
    qb = q.astype(BF16)
    head = lax.broadcasted_iota(jnp.int32, (sb, N_HEADS, 1), 1)
    in_g0 = head < grp
    kv_lanes = [slice(g * HEAD_DIM, (g + 1) * HEAD_DIM) for g in range(N_KV_HEADS)]
    ck = ck_ref[...]
    cv = cv_ref[...]
    s_g = [jnp.einsum("shd,swd->shw", qb, ck[:, :, ln].astype(BF16), preferred_element_type=F32)
           for ln in kv_lanes]
    s = jnp.where(in_g0, s_g[0], s_g[1]) * (HEAD_DIM ** -0.5)
    key_ok = lax.broadcasted_iota(jnp.int32, (sb, N_HEADS, w_buf), 2) >= 1
    s = jnp.where(key_ok, s, NEG_INF)
    rnd = lambda a: a.astype(BF16).astype(F32)
    kn_all, vn_all = kn_ref[...], vn_ref[...]
    kn = jnp.where(in_g0, kn_all[:, None, kv_lanes[0]], kn_all[:, None, kv_lanes[1]])
    vn = jnp.where(in_g0, vn_all[:, None, kv_lanes[0]], vn_all[:, None, kv_lanes[1]])
    s_self = jnp.sum(rnd(q) * rnd(kn), axis=-1, keepdims=True) * (HEAD_DIM ** -0.5)
    sink = sink_ref[...]
    m = jnp.maximum(jnp.maximum(jnp.max(s, axis=-1, keepdims=True), s_self), sink)
    p = jnp.exp(s - m)
    p_self = jnp.exp(s_self - m)
    den = jnp.sum(p, axis=-1, keepdims=True) + p_self + jnp.exp(sink - m)
    pb = p.astype(BF16)
    o_g = [jnp.einsum("shw,swd->shd", pb, cv[:, :, ln].astype(BF16), preferred_element_type=F32)
           for ln in kv_lanes]
    o = jnp.where(in_g0, o_g[0], o_g[1]) + rnd(p_self) * rnd(vn)
    o_ref[...] = o / den


def _decode_attention(sinks, q3, kn_t, vn_t, ck_t, cv_t, sb):
    s = q3.shape[0]
    w_buf = ck_t.shape[1]
    qs = pl.BlockSpec((sb, N_HEADS, HEAD_DIM), lambda i: (i, 0, 0))
    ns = pl.BlockSpec((sb, KV_DIM), lambda i: (i, 0))
    cs = pl.BlockSpec((sb, w_buf, KV_DIM), lambda i: (i, 0, 0))
    return pl.pallas_call(
        _dec_attn_kernel,
        grid=(s // sb,),
        in_specs=[_full((1, N_HEADS, 1)), qs, ns, ns, cs, cs],
        out_specs=qs,
        out_shape=jax.ShapeDtypeStruct(q3.shape, F32),
        compiler_params=_tc_params(1),
        name="dec_attn",
    )(sinks.reshape(1, N_HEADS, 1), q3, kn_t, vn_t, ck_t, cv_t)


CONV_HALO = 32
CONV_ROWS = 64


def _conv_kernel(halo_ref, u_ref, cw_ref, cb_ref, lg_ref, lb_ref, w2_ref, o_ref, ucat):
    tb = u_ref.shape[0]
    ucat[0:CONV_HALO, :] = halo_ref[0]
    ucat[CONV_HALO:, :] = u_ref[...]
    first = CONV_HALO - (CONV_WIDTH - 1)
    rows = min(CONV_ROWS, tb)
    for r0 in range(0, tb, rows):
        acc = jnp.zeros((rows, D_CONV), F32)
        for j in range(CONV_WIDTH):
            acc = acc + ucat[r0 + first + j:r0 + first + j + rows, :] * cw_ref[j:j + 1, :]
        y = acc + cb_ref[...]
        yc = y - jnp.mean(y, axis=-1, keepdims=True)
        yn = yc * lax.rsqrt(jnp.mean(yc * yc, axis=-1, keepdims=True) + EPS)
        yn = yn * lg_ref[...] + lb_ref[...]
        act = yn * (1.0 / (1.0 + jnp.exp(-yn)))
        o_ref[r0:r0 + rows, :] = jnp.dot(act.astype(BF16), w2_ref[...],
                                              preferred_element_type=F32)


def _conv_branch(halo, u, tb, conv_w, conv_b, ln_g, ln_b, w_pw2_bf):
    n = u.shape[0]
    return pl.pallas_call(
        _conv_kernel,
        grid=(n // tb,),
        in_specs=[pl.BlockSpec((1, CONV_HALO, D_CONV), lambda i: (i, 0, 0)),
                  pl.BlockSpec((tb, D_CONV), lambda i: (i, 0)),
                  _full((CONV_WIDTH, D_CONV)), _full((1, D_CONV)), _full((1, D_CONV)),
                  _full((1, D_CONV)), _full((D_CONV, D_CONV))],
        out_specs=pl.BlockSpec((tb, D_CONV), lambda i: (i, 0)),
        out_shape=jax.ShapeDtypeStruct((n, D_CONV), F32),
        scratch_shapes=[pltpu.VMEM((tb + CONV_HALO, D_CONV), F32)],
        compiler_params=_tc_params(1),
        name="conv",
    )(halo, u, conv_w, conv_b, ln_g, ln_b, w_pw2_bf)


def _rms(x, g):
    return x * lax.rsqrt(jnp.mean(x * x, axis=-1, keepdims=True) + EPS) * g


def _merge_kernel(x_ref, oa_ref, oc_ref, ga_ref, gc_ref, wa_ref, wc_ref, gf_ref, h_out, hn_out):
    a = _rms(oa_ref[...], ga_ref[...]).astype(BF16)
    c = _rms(oc_ref[...], gc_ref[...]).astype(BF16)
    h = x_ref[...] + (jnp.dot(a, wa_ref[...], preferred_element_type=F32)
                      + jnp.dot(c, wc_ref[...], preferred_element_type=F32))
    h_out[...] = h
    hn_out[...] = _rms(h, gf_ref[...])


def _merge(x, oa, oc, tb, g_a, g_c, w_out_a, w_out_c, g_f, x_row0=0, row0=0, nrows=None):
    n = oa.shape[0] if nrows is None else nrows
    xb0, b0 = x_row0 // tb, row0 // tb
    row = lambda w: pl.BlockSpec((tb, w), lambda i: (i, 0))
    src = lambda w: pl.BlockSpec((tb, w), lambda i: (i + b0, 0))
    return pl.pallas_call(
        _merge_kernel,
        grid=(n // tb,),
        in_specs=[pl.BlockSpec((tb, D_MODEL), lambda i: (i + xb0, 0)), src(D_ATTN), src(D_CONV),
                  _full((1, D_ATTN)), _full((1, D_CONV)),
                  _full((D_ATTN, D_MODEL)), _full((D_CONV, D_MODEL)), _full((1, D_MODEL))],
        out_specs=[row(D_MODEL), row(D_MODEL)],
        out_shape=[jax.ShapeDtypeStruct((n, D_MODEL), F32)] * 2,
        compiler_params=_tc_params(1),
        name="merge",
    )(x, oa, oc, g_a, g_c, w_out_a, w_out_c, g_f)


ID_BIG = 1e9


def _topk_rows(s, k):
    rows = lax.broadcasted_iota(jnp.int32, s.shape, 0).astype(F32)
    vals, idxs = [], []
    for _ in range(k):
        m = jnp.max(s, axis=0, keepdims=True)
        idx = jnp.min(jnp.where(s == m, rows, ID_BIG), axis=0, keepdims=True)
        vals.append(m)
        idxs.append(idx)
        s = jnp.where(rows == idx, NEG_INF, s)
    return jnp.concatenate(vals, axis=0), jnp.concatenate(idxs, axis=0)


PAIR_B_WIDE = 8


def _route_kernel(hn_ref, wq_ref, k1_ref, k2_ref, eid_out, gate_out):
    tb = hn_ref.shape[0]
    q = jnp.dot(hn_ref[...].astype(BF16), wq_ref[...], preferred_element_type=F32).astype(BF16)
    k1 = k1_ref[...]
    k2 = k2_ref[...]
    nt = (((1,), (1,)), ((), ()))
    r = lax.broadcasted_iota(jnp.int32, (PEER_TOPK + (PAIR_B_WIDE - 1) * PAIR_B_WIDE + PAIR_B_WIDE, tb), 0)
    mid = r - PEER_TOPK
    flat = jnp.where(r < PEER_TOPK, r,
                     jnp.where(mid < (PAIR_B_WIDE - 1) * PAIR_B_WIDE,
                               (1 + mid // PAIR_B_WIDE) * PEER_TOPK + mid % PAIR_B_WIDE,
                               (PAIR_B_WIDE + mid - (PAIR_B_WIDE - 1) * PAIR_B_WIDE) * PEER_TOPK)).astype(F32)
    half = PEER_NKEYS
    for h in range(PEER_HEADS):
        q1 = q[:, (2 * h) * half:(2 * h + 1) * half]
        q2 = q[:, (2 * h + 1) * half:(2 * h + 2) * half]
        s1 = lax.dot_general(k1, q1, nt, preferred_element_type=F32)
        s2 = lax.dot_general(k2, q2, nt, preferred_element_type=F32)
        v1, i1 = _topk_rows(s1, PEER_TOPK)
        v2, i2 = _topk_rows(s2, PEER_TOPK)
        e1 = i1 * PEER_NKEYS
        cand = jnp.concatenate(
            [v1[0:1] + v2]
            + [v1[a:a + 1] + v2[0:PAIR_B_WIDE] for a in range(1, PAIR_B_WIDE)]
            + [v1[PAIR_B_WIDE:] + v2[0:1]], axis=0)
        cid = jnp.concatenate(
            [e1[0:1] + i2]
            + [e1[a:a + 1] + i2[0:PAIR_B_WIDE] for a in range(1, PAIR_B_WIDE)]
            + [e1[PAIR_B_WIDE:] + i2[0:1]], axis=0)
        scs, eids = [], []
        for _ in range(PEER_TOPK):
            m = jnp.max(cand, axis=0, keepdims=True)
            jsel = jnp.min(jnp.where(cand == m, flat, ID_BIG), axis=0, keepdims=True)
            hit = flat == jsel
            eids.append(jnp.max(jnp.where(hit, cid, -1.0), axis=0, keepdims=True))
            scs.append(m)
            cand = jnp.where(hit, NEG_INF, cand)
        sc = jnp.concatenate(scs, axis=0)
        e = jnp.exp(sc - sc[0:1])
        gate_out[h * PEER_TOPK:(h + 1) * PEER_TOPK, :] = e / jnp.sum(e, axis=0, keepdims=True)
        eid_out[h * PEER_TOPK:(h + 1) * PEER_TOPK, :] = jnp.concatenate(eids, axis=0).astype(jnp.int32)


def _route(hn, tb, wq_bf, keys1_bf, keys2_bf):
    n = hn.shape[0]
    col = pl.BlockSpec((PEER_SEL, tb), lambda i: (0, i))
    return pl.pallas_call(
        _route_kernel,
        grid=(n // tb,),
        in_specs=[pl.BlockSpec((tb, D_MODEL), lambda i: (i, 0)),
                  _full((D_MODEL, 2 * PEER_NKEYS * PEER_HEADS)),
                  _full((PEER_NKEYS, PEER_NKEYS)), _full((PEER_NKEYS, PEER_NKEYS))],
        out_specs=[col, col],
        out_shape=[jax.ShapeDtypeStruct((PEER_SEL, n), jnp.int32),
                   jax.ShapeDtypeStruct((PEER_SEL, n), F32)],
        compiler_params=_tc_params(1),
        name="route",
    )(hn, wq_bf, keys1_bf, keys2_bf)


SC_ROWS = 32
SC_GROUP = SC_LANES
SC_TOKENS = 8
SC_CHUNKS = PEER_SEL // SC_ROWS

_SC_PARAMS = pltpu.CompilerParams(needs_layout_passes=False)


def _sc_mesh():
    return plsc.VectorSubcoreMesh(core_axis_name="c", subcore_axis_name="s",
                                  num_cores=SC_CORES, num_subcores=SC_SUBCORES)


def _sc_token_loop(n_batches, table_hbm, eid_v, rbufs, rsems, prefetch, prefetch_wait, store,
                   contract):
    def gather(slot, tl, ch, p):
        idx = eid_v.at[slot, tl, pl.ds(ch * SC_ROWS, SC_ROWS)]
        return pltpu.make_async_copy(table_hbm.at[idx], rbufs[p], rsems[p])

    prefetch(0, 0)
    prefetch_wait()
    gather(0, 0, 0, 0).start()

    @pl.loop(0, n_batches * SC_TOKENS)
    def _(i):
        b = i // SC_TOKENS
        tl = i % SC_TOKENS
        slot = b % 2
        more = b + 1 < n_batches

        @pl.when(jnp.logical_and(tl == 0, more))
        def _():
            prefetch(b + 1, 1 - slot)

        for ch in range(SC_CHUNKS):
            p = ch % 2
            if ch + 1 < SC_CHUNKS:
                gather(slot, tl, ch + 1, 1 - p).start()
            else:
                @pl.when(tl + 1 < SC_TOKENS)
                def _():
                    gather(slot, tl + 1, 0, 1 - p).start()

                @pl.when(jnp.logical_and(tl + 1 == SC_TOKENS, more))
                def _():
                    prefetch_wait()
                    gather(1 - slot, 0, 0, 1 - p).start()
            gather(slot, tl, ch, p).wait()
            contract(slot, tl, ch, rbufs[p])

        @pl.when(tl + 1 == SC_TOKENS)
        def _():
            @pl.when(b >= 1)
            def _():
                store(b - 1, 1 - slot).wait()
            store(b, slot).start()

    store(n_batches - 1, (n_batches - 1) % 2).wait()


def _sc_batches(n):
    assert n % (SC_WORKERS * SC_TOKENS) == 0, n
    return n // (SC_WORKERS * SC_TOKENS)


def _peer_hval(hn, eid, peer_u, after):
    n = hn.shape[0]
    nbw = _sc_batches(n)

    @functools.partial(
        pl.kernel, mesh=_sc_mesh(),
        out_type=jax.ShapeDtypeStruct((n // SC_TOKENS, SC_TOKENS, PEER_SEL), F32),
        scratch_types=[
            pltpu.VMEM((2, SC_TOKENS, PEER_SEL), jnp.int32),
            pltpu.VMEM((2, SC_TOKENS, D_MODEL), F32),
            pltpu.VMEM((SC_ROWS, D_MODEL), F32),
            pltpu.VMEM((SC_ROWS, D_MODEL), F32),
            pltpu.VMEM((2, SC_TOKENS, PEER_SEL), F32),
            pltpu.VMEM((SC_GROUP, SC_LANES), F32),
            pltpu.SemaphoreType.DMA, pltpu.SemaphoreType.DMA,
            pltpu.SemaphoreType.DMA, pltpu.SemaphoreType.DMA,
        ],
        compiler_params=_SC_PARAMS, name="peer_hval",
        cost_estimate=pl.CostEstimate(flops=2 * n * PEER_SEL * D_MODEL, transcendentals=0,
                                      bytes_accessed=n * PEER_SEL * D_MODEL * 4))
    def k(x_hbm, eid_hbm, u_hbm, after_hbm, o_hbm, eid_v, x_v, r0, r1, h_v, tr, sr0, sr1, spf, sout):
        del after_hbm
        blk0 = (lax.axis_index("s") * SC_CORES + lax.axis_index("c")) * nbw
        lane = lax.iota(jnp.int32, SC_LANES)
        zero = jnp.zeros((SC_LANES,), F32)

        def prefetch(b, slot):
            pltpu.async_copy(eid_hbm.at[blk0 + b], eid_v.at[slot], spf)
            pltpu.async_copy(x_hbm.at[blk0 + b], x_v.at[slot], spf)

        def prefetch_wait():
            pltpu.make_async_copy(eid_hbm.at[0], eid_v.at[0], spf).wait()
            pltpu.make_async_copy(x_hbm.at[0], x_v.at[0], spf).wait()

        def store(b, slot):
            return pltpu.make_async_copy(h_v.at[slot], o_hbm.at[blk0 + b], sout)

        def contract(slot, tl, ch, rbuf):
            for g in range(SC_ROWS // SC_GROUP):
                def body(c, accs):
                    xc = x_v[slot, tl, pl.ds(c * SC_LANES, SC_LANES)]
                    return tuple(
                        accs[r] + rbuf[g * SC_GROUP + r, pl.ds(c * SC_LANES, SC_LANES)] * xc
                        for r in range(SC_GROUP))
                accs = lax.fori_loop(0, D_MODEL // SC_LANES, body, (zero,) * SC_GROUP)
                for r in range(SC_GROUP):
                    tr[r, :] = accs[r]
                res = zero
                for jj in range(SC_LANES):
                    res = res + plsc.load_gather(tr, [lane, jnp.full((SC_LANES,), jj, jnp.int32)])
                h_v[slot, tl, pl.ds(ch * SC_ROWS + g * SC_GROUP, SC_GROUP)] = res

        _sc_token_loop(nbw, u_hbm, eid_v, (r0, r1), (sr0, sr1), prefetch, prefetch_wait, store,
                       contract)

    out = k(hn.reshape(n // SC_TOKENS, SC_TOKENS, D_MODEL),
            eid.reshape(n // SC_TOKENS, SC_TOKENS, PEER_SEL), peer_u, after)
    return out.reshape(n, PEER_SEL)


def _gate_kernel(hv_ref, gate_ref, a_out):
    hv = hv_ref[...]
    gelu = hv * (lax.erf(hv * (2.0 ** -0.5)) + 1.0) * 0.5
    a_out[...] = gate_ref[...] * gelu


def _gate(hval, gate, tb):
    n = hval.shape[0]
    row = pl.BlockSpec((tb, PEER_SEL), lambda i: (i, 0))
    return pl.pallas_call(
        _gate_kernel, grid=(n // tb,), in_specs=[row, row], out_specs=row,
        out_shape=jax.ShapeDtypeStruct((n, PEER_SEL), F32),
        compiler_params=_tc_params(1), name="gate",
    )(hval, gate)


VT_ROWS = D_MODEL // 2 // LANES
VT_TOKENS = 128
VT_UNROLL = 4
HI_MASK = -65536


def _pack_value_table(peer_v):
    e = peer_v.shape[0]
    bits = lax.bitcast_convert_type(peer_v.astype(BF16), jnp.uint16).astype(jnp.uint32)
    words = bits[:, :D_MODEL // 2] | (bits[:, D_MODEL // 2:] << 16)
    return lax.bitcast_convert_type(words, jnp.int32).reshape(e * VT_ROWS, LANES)


def _vside_kernel(row_s, a_s, h_ref, tab_ref, y_ref):
    tb = h_ref.shape[0]

    def token_group(p, carry):
        zero = jnp.zeros((VT_ROWS, LANES), F32)
        ts = tuple(VT_UNROLL * p + k for k in range(VT_UNROLL))
        lo = [zero] * VT_UNROLL
        hi = [zero] * VT_UNROLL
        for e in range(PEER_SEL):
            for k, t in enumerate(ts):
                r0 = pl.multiple_of(row_s[t, e], VT_ROWS)
                w = tab_ref[pl.ds(r0, VT_ROWS), :]
                coef = a_s[t, e]
                lo[k] = lo[k] + coef * pltpu.bitcast(w << 16, F32)
                hi[k] = hi[k] + coef * pltpu.bitcast(w & HI_MASK, F32)
        for k, t in enumerate(ts):
            y_ref[t] = h_ref[t] + jnp.concatenate([lo[k], hi[k]], axis=0)
        return carry

    lax.fori_loop(0, tb // VT_UNROLL, token_group, 0)


def _peer_values(h, a, row_ids, v_packed):
    n = h.shape[0]
    tb = min(VT_TOKENS, n)
    smem = pl.BlockSpec((tb, PEER_SEL), lambda i: (i, 0), memory_space=pltpu.SMEM)
    slab = pl.BlockSpec((tb, 2 * VT_ROWS, LANES), lambda i: (i, 0, 0))
    table = pl.BlockSpec(v_packed.shape, lambda i: (0, 0), pipeline_mode=pl.Buffered(1))
    table_bytes = v_packed.shape[0] * LANES * 4
    y = pl.pallas_call(
        _vside_kernel,
        grid=(n // tb,),
        in_specs=[smem, smem, slab, table],
        out_specs=slab,
        out_shape=jax.ShapeDtypeStruct((n, 2 * VT_ROWS, LANES), F32),
        compiler_params=pltpu.CompilerParams(
            dimension_semantics=("arbitrary",),
            vmem_limit_bytes=table_bytes + 8 * 1024 * 1024),
        name="peer_values",
        cost_estimate=pl.CostEstimate(flops=2 * n * PEER_SEL * D_MODEL, transcendentals=0,
                                      bytes_accessed=table_bytes + n * PEER_SEL * D_MODEL * 2),
    )(row_ids, a, h.reshape(n, 2 * VT_ROWS, LANES), v_packed)
    return y.reshape(n, D_MODEL)


PEER_DEPTH = 2


def _chunk_plan(t, tb, first, last):
    eighth = t // 8
    if eighth % tb:
        return [t]
    sizes = [4 * eighth, 4 * eighth]
    if last:
        tail = [eighth, eighth] if (eighth // 2) % tb else [eighth, eighth // 2, eighth // 2]
        sizes = sizes[:1] + [2 * eighth] + tail
    if first:
        sizes = [eighth, 3 * eighth] + sizes[1:]
    return sizes


def _peer_launch(h, hn, tb, after, wq_bf, keys1_bf, keys2_bf, peer_u):
    n = h.shape[0]
    eid_t, gate_t = _route(hn, tb, wq_bf, keys1_bf, keys2_bf)
    eid = eid_t.T
    pad = (-n) % (SC_WORKERS * SC_TOKENS)
    hn_p, eid_p = hn, eid
    if pad:
        spread = (jnp.arange(pad * PEER_SEL, dtype=jnp.int32) % peer_u.shape[0]).reshape(pad, PEER_SEL)
        hn_p = jnp.pad(hn, ((0, pad), (0, 0)))
        eid_p = jnp.concatenate([eid, spread], axis=0)
    hval = _peer_hval(hn_p, eid_p, peer_u, after)[:n]
    return h, eid, gate_t.T, hval


def _peer_finish(h, eid, gate, hval, tb, v_packed):
    a = _gate(hval, gate, tb)
    return _peer_values(h, a, eid * VT_ROWS, v_packed)


def _rope_tables(pos):
    half = HEAD_DIM // 2
    inv = ROPE_THETA ** (-jnp.arange(half, dtype=F32) / half)
    ang = pos.astype(F32)[:, None] * inv[None, :]
    cos = jnp.cos(ang)
    sin = jnp.sin(ang)
    reps = LANES // HEAD_DIM
    cos_f = jnp.tile(jnp.concatenate([cos, cos], axis=1), (1, reps))
    sin_s = jnp.tile(jnp.concatenate([-sin, sin], axis=1), (1, reps))
    return cos_f, sin_s


def _group_sum_matrix(width):
    g = jnp.arange(width) // HEAD_DIM
    return (g[:, None] == g[None, :]).astype(BF16)


def kernel(x_prompt, x_sample, cache_k_win, cache_v_win, state_conv, meta_tokens, norm_mix_g,
           w_in, q_norm_g, k_norm_g, attn_sinks, conv_w, conv_b, conv_ln_g, conv_ln_b, w_pw2,
           out_norm_attn_g, out_norm_conv_g, w_out, norm_ffn_g, peer_w_q, peer_keys1, peer_keys2,
           peer_u, peer_v):
    assert norm_mix_g.shape[0] == 1, "single-layer model"
    b, t, _ = x_prompt.shape
    s = x_sample.shape[0]
    w_buf = cache_k_win.shape[2]
    n = b * t

    g_mix = norm_mix_g[0][None, :]
    w_in_bf = w_in[0].astype(BF16)
    qg_t = jnp.tile(q_norm_g[0], N_HEADS)[None, :]
    kg_t = jnp.tile(k_norm_g[0], N_KV_HEADS)[None, :]
    gsum_q = _group_sum_matrix(D_ATTN)
    gsum_k = _group_sum_matrix(KV_DIM)
    sinks = attn_sinks[0]
    cw, cb = conv_w[0], conv_b[0][None, :]
    lg, lb = conv_ln_g[0][None, :], conv_ln_b[0][None, :]
    w2_bf = w_pw2[0].astype(BF16)
    g_a, g_c = out_norm_attn_g[0][None, :], out_norm_conv_g[0][None, :]
    w_out_a = w_out[0][:D_ATTN].astype(BF16)
    w_out_c = w_out[0][D_ATTN:].astype(BF16)
    g_f = norm_ffn_g[0][None, :]
    wq_bf = peer_w_q[0].astype(BF16)
    k1_bf = peer_keys1[0].astype(BF16)
    k2_bf = peer_keys2[0].astype(BF16)
    pu, vp = peer_u[0], _pack_value_table(peer_v[0])
    proj = functools.partial(_project, norm_g=g_mix, w_in_bf=w_in_bf, qg_t=qg_t, kg_t=kg_t,
                             gsum_q=gsum_q, gsum_k=gsum_k)
    conv = functools.partial(_conv_branch, conv_w=cw, conv_b=cb, ln_g=lg, ln_b=lb, w_pw2_bf=w2_bf)
    merge = functools.partial(_merge, g_a=g_a, g_c=g_c, w_out_a=w_out_a, w_out_c=w_out_c, g_f=g_f)

    tb = TB_DENSE
    no_dep = jnp.zeros((SC_TOKENS, LANES), F32)
    tab_x = _rope_tables(N_META + jnp.arange(t, dtype=jnp.int32))
    tab_m = _rope_tables(jnp.arange(N_META, dtype=jnp.int32))
    _, k_m, v_m, u_m = proj(meta_tokens, tab_m, 1, N_META, no_dep)
    halo0 = jnp.concatenate([jnp.zeros((CONV_HALO - N_META, D_CONV), F32), u_m], axis=0)

    k_meta = jnp.concatenate([jnp.zeros((META_PAD, KV_DIM), F32), k_m], axis=0)
    v_meta = jnp.concatenate([jnp.zeros((META_PAD, KV_DIM), F32), v_m], axis=0)

    xp = x_prompt.reshape(b * t, D_MODEL)

    def dense(bi, after):
        q, k, v, u = proj(xp, tab_x, t // tb, tb, after, row0=bi * t, nrows=t)
        o_attn = _prompt_attention(sinks, q, k, v, k_meta, v_meta)
        u3 = u.reshape(t // tb, tb, D_CONV)
        halo = jnp.concatenate([halo0[None], u3[:-1, tb - CONV_HALO:]], axis=0)
        o_conv = conv(halo, u, tb)
        hs, r0 = [], 0
        for rows in _chunk_plan(t, tb, bi == 0, bi == b - 1):
            hs.append(merge(xp, o_attn, o_conv, tb, x_row0=bi * t + r0, row0=r0, nrows=rows))
            r0 += rows
        return hs, k, v, u

    launched, ys, kvu = [], [], []
    cur = dense(0, no_dep)
    for bi in range(b):
        hs, k, v, u = cur
        kvu.append((k, v, u))
        for h, hn in hs:
            c = len(launched)
            if c >= PEER_DEPTH:
                ys.append(_peer_finish(*launched[c - PEER_DEPTH], TB_ROUTE, vp))
            after = ys[c - PEER_DEPTH][:SC_TOKENS] if c >= PEER_DEPTH else jnp.zeros((SC_TOKENS, D_MODEL), F32)
            launched.append(_peer_launch(h, hn, TB_ROUTE, after, wq_bf, k1_bf, k2_bf, pu))
        if bi + 1 < b:
            cur = dense(bi + 1, launched[-1][1][:SC_TOKENS, :LANES].astype(F32))
    for c in range(len(ys), len(launched)):
        ys.append(_peer_finish(*launched[c], TB_ROUTE, vp))
    y_prompt = jnp.concatenate(ys, axis=0).reshape(b, t, D_MODEL)

    new_k_prompt = jnp.stack([k[t - WINDOW:] for k, _, _ in kvu]).reshape(1, b, WINDOW, N_KV_HEADS, HEAD_DIM)
    new_v_prompt = jnp.stack([v[t - WINDOW:] for _, v, _ in kvu]).reshape(1, b, WINDOW, N_KV_HEADS, HEAD_DIM)
    new_conv_prompt = jnp.stack([u[t - (CONV_WIDTH - 1):] for _, _, u in kvu])[None]

    xs = x_sample.reshape(s, D_MODEL)
    tab_s = _rope_tables(jnp.full((s,), PAST_LEN, jnp.int32))
    qs, ks, vs, us = proj(xs, tab_s, 1, s, no_dep)
    ck = cache_k_win[0]
    cv = cache_v_win[0]
    o_attn_s = _decode_attention(
        sinks,
        qs.reshape(s, N_HEADS, HEAD_DIM), ks, vs,
        ck.reshape(s, w_buf, KV_DIM), cv.reshape(s, w_buf, KV_DIM), min(SB_DECODE, s))
    o_attn_s = o_attn_s.reshape(s, D_ATTN)
    cs = state_conv[0]
    hist = jnp.concatenate([jnp.zeros((s, CONV_HALO - (CONV_WIDTH - 1), D_CONV), F32), cs], axis=1)
    us_blk = jnp.concatenate([us[:, None, :], jnp.zeros((s, SUBLANES - 1, D_CONV), F32)], axis=1)
    o_conv_s = conv(hist, us_blk.reshape(s * SUBLANES, D_CONV), SUBLANES)
    o_conv_s = o_conv_s.reshape(s, SUBLANES, D_CONV)[:, 0]
    hs, hns = merge(xs, o_attn_s, o_conv_s, s)
    launch_s = _peer_launch(hs, hns, s, jnp.zeros((SC_TOKENS, D_MODEL), F32), wq_bf, k1_bf, k2_bf, pu)
    y_sample = _peer_finish(*launch_s, s, vp).reshape(s, 1, D_MODEL)

    new_k_sample = jnp.concatenate([ck[:, 1:], ks.reshape(s, 1, N_KV_HEADS, HEAD_DIM)], axis=1)[None]
    new_v_sample = jnp.concatenate([cv[:, 1:], vs.reshape(s, 1, N_KV_HEADS, HEAD_DIM)], axis=1)[None]
    new_conv_sample = jnp.concatenate([cs[:, 1:], us[:, None, :]], axis=1)[None]
    if w_buf != WINDOW:
        raise NotImplementedError("cache window shorter than the attention window")

    return (y_prompt, y_sample, new_k_prompt, new_v_prompt, new_conv_prompt,
            new_k_sample, new_v_sample, new_conv_sample)
```

```python
import functools

import jax
import jax.numpy as jnp
from jax import lax
from jax.experimental import pallas as pl
from jax.experimental.pallas import tpu as pltpu
from jax.experimental.pallas import tpu_sc as plsc

D_MODEL = 1024
HEAD_DIM = 64
D_ATTN = 512
N_HEADS = 8
N_KV_HEADS = 2
KV_DIM = N_KV_HEADS * HEAD_DIM
D_CONV = 512
D_IN = D_ATTN + 2 * KV_DIM + 2 * D_CONV
CONV_WIDTH = 31
WINDOW = 128
BLOCK = 128
ROPE_THETA = 10000.0
N_META = 16
META_PAD = BLOCK - N_META
PEER_HEADS = 8
PEER_NKEYS = 128
PEER_TOPK = 16
PEER_SEL = PEER_HEADS * PEER_TOPK
EPS = 1e-6
PAST_LEN = 16384

LANES = 128
SC_CORES = 2
SC_SUBCORES = 16
SC_LANES = 16
SC_WORKERS = SC_CORES * SC_SUBCORES
VMEM_LIMIT = 48 * 1024 * 1024
SUBLANES = 8

TB_DENSE = 512
TB_ROUTE = 256
SB_DECODE = 32

F32 = jnp.float32
BF16 = jnp.bfloat16
NEG_INF = float("-inf")


def _tc_params(n_axes):
    return pltpu.CompilerParams(dimension_semantics=("arbitrary",) * n_axes,
                                vmem_limit_bytes=VMEM_LIMIT)


def _full(shape):
    nd = len(shape)
    return pl.BlockSpec(shape, lambda *_: (0,) * nd)


def _group_mean(sq, gsum_ref):
    hi = sq.astype(BF16)
    lo = (sq - hi.astype(F32)).astype(BF16)
    g = gsum_ref[...]
    s = jnp.dot(hi, g, preferred_element_type=F32) + jnp.dot(lo, g, preferred_element_type=F32)
    return s * (1.0 / HEAD_DIM)


def _rope(xn, cos_f, sin_s, first_half):
    outs = []
    for s in range(xn.shape[1] // LANES):
        xs = xn[:, s * LANES:(s + 1) * LANES]
        partner = jnp.where(first_half, pltpu.roll(xs, LANES - HEAD_DIM // 2, axis=1),
                            pltpu.roll(xs, HEAD_DIM // 2, axis=1))
        outs.append(xs * cos_f + partner * sin_s)
    return outs[0] if len(outs) == 1 else jnp.concatenate(outs, axis=1)


def _proj_kernel(x_ref, g_ref, w_ref, qg_ref, kg_ref, cos_ref, sin_ref, gq_ref, gk_ref, after_ref,
                 q_out, k_out, v_out, u_out):
    del after_ref
    x = x_ref[...]
    n = x * lax.rsqrt(jnp.mean(x * x, axis=-1, keepdims=True) + EPS) * g_ref[...]
    p = jnp.dot(n.astype(BF16), w_ref[...], preferred_element_type=F32)
    q = p[:, :D_ATTN]
    k = p[:, D_ATTN:D_ATTN + KV_DIM]
    v = p[:, D_ATTN + KV_DIM:D_ATTN + 2 * KV_DIM]
    ga = p[:, D_ATTN + 2 * KV_DIM:D_ATTN + 2 * KV_DIM + D_CONV]
    gb = p[:, D_ATTN + 2 * KV_DIM + D_CONV:]
    cos_f = cos_ref[...]
    sin_s = sin_ref[...]
    lane = lax.broadcasted_iota(jnp.int32, (x.shape[0], LANES), 1)
    first_half = (lane % HEAD_DIM) < (HEAD_DIM // 2)
    qn = q * lax.rsqrt(_group_mean(q * q, gq_ref) + EPS) * qg_ref[...]
    kn = k * lax.rsqrt(_group_mean(k * k, gk_ref) + EPS) * kg_ref[...]
    q_out[...] = _rope(qn, cos_f, sin_s, first_half)
    k_out[...] = _rope(kn, cos_f, sin_s, first_half)
    v_out[...] = v
    u_out[...] = ga * (1.0 / (1.0 + jnp.exp(-gb)))


def _project(x, pos_tables, n_table_blocks, tb, after, norm_g, w_in_bf, qg_t, kg_t, gsum_q, gsum_k,
             row0=0, nrows=None):
    n = x.shape[0] if nrows is None else nrows
    cos_t, sin_t = pos_tables
    nb = n // tb
    blk0 = row0 // tb
    tab_spec = pl.BlockSpec((tb, LANES), lambda i: (i % n_table_blocks, 0))
    row = lambda w: pl.BlockSpec((tb, w), lambda i: (i, 0))
    return pl.pallas_call(
        _proj_kernel,
        grid=(nb,),
        in_specs=[pl.BlockSpec((tb, D_MODEL), lambda i: (i + blk0, 0)),
                  _full((1, D_MODEL)), _full((D_MODEL, D_IN)),
                  _full((1, D_ATTN)), _full((1, KV_DIM)), tab_spec, tab_spec,
                  _full((D_ATTN, D_ATTN)), _full((KV_DIM, KV_DIM)),
                  pl.BlockSpec(memory_space=pl.ANY)],
        out_specs=[row(D_ATTN), row(KV_DIM), row(KV_DIM), row(D_CONV)],
        out_shape=[jax.ShapeDtypeStruct((n, D_ATTN), F32), jax.ShapeDtypeStruct((n, KV_DIM), F32),
                   jax.ShapeDtypeStruct((n, KV_DIM), F32), jax.ShapeDtypeStruct((n, D_CONV), F32)],
        compiler_params=_tc_params(1),
        name="proj",
    )(x, norm_g, w_in_bf, qg_t, kg_t, cos_t, sin_t, gsum_q, gsum_k, after)


def _attn_kernel(sink_ref, q_ref, kp_ref, kc_ref, vp_ref, vc_ref, km_ref, vm_ref, o_ref):
    j = pl.program_id(0)
    r = lax.broadcasted_iota(jnp.int32, (BLOCK, 2 * BLOCK), 0)
    c = lax.broadcasted_iota(jnp.int32, (BLOCK, 2 * BLOCK), 1)
    ok = (c > r) & (c <= r + WINDOW) & ((j > 0) | (c >= META_PAD))
    grp = N_HEADS // N_KV_HEADS
    ok = jnp.concatenate([ok] * grp, axis=0)
    first = j == 0
    k_all = jnp.concatenate([jnp.where(first, km_ref[...], kp_ref[...]), kc_ref[...]], axis=0)
    v_all = jnp.concatenate([jnp.where(first, vm_ref[...], vp_ref[...]), vc_ref[...]], axis=0)
    q = q_ref[...]
    outs = []
    for g in range(N_KV_HEADS):
        lanes = slice(g * HEAD_DIM, (g + 1) * HEAD_DIM)
        k = k_all[:, lanes].astype(BF16)
        v = v_all[:, lanes].astype(BF16)
        q4 = jnp.concatenate([q[:, (g * grp + i) * HEAD_DIM:(g * grp + i + 1) * HEAD_DIM]
                              for i in range(grp)], axis=0).astype(BF16)
        s = lax.dot_general(q4, k, (((1,), (1,)), ((), ())), preferred_element_type=F32)
        s = jnp.where(ok, s * (HEAD_DIM ** -0.5), NEG_INF)
        sink = jnp.concatenate(
            [jnp.full((BLOCK, 1), sink_ref[g * grp + i], F32) for i in range(grp)], axis=0)
        m = jnp.maximum(jnp.max(s, axis=1, keepdims=True), sink)
        p = jnp.exp(s - m)
        den = jnp.sum(p, axis=1, keepdims=True) + jnp.exp(sink - m)
        o = jnp.dot(p.astype(BF16), v, preferred_element_type=F32) / den
        outs += [o[i * BLOCK:(i + 1) * BLOCK] for i in range(grp)]
    o_ref[...] = jnp.concatenate(outs, axis=1)


def _prompt_attention(sinks, q, k, v, k_meta, v_meta):
    t = q.shape[0]
    kv_prev = pl.BlockSpec((BLOCK, KV_DIM), lambda j: (jnp.maximum(j - 1, 0), 0))
    kv_cur = pl.BlockSpec((BLOCK, KV_DIM), lambda j: (j, 0))
    q_spec = pl.BlockSpec((BLOCK, D_ATTN), lambda j: (j, 0))
    meta = _full((BLOCK, KV_DIM))
    return pl.pallas_call(
        _attn_kernel,
        grid=(t // BLOCK,),
        in_specs=[pl.BlockSpec(memory_space=pltpu.SMEM), q_spec, kv_prev, kv_cur, kv_prev, kv_cur,
                  meta, meta],
        out_specs=q_spec,
        out_shape=jax.ShapeDtypeStruct((t, D_ATTN), F32),
        compiler_params=_tc_params(1),
        name="attn",
    )(sinks, q, k, k, v, v, k_meta, v_meta)


def _dec_attn_kernel(sink_ref, q_ref, kn_ref, vn_ref, ck_ref, cv_ref, o_ref):
    grp = N_HEADS // N_KV_HEADS
    sb, w_buf = ck_ref.shape[0], ck_ref.shape[1]
    q = q_ref[...]
    qb = q.astype(BF16)
    head = lax.broadcasted_iota(jnp.int32, (sb, N_HEADS, 1), 1)
    in_g0 = head < grp
    kv_lanes = [slice(g * HEAD_DIM, (g + 1) * HEAD_DIM) for g in range(N_KV_HEADS)]
    ck = ck_ref[...]
    cv = cv_ref[...]
    s_g = [jnp.einsum("shd,swd->shw", qb, ck[:, :, ln].astype(BF16), preferred_element_type=F32)
           for ln in kv_lanes]
    s = jnp.where(in_g0, s_g[0], s_g[1]) * (HEAD_DIM ** -0.5)
    key_ok = lax.broadcasted_iota(jnp.int32, (sb, N_HEADS, w_buf), 2) >= 1
    s = jnp.where(key_ok, s, NEG_INF)
    rnd = lambda a: a.astype(BF16).astype(F32)
    kn_all, vn_all = kn_ref[...], vn_ref[...]
    kn = jnp.where(in_g0, kn_all[:, None, kv_lanes[0]], kn_all[:, None, kv_lanes[1]])
    vn = jnp.where(in_g0, vn_all[:, None, kv_lanes[0]], vn_all[:, None, kv_lanes[1]])
    s_self = jnp.sum(rnd(q) * rnd(kn), axis=-1, keepdims=True) * (HEAD_DIM ** -0.5)
    sink = sink_ref[...]
    m = jnp.maximum(jnp.maximum(jnp.max(s, axis=-1, keepdims=True), s_self), sink)
    p = jnp.exp(s - m)
    p_self = jnp.exp(s_self - m)
    den = jnp.sum(p, axis=-1, keepdims=True) + p_self + jnp.exp(sink - m)
    pb = p.astype(BF16)
    o_g = [jnp.einsum("shw,swd->shd", pb, cv[:, :, ln].astype(BF16), preferred_element_type=F32)
           for ln in kv_lanes]
    o = jnp.where(in_g0, o_g[0], o_g[1]) + rnd(p_self) * rnd(vn)
    o_ref[...] = o / den


def _decode_attention(sinks, q3, kn_t, vn_t, ck_t, cv_t, sb):
    s = q3.shape[0]
    w_buf = ck_t.shape[1]
    qs = pl.BlockSpec((sb, N_HEADS, HEAD_DIM), lambda i: (i, 0, 0))
    ns = pl.BlockSpec((sb, KV_DIM), lambda i: (i, 0))
    cs = pl.BlockSpec((sb, w_buf, KV_DIM), lambda i: (i, 0, 0))
    return pl.pallas_call(
        _dec_attn_kernel,
        grid=(s // sb,),
        in_specs=[_full((1, N_HEADS, 1)), qs, ns, ns, cs, cs],
        out_specs=qs,
        out_shape=jax.ShapeDtypeStruct(q3.shape, F32),
        compiler_params=_tc_params(1),
        name="dec_attn",
    )(sinks.reshape(1, N_HEADS, 1), q3, kn_t, vn_t, ck_t, cv_t)


CONV_HALO = 32
CONV_ROWS = 64


def _conv_kernel(halo_ref, u_ref, cw_ref, cb_ref, lg_ref, lb_ref, w2_ref, o_ref, ucat):
    tb = u_ref.shape[0]
    ucat[0:CONV_HALO, :] = halo_ref[0]
    ucat[CONV_HALO:, :] = u_ref[...]
    first = CONV_HALO - (CONV_WIDTH - 1)
    rows = min(CONV_ROWS, tb)
    for r0 in range(0, tb, rows):
        acc = jnp.zeros((rows, D_CONV), F32)
        for j in range(CONV_WIDTH):
            acc = acc + ucat[r0 + first + j:r0 + first + j + rows, :] * cw_ref[j:j + 1, :]
        y = acc + cb_ref[...]
        yc = y - jnp.mean(y, axis=-1, keepdims=True)
        yn = yc * lax.rsqrt(jnp.mean(yc * yc, axis=-1, keepdims=True) + EPS)
        yn = yn * lg_ref[...] + lb_ref[...]
        act = yn * (1.0 / (1.0 + jnp.exp(-yn)))
        o_ref[r0:r0 + rows, :] = jnp.dot(act.astype(BF16), w2_ref[...],
                                              preferred_element_type=F32)


def _conv_branch(halo, u, tb, conv_w, conv_b, ln_g, ln_b, w_pw2_bf):
    n = u.shape[0]
    return pl.pallas_call(
        _conv_kernel,
        grid=(n // tb,),
        in_specs=[pl.BlockSpec((1, CONV_HALO, D_CONV), lambda i: (i, 0, 0)),
                  pl.BlockSpec((tb, D_CONV), lambda i: (i, 0)),
                  _full((CONV_WIDTH, D_CONV)), _full((1, D_CONV)), _full((1, D_CONV)),
                  _full((1, D_CONV)), _full((D_CONV, D_CONV))],
        out_specs=pl.BlockSpec((tb, D_CONV), lambda i: (i, 0)),
        out_shape=jax.ShapeDtypeStruct((n, D_CONV), F32),
        scratch_shapes=[pltpu.VMEM((tb + CONV_HALO, D_CONV), F32)],
        compiler_params=_tc_params(1),
        name="conv",
    )(halo, u, conv_w, conv_b, ln_g, ln_b, w_pw2_bf)


def _rms(x, g):
    return x * lax.rsqrt(jnp.mean(x * x, axis=-1, keepdims=True) + EPS) * g


def _merge_kernel(x_ref, oa_ref, oc_ref, ga_ref, gc_ref, wa_ref, wc_ref, gf_ref, h_out, hn_out):
    a = _rms(oa_ref[...], ga_ref[...]).astype(BF16)
    c = _rms(oc_ref[...], gc_ref[...]).astype(BF16)
    h = x_ref[...] + (jnp.dot(a, wa_ref[...], preferred_element_type=F32)
                      + jnp.dot(c, wc_ref[...], preferred_element_type=F32))
    h_out[...] = h
    hn_out[...] = _rms(h, gf_ref[...])


def _merge(x, oa, oc, tb, g_a, g_c, w_out_a, w_out_c, g_f, x_row0=0, row0=0, nrows=None):
    n = oa.shape[0] if nrows is None else nrows
    xb0, b0 = x_row0 // tb, row0 // tb
    row = lambda w: pl.BlockSpec((tb, w), lambda i: (i, 0))
    src = lambda w: pl.BlockSpec((tb, w), lambda i: (i + b0, 0))
    return pl.pallas_call(
        _merge_kernel,
        grid=(n // tb,),
        in_specs=[pl.BlockSpec((tb, D_MODEL), lambda i: (i + xb0, 0)), src(D_ATTN), src(D_CONV),
                  _full((1, D_ATTN)), _full((1, D_CONV)),
                  _full((D_ATTN, D_MODEL)), _full((D_CONV, D_MODEL)), _full((1, D_MODEL))],
        out_specs=[row(D_MODEL), row(D_MODEL)],
        out_shape=[jax.ShapeDtypeStruct((n, D_MODEL), F32)] * 2,
        compiler_params=_tc_params(1),
        name="merge",
    )(x, oa, oc, g_a, g_c, w_out_a, w_out_c, g_f)


ID_BIG = 1e9


def _topk_rows(s, k):
    rows = lax.broadcasted_iota(jnp.int32, s.shape, 0).astype(F32)
    vals, idxs = [], []
    for _ in range(k):
        m = jnp.max(s, axis=0, keepdims=True)
        idx = jnp.min(jnp.where(s == m, rows, ID_BIG), axis=0, keepdims=True)
        vals.append(m)
        idxs.append(idx)
        s = jnp.where(rows == idx, NEG_INF, s)
    return jnp.concatenate(vals, axis=0), jnp.concatenate(idxs, axis=0)


PAIR_B_WIDE = 8


def _route_kernel(hn_ref, wq_ref, k1_ref, k2_ref, eid_out, gate_out):
    tb = hn_ref.shape[0]
    q = jnp.dot(hn_ref[...].astype(BF16), wq_ref[...], preferred_element_type=F32).astype(BF16)
    k1 = k1_ref[...]
    k2 = k2_ref[...]
    nt = (((1,), (1,)), ((), ()))
    r = lax.broadcasted_iota(jnp.int32, (PEER_TOPK + (PAIR_B_WIDE - 1) * PAIR_B_WIDE + PAIR_B_WIDE, tb), 0)
    mid = r - PEER_TOPK
    flat = jnp.where(r < PEER_TOPK, r,
                     jnp.where(mid < (PAIR_B_WIDE - 1) * PAIR_B_WIDE,
                               (1 + mid // PAIR_B_WIDE) * PEER_TOPK + mid % PAIR_B_WIDE,
                               (PAIR_B_WIDE + mid - (PAIR_B_WIDE - 1) * PAIR_B_WIDE) * PEER_TOPK)).astype(F32)
    half = PEER_NKEYS
    for h in range(PEER_HEADS):
        q1 = q[:, (2 * h) * half:(2 * h + 1) * half]
        q2 = q[:, (2 * h + 1) * half:(2 * h + 2) * half]
        s1 = lax.dot_general(k1, q1, nt, preferred_element_type=F32)
        s2 = lax.dot_general(k2, q2, nt, preferred_element_type=F32)
        v1, i1 = _topk_rows(s1, PEER_TOPK)
        v2, i2 = _topk_rows(s2, PEER_TOPK)
        e1 = i1 * PEER_NKEYS
        cand = jnp.concatenate(
            [v1[0:1] + v2]
            + [v1[a:a + 1] + v2[0:PAIR_B_WIDE] for a in range(1, PAIR_B_WIDE)]
            + [v1[PAIR_B_WIDE:] + v2[0:1]], axis=0)
        cid = jnp.concatenate(
            [e1[0:1] + i2]
            + [e1[a:a + 1] + i2[0:PAIR_B_WIDE] for a in range(1, PAIR_B_WIDE)]
            + [e1[PAIR_B_WIDE:] + i2[0:1]], axis=0)
        scs, eids = [], []
        for _ in range(PEER_TOPK):
            m = jnp.max(cand, axis=0, keepdims=True)
            jsel = jnp.min(jnp.where(cand == m, flat, ID_BIG), axis=0, keepdims=True)
            hit = flat == jsel
            eids.append(jnp.max(jnp.where(hit, cid, -1.0), axis=0, keepdims=True))
            scs.append(m)
            cand = jnp.where(hit, NEG_INF, cand)
        sc = jnp.concatenate(scs, axis=0)
        e = jnp.exp(sc - sc[0:1])
        gate_out[h * PEER_TOPK:(h + 1) * PEER_TOPK, :] = e / jnp.sum(e, axis=0, keepdims=True)
        eid_out[h * PEER_TOPK:(h + 1) * PEER_TOPK, :] = jnp.concatenate(eids, axis=0).astype(jnp.int32)


def _route(hn, tb, wq_bf, keys1_bf, keys2_bf):
    n = hn.shape[0]
    col = pl.BlockSpec((PEER_SEL, tb), lambda i: (0, i))
    return pl.pallas_call(
        _route_kernel,
        grid=(n // tb,),
        in_specs=[pl.BlockSpec((tb, D_MODEL), lambda i: (i, 0)),
                  _full((D_MODEL, 2 * PEER_NKEYS * PEER_HEADS)),
                  _full((PEER_NKEYS, PEER_NKEYS)), _full((PEER_NKEYS, PEER_NKEYS))],
        out_specs=[col, col],
        out_shape=[jax.ShapeDtypeStruct((PEER_SEL, n), jnp.int32),
                   jax.ShapeDtypeStruct((PEER_SEL, n), F32)],
        compiler_params=_tc_params(1),
        name="route",
    )(hn, wq_bf, keys1_bf, keys2_bf)


SC_ROWS = 32
SC_GROUP = SC_LANES
SC_TOKENS = 8
SC_CHUNKS = PEER_SEL // SC_ROWS

_SC_PARAMS = pltpu.CompilerParams(needs_layout_passes=False)


def _sc_mesh():
    return plsc.VectorSubcoreMesh(core_axis_name="c", subcore_axis_name="s",
                                  num_cores=SC_CORES, num_subcores=SC_SUBCORES)


def _sc_token_loop(n_batches, table_hbm, eid_v, rbufs, rsems, prefetch, prefetch_wait, store,
                   contract):
    def gather(slot, tl, ch, p):
        idx = eid_v.at[slot, tl, pl.ds(ch * SC_ROWS, SC_ROWS)]
        return pltpu.make_async_copy(table_hbm.at[idx], rbufs[p], rsems[p])

    prefetch(0, 0)
    prefetch_wait()
    gather(0, 0, 0, 0).start()

    @pl.loop(0, n_batches * SC_TOKENS)
    def _(i):
        b = i // SC_TOKENS
        tl = i % SC_TOKENS
        slot = b % 2
        more = b + 1 < n_batches

        @pl.when(jnp.logical_and(tl == 0, more))
        def _():
            prefetch(b + 1, 1 - slot)

        for ch in range(SC_CHUNKS):
            p = ch % 2
            if ch + 1 < SC_CHUNKS:
                gather(slot, tl, ch + 1, 1 - p).start()
            else:
                @pl.when(tl + 1 < SC_TOKENS)
                def _():
                    gather(slot, tl + 1, 0, 1 - p).start()

                @pl.when(jnp.logical_and(tl + 1 == SC_TOKENS, more))
                def _():
                    prefetch_wait()
                    gather(1 - slot, 0, 0, 1 - p).start()
            gather(slot, tl, ch, p).wait()
            contract(slot, tl, ch, rbufs[p])

        @pl.when(tl + 1 == SC_TOKENS)
        def _():
            @pl.when(b >= 1)
            def _():
                store(b - 1, 1 - slot).wait()
            store(b, slot).start()

    store(n_batches - 1, (n_batches - 1) % 2).wait()


def _sc_batches(n):
    assert n % (SC_WORKERS * SC_TOKENS) == 0, n
    return n // (SC_WORKERS * SC_TOKENS)


def _peer_hval(hn, eid, peer_u, after):
    n = hn.shape[0]
    nbw = _sc_batches(n)

    @functools.partial(
        pl.kernel, mesh=_sc_mesh(),
        out_type=jax.ShapeDtypeStruct((n // SC_TOKENS, SC_TOKENS, PEER_SEL), F32),
        scratch_types=[
            pltpu.VMEM((2, SC_TOKENS, PEER_SEL), jnp.int32),
            pltpu.VMEM((2, SC_TOKENS, D_MODEL), F32),
            pltpu.VMEM((SC_ROWS, D_MODEL), F32),
            pltpu.VMEM((SC_ROWS, D_MODEL), F32),
            pltpu.VMEM((2, SC_TOKENS, PEER_SEL), F32),
            pltpu.VMEM((SC_GROUP, SC_LANES), F32),
            pltpu.SemaphoreType.DMA, pltpu.SemaphoreType.DMA,
            pltpu.SemaphoreType.DMA, pltpu.SemaphoreType.DMA,
        ],
        compiler_params=_SC_PARAMS, name="peer_hval",
        cost_estimate=pl.CostEstimate(flops=2 * n * PEER_SEL * D_MODEL, transcendentals=0,
                                      bytes_accessed=n * PEER_SEL * D_MODEL * 4))
    def k(x_hbm, eid_hbm, u_hbm, after_hbm, o_hbm, eid_v, x_v, r0, r1, h_v, tr, sr0, sr1, spf, sout):
        del after_hbm
        blk0 = (lax.axis_index("s") * SC_CORES + lax.axis_index("c")) * nbw
        lane = lax.iota(jnp.int32, SC_LANES)
        zero = jnp.zeros((SC_LANES,), F32)

        def prefetch(b, slot):
            pltpu.async_copy(eid_hbm.at[blk0 + b], eid_v.at[slot], spf)
            pltpu.async_copy(x_hbm.at[blk0 + b], x_v.at[slot], spf)

        def prefetch_wait():
            pltpu.make_async_copy(eid_hbm.at[0], eid_v.at[0], spf).wait()
            pltpu.make_async_copy(x_hbm.at[0], x_v.at[0], spf).wait()

        def store(b, slot):
            return pltpu.make_async_copy(h_v.at[slot], o_hbm.at[blk0 + b], sout)

        def contract(slot, tl, ch, rbuf):
            for g in range(SC_ROWS // SC_GROUP):
                def body(c, accs):
                    xc = x_v[slot, tl, pl.ds(c * SC_LANES, SC_LANES)]
                    return tuple(
                        accs[r] + rbuf[g * SC_GROUP + r, pl.ds(c * SC_LANES, SC_LANES)] * xc
                        for r in range(SC_GROUP))
                accs = lax.fori_loop(0, D_MODEL // SC_LANES, body, (zero,) * SC_GROUP)
                for r in range(SC_GROUP):
                    tr[r, :] = accs[r]
                res = zero
                for jj in range(SC_LANES):
                    res = res + plsc.load_gather(tr, [lane, jnp.full((SC_LANES,), jj, jnp.int32)])
                h_v[slot, tl, pl.ds(ch * SC_ROWS + g * SC_GROUP, SC_GROUP)] = res

        _sc_token_loop(nbw, u_hbm, eid_v, (r0, r1), (sr0, sr1), prefetch, prefetch_wait, store,
                       contract)

    out = k(hn.reshape(n // SC_TOKENS, SC_TOKENS, D_MODEL),
            eid.reshape(n // SC_TOKENS, SC_TOKENS, PEER_SEL), peer_u, after)
    return out.reshape(n, PEER_SEL)


def _gate_kernel(hv_ref, gate_ref, a_out):
    hv = hv_ref[...]
    gelu = hv * (lax.erf(hv * (2.0 ** -0.5)) + 1.0) * 0.5
    a_out[...] = gate_ref[...] * gelu


def _gate(hval, gate, tb):
    n = hval.shape[0]
    row = pl.BlockSpec((tb, PEER_SEL), lambda i: (i, 0))
    return pl.pallas_call(
        _gate_kernel, grid=(n // tb,), in_specs=[row, row], out_specs=row,
        out_shape=jax.ShapeDtypeStruct((n, PEER_SEL), F32),
        compiler_params=_tc_params(1), name="gate",
    )(hval, gate)


VT_ROWS = D_MODEL // 2 // LANES
VT_TOKENS = 128
VT_UNROLL = 8
HI_MASK = -65536


def _pack_value_table(peer_v):
    e = peer_v.shape[0]
    bits = lax.bitcast_convert_type(peer_v.astype(BF16), jnp.uint16).astype(jnp.uint32)
    words = bits[:, :D_MODEL // 2] | (bits[:, D_MODEL // 2:] << 16)
    return lax.bitcast_convert_type(words, jnp.int32).reshape(e * VT_ROWS, LANES)


def _vside_kernel(row_s, a_s, h_ref, tab_ref, y_ref):
    tb = h_ref.shape[0]

    def token_group(p, carry):
        zero = jnp.zeros((VT_ROWS, LANES), F32)
        ts = tuple(VT_UNROLL * p + k for k in range(VT_UNROLL))
        lo = [zero] * VT_UNROLL
        hi = [zero] * VT_UNROLL
        for e in range(PEER_SEL):
            for k, t in enumerate(ts):
                r0 = pl.multiple_of(row_s[t, e], VT_ROWS)
                w = tab_ref[pl.ds(r0, VT_ROWS), :]
                coef = a_s[t, e]
                lo[k] = lo[k] + coef * pltpu.bitcast(w << 16, F32)
                hi[k] = hi[k] + coef * pltpu.bitcast(w & HI_MASK, F32)
        for k, t in enumerate(ts):
            y_ref[t] = h_ref[t] + jnp.concatenate([lo[k], hi[k]], axis=0)
        return carry

    lax.fori_loop(0, tb // VT_UNROLL, token_group, 0)


def _peer_values(h, a, row_ids, v_packed):
    n = h.shape[0]
    tb = min(VT_TOKENS, n)
    smem = pl.BlockSpec((tb, PEER_SEL), lambda i: (i, 0), memory_space=pltpu.SMEM)
    slab = pl.BlockSpec((tb, 2 * VT_ROWS, LANES), lambda i: (i, 0, 0))
    table = pl.BlockSpec(v_packed.shape, lambda i: (0, 0), pipeline_mode=pl.Buffered(1))
    table_bytes = v_packed.shape[0] * LANES * 4
    y = pl.pallas_call(
        _vside_kernel,
        grid=(n // tb,),
        in_specs=[smem, smem, slab, table],
        out_specs=slab,
        out_shape=jax.ShapeDtypeStruct((n, 2 * VT_ROWS, LANES), F32),
        compiler_params=pltpu.CompilerParams(
            dimension_semantics=("arbitrary",),
            vmem_limit_bytes=table_bytes + 8 * 1024 * 1024),
        name="peer_values",
        cost_estimate=pl.CostEstimate(flops=2 * n * PEER_SEL * D_MODEL, transcendentals=0,
                                      bytes_accessed=table_bytes + n * PEER_SEL * D_MODEL * 2),
    )(row_ids, a, h.reshape(n, 2 * VT_ROWS, LANES), v_packed)
    return y.reshape(n, D_MODEL)


PEER_DEPTH = 2


def _chunk_plan(t, tb, first, last):
    eighth = t // 8
    if eighth % tb:
        return [t]
    sizes = [4 * eighth, 4 * eighth]
    if last:
        sizes = sizes[:1] + [2 * eighth, eighth, eighth]
    if first:
        sizes = [eighth, 3 * eighth] + sizes[1:]
    return sizes


def _peer_launch(h, hn, tb, after, wq_bf, keys1_bf, keys2_bf, peer_u):
    n = h.shape[0]
    eid_t, gate_t = _route(hn, tb, wq_bf, keys1_bf, keys2_bf)
    eid = eid_t.T
    pad = (-n) % (SC_WORKERS * SC_TOKENS)
    hn_p, eid_p = hn, eid
    if pad:
        spread = (jnp.arange(pad * PEER_SEL, dtype=jnp.int32) % peer_u.shape[0]).reshape(pad, PEER_SEL)
        hn_p = jnp.pad(hn, ((0, pad), (0, 0)))
        eid_p = jnp.concatenate([eid, spread], axis=0)
    hval = _peer_hval(hn_p, eid_p, peer_u, after)[:n]
    return h, eid, gate_t.T, hval


def _peer_finish(h, eid, gate, hval, tb, v_packed):
    a = _gate(hval, gate, tb)
    return _peer_values(h, a, eid * VT_ROWS, v_packed)


def _rope_tables(pos):
    half = HEAD_DIM // 2
    inv = ROPE_THETA ** (-jnp.arange(half, dtype=F32) / half)
    ang = pos.astype(F32)[:, None] * inv[None, :]
    cos = jnp.cos(ang)
    sin = jnp.sin(ang)
    reps = LANES // HEAD_DIM
    cos_f = jnp.tile(jnp.concatenate([cos, cos], axis=1), (1, reps))
    sin_s = jnp.tile(jnp.concatenate([-sin, sin], axis=1), (1, reps))
    return cos_f, sin_s


def _group_sum_matrix(width):
    g = jnp.arange(width) // HEAD_DIM
    return (g[:, None] == g[None, :]).astype(BF16)


def kernel(x_prompt, x_sample, cache_k_win, cache_v_win, state_conv, meta_tokens, norm_mix_g,
           w_in, q_norm_g, k_norm_g, attn_sinks, conv_w, conv_b, conv_ln_g, conv_ln_b, w_pw2,
           out_norm_attn_g, out_norm_conv_g, w_out, norm_ffn_g, peer_w_q, peer_keys1, peer_keys2,
           peer_u, peer_v):
    assert norm_mix_g.shape[0] == 1, "single-layer model"
    b, t, _ = x_prompt.shape
    s = x_sample.shape[0]
    w_buf = cache_k_win.shape[2]
    n = b * t

    g_mix = norm_mix_g[0][None, :]
    w_in_bf = w_in[0].astype(BF16)
    qg_t = jnp.tile(q_norm_g[0], N_HEADS)[None, :]
    kg_t = jnp.tile(k_norm_g[0], N_KV_HEADS)[None, :]
    gsum_q = _group_sum_matrix(D_ATTN)
    gsum_k = _group_sum_matrix(KV_DIM)
    sinks = attn_sinks[0]
    cw, cb = conv_w[0], conv_b[0][None, :]
    lg, lb = conv_ln_g[0][None, :], conv_ln_b[0][None, :]
    w2_bf = w_pw2[0].astype(BF16)
    g_a, g_c = out_norm_attn_g[0][None, :], out_norm_conv_g[0][None, :]
    w_out_a = w_out[0][:D_ATTN].astype(BF16)
    w_out_c = w_out[0][D_ATTN:].astype(BF16)
    g_f = norm_ffn_g[0][None, :]
    wq_bf = peer_w_q[0].astype(BF16)
    k1_bf = peer_keys1[0].astype(BF16)
    k2_bf = peer_keys2[0].astype(BF16)
    pu, vp = peer_u[0], _pack_value_table(peer_v[0])
    proj = functools.partial(_project, norm_g=g_mix, w_in_bf=w_in_bf, qg_t=qg_t, kg_t=kg_t,
                             gsum_q=gsum_q, gsum_k=gsum_k)
    conv = functools.partial(_conv_branch, conv_w=cw, conv_b=cb, ln_g=lg, ln_b=lb, w_pw2_bf=w2_bf)
    merge = functools.partial(_merge, g_a=g_a, g_c=g_c, w_out_a=w_out_a, w_out_c=w_out_c, g_f=g_f)

    tb = TB_DENSE
    no_dep = jnp.zeros((SC_TOKENS, LANES), F32)
    tab_x = _rope_tables(N_META + jnp.arange(t, dtype=jnp.int32))
    tab_m = _rope_tables(jnp.arange(N_META, dtype=jnp.int32))
    _, k_m, v_m, u_m = proj(meta_tokens, tab_m, 1, N_META, no_dep)
    halo0 = jnp.concatenate([jnp.zeros((CONV_HALO - N_META, D_CONV), F32), u_m], axis=0)

    k_meta = jnp.concatenate([jnp.zeros((META_PAD, KV_DIM), F32), k_m], axis=0)
    v_meta = jnp.concatenate([jnp.zeros((META_PAD, KV_DIM), F32), v_m], axis=0)

    xp = x_prompt.reshape(b * t, D_MODEL)

    def dense(bi, after):
        q, k, v, u = proj(xp, tab_x, t // tb, tb, after, row0=bi * t, nrows=t)
        o_attn = _prompt_attention(sinks, q, k, v, k_meta, v_meta)
        u3 = u.reshape(t // tb, tb, D_CONV)
        halo = jnp.concatenate([halo0[None], u3[:-1, tb - CONV_HALO:]], axis=0)
        o_conv = conv(halo, u, tb)
        hs, r0 = [], 0
        for rows in _chunk_plan(t, tb, bi == 0, bi == b - 1):
            hs.append(merge(xp, o_attn, o_conv, tb, x_row0=bi * t + r0, row0=r0, nrows=rows))
            r0 += rows
        return hs, k, v, u

    launched, ys, kvu = [], [], []
    cur = dense(0, no_dep)
    for bi in range(b):
        hs, k, v, u = cur
        kvu.append((k, v, u))
        for h, hn in hs:
            c = len(launched)
            if c >= PEER_DEPTH:
                ys.append(_peer_finish(*launched[c - PEER_DEPTH], TB_ROUTE, vp))
            after = ys[c - PEER_DEPTH][:SC_TOKENS] if c >= PEER_DEPTH else jnp.zeros((SC_TOKENS, D_MODEL), F32)
            launched.append(_peer_launch(h, hn, TB_ROUTE, after, wq_bf, k1_bf, k2_bf, pu))
        if bi + 1 < b:
            cur = dense(bi + 1, launched[-1][1][:SC_TOKENS, :LANES].astype(F32))
    for c in range(len(ys), len(launched)):
        ys.append(_peer_finish(*launched[c], TB_ROUTE, vp))
    y_prompt = jnp.concatenate(ys, axis=0).reshape(b, t, D_MODEL)

    new_k_prompt = jnp.stack([k[t - WINDOW:] for k, _, _ in kvu]).reshape(1, b, WINDOW, N_KV_HEADS, HEAD_DIM)
    new_v_prompt = jnp.stack([v[t - WINDOW:] for _, v, _ in kvu]).reshape(1, b, WINDOW, N_KV_HEADS, HEAD_DIM)
    new_conv_prompt = jnp.stack([u[t - (CONV_WIDTH - 1):] for _, _, u in kvu])[None]

    xs = x_sample.reshape(s, D_MODEL)
    tab_s = _rope_tables(jnp.full((s,), PAST_LEN, jnp.int32))
    qs, ks, vs, us = proj(xs, tab_s, 1, s, no_dep)
    ck = cache_k_win[0]
    cv = cache_v_win[0]
    o_attn_s = _decode_attention(
        sinks,
        qs.reshape(s, N_HEADS, HEAD_DIM), ks, vs,
        ck.reshape(s, w_buf, KV_DIM), cv.reshape(s, w_buf, KV_DIM), min(SB_DECODE, s))
    o_attn_s = o_attn_s.reshape(s, D_ATTN)
    cs = state_conv[0]
    hist = jnp.concatenate([jnp.zeros((s, CONV_HALO - (CONV_WIDTH - 1), D_CONV), F32), cs], axis=1)
    us_blk = jnp.concatenate([us[:, None, :], jnp.zeros((s, SUBLANES - 1, D_CONV), F32)], axis=1)
    o_conv_s = conv(hist, us_blk.reshape(s * SUBLANES, D_CONV), SUBLANES)
    o_conv_s = o_conv_s.reshape(s, SUBLANES, D_CONV)[:, 0]
    hs, hns = merge(xs, o_attn_s, o_conv_s, s)
    launch_s = _peer_launch(hs, hns, s, jnp.zeros((SC_TOKENS, D_MODEL), F32), wq_bf, k1_bf, k2_bf, pu)
    y_sample = _peer_finish(*launch_s, s, vp).reshape(s, 1, D_MODEL)

    new_k_sample = jnp.concatenate([ck[:, 1:], ks.reshape(s, 1, N_KV_HEADS, HEAD_DIM)], axis=1)[None]
    new_v_sample = jnp.concatenate([cv[:, 1:], vs.reshape(s, 1, N_KV_HEADS, HEAD_DIM)], axis=1)[None]
    new_conv_sample = jnp.concatenate([cs[:, 1:], us[:, None, :]], axis=1)[None]
    if w_buf != WINDOW:
        raise NotImplementedError("cache window shorter than the attention window")

    return (y_prompt, y_sample, new_k_prompt, new_v_prompt, new_conv_prompt,
            new_k_sample, new_v_sample, new_conv_sample)
```

```python
import functools

import jax
import jax.numpy as jnp
from jax import lax
from jax.experimental import pallas as pl
from jax.experimental.pallas import tpu as pltpu
from jax.experimental.pallas import tpu_sc as plsc

D_MODEL = 1024
HEAD_DIM = 64
D_ATTN = 512
N_HEADS = 8
N_KV_HEADS = 2
KV_DIM = N_KV_HEADS * HEAD_DIM
D_CONV = 512
D_IN = D_ATTN + 2 * KV_DIM + 2 * D_CONV
CONV_WIDTH = 31
WINDOW = 128
BLOCK = 128
ROPE_THETA = 10000.0
N_META = 16
META_PAD = BLOCK - N_META
PEER_HEADS = 8
PEER_NKEYS = 128
PEER_TOPK = 16
PEER_SEL = PEER_HEADS * PEER_TOPK
EPS = 1e-6
PAST_LEN = 16384

LANES = 128
SC_CORES = 2
SC_SUBCORES = 16
SC_LANES = 16
SC_WORKERS = SC_CORES * SC_SUBCORES
VMEM_LIMIT = 48 * 1024 * 1024
SUBLANES = 8

TB_DENSE = 512
TB_ROUTE = 256
SB_DECODE = 32

F32 = jnp.float32
BF16 = jnp.bfloat16
NEG_INF = float("-inf")


def _tc_params(n_axes):
    return pltpu.CompilerParams(dimension_semantics=("arbitrary",) * n_axes,
                                vmem_limit_bytes=VMEM_LIMIT)


def _full(shape):
    nd = len(shape)
    return pl.BlockSpec(shape, lambda *_: (0,) * nd)


def _group_mean(sq, gsum_ref):
    hi = sq.astype(BF16)
    lo = (sq - hi.astype(F32)).astype(BF16)
    g = gsum_ref[...]
    s = jnp.dot(hi, g, preferred_element_type=F32) + jnp.dot(lo, g, preferred_element_type=F32)
    return s * (1.0 / HEAD_DIM)


def _rope(xn, cos_f, sin_s, first_half):
    outs = []
    for s in range(xn.shape[1] // LANES):
        xs = xn[:, s * LANES:(s + 1) * LANES]
        partner = jnp.where(first_half, pltpu.roll(xs, LANES - HEAD_DIM // 2, axis=1),
                            pltpu.roll(xs, HEAD_DIM // 2, axis=1))
        outs.append(xs * cos_f + partner * sin_s)
    return outs[0] if len(outs) == 1 else jnp.concatenate(outs, axis=1)


def _proj_kernel(x_ref, g_ref, w_ref, qg_ref, kg_ref, cos_ref, sin_ref, gq_ref, gk_ref, after_ref,
                 q_out, k_out, v_out, u_out):
    del after_ref
    x = x_ref[...]
    n = x * lax.rsqrt(jnp.mean(x * x, axis=-1, keepdims=True) + EPS) * g_ref[...]
    p = jnp.dot(n.astype(BF16), w_ref[...], preferred_element_type=F32)
    q = p[:, :D_ATTN]
    k = p[:, D_ATTN:D_ATTN + KV_DIM]
    v = p[:, D_ATTN + KV_DIM:D_ATTN + 2 * KV_DIM]
    ga = p[:, D_ATTN + 2 * KV_DIM:D_ATTN + 2 * KV_DIM + D_CONV]
    gb = p[:, D_ATTN + 2 * KV_DIM + D_CONV:]
    cos_f = cos_ref[...]
    sin_s = sin_ref[...]
    lane = lax.broadcasted_iota(jnp.int32, (x.shape[0], LANES), 1)
    first_half = (lane % HEAD_DIM) < (HEAD_DIM // 2)
    qn = q * lax.rsqrt(_group_mean(q * q, gq_ref) + EPS) * qg_ref[...]
    kn = k * lax.rsqrt(_group_mean(k * k, gk_ref) + EPS) * kg_ref[...]
    q_out[...] = _rope(qn, cos_f, sin_s, first_half)
    k_out[...] = _rope(kn, cos_f, sin_s, first_half)
    v_out[...] = v
    u_out[...] = ga * (1.0 / (1.0 + jnp.exp(-gb)))


def _project(x, pos_tables, n_table_blocks, tb, after, norm_g, w_in_bf, qg_t, kg_t, gsum_q, gsum_k,
             row0=0, nrows=None):
    n = x.shape[0] if nrows is None else nrows
    cos_t, sin_t = pos_tables
    nb = n // tb
    blk0 = row0 // tb
    tab_spec = pl.BlockSpec((tb, LANES), lambda i: (i % n_table_blocks, 0))
    row = lambda w: pl.BlockSpec((tb, w), lambda i: (i, 0))
    return pl.pallas_call(
        _proj_kernel,
        grid=(nb,),
        in_specs=[pl.BlockSpec((tb, D_MODEL), lambda i: (i + blk0, 0)),
                  _full((1, D_MODEL)), _full((D_MODEL, D_IN)),
                  _full((1, D_ATTN)), _full((1, KV_DIM)), tab_spec, tab_spec,
                  _full((D_ATTN, D_ATTN)), _full((KV_DIM, KV_DIM)),
                  pl.BlockSpec(memory_space=pl.ANY)],
        out_specs=[row(D_ATTN), row(KV_DIM), row(KV_DIM), row(D_CONV)],
        out_shape=[jax.ShapeDtypeStruct((n, D_ATTN), F32), jax.ShapeDtypeStruct((n, KV_DIM), F32),
                   jax.ShapeDtypeStruct((n, KV_DIM), F32), jax.ShapeDtypeStruct((n, D_CONV), F32)],
        compiler_params=_tc_params(1),
        name="proj",
    )(x, norm_g, w_in_bf, qg_t, kg_t, cos_t, sin_t, gsum_q, gsum_k, after)


def _attn_kernel(sink_ref, q_ref, kp_ref, kc_ref, vp_ref, vc_ref, km_ref, vm_ref, o_ref):
    j = pl.program_id(0)
    r = lax.broadcasted_iota(jnp.int32, (BLOCK, 2 * BLOCK), 0)
    c = lax.broadcasted_iota(jnp.int32, (BLOCK, 2 * BLOCK), 1)
    ok = (c > r) & (c <= r + WINDOW) & ((j > 0) | (c >= META_PAD))
    grp = N_HEADS // N_KV_HEADS
    ok = jnp.concatenate([ok] * grp, axis=0)
    first = j == 0
    k_all = jnp.concatenate([jnp.where(first, km_ref[...], kp_ref[...]), kc_ref[...]], axis=0)
    v_all = jnp.concatenate([jnp.where(first, vm_ref[...], vp_ref[...]), vc_ref[...]], axis=0)
    q = q_ref[...]
    outs = []
    for g in range(N_KV_HEADS):
        lanes = slice(g * HEAD_DIM, (g + 1) * HEAD_DIM)
        k = k_all[:, lanes].astype(BF16)
        v = v_all[:, lanes].astype(BF16)
        q4 = jnp.concatenate([q[:, (g * grp + i) * HEAD_DIM:(g * grp + i + 1) * HEAD_DIM]
                              for i in range(grp)], axis=0).astype(BF16)
        s = lax.dot_general(q4, k, (((1,), (1,)), ((), ())), preferred_element_type=F32)
        s = jnp.where(ok, s * (HEAD_DIM ** -0.5), NEG_INF)
        sink = jnp.concatenate(
            [jnp.full((BLOCK, 1), sink_ref[g * grp + i], F32) for i in range(grp)], axis=0)
        m = jnp.maximum(jnp.max(s, axis=1, keepdims=True), sink)
        p = jnp.exp(s - m)
        den = jnp.sum(p, axis=1, keepdims=True) + jnp.exp(sink - m)
        o = jnp.dot(p.astype(BF16), v, preferred_element_type=F32) / den
        outs += [o[i * BLOCK:(i + 1) * BLOCK] for i in range(grp)]
    o_ref[...] = jnp.concatenate(outs, axis=1)


def _prompt_attention(sinks, q, k, v, k_meta, v_meta):
    t = q.shape[0]
    kv_prev = pl.BlockSpec((BLOCK, KV_DIM), lambda j: (jnp.maximum(j - 1, 0), 0))
    kv_cur = pl.BlockSpec((BLOCK, KV_DIM), lambda j: (j, 0))
    q_spec = pl.BlockSpec((BLOCK, D_ATTN), lambda j: (j, 0))
    meta = _full((BLOCK, KV_DIM))
    return pl.pallas_call(
        _attn_kernel,
        grid=(t // BLOCK,),
        in_specs=[pl.BlockSpec(memory_space=pltpu.SMEM), q_spec, kv_prev, kv_cur, kv_prev, kv_cur,
                  meta, meta],
        out_specs=q_spec,
        out_shape=jax.ShapeDtypeStruct((t, D_ATTN), F32),
        compiler_params=_tc_params(1),
        name="attn",
    )(sinks, q, k, k, v, v, k_meta, v_meta)


def _dec_attn_kernel(sink_ref, q_ref, kn_ref, vn_ref, ck_ref, cv_ref, o_ref):
    grp = N_HEADS // N_KV_HEADS
    sb, w_buf = ck_ref.shape[0], ck_ref.shape[1]
    q = q_ref[...]
    qb = q.astype(BF16)
    head = lax.broadcasted_iota(jnp.int32, (sb, N_HEADS, 1), 1)
    in_g0 = head < grp
    kv_lanes = [slice(g * HEAD_DIM, (g + 1) * HEAD_DIM) for g in range(N_KV_HEADS)]
    ck = ck_ref[...]
    cv = cv_ref[...]
    s_g = [jnp.einsum("shd,swd->shw", qb, ck[:, :, ln].astype(BF16), preferred_element_type=F32)
           for ln in kv_lanes]
    s = jnp.where(in_g0, s_g[0], s_g[1]) * (HEAD_DIM ** -0.5)
    key_ok = lax.broadcasted_iota(jnp.int32, (sb, N_HEADS, w_buf), 2) >= 1
    s = jnp.where(key_ok, s, NEG_INF)
    rnd = lambda a: a.astype(BF16).astype(F32)
    kn_all, vn_all = kn_ref[...], vn_ref[...]
    kn = jnp.where(in_g0, kn_all[:, None, kv_lanes[0]], kn_all[:, None, kv_lanes[1]])
    vn = jnp.where(in_g0, vn_all[:, None, kv_lanes[0]], vn_all[:, None, kv_lanes[1]])
    s_self = jnp.sum(rnd(q) * rnd(kn), axis=-1, keepdims=True) * (HEAD_DIM ** -0.5)
    sink = sink_ref[...]
    m = jnp.maximum(jnp.maximum(jnp.max(s, axis=-1, keepdims=True), s_self), sink)
    p = jnp.exp(s - m)
    p_self = jnp.exp(s_self - m)
    den = jnp.sum(p, axis=-1, keepdims=True) + p_self + jnp.exp(sink - m)
    pb = p.astype(BF16)
    o_g = [jnp.einsum("shw,swd->shd", pb, cv[:, :, ln].astype(BF16), preferred_element_type=F32)
           for ln in kv_lanes]
    o = jnp.where(in_g0, o_g[0], o_g[1]) + rnd(p_self) * rnd(vn)
    o_ref[...] = o / den


def _decode_attention(sinks, q3, kn_t, vn_t, ck_t, cv_t, sb):
    s = q3.shape[0]
    w_buf = ck_t.shape[1]
    qs = pl.BlockSpec((sb, N_HEADS, HEAD_DIM), lambda i: (i, 0, 0))
    ns = pl.BlockSpec((sb, KV_DIM), lambda i: (i, 0))
    cs = pl.BlockSpec((sb, w_buf, KV_DIM), lambda i: (i, 0, 0))
    return pl.pallas_call(
        _dec_attn_kernel,
        grid=(s // sb,),
        in_specs=[_full((1, N_HEADS, 1)), qs, ns, ns, cs, cs],
        out_specs=qs,
        out_shape=jax.ShapeDtypeStruct(q3.shape, F32),
        compiler_params=_tc_params(1),
        name="dec_attn",
    )(sinks.reshape(1, N_HEADS, 1), q3, kn_t, vn_t, ck_t, cv_t)


CONV_HALO = 32
CONV_ROWS = 64


def _conv_kernel(halo_ref, u_ref, cw_ref, cb_ref, lg_ref, lb_ref, w2_ref, o_ref, ucat):
    tb = u_ref.shape[0]
    ucat[0:CONV_HALO, :] = halo_ref[0]
    ucat[CONV_HALO:, :] = u_ref[...]
    first = CONV_HALO - (CONV_WIDTH - 1)
    rows = min(CONV_ROWS, tb)
    for r0 in range(0, tb, rows):
        acc = jnp.zeros((rows, D_CONV), F32)
        for j in range(CONV_WIDTH):
            acc = acc + ucat[r0 + first + j:r0 + first + j + rows, :] * cw_ref[j:j + 1, :]
        y = acc + cb_ref[...]
        yc = y - jnp.mean(y, axis=-1, keepdims=True)
        yn = yc * lax.rsqrt(jnp.mean(yc * yc, axis=-1, keepdims=True) + EPS)
        yn = yn * lg_ref[...] + lb_ref[...]
        act = yn * (1.0 / (1.0 + jnp.exp(-yn)))
        o_ref[r0:r0 + rows, :] = jnp.dot(act.astype(BF16), w2_ref[...],
                                              preferred_element_type=F32)


def _conv_branch(halo, u, tb, conv_w, conv_b, ln_g, ln_b, w_pw2_bf):
    n = u.shape[0]
    return pl.pallas_call(
        _conv_kernel,
        grid=(n // tb,),
        in_specs=[pl.BlockSpec((1, CONV_HALO, D_CONV), lambda i: (i, 0, 0)),
                  pl.BlockSpec((tb, D_CONV), lambda i: (i, 0)),
                  _full((CONV_WIDTH, D_CONV)), _full((1, D_CONV)), _full((1, D_CONV)),
                  _full((1, D_CONV)), _full((D_CONV, D_CONV))],
        out_specs=pl.BlockSpec((tb, D_CONV), lambda i: (i, 0)),
        out_shape=jax.ShapeDtypeStruct((n, D_CONV), F32),
        scratch_shapes=[pltpu.VMEM((tb + CONV_HALO, D_CONV), F32)],
        compiler_params=_tc_params(1),
        name="conv",
    )(halo, u, conv_w, conv_b, ln_g, ln_b, w_pw2_bf)


def _rms(x, g):
    return x * lax.rsqrt(jnp.mean(x * x, axis=-1, keepdims=True) + EPS) * g


def _merge_kernel(x_ref, oa_ref, oc_ref, ga_ref, gc_ref, wa_ref, wc_ref, gf_ref, h_out, hn_out):
    a = _rms(oa_ref[...], ga_ref[...]).astype(BF16)
    c = _rms(oc_ref[...], gc_ref[...]).astype(BF16)
    h = x_ref[...] + (jnp.dot(a, wa_ref[...], preferred_element_type=F32)
                      + jnp.dot(c, wc_ref[...], preferred_element_type=F32))
    h_out[...] = h
    hn_out[...] = _rms(h, gf_ref[...])


def _merge(x, oa, oc, tb, g_a, g_c, w_out_a, w_out_c, g_f, x_row0=0, row0=0, nrows=None):
    n = oa.shape[0] if nrows is None else nrows
    xb0, b0 = x_row0 // tb, row0 // tb
    row = lambda w: pl.BlockSpec((tb, w), lambda i: (i, 0))
    src = lambda w: pl.BlockSpec((tb, w), lambda i: (i + b0, 0))
    return pl.pallas_call(
        _merge_kernel,
        grid=(n // tb,),
        in_specs=[pl.BlockSpec((tb, D_MODEL), lambda i: (i + xb0, 0)), src(D_ATTN), src(D_CONV),
                  _full((1, D_ATTN)), _full((1, D_CONV)),
                  _full((D_ATTN, D_MODEL)), _full((D_CONV, D_MODEL)), _full((1, D_MODEL))],
        out_specs=[row(D_MODEL), row(D_MODEL)],
        out_shape=[jax.ShapeDtypeStruct((n, D_MODEL), F32)] * 2,
        compiler_params=_tc_params(1),
        name="merge",
    )(x, oa, oc, g_a, g_c, w_out_a, w_out_c, g_f)


ID_BIG = 1e9


def _topk_rows(s, k):
    rows = lax.broadcasted_iota(jnp.int32, s.shape, 0).astype(F32)
    vals, idxs = [], []
    for _ in range(k):
        m = jnp.max(s, axis=0, keepdims=True)
        idx = jnp.min(jnp.where(s == m, rows, ID_BIG), axis=0, keepdims=True)
        vals.append(m)
        idxs.append(idx)
        s = jnp.where(rows == idx, NEG_INF, s)
    return jnp.concatenate(vals, axis=0), jnp.concatenate(idxs, axis=0)


PAIR_B_WIDE = 8


def _route_kernel(hn_ref, wq_ref, k1_ref, k2_ref, eid_out, gate_out):
    tb = hn_ref.shape[0]
    q = jnp.dot(hn_ref[...].astype(BF16), wq_ref[...], preferred_element_type=F32).astype(BF16)
    k1 = k1_ref[...]
    k2 = k2_ref[...]
    nt = (((1,), (1,)), ((), ()))
    r = lax.broadcasted_iota(jnp.int32, (PEER_TOPK + (PAIR_B_WIDE - 1) * PAIR_B_WIDE + PAIR_B_WIDE, tb), 0)
    mid = r - PEER_TOPK
    flat = jnp.where(r < PEER_TOPK, r,
                     jnp.where(mid < (PAIR_B_WIDE - 1) * PAIR_B_WIDE,
                               (1 + mid // PAIR_B_WIDE) * PEER_TOPK + mid % PAIR_B_WIDE,
                               (PAIR_B_WIDE + mid - (PAIR_B_WIDE - 1) * PAIR_B_WIDE) * PEER_TOPK)).astype(F32)
    half = PEER_NKEYS
    for h in range(PEER_HEADS):
        q1 = q[:, (2 * h) * half:(2 * h + 1) * half]
        q2 = q[:, (2 * h + 1) * half:(2 * h + 2) * half]
        s1 = lax.dot_general(k1, q1, nt, preferred_element_type=F32)
        s2 = lax.dot_general(k2, q2, nt, preferred_element_type=F32)
        v1, i1 = _topk_rows(s1, PEER_TOPK)
        v2, i2 = _topk_rows(s2, PEER_TOPK)
        e1 = i1 * PEER_NKEYS
        cand = jnp.concatenate(
            [v1[0:1] + v2]
            + [v1[a:a + 1] + v2[0:PAIR_B_WIDE] for a in range(1, PAIR_B_WIDE)]
            + [v1[PAIR_B_WIDE:] + v2[0:1]], axis=0)
        cid = jnp.concatenate(
            [e1[0:1] + i2]
            + [e1[a:a + 1] + i2[0:PAIR_B_WIDE] for a in range(1, PAIR_B_WIDE)]
            + [e1[PAIR_B_WIDE:] + i2[0:1]], axis=0)
        scs, eids = [], []
        for _ in range(PEER_TOPK):
            m = jnp.max(cand, axis=0, keepdims=True)
            jsel = jnp.min(jnp.where(cand == m, flat, ID_BIG), axis=0, keepdims=True)
            hit = flat == jsel
            eids.append(jnp.max(jnp.where(hit, cid, -1.0), axis=0, keepdims=True))
            scs.append(m)
            cand = jnp.where(hit, NEG_INF, cand)
        sc = jnp.concatenate(scs, axis=0)
        e = jnp.exp(sc - sc[0:1])
        gate_out[h * PEER_TOPK:(h + 1) * PEER_TOPK, :] = e / jnp.sum(e, axis=0, keepdims=True)
        eid_out[h * PEER_TOPK:(h + 1) * PEER_TOPK, :] = jnp.concatenate(eids, axis=0).astype(jnp.int32)


def _route(hn, tb, wq_bf, keys1_bf, keys2_bf):
    n = hn.shape[0]
    col = pl.BlockSpec((PEER_SEL, tb), lambda i: (0, i))
    return pl.pallas_call(
        _route_kernel,
        grid=(n // tb,),
        in_specs=[pl.BlockSpec((tb, D_MODEL), lambda i: (i, 0)),
                  _full((D_MODEL, 2 * PEER_NKEYS * PEER_HEADS)),
                  _full((PEER_NKEYS, PEER_NKEYS)), _full((PEER_NKEYS, PEER_NKEYS))],
        out_specs=[col, col],
        out_shape=[jax.ShapeDtypeStruct((PEER_SEL, n), jnp.int32),
                   jax.ShapeDtypeStruct((PEER_SEL, n), F32)],
        compiler_params=_tc_params(1),
        name="route",
    )(hn, wq_bf, keys1_bf, keys2_bf)


SC_ROWS = 32
SC_GROUP = SC_LANES
SC_TOKENS = 8
SC_CHUNKS = PEER_SEL // SC_ROWS

_SC_PARAMS = pltpu.CompilerParams(needs_layout_passes=False)


def _sc_mesh():
    return plsc.VectorSubcoreMesh(core_axis_name="c", subcore_axis_name="s",
                                  num_cores=SC_CORES, num_subcores=SC_SUBCORES)


def _sc_token_loop(n_batches, table_hbm, eid_v, rbufs, rsems, prefetch, prefetch_wait, store,
                   contract):
    def gather(slot, tl, ch, p):
        idx = eid_v.at[slot, tl, pl.ds(ch * SC_ROWS, SC_ROWS)]
        return pltpu.make_async_copy(table_hbm.at[idx], rbufs[p], rsems[p])

    prefetch(0, 0)
    prefetch_wait()
    gather(0, 0, 0, 0).start()

    @pl.loop(0, n_batches * SC_TOKENS)
    def _(i):
        b = i // SC_TOKENS
        tl = i % SC_TOKENS
        slot = b % 2
        more = b + 1 < n_batches

        @pl.when(jnp.logical_and(tl == 0, more))
        def _():
            prefetch(b + 1, 1 - slot)

        for ch in range(SC_CHUNKS):
            p = ch % 2
            if ch + 1 < SC_CHUNKS:
                gather(slot, tl, ch + 1, 1 - p).start()
            else:
                @pl.when(tl + 1 < SC_TOKENS)
                def _():
                    gather(slot, tl + 1, 0, 1 - p).start()

                @pl.when(jnp.logical_and(tl + 1 == SC_TOKENS, more))
                def _():
                    prefetch_wait()
                    gather(1 - slot, 0, 0, 1 - p).start()
            gather(slot, tl, ch, p).wait()
            contract(slot, tl, ch, rbufs[p])

        @pl.when(tl + 1 == SC_TOKENS)
        def _():
            @pl.when(b >= 1)
            def _():
                store(b - 1, 1 - slot).wait()
            store(b, slot).start()

    store(n_batches - 1, (n_batches - 1) % 2).wait()


def _sc_batches(n):
    assert n % (SC_WORKERS * SC_TOKENS) == 0, n
    return n // (SC_WORKERS * SC_TOKENS)


def _peer_hval(hn, eid, peer_u, after):
    n = hn.shape[0]
    nbw = _sc_batches(n)

    @functools.partial(
        pl.kernel, mesh=_sc_mesh(),
        out_type=jax.ShapeDtypeStruct((n // SC_TOKENS, SC_TOKENS, PEER_SEL), F32),
        scratch_types=[
            pltpu.VMEM((2, SC_TOKENS, PEER_SEL), jnp.int32),
            pltpu.VMEM((2, SC_TOKENS, D_MODEL), F32),
            pltpu.VMEM((SC_ROWS, D_MODEL), F32),
            pltpu.VMEM((SC_ROWS, D_MODEL), F32),
            pltpu.VMEM((2, SC_TOKENS, PEER_SEL), F32),
            pltpu.VMEM((SC_GROUP, SC_LANES), F32),
            pltpu.SemaphoreType.DMA, pltpu.SemaphoreType.DMA,
            pltpu.SemaphoreType.DMA, pltpu.SemaphoreType.DMA,
        ],
        compiler_params=_SC_PARAMS, name="peer_hval",
        cost_estimate=pl.CostEstimate(flops=2 * n * PEER_SEL * D_MODEL, transcendentals=0,
                                      bytes_accessed=n * PEER_SEL * D_MODEL * 4))
    def k(x_hbm, eid_hbm, u_hbm, after_hbm, o_hbm, eid_v, x_v, r0, r1, h_v, tr, sr0, sr1, spf, sout):
        del after_hbm
        blk0 = (lax.axis_index("s") * SC_CORES + lax.axis_index("c")) * nbw
        lane = lax.iota(jnp.int32, SC_LANES)
        zero = jnp.zeros((SC_LANES,), F32)

        def prefetch(b, slot):
            pltpu.async_copy(eid_hbm.at[blk0 + b], eid_v.at[slot], spf)
            pltpu.async_copy(x_hbm.at[blk0 + b], x_v.at[slot], spf)

        def prefetch_wait():
            pltpu.make_async_copy(eid_hbm.at[0], eid_v.at[0], spf).wait()
            pltpu.make_async_copy(x_hbm.at[0], x_v.at[0], spf).wait()

        def store(b, slot):
            return pltpu.make_async_copy(h_v.at[slot], o_hbm.at[blk0 + b], sout)

        def contract(slot, tl, ch, rbuf):
            for g in range(SC_ROWS // SC_GROUP):
                def body(c, accs):
                    xc = x_v[slot, tl, pl.ds(c * SC_LANES, SC_LANES)]
                    return tuple(
                        accs[r] + rbuf[g * SC_GROUP + r, pl.ds(c * SC_LANES, SC_LANES)] * xc
                        for r in range(SC_GROUP))
                accs = lax.fori_loop(0, D_MODEL // SC_LANES, body, (zero,) * SC_GROUP)
                for r in range(SC_GROUP):
                    tr[r, :] = accs[r]
                res = zero
                for jj in range(SC_LANES):
                    res = res + plsc.load_gather(tr, [lane, jnp.full((SC_LANES,), jj, jnp.int32)])
                h_v[slot, tl, pl.ds(ch * SC_ROWS + g * SC_GROUP, SC_GROUP)] = res

        _sc_token_loop(nbw, u_hbm, eid_v, (r0, r1), (sr0, sr1), prefetch, prefetch_wait, store,
                       contract)

    out = k(hn.reshape(n // SC_TOKENS, SC_TOKENS, D_MODEL),
            eid.reshape(n // SC_TOKENS, SC_TOKENS, PEER_SEL), peer_u, after)
    return out.reshape(n, PEER_SEL)


def _gate_kernel(hv_ref, gate_ref, a_out):
    hv = hv_ref[...]
    gelu = hv * (lax.erf(hv * (2.0 ** -0.5)) + 1.0) * 0.5
    a_out[...] = gate_ref[...] * gelu


def _gate(hval, gate, tb):
    n = hval.shape[0]
    row = pl.BlockSpec((tb, PEER_SEL), lambda i: (i, 0))
    return pl.pallas_call(
        _gate_kernel, grid=(n // tb,), in_specs=[row, row], out_specs=row,
        out_shape=jax.ShapeDtypeStruct((n, PEER_SEL), F32),
        compiler_params=_tc_params(1), name="gate",
    )(hval, gate)


VT_ROWS = D_MODEL // 2 // LANES
VT_TOKENS = 128
VT_UNROLL = 8
HI_MASK = -65536


def _pack_value_table(peer_v):
    e = peer_v.shape[0]
    bits = lax.bitcast_convert_type(peer_v.astype(BF16), jnp.uint16).astype(jnp.uint32)
    words = bits[:, :D_MODEL // 2] | (bits[:, D_MODEL // 2:] << 16)
    return lax.bitcast_convert_type(words, jnp.int32).reshape(e * VT_ROWS, LANES)


def _vside_kernel(row_s, a_s, h_ref, tab_ref, y_ref):
    tb = h_ref.shape[0]

    def token_group(p, carry):
        zero = jnp.zeros((VT_ROWS, LANES), F32)
        ts = tuple(VT_UNROLL * p + k for k in range(VT_UNROLL))
        lo = [zero] * VT_UNROLL
        hi = [zero] * VT_UNROLL
        for e in range(PEER_SEL):
            for k, t in enumerate(ts):
                r0 = pl.multiple_of(row_s[t, e], VT_ROWS)
                w = tab_ref[pl.ds(r0, VT_ROWS), :]
                coef = a_s[t, e]
                lo[k] = lo[k] + coef * pltpu.bitcast(w << 16, F32)
                hi[k] = hi[k] + coef * pltpu.bitcast(w & HI_MASK, F32)
        for k, t in enumerate(ts):
            y_ref[t] = h_ref[t] + jnp.concatenate([lo[k], hi[k]], axis=0)
        return carry

    lax.fori_loop(0, tb // VT_UNROLL, token_group, 0)


def _peer_values(h, a, row_ids, v_packed):
    n = h.shape[0]
    tb = min(VT_TOKENS, n)
    smem = pl.BlockSpec((tb, PEER_SEL), lambda i: (i, 0), memory_space=pltpu.SMEM)
    slab = pl.BlockSpec((tb, 2 * VT_ROWS, LANES), lambda i: (i, 0, 0))
    table = pl.BlockSpec(v_packed.shape, lambda i: (0, 0), pipeline_mode=pl.Buffered(1))
    table_bytes = v_packed.shape[0] * LANES * 4
    y = pl.pallas_call(
        _vside_kernel,
        grid=(n // tb,),
        in_specs=[smem, smem, slab, table],
        out_specs=slab,
        out_shape=jax.ShapeDtypeStruct((n, 2 * VT_ROWS, LANES), F32),
        compiler_params=pltpu.CompilerParams(
            dimension_semantics=("arbitrary",),
            vmem_limit_bytes=table_bytes + 8 * 1024 * 1024),
        name="peer_values",
        cost_estimate=pl.CostEstimate(flops=2 * n * PEER_SEL * D_MODEL, transcendentals=0,
                                      bytes_accessed=table_bytes + n * PEER_SEL * D_MODEL * 2),
    )(row_ids, a, h.reshape(n, 2 * VT_ROWS, LANES), v_packed)
    return y.reshape(n, D_MODEL)


PEER_DEPTH = 3


def _chunk_plan(t, tb, first, last):
    eighth = t // 8
    if eighth % tb:
        return [t]
    sizes = [4 * eighth, 4 * eighth]
    if last:
        sizes = sizes[:1] + [2 * eighth, eighth, eighth]
    if first:
        sizes = [eighth, 3 * eighth] + sizes[1:]
    return sizes


def _peer_launch(h, hn, tb, after, wq_bf, keys1_bf, keys2_bf, peer_u):
    n = h.shape[0]
    eid_t, gate_t = _route(hn, tb, wq_bf, keys1_bf, keys2_bf)
    eid = eid_t.T
    pad = (-n) % (SC_WORKERS * SC_TOKENS)
    hn_p, eid_p = hn, eid
    if pad:
        spread = (jnp.arange(pad * PEER_SEL, dtype=jnp.int32) % peer_u.shape[0]).reshape(pad, PEER_SEL)
        hn_p = jnp.pad(hn, ((0, pad), (0, 0)))
        eid_p = jnp.concatenate([eid, spread], axis=0)
    hval = _peer_hval(hn_p, eid_p, peer_u, after)[:n]
    return h, eid, gate_t.T, hval


def _peer_finish(h, eid, gate, hval, tb, v_packed):
    a = _gate(hval, gate, tb)
    return _peer_values(h, a, eid * VT_ROWS, v_packed)


def _rope_tables(pos):
    half = HEAD_DIM // 2
    inv = ROPE_THETA ** (-jnp.arange(half, dtype=F32) / half)
    ang = pos.astype(F32)[:, None] * inv[None, :]
    cos = jnp.cos(ang)
    sin = jnp.sin(ang)
    reps = LANES // HEAD_DIM
    cos_f = jnp.tile(jnp.concatenate([cos, cos], axis=1), (1, reps))
    sin_s = jnp.tile(jnp.concatenate([-sin, sin], axis=1), (1, reps))
    return cos_f, sin_s


def _group_sum_matrix(width):
    g = jnp.arange(width) // HEAD_DIM
    return (g[:, None] == g[None, :]).astype(BF16)


def kernel(x_prompt, x_sample, cache_k_win, cache_v_win, state_conv, meta_tokens, norm_mix_g,
           w_in, q_norm_g, k_norm_g, attn_sinks, conv_w, conv_b, conv_ln_g, conv_ln_b, w_pw2,
           out_norm_attn_g, out_norm_conv_g, w_out, norm_ffn_g, peer_w_q, peer_keys1, peer_keys2,
           peer_u, peer_v):
    assert norm_mix_g.shape[0] == 1, "single-layer model"
    b, t, _ = x_prompt.shape
    s = x_sample.shape[0]
    w_buf = cache_k_win.shape[2]
    n = b * t

    g_mix = norm_mix_g[0][None, :]
    w_in_bf = w_in[0].astype(BF16)
    qg_t = jnp.tile(q_norm_g[0], N_HEADS)[None, :]
    kg_t = jnp.tile(k_norm_g[0], N_KV_HEADS)[None, :]
    gsum_q = _group_sum_matrix(D_ATTN)
    gsum_k = _group_sum_matrix(KV_DIM)
    sinks = attn_sinks[0]
    cw, cb = conv_w[0], conv_b[0][None, :]
    lg, lb = conv_ln_g[0][None, :], conv_ln_b[0][None, :]
    w2_bf = w_pw2[0].astype(BF16)
    g_a, g_c = out_norm_attn_g[0][None, :], out_norm_conv_g[0][None, :]
    w_out_a = w_out[0][:D_ATTN].astype(BF16)
    w_out_c = w_out[0][D_ATTN:].astype(BF16)
    g_f = norm_ffn_g[0][None, :]
    wq_bf = peer_w_q[0].astype(BF16)
    k1_bf = peer_keys1[0].astype(BF16)
    k2_bf = peer_keys2[0].astype(BF16)
    pu, vp = peer_u[0], _pack_value_table(peer_v[0])
    proj = functools.partial(_project, norm_g=g_mix, w_in_bf=w_in_bf, qg_t=qg_t, kg_t=kg_t,
                             gsum_q=gsum_q, gsum_k=gsum_k)
    conv = functools.partial(_conv_branch, conv_w=cw, conv_b=cb, ln_g=lg, ln_b=lb, w_pw2_bf=w2_bf)
    merge = functools.partial(_merge, g_a=g_a, g_c=g_c, w_out_a=w_out_a, w_out_c=w_out_c, g_f=g_f)

    tb = TB_DENSE
    no_dep = jnp.zeros((SC_TOKENS, LANES), F32)
    tab_x = _rope_tables(N_META + jnp.arange(t, dtype=jnp.int32))
    tab_m = _rope_tables(jnp.arange(N_META, dtype=jnp.int32))
    _, k_m, v_m, u_m = proj(meta_tokens, tab_m, 1, N_META, no_dep)
    halo0 = jnp.concatenate([jnp.zeros((CONV_HALO - N_META, D_CONV), F32), u_m], axis=0)

    k_meta = jnp.concatenate([jnp.zeros((META_PAD, KV_DIM), F32), k_m], axis=0)
    v_meta = jnp.concatenate([jnp.zeros((META_PAD, KV_DIM), F32), v_m], axis=0)

    xp = x_prompt.reshape(b * t, D_MODEL)

    def dense(bi, after):
        q, k, v, u = proj(xp, tab_x, t // tb, tb, after, row0=bi * t, nrows=t)
        o_attn = _prompt_attention(sinks, q, k, v, k_meta, v_meta)
        u3 = u.reshape(t // tb, tb, D_CONV)
        halo = jnp.concatenate([halo0[None], u3[:-1, tb - CONV_HALO:]], axis=0)
        o_conv = conv(halo, u, tb)
        hs, r0 = [], 0
        for rows in _chunk_plan(t, tb, bi == 0, bi == b - 1):
            hs.append(merge(xp, o_attn, o_conv, tb, x_row0=bi * t + r0, row0=r0, nrows=rows))
            r0 += rows
        return hs, k, v, u

    launched, ys, kvu = [], [], []
    cur = dense(0, no_dep)
    for bi in range(b):
        hs, k, v, u = cur
        kvu.append((k, v, u))
        for h, hn in hs:
            c = len(launched)
            if c >= PEER_DEPTH:
                ys.append(_peer_finish(*launched[c - PEER_DEPTH], TB_ROUTE, vp))
            after = ys[c - PEER_DEPTH][:SC_TOKENS] if c >= PEER_DEPTH else jnp.zeros((SC_TOKENS, D_MODEL), F32)
            launched.append(_peer_launch(h, hn, TB_ROUTE, after, wq_bf, k1_bf, k2_bf, pu))
        if bi + 1 < b:
            cur = dense(bi + 1, launched[-1][1][:SC_TOKENS, :LANES].astype(F32))
    for c in range(len(ys), len(launched)):
        ys.append(_peer_finish(*launched[c], TB_ROUTE, vp))
    y_prompt = jnp.concatenate(ys, axis=0).reshape(b, t, D_MODEL)

    new_k_prompt = jnp.stack([k[t - WINDOW:] for k, _, _ in kvu]).reshape(1, b, WINDOW, N_KV_HEADS, HEAD_DIM)
    new_v_prompt = jnp.stack([v[t - WINDOW:] for _, v, _ in kvu]).reshape(1, b, WINDOW, N_KV_HEADS, HEAD_DIM)
    new_conv_prompt = jnp.stack([u[t - (CONV_WIDTH - 1):] for _, _, u in kvu])[None]

    xs = x_sample.reshape(s, D_MODEL)
    tab_s = _rope_tables(jnp.full((s,), PAST_LEN, jnp.int32))
    qs, ks, vs, us = proj(xs, tab_s, 1, s, no_dep)
    ck = cache_k_win[0]
    cv = cache_v_win[0]
    o_attn_s = _decode_attention(
        sinks,
        qs.reshape(s, N_HEADS, HEAD_DIM), ks, vs,
        ck.reshape(s, w_buf, KV_DIM), cv.reshape(s, w_buf, KV_DIM), min(SB_DECODE, s))
    o_attn_s = o_attn_s.reshape(s, D_ATTN)
    cs = state_conv[0]
    hist = jnp.concatenate([jnp.zeros((s, CONV_HALO - (CONV_WIDTH - 1), D_CONV), F32), cs], axis=1)
    us_blk = jnp.concatenate([us[:, None, :], jnp.zeros((s, SUBLANES - 1, D_CONV), F32)], axis=1)
    o_conv_s = conv(hist, us_blk.reshape(s * SUBLANES, D_CONV), SUBLANES)
    o_conv_s = o_conv_s.reshape(s, SUBLANES, D_CONV)[:, 0]
    hs, hns = merge(xs, o_attn_s, o_conv_s, s)
    launch_s = _peer_launch(hs, hns, s, jnp.zeros((SC_TOKENS, D_MODEL), F32), wq_bf, k1_bf, k2_bf, pu)
    y_sample = _peer_finish(*launch_s, s, vp).reshape(s, 1, D_MODEL)

    new_k_sample = jnp.concatenate([ck[:, 1:], ks.reshape(s, 1, N_KV_HEADS, HEAD_DIM)], axis=1)[None]
    new_v_sample = jnp.concatenate([cv[:, 1:], vs.reshape(s, 1, N_KV_HEADS, HEAD_DIM)], axis=1)[None]
    new_conv_sample = jnp.concatenate([cs[:, 1:], us[:, None, :]], axis=1)[None]
    if w_buf != WINDOW:
        raise NotImplementedError("cache window shorter than the attention window")

    return (y_prompt, y_sample, new_k_prompt, new_v_prompt, new_conv_prompt,
            new_k_sample, new_v_sample, new_conv_sample)
```
